```python
import jax, jax.numpy as jnp
from jax import lax
import numpy as np

D_MODEL = 2048
BATCH = 8
SEQ = 2048
DEPTH = 2
DEC_BATCH = 8
DEC_SEQ = 16
PAST_LEN = 4096

CHUNK = 64
EPS = 1e-6
RET_H = 4
RET_DK = 256
RET_DV = 256
ROPE_BASE = 10000.0
GLA_H = 4
GLA_DK = 128
GLA_DV = 256
GLA_RANK = 16
GLA_NORMALIZER = 16.0
GDN_H = 8
GDN_DK = 128
GDN_DV = 128
GDN_CONV = 4
GDN_CONV_CH = GDN_H * (2 * GDN_DK + GDN_DV)
N_BRANCH = 3
BRANCH_W = 1024
N_MEM = 256
X_H = 4
X_DH = D_MODEL // X_H
D_FF = 5632
FFN_CONV = 3
IN_SIZES = (RET_H * RET_DK, RET_H * RET_DK, RET_H * RET_DV, RET_H * RET_DV,
            GLA_H * GLA_DK, GLA_H * GLA_DK, GLA_H * GLA_DV, GLA_H * GLA_DV, GLA_RANK,
            GDN_H * GDN_DK, GDN_H * GDN_DK, GDN_H * GDN_DV, GDN_H * GDN_DV, GDN_H, GDN_H,
            N_BRANCH * D_MODEL)
D_IN = sum(IN_SIZES)

kernel_name = 'hybrid_retention_gla_gdn_streaming_step'

F32 = jnp.float32


def _rmsnorm(x, g):
    xf = x.astype(F32)
    y = xf * lax.rsqrt(jnp.mean(xf * xf, axis=-1, keepdims=True) + EPS)
    return (y * g.astype(F32)).astype(x.dtype)


def _head_rmsnorm(o, g=None):
    y = o * lax.rsqrt(jnp.mean(o * o, axis=-1, keepdims=True) + EPS)
    return y if g is None else y * g.astype(F32)


def _l2norm(x):
    return x * lax.rsqrt(jnp.sum(x * x, axis=-1, keepdims=True) + EPS)


def _causal_conv(x, buf, w):
    width, t = w.shape[0], x.shape[1]
    xp = jnp.concatenate([buf.astype(x.dtype), x], axis=1)
    y = xp[:, 0:t] * w[0]
    for i in range(1, width):
        y = y + xp[:, i:i + t] * w[i]
    return y, xp[:, t:]


def _to_chunks(a, c):
    b, t = a.shape[:2]
    return jnp.moveaxis(a.reshape((b, t // c, c) + a.shape[2:]), 1, 0)


def _from_chunks(a):
    n, b, c = a.shape[:3]
    return jnp.moveaxis(a, 0, 1).reshape((b, n * c) + a.shape[3:])


def _rotary(x, pos):
    half = x.shape[-1] // 2
    freqs = 1.0 / (ROPE_BASE ** (jnp.arange(half, dtype=F32) / half))
    ang = pos[:, None] * freqs[None, :]
    cos = jnp.cos(ang)[None, :, None, :]
    sin = jnp.sin(ang)[None, :, None, :]
    x1, x2 = x[..., :half], x[..., half:]
    return jnp.concatenate([x1 * cos - x2 * sin, x1 * sin + x2 * cos], axis=-1)


def _retention(q, k, v, s0):
    t = q.shape[1]
    c = min(CHUNK, t)
    ld = jnp.log1p(-jnp.exp2(-5.0 - jnp.arange(RET_H, dtype=F32)))
    idx = jnp.arange(c, dtype=F32)
    diff = idx[:, None] - idx[None, :]
    causal = diff >= 0
    dmask = jnp.where(causal[None], jnp.exp(ld[:, None, None] * jnp.where(causal, diff, 0.0)[None]), 0.0)
    q_dec = jnp.exp(ld[None, :] * (idx[:, None] + 1.0))
    k_dec = jnp.exp(ld[None, :] * (c - 1.0 - idx[:, None]))
    s_dec = jnp.exp(ld * c)

    def step(s, inp):
        qc, kc, vc = inp
        att = jnp.einsum('bihd,bjhd->bhij', qc, kc) * dmask[None]
        o = (jnp.einsum('bhij,bjhv->bihv', att, vc)
             + jnp.einsum('bihd,bhdv->bihv', qc, s) * q_dec[None, :, :, None])
        s = s_dec[None, :, None, None] * s + jnp.einsum('bjhd,bjhv->bhdv', kc * k_dec[None, :, :, None], vc)
        return s, o

    s, o = lax.scan(step, s0, (_to_chunks(q, c), _to_chunks(k, c), _to_chunks(v, c)))
    return _from_chunks(o), s


def _gla(q, k, v, g, s0):
    t = q.shape[1]
    c = min(CHUNK, t)
    causal = jnp.tril(jnp.ones((c, c), dtype=bool))

    def step(s, inp):
        qc, kc, vc, gc = inp
        b = jnp.cumsum(gc, axis=1)
        qe = qc * jnp.exp(b)
        ke = kc * jnp.exp(-b)
        att = jnp.where(causal[None, None], jnp.einsum('bihd,bjhd->bhij', qe, ke), 0.0)
        o = jnp.einsum('bhij,bjhv->bihv', att, vc) + jnp.einsum('bihd,bhdv->bihv', qe, s)
        b_last = b[:, -1]
        s = (jnp.exp(b_last)[..., None] * s
             + jnp.einsum('bjhd,bjhv->bhdv', kc * jnp.exp(b_last[:, None] - b), vc))
        return s, o

    s, o = lax.scan(step, s0, (_to_chunks(q, c), _to_chunks(k, c), _to_chunks(v, c), _to_chunks(g, c)))
    return _from_chunks(o), s


def _gated_delta(q, k, v, g, beta, s0):
    t = q.shape[1]
    c = min(CHUNK, t)
    incl = jnp.tril(jnp.ones((c, c), dtype=bool))
    strict = jnp.tril(jnp.ones((c, c), dtype=bool), k=-1)
    eye = jnp.eye(c, dtype=F32)

    def step(s, inp):
        qc, kc, vc, gc, bc = inp
        gam = jnp.moveaxis(jnp.cumsum(gc, axis=1), 1, 2)
        bh = jnp.moveaxis(bc, 1, 2)
        qh, kh, vh = (jnp.moveaxis(a, 1, 2) for a in (qc, kc, vc))
        decay = jnp.exp(jnp.where(incl, gam[..., :, None] - gam[..., None, :], -jnp.inf))
        kk = jnp.einsum('bhid,bhjd->bhij', kh, kh)
        m = eye + jnp.where(strict, bh[..., :, None] * kk * decay, 0.0)
        rhs = jnp.concatenate([bh[..., None] * vh, (bh * jnp.exp(gam))[..., None] * kh], axis=-1)
        sol = lax.linalg.triangular_solve(m, rhs, left_side=True, lower=True, unit_diagonal=True)
        u, wk = sol[..., :GDN_DV], sol[..., GDN_DV:]
        w = u - jnp.einsum('bhcd,bhdv->bhcv', wk, s)
        att = jnp.einsum('bhid,bhjd->bhij', qh, kh) * decay
        o = (jnp.exp(gam)[..., None] * jnp.einsum('bhid,bhdv->bhiv', qh, s)
             + jnp.einsum('bhij,bhjv->bhiv', att, w))
        g_last = gam[..., -1]
        s = (jnp.exp(g_last)[..., None, None] * s
             + jnp.einsum('bhjd,bhjv->bhdv', kh * jnp.exp(g_last[..., None] - gam)[..., None], w))
        return s, jnp.moveaxis(o, 1, 2)

    xs = (_to_chunks(q, c), _to_chunks(k, c), _to_chunks(v, c), _to_chunks(g, c), _to_chunks(beta, c))
    s, o = lax.scan(step, s0, xs)
    return _from_chunks(o), s


def _token_mixer(h, pos, s_ret, s_gla, s_gdn, buf_gdn, w_in, w_gla_gk, b_gla_gk, gla_norm,
                 gdn_conv_w, gdn_a_log, gdn_dt_bias, gdn_norm, b_gate, w_branch, w_out):
    bsz, t, _ = h.shape
    splits = np.cumsum(IN_SIZES)[:-1].tolist()
    (r_q, r_k, r_v, r_g, l_q, l_k, l_v, l_g, l_lr,
     d_q, d_k, d_v, d_z, d_a, d_b, g_br) = jnp.split(h @ w_in, splits, axis=-1)

    def heads(a, n):
        return a.reshape(bsz, t, n, -1).astype(F32)

    q = _rotary(heads(r_q, RET_H), pos)
    k = _rotary(heads(r_k, RET_H), pos) * RET_DK ** -0.5
    o_ret, s_ret = _retention(q, k, heads(r_v, RET_H), s_ret.astype(F32))
    y_ret = _head_rmsnorm(o_ret).reshape(bsz, t, -1) * jax.nn.silu(r_g.astype(F32))

    gk = jax.nn.log_sigmoid((l_lr @ w_gla_gk + b_gla_gk).astype(F32)) / GLA_NORMALIZER
    o_gla, s_gla = _gla(heads(l_q, GLA_H) * GLA_DK ** -0.5, heads(l_k, GLA_H), heads(l_v, GLA_H),
                        gk.reshape(bsz, t, GLA_H, GLA_DK), s_gla.astype(F32))
    y_gla = _head_rmsnorm(o_gla, gla_norm).reshape(bsz, t, -1) * jax.nn.silu(l_g.astype(F32))

    qkv, buf_gdn = _causal_conv(jnp.concatenate([d_q, d_k, d_v], axis=-1), buf_gdn, gdn_conv_w)
    qkv = jax.nn.silu(qkv.astype(F32))
    c_q, c_k, c_v = jnp.split(qkv, [GDN_H * GDN_DK, 2 * GDN_H * GDN_DK], axis=-1)
    q = _l2norm(c_q.reshape(bsz, t, GDN_H, GDN_DK)) * GDN_DK ** -0.5
    k = _l2norm(c_k.reshape(bsz, t, GDN_H, GDN_DK))
    beta = jax.nn.sigmoid(d_b.astype(F32))
    g = -jnp.exp(gdn_a_log.astype(F32)) * jax.nn.softplus(d_a.astype(F32) + gdn_dt_bias.astype(F32))
    o_gdn, s_gdn = _gated_delta(q, k, c_v.reshape(bsz, t, GDN_H, GDN_DV), g, beta, s_gdn.astype(F32))
    y_gdn = _head_rmsnorm(o_gdn, gdn_norm).reshape(bsz, t, -1) * jax.nn.silu(d_z.astype(F32))

    gates = jax.nn.sigmoid(g_br.reshape(bsz, t, N_BRANCH, D_MODEL).astype(F32) + b_gate.astype(F32))
    branches = (y_ret, y_gla, y_gdn)
    merged = gates[:, :, 0] * (branches[0].astype(h.dtype) @ w_branch[0]).astype(F32)
    for n in range(1, N_BRANCH):
        merged = merged + gates[:, :, n] * (branches[n].astype(h.dtype) @ w_branch[n]).astype(F32)
    return merged.astype(h.dtype) @ w_out, s_ret, s_gla, s_gdn, buf_gdn


def _mem_kv(mem, ln_mem, w_xkv):
    bsz, m, _ = mem.shape
    k, v = jnp.split(_rmsnorm(mem, ln_mem) @ w_xkv, 2, axis=-1)
    return k.reshape(bsz, m, X_H, X_DH), v.reshape(bsz, m, X_H, X_DH)


def _cross_attn(h, mem_k, mem_v, w_xq, w_xo):
    bsz, t, _ = h.shape
    q = (h @ w_xq).reshape(bsz, t, X_H, X_DH).astype(F32)
    s = jnp.einsum('bthd,bmhd->bhtm', q, mem_k.astype(F32)) * X_DH ** -0.5
    p = jax.nn.softmax(s, axis=-1)
    o = jnp.einsum('bhtm,bmhd->bthd', p, mem_v.astype(F32)).reshape(bsz, t, D_MODEL)
    return o.astype(h.dtype) @ w_xo


def _conv_ffn(h, buf, w_ffn_in, conv_w, conv_b, w_ffn_out):
    gate, up = jnp.split(h @ w_ffn_in, 2, axis=-1)
    gc, buf = _causal_conv(gate, buf, conv_w)
    act = jax.nn.gelu((gc + conv_b).astype(F32)) * up.astype(F32)
    return act.astype(h.dtype) @ w_ffn_out, buf


def _trunk(x, offset, mem_k, mem_v, s_ret, s_gla, s_gdn, buf_gdn, buf_ffn, wts):
    pos = offset + jnp.arange(x.shape[1], dtype=F32)
    outs = ([], [], [], [], [])
    for l in range(DEPTH):
        mix, sr, sg, sd, bg = _token_mixer(
            _rmsnorm(x, wts['ln_mix'][l]), pos, s_ret[l], s_gla[l], s_gdn[l], buf_gdn[l],
            wts['w_in'][l], wts['w_gla_gk'][l], wts['b_gla_gk'][l], wts['gla_norm'][l],
            wts['gdn_conv_w'][l], wts['gdn_a_log'][l], wts['gdn_dt_bias'][l], wts['gdn_norm'][l],
            wts['b_gate'][l], wts['w_branch'][l], wts['w_out'][l])
        x = x + mix
        x = x + _cross_attn(_rmsnorm(x, wts['ln_xattn'][l]), mem_k[l], mem_v[l], wts['w_xq'][l], wts['w_xo'][l])
        ffn, bf = _conv_ffn(_rmsnorm(x, wts['ln_ffn'][l]), buf_ffn[l], wts['w_ffn_in'][l],
                            wts['ffn_conv_w'][l], wts['ffn_conv_b'][l], wts['w_ffn_out'][l])
        x = x + ffn
        for lst, val in zip(outs, (sr, sg, sd, bg, bf)):
            lst.append(val.astype(x.dtype))
    y = _rmsnorm(x, wts['ln_final'])
    return (y,) + tuple(jnp.stack(lst) for lst in outs)


def setup_inputs(seed: int = 0) -> dict:
    key = jax.random.key(seed)
    keys = list(jax.random.split(key, 40))

    def nrm(shape, scale):
        return scale * jax.random.normal(keys.pop(), shape, F32)

    def gain(shape):
        return 1.0 + nrm(shape, 0.02)

    dt = jnp.exp(jax.random.uniform(keys.pop(), (DEPTH, GDN_H), F32, np.log(1e-3), np.log(1e-1)))
    return {
        'x_prompt': nrm((BATCH, SEQ, D_MODEL), 1.0),
        'x_sample': nrm((DEC_BATCH, DEC_SEQ, D_MODEL), 1.0),
        'mem_prompt': nrm((BATCH, N_MEM, D_MODEL), 1.0),
        'state_ret': nrm((DEPTH, DEC_BATCH, RET_H, RET_DK, RET_DV), 0.5),
        'state_gla': nrm((DEPTH, DEC_BATCH, GLA_H, GLA_DK, GLA_DV), 0.5),
        'state_gdn': nrm((DEPTH, DEC_BATCH, GDN_H, GDN_DK, GDN_DV), 0.2),
        'state_gdn_conv': nrm((DEPTH, DEC_BATCH, GDN_CONV - 1, GDN_CONV_CH), 1.0),
        'state_ffn_conv': nrm((DEPTH, DEC_BATCH, FFN_CONV - 1, D_FF), 1.0),
        'cache_mem_k': nrm((DEPTH, DEC_BATCH, N_MEM, X_H, X_DH), 1.0),
        'cache_mem_v': nrm((DEPTH, DEC_BATCH, N_MEM, X_H, X_DH), 1.0),
        'ln_mix': gain((DEPTH, D_MODEL)),
        'w_in': nrm((DEPTH, D_MODEL, D_IN), D_MODEL ** -0.5),
        'w_gla_gk': nrm((DEPTH, GLA_RANK, GLA_H * GLA_DK), GLA_RANK ** -0.5),
        'b_gla_gk': nrm((DEPTH, GLA_H * GLA_DK), 0.01),
        'gla_norm': gain((DEPTH, GLA_DV)),
        'gdn_conv_w': nrm((DEPTH, GDN_CONV, GDN_CONV_CH), GDN_CONV ** -0.5),
        'gdn_a_log': jnp.log(jax.random.uniform(keys.pop(), (DEPTH, GDN_H), F32, 1.0, 16.0)),
        'gdn_dt_bias': dt + jnp.log(-jnp.expm1(-dt)),
        'gdn_norm': gain((DEPTH, GDN_DV)),
        'b_gate': nrm((DEPTH, N_BRANCH, D_MODEL), 0.01),
        'w_branch': nrm((DEPTH, N_BRANCH, BRANCH_W, D_MODEL), BRANCH_W ** -0.5),
        'w_out': nrm((DEPTH, D_MODEL, D_MODEL), 0.5 * D_MODEL ** -0.5),
        'ln_xattn': gain((DEPTH, D_MODEL)),
        'ln_mem': gain((DEPTH, D_MODEL)),
        'w_xq': nrm((DEPTH, D_MODEL, D_MODEL), D_MODEL ** -0.5),
        'w_xkv': nrm((DEPTH, D_MODEL, 2 * D_MODEL), D_MODEL ** -0.5),
        'w_xo': nrm((DEPTH, D_MODEL, D_MODEL), 0.5 * D_MODEL ** -0.5),
        'ln_ffn': gain((DEPTH, D_MODEL)),
        'w_ffn_in': nrm((DEPTH, D_MODEL, 2 * D_FF), D_MODEL ** -0.5),
        'ffn_conv_w': nrm((DEPTH, FFN_CONV, D_FF), FFN_CONV ** -0.5),
        'ffn_conv_b': nrm((DEPTH, D_FF), 0.01),
        'w_ffn_out': nrm((DEPTH, D_FF, D_MODEL), 0.5 * D_FF ** -0.5),
        'ln_final': gain((D_MODEL,)),
    }


def reference(x_prompt, x_sample, mem_prompt, state_ret, state_gla, state_gdn, state_gdn_conv,
              state_ffn_conv, cache_mem_k, cache_mem_v, ln_mix, w_in, w_gla_gk, b_gla_gk, gla_norm,
              gdn_conv_w, gdn_a_log, gdn_dt_bias, gdn_norm, b_gate, w_branch, w_out, ln_xattn, ln_mem,
              w_xq, w_xkv, w_xo, ln_ffn, w_ffn_in, ffn_conv_w, ffn_conv_b, w_ffn_out, ln_final):
    wts = dict(ln_mix=ln_mix, w_in=w_in, w_gla_gk=w_gla_gk, b_gla_gk=b_gla_gk, gla_norm=gla_norm,
               gdn_conv_w=gdn_conv_w, gdn_a_log=gdn_a_log, gdn_dt_bias=gdn_dt_bias, gdn_norm=gdn_norm,
               b_gate=b_gate, w_branch=w_branch, w_out=w_out, ln_xattn=ln_xattn, w_xq=w_xq, w_xo=w_xo,
               ln_ffn=ln_ffn, w_ffn_in=w_ffn_in, ffn_conv_w=ffn_conv_w, ffn_conv_b=ffn_conv_b,
               w_ffn_out=w_ffn_out, ln_final=ln_final)

    mem_kv = [_mem_kv(mem_prompt, ln_mem[l], w_xkv[l]) for l in range(DEPTH)]
    mem_k_p = jnp.stack([kv[0] for kv in mem_kv])
    mem_v_p = jnp.stack([kv[1] for kv in mem_kv])
    bp, dt = x_prompt.shape[0], x_prompt.dtype
    y_p, ret_p, gla_p, gdn_p, gdn_conv_p, ffn_conv_p = _trunk(
        x_prompt, 0.0, mem_k_p, mem_v_p,
        jnp.zeros((DEPTH, bp, RET_H, RET_DK, RET_DV), dt),
        jnp.zeros((DEPTH, bp, GLA_H, GLA_DK, GLA_DV), dt),
        jnp.zeros((DEPTH, bp, GDN_H, GDN_DK, GDN_DV), dt),
        jnp.zeros((DEPTH, bp, GDN_CONV - 1, GDN_CONV_CH), dt),
        jnp.zeros((DEPTH, bp, FFN_CONV - 1, D_FF), dt),
        wts)

    y_s, ret_s, gla_s, gdn_s, gdn_conv_s, ffn_conv_s = _trunk(
        x_sample, float(PAST_LEN), cache_mem_k, cache_mem_v,
        state_ret, state_gla, state_gdn, state_gdn_conv, state_ffn_conv, wts)

    return (y_p, y_s, ret_p, gla_p, gdn_p, gdn_conv_p, ffn_conv_p, mem_k_p, mem_v_p,
            ret_s, gla_s, gdn_s, gdn_conv_s, ffn_conv_s)
```

```python
import functools

import jax
import jax.numpy as jnp
from jax import lax
from jax.experimental import pallas as pl
from jax.experimental.pallas import tpu as pltpu

F32 = jnp.float32
BF16 = jnp.bfloat16

EPS = 1e-6
CHUNK = 64
RET_H, RET_DK, RET_DV = 4, 256, 256
GLA_H, GLA_DK, GLA_DV = 4, 128, 256
GLA_RANK = 16
GLA_NORMALIZER = 16.0
GDN_H, GDN_DK, GDN_DV = 8, 128, 128
GDN_CONV = 4
N_BRANCH = 3
X_H = 4
FFN_CONV = 3
ROPE_BASE = 10000.0

LANES = 128
SUBLANES = 8
SMALL_A0 = GLA_RANK
SMALL_B0 = GLA_RANK + GDN_H
VMEM_LIMIT = 52 * 1024 * 1024
MM_TM = 1024
MM_TN = 1024
RES_TM = 256
RES_TM_K = 512
RES_TK = 1536
SEQ_TB = 512
SEQ_TB_GDN = 256
XATTN_TQ = 512
FFN_TM = 1024
FFN_TN = 512


def _cp(*sem):
    return pltpu.CompilerParams(dimension_semantics=sem, vmem_limit_bytes=VMEM_LIMIT)


def _dot(a, b):
    return jnp.dot(a, b, preferred_element_type=F32)


def _dot_nt(a, b):
    return lax.dot_general(a, b, (((1,), (1,)), ((), ())), preferred_element_type=F32)


def _dot_tn(a, b):
    return lax.dot_general(a, b, (((0,), (0,)), ((), ())), preferred_element_type=F32)


def _sigmoid(x):
    return 1.0 / (1.0 + jnp.exp(-x))


def _silu(x):
    return x * _sigmoid(x)


def _softplus(x):
    return jnp.maximum(x, 0.0) + jnp.log1p(jnp.exp(-jnp.abs(x)))


def _rms(x, g=None):
    y = x * lax.rsqrt(jnp.mean(x * x, axis=-1, keepdims=True) + EPS)
    return y if g is None else y * g


def _split3(x):
    hi = x.astype(BF16)
    r = x - hi.astype(F32)
    mid = r.astype(BF16)
    lo = (r - mid.astype(F32)).astype(BF16)
    return hi, mid, lo


def _tri_cumsum(tri, x):
    hi, mid, lo = _split3(x)
    return _dot(tri, hi) + _dot(tri, mid) + _dot(tri, lo)


def _norm_kernel(x_ref, g_ref, o_ref):
    o_ref[...] = _rms(x_ref[...], g_ref[...]).astype(o_ref.dtype)


def _norm(x, g, tm):
    m, d = x.shape
    return pl.pallas_call(
        _norm_kernel,
        grid=(m // tm,),
        in_specs=[pl.BlockSpec((tm, d), lambda i: (i, 0)),
                  pl.BlockSpec((1, d), lambda i: (0, 0))],
        out_specs=pl.BlockSpec((tm, d), lambda i: (i, 0)),
        out_shape=jax.ShapeDtypeStruct((m, d), BF16),
        compiler_params=_cp("parallel"),
        name="rmsnorm",
    )(x, g.reshape(1, d))


def _mm_kernel(a_ref, w_ref, o_ref):
    o_ref[...] = _dot(a_ref[...], w_ref[...]).astype(o_ref.dtype)


def _mm(a, w, out_dtype, tm, tn, name):
    m, k = a.shape
    n = w.shape[1]
    return pl.pallas_call(
        _mm_kernel,
        grid=(m // tm, n // tn),
        in_specs=[pl.BlockSpec((tm, k), lambda i, j: (i, 0)),
                  pl.BlockSpec((k, tn), lambda i, j: (0, j))],
        out_specs=pl.BlockSpec((tm, tn), lambda i, j: (i, j)),
        out_shape=jax.ShapeDtypeStruct((m, n), out_dtype),
        compiler_params=_cp("parallel", "arbitrary"),
        name=name,
    )(a, w)


def _proj_res_kernel(a_ref, w_ref, x_ref, g_ref, *rest, nk, final):
    if final:
        y_ref, acc_ref = rest
    else:
        xo_ref, h_ref, acc_ref = rest
    k = pl.program_id(1)
    part = _dot(a_ref[...], w_ref[...])

    def finish(total):
        xn = x_ref[...] + total
        if final:
            y_ref[...] = _rms(xn, g_ref[...])
        else:
            xo_ref[...] = xn
            h_ref[...] = _rms(xn, g_ref[...]).astype(h_ref.dtype)

    if nk == 1:
        finish(part)
    else:
        @pl.when(k == 0)
        def _():
            acc_ref[...] = part

        @pl.when(jnp.logical_and(k > 0, k < nk - 1))
        def _():
            acc_ref[...] += part

        @pl.when(k == nk - 1)
        def _():
            finish(acc_ref[...] + part)


def _proj_res(a, w, x, g, tm, tk, final, name):
    m, kk = a.shape
    d = w.shape[1]
    nk = kk // tk
    row = lambda i, k: (i, 0)
    out_specs = [pl.BlockSpec((tm, d), row)]
    out_shape = [jax.ShapeDtypeStruct((m, d), F32)]
    if not final:
        out_specs.append(pl.BlockSpec((tm, d), row))
        out_shape.append(jax.ShapeDtypeStruct((m, d), BF16))
    return pl.pallas_call(
        functools.partial(_proj_res_kernel, nk=nk, final=final),
        grid=(m // tm, nk),
        in_specs=[pl.BlockSpec((tm, tk), lambda i, k: (i, k)),
                  pl.BlockSpec((tk, d), lambda i, k: (k, 0)),
                  pl.BlockSpec((tm, d), row),
                  pl.BlockSpec((1, d), lambda i, k: (0, 0))],
        out_specs=out_specs,
        out_shape=out_shape,
        scratch_shapes=[pltpu.VMEM((tm, d) if nk > 1 else (SUBLANES, LANES), F32)],
        compiler_params=_cp("parallel", "arbitrary"),
        name=name,
    )(a, w, x, g.reshape(1, d))


def _merge_kernel(y0_ref, y1_ref, y2_ref, w_ref, g0_ref, g1_ref, g2_ref, b_ref, o_ref):
    acc = None
    for n, (y_ref, g_ref) in enumerate(((y0_ref, g0_ref), (y1_ref, g1_ref), (y2_ref, g2_ref))):
        p = _dot(y_ref[...], w_ref[n])
        t = _sigmoid(g_ref[...] + b_ref[n:n + 1, :]) * p
        acc = t if acc is None else acc + t
    o_ref[...] = acc.astype(o_ref.dtype)


def _merge(ys, w_branch, g_br, b_gate, tm, tn):
    m, bw = ys[0].shape
    d = w_branch.shape[2]
    nj = d // tn
    y_spec = pl.BlockSpec((tm, bw), lambda i, j: (i, 0))

    def g_spec(n):
        return pl.BlockSpec((tm, tn), lambda i, j: (i, n * nj + j))

    return pl.pallas_call(
        _merge_kernel,
        grid=(m // tm, nj),
        in_specs=[y_spec, y_spec, y_spec,
                  pl.BlockSpec((N_BRANCH, bw, tn), lambda i, j: (0, 0, j)),
                  g_spec(0), g_spec(1), g_spec(2),
                  pl.BlockSpec((N_BRANCH, tn), lambda i, j: (0, j))],
        out_specs=pl.BlockSpec((tm, tn), lambda i, j: (i, j)),
        out_shape=jax.ShapeDtypeStruct((m, d), BF16),
        compiler_params=_cp("parallel", "arbitrary"),
        name="branch_merge",
    )(ys[0], ys[1], ys[2], w_branch, g_br, g_br, g_br, b_gate)


def _xattn_kernel(q_ref, k_ref, v_ref, o_ref, *, scale):
    q = q_ref[...]
    k = k_ref[0].astype(BF16)
    v = v_ref[0].astype(BF16)
    s = _dot_nt(q, k) * scale
    s = s - jnp.max(s, axis=-1, keepdims=True)
    e = jnp.exp(s)
    p = e / jnp.sum(e, axis=-1, keepdims=True)
    o_ref[...] = _dot(p.astype(BF16), v).astype(o_ref.dtype)


def _xattn(q, mem_k, mem_v, t, tq):
    m, d = q.shape
    b = m // t
    nmem = mem_k.shape[1]
    dh = d // X_H
    nq = t // tq
    kv_spec = pl.BlockSpec((1, nmem, dh), lambda bi, i, h: (bi, 0, h))
    qo_spec = pl.BlockSpec((tq, dh), lambda bi, i, h: (bi * nq + i, h))
    return pl.pallas_call(
        functools.partial(_xattn_kernel, scale=dh ** -0.5),
        grid=(b, nq, X_H),
        in_specs=[qo_spec, kv_spec, kv_spec],
        out_specs=qo_spec,
        out_shape=jax.ShapeDtypeStruct((m, d), BF16),
        compiler_params=_cp("parallel", "parallel", "arbitrary"),
        name="xattn",
    )(q, mem_k, mem_v)


def _gelu_tanh(x):
    c = 0.7978845608028654
    return x * (0.5 * (1.0 + jnp.tanh(c * (x + 0.044715 * (x * x * x)))))


def _shift_rows(x, prev8, s):
    xs = pltpu.roll(x, s, 0)
    ps = pltpu.roll(prev8, s, 0)
    r8 = lax.broadcasted_iota(jnp.int32, prev8.shape, 0)
    top = jnp.where(r8 < s, ps, xs[:SUBLANES])
    if x.shape[0] == SUBLANES:
        return top
    return jnp.concatenate([top, xs[SUBLANES:]], axis=0)


def _ffn_in_kernel(h_ref, wg_ref, wu_ref, cw_ref, cb_ref, buf_ref, act_ref, nbuf_ref, carry_ref,
                   *, tiles_per_seq, swap):
    i = pl.program_id(1 if swap else 0)
    j = pl.program_id(0 if swap else 1)
    h = h_ref[...]
    gate = _dot(h, wg_ref[...])
    up = _dot(h, wu_ref[...])
    tm = gate.shape[0]

    @pl.when(i % tiles_per_seq == 0)
    def _():
        carry_ref[j] = buf_ref[0]

    prev = carry_ref[j]
    g1 = _shift_rows(gate, prev, 1)
    g2 = _shift_rows(gate, prev, 2)
    y = g2 * cw_ref[0:1, :]
    y = y + g1 * cw_ref[1:2, :]
    y = y + gate * cw_ref[2:3, :]
    act_ref[...] = (_gelu_tanh(y + cb_ref[...]) * up).astype(act_ref.dtype)
    last = gate[tm - SUBLANES:, :]
    carry_ref[j] = last
    nbuf_ref[0] = last


def _ffn_in(h, wg, wu, conv_w, conv_b, buf8, t, tm, tn):
    m, d = h.shape
    dff = wg.shape[1]
    ni, nj = m // tm, dff // tn
    tps = t // tm
    swap = tps == 1

    def ix(f):
        return (lambda j, i: f(i, j)) if swap else f

    act, tails = pl.pallas_call(
        functools.partial(_ffn_in_kernel, tiles_per_seq=tps, swap=swap),
        grid=(nj, ni) if swap else (ni, nj),
        in_specs=[pl.BlockSpec((tm, d), ix(lambda i, j: (i, 0))),
                  pl.BlockSpec((d, tn), ix(lambda i, j: (0, j))),
                  pl.BlockSpec((d, tn), ix(lambda i, j: (0, j))),
                  pl.BlockSpec((FFN_CONV, tn), ix(lambda i, j: (0, j))),
                  pl.BlockSpec((1, tn), ix(lambda i, j: (0, j))),
                  pl.BlockSpec((1, SUBLANES, tn), ix(lambda i, j: (i // tps, 0, j)))],
        out_specs=[pl.BlockSpec((tm, tn), ix(lambda i, j: (i, j))),
                   pl.BlockSpec((1, SUBLANES, tn), ix(lambda i, j: (i, 0, j)))],
        out_shape=[jax.ShapeDtypeStruct((m, dff), BF16),
                   jax.ShapeDtypeStruct((ni, SUBLANES, dff), F32)],
        scratch_shapes=[pltpu.VMEM((nj, SUBLANES, tn), F32)],
        compiler_params=_cp("arbitrary", "arbitrary"),
        name="ffn_in",
    )(h, wg, wu, conv_w, conv_b.reshape(1, dff), buf8)
    return act, tails[tps - 1::tps]


def _ret_kernel(ld_ref, q_ref, k_ref, v_ref, g_ref, cos_ref, sin_ref, s0_ref, y_ref, s_ref, *, c):
    hd = pl.program_id(1)
    t = pl.program_id(2)
    tb = q_ref.shape[0]
    half = RET_DK // 2
    ld = ld_ref[hd]

    @pl.when(t == 0)
    def _():
        s_ref[...] = s0_ref[...]

    ri = lax.broadcasted_iota(jnp.int32, (c, c), 0)
    ci = lax.broadcasted_iota(jnp.int32, (c, c), 1)
    diff = (ri - ci).astype(F32)
    causal = ri >= ci
    dmask = jnp.where(causal, jnp.exp(ld * jnp.where(causal, diff, 0.0)), 0.0)
    idx = lax.broadcasted_iota(jnp.int32, (c, RET_DK), 0).astype(F32)
    q_dec = jnp.exp(ld * (idx + 1.0))
    k_dec = jnp.exp(ld * (c - 1.0 - idx))
    s_dec = jnp.exp(jnp.full((1, RET_DV), ld * c, F32))

    def rot(x, cos, sin):
        x1, x2 = x[:, :half], x[:, half:]
        return jnp.concatenate([x1 * cos - x2 * sin, x1 * sin + x2 * cos], axis=-1)

    def body(j, carry):
        rows = pl.ds(pl.multiple_of(j * c, c), c)
        cos, sin = cos_ref[rows, :], sin_ref[rows, :]
        q = rot(q_ref[rows, :], cos, sin)
        k = rot(k_ref[rows, :], cos, sin) * (RET_DK ** -0.5)
        vb = v_ref[rows, :].astype(BF16)
        qb = q.astype(BF16)
        att = _dot_nt(qb, k.astype(BF16)) * dmask
        s = s_ref[0, 0]
        o = _dot(att.astype(BF16), vb) + _dot(qb, s.astype(BF16)) * q_dec
        s_ref[0, 0] = s_dec * s + _dot_tn((k * k_dec).astype(BF16), vb)
        y = _rms(o) * _silu(g_ref[rows, :])
        y_ref[rows, :] = y.astype(y_ref.dtype)
        return carry

    lax.fori_loop(0, tb // c, body, 0)


def _retention(p_ret, cos, sin, ld, s0, t, tb):
    m = p_ret.shape[0]
    b = m // t
    c = min(CHUNK, t)
    nt = t // tb

    def col(off):
        return pl.BlockSpec((tb, RET_DK), lambda bi, h, ti: (bi * nt + ti, off + h))

    st_spec = pl.BlockSpec((1, 1, RET_DK, RET_DV), lambda bi, h, ti: (bi, h, 0, 0))
    rope_spec = pl.BlockSpec((tb, RET_DK // 2), lambda bi, h, ti: (ti, 0))
    return pl.pallas_call(
        functools.partial(_ret_kernel, c=c),
        grid=(b, RET_H, nt),
        in_specs=[pl.BlockSpec(memory_space=pltpu.SMEM),
                  col(0), col(RET_H), col(2 * RET_H), col(3 * RET_H),
                  rope_spec, rope_spec, st_spec],
        out_specs=[pl.BlockSpec((tb, RET_DV), lambda bi, h, ti: (bi * nt + ti, h)), st_spec],
        out_shape=[jax.ShapeDtypeStruct((m, RET_H * RET_DV), BF16),
                   jax.ShapeDtypeStruct((b, RET_H, RET_DK, RET_DV), F32)],
        compiler_params=_cp("parallel", "parallel", "arbitrary"),
        name="retention",
    )(ld, p_ret, p_ret, p_ret, p_ret, cos, sin, s0)


def _gla_kernel(q_ref, k_ref, v_ref, g_ref, lr_ref, wgk_ref, bgk_ref, nrm_ref, s0_ref, y_ref, s_ref,
                st_ref, *, c, nt):
    t = pl.program_id(2)
    tb = q_ref.shape[0]

    @pl.when(t == 0)
    def _():
        st_ref[...] = s0_ref[0, 0].T

    ri = lax.broadcasted_iota(jnp.int32, (c, c), 0)
    ci = lax.broadcasted_iota(jnp.int32, (c, c), 1)
    causal = ri >= ci
    tri = jnp.where(causal, 1.0, 0.0).astype(BF16)
    wgk = wgk_ref[...]
    bgk = bgk_ref[...]
    nrm = nrm_ref[...]

    def body(j, carry):
        rows = pl.ds(pl.multiple_of(j * c, c), c)
        z = _dot(lr_ref[rows, :].astype(BF16), wgk) + bgk
        gk = -_softplus(-z) / GLA_NORMALIZER
        bc = _tri_cumsum(tri, gk)
        q = q_ref[rows, :] * (GLA_DK ** -0.5)
        k = k_ref[rows, :]
        vb = v_ref[rows, :].astype(BF16)
        qe = (q * jnp.exp(bc)).astype(BF16)
        ke = (k * jnp.exp(-bc)).astype(BF16)
        att = jnp.where(causal, _dot_nt(qe, ke), 0.0)
        st = st_ref[...]
        o = _dot(att.astype(BF16), vb) + _dot_nt(qe, st.astype(BF16))
        b_last = bc[c - 1:c, :]
        kd = (k * jnp.exp(b_last - bc)).astype(BF16)
        st_ref[...] = jnp.exp(b_last) * st + _dot_tn(vb, kd)
        y = _rms(o, nrm) * _silu(g_ref[rows, :])
        y_ref[rows, :] = y.astype(y_ref.dtype)
        return carry

    lax.fori_loop(0, tb // c, body, 0)

    @pl.when(t == nt - 1)
    def _():
        s_ref[0, 0] = st_ref[...].T


def _gla(p_gla, p_small, wgk, bgk, nrm, s0, t, tb):
    m = p_gla.shape[0]
    b = m // t
    c = min(CHUNK, t)
    nt = t // tb
    row = lambda bi, h, ti: bi * nt + ti
    nq = GLA_H
    nv = GLA_H * GLA_DK * 2 // GLA_DV
    st_spec = pl.BlockSpec((1, 1, GLA_DK, GLA_DV), lambda bi, h, ti: (bi, h, 0, 0))
    return pl.pallas_call(
        functools.partial(_gla_kernel, c=c, nt=nt),
        grid=(b, GLA_H, nt),
        in_specs=[pl.BlockSpec((tb, GLA_DK), lambda bi, h, ti: (row(bi, h, ti), h)),
                  pl.BlockSpec((tb, GLA_DK), lambda bi, h, ti: (row(bi, h, ti), nq + h)),
                  pl.BlockSpec((tb, GLA_DV), lambda bi, h, ti: (row(bi, h, ti), nv + h)),
                  pl.BlockSpec((tb, GLA_DV), lambda bi, h, ti: (row(bi, h, ti), nv + GLA_H + h)),
                  pl.BlockSpec((tb, LANES), lambda bi, h, ti: (row(bi, h, ti), 0)),
                  pl.BlockSpec((LANES, GLA_DK), lambda bi, h, ti: (0, h)),
                  pl.BlockSpec((1, GLA_DK), lambda bi, h, ti: (0, h)),
                  pl.BlockSpec((1, GLA_DV), lambda bi, h, ti: (0, 0)),
                  st_spec],
        out_specs=[pl.BlockSpec((tb, GLA_DV), lambda bi, h, ti: (row(bi, h, ti), h)), st_spec],
        out_shape=[jax.ShapeDtypeStruct((m, GLA_H * GLA_DV), BF16),
                   jax.ShapeDtypeStruct((b, GLA_H, GLA_DK, GLA_DV), F32)],
        scratch_shapes=[pltpu.VMEM((GLA_DV, GLA_DK), F32)],
        compiler_params=_cp("parallel", "parallel", "arbitrary"),
        name="gla",
    )(p_gla, p_gla, p_gla, p_gla, p_small, wgk, bgk, nrm, s0)


def _unit_lower_inverse_minus_eye(a, c):
    ri = lax.broadcasted_iota(jnp.int32, (c, c), 0)
    ci = lax.broadcasted_iota(jnp.int32, (c, c), 1)

    def level_mask(k):
        same = (ri ^ ci) < 2 * k
        return same & ((ri & k) != 0) & ((ci & k) == 0)

    n = -jnp.where(level_mask(1), a, 0.0)
    k = 2
    while k < c:
        l = jnp.where(level_mask(k), a, 0.0)
        nb = n.astype(BF16)
        y = l + _dot(nb, l.astype(BF16))
        x = y + _dot(y.astype(BF16), nb)
        n = n - x
        k *= 2
    return n


def _gdn_kernel(p_ref, sm_ref, cw_ref, buf_ref, alog_ref, dtb_ref, nrm_ref, s0_ref,
                y_ref, s_ref, nbuf_ref, carry_ref, *, c, nt):
    t = pl.program_id(1)
    tb = p_ref.shape[0]
    hq = GDN_H * GDN_DK
    hv = GDN_H * GDN_DV
    nconv = 2 * hq + hv

    @pl.when(t == 0)
    def _():
        s_ref[...] = s0_ref[...]
        carry_ref[...] = buf_ref[0]

    ri = lax.broadcasted_iota(jnp.int32, (c, c), 0)
    ci = lax.broadcasted_iota(jnp.int32, (c, c), 1)
    incl = ri >= ci
    strict = ri > ci
    tri = jnp.where(incl, 1.0, 0.0).astype(BF16)
    lane = lax.broadcasted_iota(jnp.int32, (1, LANES), 1)
    a_lanes = jnp.logical_and(lane >= SMALL_A0, lane < SMALL_A0 + GDN_H)
    sel = jnp.where(lax.broadcasted_iota(jnp.int32, (SUBLANES, LANES), 1)
                    == lax.broadcasted_iota(jnp.int32, (SUBLANES, LANES), 0) + SMALL_A0,
                    1.0, 0.0).astype(BF16)
    neg_a = -jnp.exp(alog_ref[...])
    dtb = dtb_ref[...]
    nrm = nrm_ref[...]
    w0, w1, w2, w3 = (cw_ref[i:i + 1, :] for i in range(GDN_CONV))

    def col(x, lane_idx):
        return x[:, lane_idx:lane_idx + 1]

    def body(j, carry):
        rows = pl.ds(pl.multiple_of(j * c, c), c)
        x = p_ref[rows, 0:nconv]
        prev = carry_ref[...]
        conv = _shift_rows(x, prev, 3) * w0
        conv = conv + _shift_rows(x, prev, 2) * w1
        conv = conv + _shift_rows(x, prev, 1) * w2
        conv = conv + x * w3
        carry_ref[...] = x[c - SUBLANES:, :]
        qkv = _silu(conv)

        sm = sm_ref[rows, :]
        g_all = jnp.where(a_lanes, neg_a * _softplus(sm + dtb), 0.0)
        gam = _tri_cumsum(tri, g_all)
        ghi, gmid, glo = _split3(gam)
        gam_rows = _dot_nt(sel, ghi) + _dot_nt(sel, gmid) + _dot_nt(sel, glo)
        beta_all = _sigmoid(sm)
        eg_all = jnp.exp(gam)
        g_last_all = gam[c - 1:c, :]
        egl_all = jnp.exp(g_last_all - gam)
        eg_last_all = jnp.exp(g_last_all)

        for h in range(GDN_H):
            q = qkv[:, h * GDN_DK:(h + 1) * GDN_DK]
            k = qkv[:, hq + h * GDN_DK:hq + (h + 1) * GDN_DK]
            v = qkv[:, 2 * hq + h * GDN_DV:2 * hq + (h + 1) * GDN_DV]
            q = q * lax.rsqrt(jnp.sum(q * q, axis=-1, keepdims=True) + EPS) * (GDN_DK ** -0.5)
            k = k * lax.rsqrt(jnp.sum(k * k, axis=-1, keepdims=True) + EPS)
            gam_c = col(gam, SMALL_A0 + h)
            gam_r = gam_rows[h:h + 1, :]
            beta = col(beta_all, SMALL_B0 + h)
            eg = col(eg_all, SMALL_A0 + h)
            egl = col(egl_all, SMALL_A0 + h)
            eg_last = col(eg_last_all, SMALL_A0 + h)

            decay = jnp.where(incl, jnp.exp(jnp.where(incl, gam_c - gam_r, 0.0)), 0.0)
            qb, kb = q.astype(BF16), k.astype(BF16)
            kk = _dot_nt(kb, kb)
            a = jnp.where(strict, beta * kk * decay, 0.0)
            n = _unit_lower_inverse_minus_eye(a, c)
            rhs = jnp.concatenate([beta * v, (beta * eg) * k], axis=-1)
            sol = rhs + _dot(n.astype(BF16), rhs.astype(BF16))
            u, wk = sol[:, :GDN_DV], sol[:, GDN_DV:]
            s = s_ref[0, h]
            sb = s.astype(BF16)
            w = u - _dot(wk.astype(BF16), sb)
            wb = w.astype(BF16)
            att = _dot_nt(qb, kb) * decay
            o = eg * _dot(qb, sb) + _dot(att.astype(BF16), wb)
            s_ref[0, h] = eg_last * s + _dot_tn((k * egl).astype(BF16), wb)
            z = p_ref[rows, nconv + h * GDN_DV:nconv + (h + 1) * GDN_DV]
            y = _rms(o, nrm) * _silu(z)
            y_ref[rows, h * GDN_DV:(h + 1) * GDN_DV] = y.astype(y_ref.dtype)
        return carry

    lax.fori_loop(0, tb // c, body, 0)

    @pl.when(t == nt - 1)
    def _():
        nbuf_ref[0] = carry_ref[...]


def _gdn(p_gdn, p_small, conv_w, buf8, alog_row, dtb_row, nrm, s0, t, tb):
    m, width = p_gdn.shape
    b = m // t
    c = min(CHUNK, t)
    nt = t // tb
    nconv = conv_w.shape[1]
    st_spec = pl.BlockSpec((1, GDN_H, GDN_DK, GDN_DV), lambda bi, ti: (bi, 0, 0, 0))
    buf_spec = pl.BlockSpec((1, SUBLANES, nconv), lambda bi, ti: (bi, 0, 0))
    const = lambda bi, ti: (0, 0)
    return pl.pallas_call(
        functools.partial(_gdn_kernel, c=c, nt=nt),
        grid=(b, nt),
        in_specs=[pl.BlockSpec((tb, width), lambda bi, ti: (bi * nt + ti, 0)),
                  pl.BlockSpec((tb, LANES), lambda bi, ti: (bi * nt + ti, 0)),
                  pl.BlockSpec((GDN_CONV, nconv), const),
                  buf_spec,
                  pl.BlockSpec((1, LANES), const),
                  pl.BlockSpec((1, LANES), const),
                  pl.BlockSpec((1, GDN_DV), const),
                  st_spec],
        out_specs=[pl.BlockSpec((tb, GDN_H * GDN_DV), lambda bi, ti: (bi * nt + ti, 0)),
                   st_spec, buf_spec],
        out_shape=[jax.ShapeDtypeStruct((m, GDN_H * GDN_DV), BF16),
                   jax.ShapeDtypeStruct((b, GDN_H, GDN_DK, GDN_DV), F32),
                   jax.ShapeDtypeStruct((b, SUBLANES, nconv), F32)],
        scratch_shapes=[pltpu.VMEM((SUBLANES, nconv), F32)],
        compiler_params=_cp("parallel", "arbitrary"),
        name="gated_deltanet",
    )(p_gdn, p_small, conv_w, buf8, alog_row, dtb_row, nrm, s0)


def _pad_rows_front(buf, rows):
    return jnp.pad(buf, ((0, 0), (0, 0), (rows - buf.shape[2], 0), (0, 0)))


def _prep_weights(w):
    depth, d, _ = w['w_in'].shape
    o = 0
    sizes = dict(ret=4 * RET_H * RET_DK, gla=2 * GLA_H * GLA_DK + 2 * GLA_H * GLA_DV, lr=GLA_RANK,
                 gdn=2 * GDN_H * GDN_DK + 2 * GDN_H * GDN_DV, ab=2 * GDN_H, gbr=N_BRANCH * d)
    sl = {}
    for name in ('ret', 'gla', 'lr', 'gdn', 'ab', 'gbr'):
        sl[name] = (o, o + sizes[name])
        o += sizes[name]
    w_in = w['w_in']
    cut = lambda name: w_in[:, :, sl[name][0]:sl[name][1]].astype(BF16)
    small = jnp.concatenate([w_in[:, :, sl['lr'][0]:sl['lr'][1]], w_in[:, :, sl['ab'][0]:sl['ab'][1]]], axis=-1)
    small = jnp.pad(small, ((0, 0), (0, 0), (0, LANES - small.shape[-1]))).astype(BF16)
    dff = w['w_ffn_out'].shape[1]
    lane_row = lambda v, at: jnp.pad(v, ((0, 0), (at, LANES - at - v.shape[1])))[:, None, :]
    return dict(
        w_ret=cut('ret'), w_gla=cut('gla'), w_gdn=cut('gdn'), w_gbr=cut('gbr'), w_small=small,
        w_gk=jnp.pad(w['w_gla_gk'], ((0, 0), (0, LANES - GLA_RANK), (0, 0))).astype(BF16),
        b_gk=w['b_gla_gk'][:, None, :],
        gla_norm=w['gla_norm'][:, None, :], gdn_norm=w['gdn_norm'][:, None, :],
        alog_row=lane_row(w['gdn_a_log'], SMALL_A0), dtb_row=lane_row(w['gdn_dt_bias'], SMALL_A0),
        gdn_conv_w=w['gdn_conv_w'], b_gate=w['b_gate'],
        w_branch=w['w_branch'].astype(BF16), w_out=w['w_out'].astype(BF16),
        w_xq=w['w_xq'].astype(BF16), w_xo=w['w_xo'].astype(BF16),
        w_ffn_g=w['w_ffn_in'][:, :, :dff].astype(BF16), w_ffn_u=w['w_ffn_in'][:, :, dff:].astype(BF16),
        ffn_conv_w=w['ffn_conv_w'], ffn_conv_b=w['ffn_conv_b'], w_ffn_out=w['w_ffn_out'].astype(BF16),
        ln_mix=w['ln_mix'], ln_xattn=w['ln_xattn'], ln_ffn=w['ln_ffn'], ln_final=w['ln_final'],
    )


def _tile(n, cap):
    t = min(n, cap)
    while n % t:
        t -= 1
    return t


def _trunk(x, offset, mem_k, mem_v, s_ret, s_gla, s_gdn, buf_gdn, buf_ffn, pw):
    b, t, d = x.shape
    m = b * t
    depth = s_ret.shape[0]
    x2 = x.reshape(m, d)
    tm = _tile(m, MM_TM)
    tm_res = _tile(m, RES_TM)
    tm_res_k = _tile(m, RES_TM_K)
    tb = _tile(t, SEQ_TB)
    tb_gdn = _tile(t, SEQ_TB_GDN)
    tq = _tile(t, XATTN_TQ)
    tm_ffn = _tile(t, FFN_TM)

    pos = offset + jnp.arange(t, dtype=F32)
    half = RET_DK // 2
    freqs = 1.0 / (ROPE_BASE ** (jnp.arange(half, dtype=F32) / half))
    ang = pos[:, None] * freqs[None, :]
    cos, sin = jnp.cos(ang), jnp.sin(ang)
    ld = jnp.log1p(-jnp.exp2(-5.0 - jnp.arange(RET_H, dtype=F32)))

    buf_gdn8 = _pad_rows_front(buf_gdn, SUBLANES)
    buf_ffn8 = _pad_rows_front(buf_ffn, SUBLANES)
    memk = mem_k.reshape(depth, b, mem_k.shape[2], d)
    memv = mem_v.reshape(depth, b, mem_v.shape[2], d)

    outs = ([], [], [], [], [])
    hn = _norm(x2, pw['ln_mix'][0], tm_res_k)
    y = None
    for l in range(depth):
        p_ret = _mm(hn, pw['w_ret'][l], F32, tm, MM_TN, "in_proj_ret")
        p_gla = _mm(hn, pw['w_gla'][l], F32, tm, MM_TN, "in_proj_gla")
        p_gdn = _mm(hn, pw['w_gdn'][l], F32, tm, MM_TN, "in_proj_gdn")
        g_br = _mm(hn, pw['w_gbr'][l], F32, tm, MM_TN, "in_proj_gates")
        p_small = _mm(hn, pw['w_small'][l], F32, tm, LANES, "in_proj_small")

        y_ret, sr = _retention(p_ret, cos, sin, ld, s_ret[l], t, tb)
        y_gla, sg = _gla(p_gla, p_small, pw['w_gk'][l], pw['b_gk'][l], pw['gla_norm'][l], s_gla[l], t, tb)
        y_gdn, sd, bg8 = _gdn(p_gdn, p_small, pw['gdn_conv_w'][l], buf_gdn8[l], pw['alog_row'][l],
                              pw['dtb_row'][l], pw['gdn_norm'][l], s_gdn[l], t, tb_gdn)

        merged = _merge((y_ret, y_gla, y_gdn), pw['w_branch'][l], g_br, pw['b_gate'][l], tm_res_k, MM_TN)
        x2, hx = _proj_res(merged, pw['w_out'][l], x2, pw['ln_xattn'][l], tm_res, d, False, "out_proj")

        q = _mm(hx, pw['w_xq'][l], BF16, tm, MM_TN, "xattn_q")
        o = _xattn(q, memk[l], memv[l], t, tq)
        x2, hf = _proj_res(o, pw['w_xo'][l], x2, pw['ln_ffn'][l], tm_res, d, False, "xattn_out")

        act, bf8 = _ffn_in(hf, pw['w_ffn_g'][l], pw['w_ffn_u'][l], pw['ffn_conv_w'][l], pw['ffn_conv_b'][l],
                           buf_ffn8[l], t, tm_ffn, FFN_TN)
        dff = act.shape[1]
        tk = _tile(dff, RES_TK)
        tmk = tm_res if tk == dff else tm_res_k
        if l + 1 < depth:
            x2, hn = _proj_res(act, pw['w_ffn_out'][l], x2, pw['ln_mix'][l + 1], tmk, tk, False, "ffn_out")
        else:
            y, = _proj_res(act, pw['w_ffn_out'][l], x2, pw['ln_final'], tmk, tk, True, "ffn_out_final")

        for lst, val in zip(outs, (sr, sg, sd, bg8[:, SUBLANES - (GDN_CONV - 1):], bf8[:, SUBLANES - (FFN_CONV - 1):])):
            lst.append(val)
    return (y.reshape(b, t, d),) + tuple(jnp.stack(lst) for lst in outs)


def kernel(x_prompt, x_sample, mem_prompt, state_ret, state_gla, state_gdn, state_gdn_conv, state_ffn_conv, cache_mem_k, cache_mem_v, ln_mix, w_in, w_gla_gk, b_gla_gk, gla_norm, gdn_conv_w, gdn_a_log, gdn_dt_bias, gdn_norm, b_gate, w_branch, w_out, ln_xattn, ln_mem, w_xq, w_xkv, w_xo, ln_ffn, w_ffn_in, ffn_conv_w, ffn_conv_b, w_ffn_out, ln_final):
    pw = _prep_weights(dict(
        w_in=w_in, w_gla_gk=w_gla_gk, b_gla_gk=b_gla_gk, gla_norm=gla_norm, gdn_conv_w=gdn_conv_w,
        gdn_a_log=gdn_a_log, gdn_dt_bias=gdn_dt_bias, gdn_norm=gdn_norm, b_gate=b_gate, w_branch=w_branch,
        w_out=w_out, w_xq=w_xq, w_xo=w_xo, w_ffn_in=w_ffn_in, ffn_conv_w=ffn_conv_w, ffn_conv_b=ffn_conv_b,
        w_ffn_out=w_ffn_out, ln_mix=ln_mix, ln_xattn=ln_xattn, ln_ffn=ln_ffn, ln_final=ln_final))
    depth = w_in.shape[0]
    bp, nmem, d = mem_prompt.shape
    dt = x_prompt.dtype

    mem2 = mem_prompt.reshape(bp * nmem, d)
    w_xkv_b = w_xkv.astype(BF16)
    tmem = _tile(bp * nmem, MM_TM)
    mk, mv = [], []
    for l in range(depth):
        hm = _norm(mem2, ln_mem[l], _tile(bp * nmem, RES_TM_K))
        mk.append(_mm(hm, w_xkv_b[l, :, :d], F32, tmem, MM_TN, "mem_k"))
        mv.append(_mm(hm, w_xkv_b[l, :, d:], F32, tmem, MM_TN, "mem_v"))
    dh = d // X_H
    mem_k_p = jnp.stack(mk).reshape(depth, bp, nmem, X_H, dh)
    mem_v_p = jnp.stack(mv).reshape(depth, bp, nmem, X_H, dh)

    zeros = lambda *s: jnp.zeros((depth, bp) + s, dt)
    y_p, ret_p, gla_p, gdn_p, gdn_conv_p, ffn_conv_p = _trunk(
        x_prompt, 0.0, mem_k_p, mem_v_p,
        zeros(RET_H, RET_DK, RET_DV), zeros(GLA_H, GLA_DK, GLA_DV), zeros(GDN_H, GDN_DK, GDN_DV),
        zeros(GDN_CONV - 1, state_gdn_conv.shape[-1]), zeros(FFN_CONV - 1, state_ffn_conv.shape[-1]), pw)

    past_len = 4096.0
    y_s, ret_s, gla_s, gdn_s, gdn_conv_s, ffn_conv_s = _trunk(
        x_sample, past_len, cache_mem_k, cache_mem_v,
        state_ret, state_gla, state_gdn, state_gdn_conv, state_ffn_conv, pw)

    return (y_p, y_s, ret_p, gla_p, gdn_p, gdn_conv_p, ffn_conv_p, mem_k_p, mem_v_p,
            ret_s, gla_s, gdn_s, gdn_conv_s, ffn_conv_s)
```

```python
import functools

import jax
import jax.numpy as jnp
from jax import lax
from jax.experimental import pallas as pl
from jax.experimental.pallas import tpu as pltpu

F32 = jnp.float32
BF16 = jnp.bfloat16

EPS = 1e-6
CHUNK = 64
RET_H, RET_DK, RET_DV = 4, 256, 256
GLA_H, GLA_DK, GLA_DV = 4, 128, 256
GLA_RANK = 16
GLA_NORMALIZER = 16.0
GDN_H, GDN_DK, GDN_DV = 8, 128, 128
GDN_CONV = 4
N_BRANCH = 3
X_H = 4
FFN_CONV = 3
ROPE_BASE = 10000.0

LANES = 128
SUBLANES = 8
SMALL_A0 = GLA_RANK
SMALL_B0 = GLA_RANK + GDN_H
VMEM_LIMIT = 52 * 1024 * 1024
MM_TM = 1024
MM_TN = 1024
RES_TM = 256
RES_TM_K = 512
RES_TK = 1536
SEQ_TB = 512
SEQ_TB_GDN = 256
XATTN_TQ = 512
FFN_TM = 1024
FFN_TN = 512


def _cp(*sem):
    return pltpu.CompilerParams(dimension_semantics=sem, vmem_limit_bytes=VMEM_LIMIT)


def _dot(a, b):
    return jnp.dot(a, b, preferred_element_type=F32)


def _dot_nt(a, b):
    return lax.dot_general(a, b, (((1,), (1,)), ((), ())), preferred_element_type=F32)


def _dot_tn(a, b):
    return lax.dot_general(a, b, (((0,), (0,)), ((), ())), preferred_element_type=F32)


def _sigmoid(x):
    return 1.0 / (1.0 + jnp.exp(-x))


def _silu(x):
    return x * _sigmoid(x)


def _softplus(x):
    return jnp.maximum(x, 0.0) + jnp.log1p(jnp.exp(-jnp.abs(x)))


def _rms(x, g=None):
    y = x * lax.rsqrt(jnp.mean(x * x, axis=-1, keepdims=True) + EPS)
    return y if g is None else y * g


def _split3(x):
    hi = x.astype(BF16)
    r = x - hi.astype(F32)
    mid = r.astype(BF16)
    lo = (r - mid.astype(F32)).astype(BF16)
    return hi, mid, lo


def _tri_cumsum(tri, x):
    hi, mid, lo = _split3(x)
    return _dot(tri, hi) + _dot(tri, mid) + _dot(tri, lo)


def _norm_kernel(x_ref, g_ref, o_ref):
    o_ref[...] = _rms(x_ref[...], g_ref[...]).astype(o_ref.dtype)


def _norm(x, g, tm):
    m, d = x.shape
    return pl.pallas_call(
        _norm_kernel,
        grid=(m // tm,),
        in_specs=[pl.BlockSpec((tm, d), lambda i: (i, 0)),
                  pl.BlockSpec((1, d), lambda i: (0, 0))],
        out_specs=pl.BlockSpec((tm, d), lambda i: (i, 0)),
        out_shape=jax.ShapeDtypeStruct((m, d), BF16),
        compiler_params=_cp("parallel"),
        name="rmsnorm",
    )(x, g.reshape(1, d))


def _mm_kernel(a_ref, w_ref, o_ref):
    o_ref[...] = _dot(a_ref[...], w_ref[...]).astype(o_ref.dtype)


def _mm(a, w, out_dtype, tm, tn, name):
    m, k = a.shape
    n = w.shape[1]
    return pl.pallas_call(
        _mm_kernel,
        grid=(m // tm, n // tn),
        in_specs=[pl.BlockSpec((tm, k), lambda i, j: (i, 0)),
                  pl.BlockSpec((k, tn), lambda i, j: (0, j))],
        out_specs=pl.BlockSpec((tm, tn), lambda i, j: (i, j)),
        out_shape=jax.ShapeDtypeStruct((m, n), out_dtype),
        compiler_params=_cp("parallel", "arbitrary"),
        name=name,
    )(a, w)


def _proj_res_kernel(a_ref, w_ref, x_ref, g_ref, *rest, nk, final):
    if final:
        y_ref, acc_ref = rest
    else:
        xo_ref, h_ref, acc_ref = rest
    k = pl.program_id(1)
    part = _dot(a_ref[...], w_ref[...])

    def finish(total):
        xn = x_ref[...] + total
        if final:
            y_ref[...] = _rms(xn, g_ref[...])
        else:
            xo_ref[...] = xn
            h_ref[...] = _rms(xn, g_ref[...]).astype(h_ref.dtype)

    if nk == 1:
        finish(part)
    else:
        @pl.when(k == 0)
        def _():
            acc_ref[...] = part

        @pl.when(jnp.logical_and(k > 0, k < nk - 1))
        def _():
            acc_ref[...] += part

        @pl.when(k == nk - 1)
        def _():
            finish(acc_ref[...] + part)


def _proj_res(a, w, x, g, tm, tk, final, name):
    m, kk = a.shape
    d = w.shape[1]
    nk = kk // tk
    row = lambda i, k: (i, 0)
    out_specs = [pl.BlockSpec((tm, d), row)]
    out_shape = [jax.ShapeDtypeStruct((m, d), F32)]
    if not final:
        out_specs.append(pl.BlockSpec((tm, d), row))
        out_shape.append(jax.ShapeDtypeStruct((m, d), BF16))
    return pl.pallas_call(
        functools.partial(_proj_res_kernel, nk=nk, final=final),
        grid=(m // tm, nk),
        in_specs=[pl.BlockSpec((tm, tk), lambda i, k: (i, k)),
                  pl.BlockSpec((tk, d), lambda i, k: (k, 0)),
                  pl.BlockSpec((tm, d), row),
                  pl.BlockSpec((1, d), lambda i, k: (0, 0))],
        out_specs=out_specs,
        out_shape=out_shape,
        scratch_shapes=[pltpu.VMEM((tm, d) if nk > 1 else (SUBLANES, LANES), F32)],
        compiler_params=_cp("parallel", "arbitrary"),
        name=name,
    )(a, w, x, g.reshape(1, d))


def _merge_kernel(y0_ref, y1_ref, y2_ref, w_ref, g0_ref, g1_ref, g2_ref, b_ref, o_ref):
    acc = None
    for n, (y_ref, g_ref) in enumerate(((y0_ref, g0_ref), (y1_ref, g1_ref), (y2_ref, g2_ref))):
        p = _dot(y_ref[...], w_ref[n])
        t = _sigmoid(g_ref[...] + b_ref[n:n + 1, :]) * p
        acc = t if acc is None else acc + t
    o_ref[...] = acc.astype(o_ref.dtype)


def _merge(ys, w_branch, g_br, b_gate, tm, tn):
    m, bw = ys[0].shape
    d = w_branch.shape[2]
    nj = d // tn
    y_spec = pl.BlockSpec((tm, bw), lambda i, j: (i, 0))

    def g_spec(n):
        return pl.BlockSpec((tm, tn), lambda i, j: (i, n * nj + j))

    return pl.pallas_call(
        _merge_kernel,
        grid=(m // tm, nj),
        in_specs=[y_spec, y_spec, y_spec,
                  pl.BlockSpec((N_BRANCH, bw, tn), lambda i, j: (0, 0, j)),
                  g_spec(0), g_spec(1), g_spec(2),
                  pl.BlockSpec((N_BRANCH, tn), lambda i, j: (0, j))],
        out_specs=pl.BlockSpec((tm, tn), lambda i, j: (i, j)),
        out_shape=jax.ShapeDtypeStruct((m, d), BF16),
        compiler_params=_cp("parallel", "arbitrary"),
        name="branch_merge",
    )(ys[0], ys[1], ys[2], w_branch, g_br, g_br, g_br, b_gate)


def _xattn_kernel(q_ref, k_ref, v_ref, o_ref, *, scale):
    q = q_ref[...]
    k = k_ref[0].astype(BF16)
    v = v_ref[0].astype(BF16)
    s = _dot_nt(q, k) * scale
    s = s - jnp.max(s, axis=-1, keepdims=True)
    e = jnp.exp(s)
    p = e / jnp.sum(e, axis=-1, keepdims=True)
    o_ref[...] = _dot(p.astype(BF16), v).astype(o_ref.dtype)


def _xattn(q, mem_k, mem_v, t, tq):
    m, d = q.shape
    b = m // t
    nmem = mem_k.shape[1]
    dh = d // X_H
    nq = t // tq
    kv_spec = pl.BlockSpec((1, nmem, dh), lambda bi, i, h: (bi, 0, h))
    qo_spec = pl.BlockSpec((tq, dh), lambda bi, i, h: (bi * nq + i, h))
    return pl.pallas_call(
        functools.partial(_xattn_kernel, scale=dh ** -0.5),
        grid=(b, nq, X_H),
        in_specs=[qo_spec, kv_spec, kv_spec],
        out_specs=qo_spec,
        out_shape=jax.ShapeDtypeStruct((m, d), BF16),
        compiler_params=_cp("parallel", "parallel", "arbitrary"),
        name="xattn",
    )(q, mem_k, mem_v)


def _gelu_tanh(x):
    c = 0.7978845608028654
    return x * (0.5 * (1.0 + jnp.tanh(c * (x + 0.044715 * (x * x * x)))))


def _shift_rows(x, prev8, s):
    xs = pltpu.roll(x, s, 0)
    ps = pltpu.roll(prev8, s, 0)
    r8 = lax.broadcasted_iota(jnp.int32, prev8.shape, 0)
    top = jnp.where(r8 < s, ps, xs[:SUBLANES])
    if x.shape[0] == SUBLANES:
        return top
    return jnp.concatenate([top, xs[SUBLANES:]], axis=0)


def _ffn_in_kernel(h_ref, wg_ref, wu_ref, cw_ref, cb_ref, buf_ref, act_ref, nbuf_ref, carry_ref,
                   *, tiles_per_seq, swap):
    i = pl.program_id(1 if swap else 0)
    j = pl.program_id(0 if swap else 1)
    h = h_ref[...]
    gate = _dot(h, wg_ref[...])
    up = _dot(h, wu_ref[...])
    tm = gate.shape[0]

    @pl.when(i % tiles_per_seq == 0)
    def _():
        carry_ref[j] = buf_ref[0]

    prev = carry_ref[j]
    g1 = _shift_rows(gate, prev, 1)
    g2 = _shift_rows(gate, prev, 2)
    y = g2 * cw_ref[0:1, :]
    y = y + g1 * cw_ref[1:2, :]
    y = y + gate * cw_ref[2:3, :]
    act_ref[...] = (_gelu_tanh(y + cb_ref[...]) * up).astype(act_ref.dtype)
    last = gate[tm - SUBLANES:, :]
    carry_ref[j] = last
    nbuf_ref[0] = last


def _ffn_in(h, wg, wu, conv_w, conv_b, buf8, t, tm, tn):
    m, d = h.shape
    dff = wg.shape[1]
    ni, nj = m // tm, dff // tn
    tps = t // tm
    swap = tps == 1

    def ix(f):
        return (lambda j, i: f(i, j)) if swap else f

    act, tails = pl.pallas_call(
        functools.partial(_ffn_in_kernel, tiles_per_seq=tps, swap=swap),
        grid=(nj, ni) if swap else (ni, nj),
        in_specs=[pl.BlockSpec((tm, d), ix(lambda i, j: (i, 0))),
                  pl.BlockSpec((d, tn), ix(lambda i, j: (0, j))),
                  pl.BlockSpec((d, tn), ix(lambda i, j: (0, j))),
                  pl.BlockSpec((FFN_CONV, tn), ix(lambda i, j: (0, j))),
                  pl.BlockSpec((1, tn), ix(lambda i, j: (0, j))),
                  pl.BlockSpec((1, SUBLANES, tn), ix(lambda i, j: (i // tps, 0, j)))],
        out_specs=[pl.BlockSpec((tm, tn), ix(lambda i, j: (i, j))),
                   pl.BlockSpec((1, SUBLANES, tn), ix(lambda i, j: (i, 0, j)))],
        out_shape=[jax.ShapeDtypeStruct((m, dff), BF16),
                   jax.ShapeDtypeStruct((ni, SUBLANES, dff), F32)],
        scratch_shapes=[pltpu.VMEM((nj, SUBLANES, tn), F32)],
        compiler_params=_cp("arbitrary", "arbitrary"),
        name="ffn_in",
    )(h, wg, wu, conv_w, conv_b.reshape(1, dff), buf8)
    return act, tails[tps - 1::tps]


def _ret_kernel(ld_ref, p_ref, cos_ref, sin_ref, s0_ref, y_ref, s_ref, *, c):
    t = pl.program_id(1)
    tb = p_ref.shape[0]
    half = RET_DK // 2
    hw = RET_H * RET_DK
    heads = range(RET_H)

    @pl.when(t == 0)
    def _():
        s_ref[...] = s0_ref[...]

    ri = lax.broadcasted_iota(jnp.int32, (c, c), 0)
    ci = lax.broadcasted_iota(jnp.int32, (c, c), 1)
    diff = (ri - ci).astype(F32)
    causal = ri >= ci
    idx = lax.broadcasted_iota(jnp.int32, (c, RET_DK), 0).astype(F32)
    lds = [ld_ref[h] for h in heads]
    dmask = [jnp.where(causal, jnp.exp(ld * jnp.where(causal, diff, 0.0)), 0.0) for ld in lds]
    q_dec = [jnp.exp(ld * (idx + 1.0)) for ld in lds]
    k_dec = [jnp.exp(ld * (c - 1.0 - idx)) for ld in lds]
    s_dec = [jnp.exp(jnp.full((1, RET_DV), ld * c, F32)) for ld in lds]

    def rot(x, cos, sin):
        x1, x2 = x[:, :half], x[:, half:]
        return jnp.concatenate([x1 * cos - x2 * sin, x1 * sin + x2 * cos], axis=-1)

    def body(j, carry):
        rows = pl.ds(pl.multiple_of(j * c, c), c)
        cos, sin = cos_ref[rows, :], sin_ref[rows, :]
        q = [rot(p_ref[rows, h * RET_DK:(h + 1) * RET_DK], cos, sin) for h in heads]
        k = [rot(p_ref[rows, hw + h * RET_DK:hw + (h + 1) * RET_DK], cos, sin) * (RET_DK ** -0.5) for h in heads]
        vb = [p_ref[rows, 2 * hw + h * RET_DV:2 * hw + (h + 1) * RET_DV].astype(BF16) for h in heads]
        qb = [x.astype(BF16) for x in q]
        att = [(_dot_nt(qb[h], k[h].astype(BF16)) * dmask[h]).astype(BF16) for h in heads]
        s = [s_ref[0, h] for h in heads]
        o = [_dot(att[h], vb[h]) + _dot(qb[h], s[h].astype(BF16)) * q_dec[h] for h in heads]
        for h in heads:
            s_ref[0, h] = s_dec[h] * s[h] + _dot_tn((k[h] * k_dec[h]).astype(BF16), vb[h])
        for h in heads:
            g = p_ref[rows, 3 * hw + h * RET_DV:3 * hw + (h + 1) * RET_DV]
            y_ref[rows, h * RET_DV:(h + 1) * RET_DV] = (_rms(o[h]) * _silu(g)).astype(y_ref.dtype)
        return carry

    lax.fori_loop(0, tb // c, body, 0)


def _retention(p_ret, cos, sin, ld, s0, t, tb):
    m, width = p_ret.shape
    b = m // t
    c = min(CHUNK, t)
    nt = t // tb
    st_spec = pl.BlockSpec((1, RET_H, RET_DK, RET_DV), lambda bi, ti: (bi, 0, 0, 0))
    rope_spec = pl.BlockSpec((tb, RET_DK // 2), lambda bi, ti: (ti, 0))
    return pl.pallas_call(
        functools.partial(_ret_kernel, c=c),
        grid=(b, nt),
        in_specs=[pl.BlockSpec(memory_space=pltpu.SMEM),
                  pl.BlockSpec((tb, width), lambda bi, ti: (bi * nt + ti, 0)),
                  rope_spec, rope_spec, st_spec],
        out_specs=[pl.BlockSpec((tb, RET_H * RET_DV), lambda bi, ti: (bi * nt + ti, 0)), st_spec],
        out_shape=[jax.ShapeDtypeStruct((m, RET_H * RET_DV), BF16),
                   jax.ShapeDtypeStruct((b, RET_H, RET_DK, RET_DV), F32)],
        compiler_params=_cp("parallel", "arbitrary"),
        name="retention",
    )(ld, p_ret, cos, sin, s0)


def _gla_kernel(p_ref, lr_ref, wgk_ref, bgk_ref, nrm_ref, s0_ref, y_ref, s_ref, st_ref, *, c, nt):
    t = pl.program_id(1)
    tb = p_ref.shape[0]
    hk = GLA_H * GLA_DK
    hv = GLA_H * GLA_DV
    heads = range(GLA_H)

    @pl.when(t == 0)
    def _():
        for h in heads:
            st_ref[h] = s0_ref[0, h].T

    ri = lax.broadcasted_iota(jnp.int32, (c, c), 0)
    ci = lax.broadcasted_iota(jnp.int32, (c, c), 1)
    causal = ri >= ci
    tri = jnp.where(causal, 1.0, 0.0).astype(BF16)
    wgk = wgk_ref[...]
    bgk = bgk_ref[...]
    nrm = nrm_ref[...]

    def body(j, carry):
        rows = pl.ds(pl.multiple_of(j * c, c), c)
        z = _dot(lr_ref[rows, :].astype(BF16), wgk) + bgk
        gk = -_softplus(-z) / GLA_NORMALIZER
        bc = _tri_cumsum(tri, gk)
        q = p_ref[rows, 0:hk] * (GLA_DK ** -0.5)
        k = p_ref[rows, hk:2 * hk]
        qe_all = (q * jnp.exp(bc)).astype(BF16)
        ke_all = (k * jnp.exp(-bc)).astype(BF16)
        b_last = bc[c - 1:c, :]
        kd_all = (k * jnp.exp(b_last - bc)).astype(BF16)
        eb_last = jnp.exp(b_last)
        hs = lambda x, h: x[:, h * GLA_DK:(h + 1) * GLA_DK]
        vb = [p_ref[rows, 2 * hk + h * GLA_DV:2 * hk + (h + 1) * GLA_DV].astype(BF16) for h in heads]
        att = [jnp.where(causal, _dot_nt(hs(qe_all, h), hs(ke_all, h)), 0.0).astype(BF16) for h in heads]
        st = [st_ref[h] for h in heads]
        o = [_dot(att[h], vb[h]) + _dot_nt(hs(qe_all, h), st[h].astype(BF16)) for h in heads]
        for h in heads:
            st_ref[h] = hs(eb_last, h) * st[h] + _dot_tn(vb[h], hs(kd_all, h))
        for h in heads:
            g = p_ref[rows, 2 * hk + hv + h * GLA_DV:2 * hk + hv + (h + 1) * GLA_DV]
            y_ref[rows, h * GLA_DV:(h + 1) * GLA_DV] = (_rms(o[h], nrm) * _silu(g)).astype(y_ref.dtype)
        return carry

    lax.fori_loop(0, tb // c, body, 0)

    @pl.when(t == nt - 1)
    def _():
        for h in heads:
            s_ref[0, h] = st_ref[h].T


def _gla(p_gla, p_small, wgk, bgk, nrm, s0, t, tb):
    m, width = p_gla.shape
    b = m // t
    c = min(CHUNK, t)
    nt = t // tb
    const = lambda bi, ti: (0, 0)
    st_spec = pl.BlockSpec((1, GLA_H, GLA_DK, GLA_DV), lambda bi, ti: (bi, 0, 0, 0))
    return pl.pallas_call(
        functools.partial(_gla_kernel, c=c, nt=nt),
        grid=(b, nt),
        in_specs=[pl.BlockSpec((tb, width), lambda bi, ti: (bi * nt + ti, 0)),
                  pl.BlockSpec((tb, LANES), lambda bi, ti: (bi * nt + ti, 0)),
                  pl.BlockSpec((LANES, GLA_H * GLA_DK), const),
                  pl.BlockSpec((1, GLA_H * GLA_DK), const),
                  pl.BlockSpec((1, GLA_DV), const),
                  st_spec],
        out_specs=[pl.BlockSpec((tb, GLA_H * GLA_DV), lambda bi, ti: (bi * nt + ti, 0)), st_spec],
        out_shape=[jax.ShapeDtypeStruct((m, GLA_H * GLA_DV), BF16),
                   jax.ShapeDtypeStruct((b, GLA_H, GLA_DK, GLA_DV), F32)],
        scratch_shapes=[pltpu.VMEM((GLA_H, GLA_DV, GLA_DK), F32)],
        compiler_params=_cp("parallel", "arbitrary"),
        name="gla",
    )(p_gla, p_small, wgk, bgk, nrm, s0)


def _unit_lower_inverse_minus_eye(mats, c):
    ri = lax.broadcasted_iota(jnp.int32, (c, c), 0)
    ci = lax.broadcasted_iota(jnp.int32, (c, c), 1)

    def level_mask(k):
        same = (ri ^ ci) < 2 * k
        return same & ((ri & k) != 0) & ((ci & k) == 0)

    m1 = level_mask(1)
    ns = [-jnp.where(m1, a, 0.0) for a in mats]
    k = 2
    while k < c:
        mk = level_mask(k)
        ls = [jnp.where(mk, a, 0.0) for a in mats]
        nbs = [n.astype(BF16) for n in ns]
        ys = [l + _dot(nb, l.astype(BF16)) for l, nb in zip(ls, nbs)]
        xs = [y + _dot(y.astype(BF16), nb) for y, nb in zip(ys, nbs)]
        ns = [n - x for n, x in zip(ns, xs)]
        k *= 2
    return ns


def _gdn_kernel(p_ref, sm_ref, cw_ref, buf_ref, alog_ref, dtb_ref, nrm_ref, s0_ref,
                y_ref, s_ref, nbuf_ref, carry_ref, *, c, nt):
    t = pl.program_id(1)
    tb = p_ref.shape[0]
    hq = GDN_H * GDN_DK
    hv = GDN_H * GDN_DV
    nconv = 2 * hq + hv

    @pl.when(t == 0)
    def _():
        s_ref[...] = s0_ref[...]
        carry_ref[...] = buf_ref[0]

    ri = lax.broadcasted_iota(jnp.int32, (c, c), 0)
    ci = lax.broadcasted_iota(jnp.int32, (c, c), 1)
    incl = ri >= ci
    strict = ri > ci
    tri = jnp.where(incl, 1.0, 0.0).astype(BF16)
    lane = lax.broadcasted_iota(jnp.int32, (1, LANES), 1)
    a_lanes = jnp.logical_and(lane >= SMALL_A0, lane < SMALL_A0 + GDN_H)
    sel = jnp.where(lax.broadcasted_iota(jnp.int32, (SUBLANES, LANES), 1)
                    == lax.broadcasted_iota(jnp.int32, (SUBLANES, LANES), 0) + SMALL_A0,
                    1.0, 0.0).astype(BF16)
    neg_a = -jnp.exp(alog_ref[...])
    dtb = dtb_ref[...]
    nrm = nrm_ref[...]
    w0, w1, w2, w3 = (cw_ref[i:i + 1, :] for i in range(GDN_CONV))

    def col(x, lane_idx):
        return x[:, lane_idx:lane_idx + 1]

    def body(j, carry):
        rows = pl.ds(pl.multiple_of(j * c, c), c)
        x = p_ref[rows, 0:nconv]
        prev = carry_ref[...]
        conv = _shift_rows(x, prev, 3) * w0
        conv = conv + _shift_rows(x, prev, 2) * w1
        conv = conv + _shift_rows(x, prev, 1) * w2
        conv = conv + x * w3
        carry_ref[...] = x[c - SUBLANES:, :]
        qkv = _silu(conv)

        sm = sm_ref[rows, :]
        g_all = jnp.where(a_lanes, neg_a * _softplus(sm + dtb), 0.0)
        gam = _tri_cumsum(tri, g_all)
        ghi, gmid, glo = _split3(gam)
        gam_rows = _dot_nt(sel, ghi) + _dot_nt(sel, gmid) + _dot_nt(sel, glo)
        beta_all = _sigmoid(sm)
        eg_all = jnp.exp(gam)
        g_last_all = gam[c - 1:c, :]
        egl_all = jnp.exp(g_last_all - gam)
        eg_last_all = jnp.exp(g_last_all)

        heads = range(GDN_H)

        def l2n(x):
            return x * lax.rsqrt(jnp.sum(x * x, axis=-1, keepdims=True) + EPS)

        q = [l2n(qkv[:, h * GDN_DK:(h + 1) * GDN_DK]) * (GDN_DK ** -0.5) for h in heads]
        k = [l2n(qkv[:, hq + h * GDN_DK:hq + (h + 1) * GDN_DK]) for h in heads]
        v = [qkv[:, 2 * hq + h * GDN_DV:2 * hq + (h + 1) * GDN_DV] for h in heads]
        beta = [col(beta_all, SMALL_B0 + h) for h in heads]
        eg = [col(eg_all, SMALL_A0 + h) for h in heads]
        decay = [jnp.where(incl, jnp.exp(jnp.where(incl, col(gam, SMALL_A0 + h) - gam_rows[h:h + 1, :], 0.0)), 0.0)
                 for h in heads]
        kb = [x.astype(BF16) for x in k]
        qk_kk = [_dot_nt(jnp.concatenate([q[h], k[h]], axis=0).astype(BF16), kb[h]) for h in heads]
        att = [(qk_kk[h][:c] * decay[h]).astype(BF16) for h in heads]
        a = [jnp.where(strict, beta[h] * qk_kk[h][c:] * decay[h], 0.0) for h in heads]
        n = _unit_lower_inverse_minus_eye(a, c)
        rhs = [jnp.concatenate([beta[h] * v[h], (beta[h] * eg[h]) * k[h]], axis=-1) for h in heads]
        sol = [rhs[h] + _dot(n[h].astype(BF16), rhs[h].astype(BF16)) for h in heads]
        s = [s_ref[0, h] for h in heads]
        wq_s = [_dot(jnp.concatenate([sol[h][:, GDN_DV:], q[h]], axis=0).astype(BF16), s[h].astype(BF16))
                for h in heads]
        wb = [(sol[h][:, :GDN_DV] - wq_s[h][:c]).astype(BF16) for h in heads]
        o = [eg[h] * wq_s[h][c:] + _dot(att[h], wb[h]) for h in heads]
        kd = [(k[h] * col(egl_all, SMALL_A0 + h)).astype(BF16) for h in heads]
        for h in heads:
            s_ref[0, h] = col(eg_last_all, SMALL_A0 + h) * s[h] + _dot_tn(kd[h], wb[h])
        for h in heads:
            z = p_ref[rows, nconv + h * GDN_DV:nconv + (h + 1) * GDN_DV]
            y = _rms(o[h], nrm) * _silu(z)
            y_ref[rows, h * GDN_DV:(h + 1) * GDN_DV] = y.astype(y_ref.dtype)
        return carry

    lax.fori_loop(0, tb // c, body, 0)

    @pl.when(t == nt - 1)
    def _():
        nbuf_ref[0] = carry_ref[...]


def _gdn(p_gdn, p_small, conv_w, buf8, alog_row, dtb_row, nrm, s0, t, tb):
    m, width = p_gdn.shape
    b = m // t
    c = min(CHUNK, t)
    nt = t // tb
    nconv = conv_w.shape[1]
    st_spec = pl.BlockSpec((1, GDN_H, GDN_DK, GDN_DV), lambda bi, ti: (bi, 0, 0, 0))
    buf_spec = pl.BlockSpec((1, SUBLANES, nconv), lambda bi, ti: (bi, 0, 0))
    const = lambda bi, ti: (0, 0)
    return pl.pallas_call(
        functools.partial(_gdn_kernel, c=c, nt=nt),
        grid=(b, nt),
        in_specs=[pl.BlockSpec((tb, width), lambda bi, ti: (bi * nt + ti, 0)),
                  pl.BlockSpec((tb, LANES), lambda bi, ti: (bi * nt + ti, 0)),
                  pl.BlockSpec((GDN_CONV, nconv), const),
                  buf_spec,
                  pl.BlockSpec((1, LANES), const),
                  pl.BlockSpec((1, LANES), const),
                  pl.BlockSpec((1, GDN_DV), const),
                  st_spec],
        out_specs=[pl.BlockSpec((tb, GDN_H * GDN_DV), lambda bi, ti: (bi * nt + ti, 0)),
                   st_spec, buf_spec],
        out_shape=[jax.ShapeDtypeStruct((m, GDN_H * GDN_DV), BF16),
                   jax.ShapeDtypeStruct((b, GDN_H, GDN_DK, GDN_DV), F32),
                   jax.ShapeDtypeStruct((b, SUBLANES, nconv), F32)],
        scratch_shapes=[pltpu.VMEM((SUBLANES, nconv), F32)],
        compiler_params=_cp("parallel", "arbitrary"),
        name="gated_deltanet",
    )(p_gdn, p_small, conv_w, buf8, alog_row, dtb_row, nrm, s0)


def _pad_rows_front(buf, rows):
    return jnp.pad(buf, ((0, 0), (0, 0), (rows - buf.shape[2], 0), (0, 0)))


def _prep_weights(w):
    depth, d, _ = w['w_in'].shape
    o = 0
    sizes = dict(ret=4 * RET_H * RET_DK, gla=2 * GLA_H * GLA_DK + 2 * GLA_H * GLA_DV, lr=GLA_RANK,
                 gdn=2 * GDN_H * GDN_DK + 2 * GDN_H * GDN_DV, ab=2 * GDN_H, gbr=N_BRANCH * d)
    sl = {}
    for name in ('ret', 'gla', 'lr', 'gdn', 'ab', 'gbr'):
        sl[name] = (o, o + sizes[name])
        o += sizes[name]
    w_in = w['w_in']
    cut = lambda name: w_in[:, :, sl[name][0]:sl[name][1]].astype(BF16)
    small = jnp.concatenate([w_in[:, :, sl['lr'][0]:sl['lr'][1]], w_in[:, :, sl['ab'][0]:sl['ab'][1]]], axis=-1)
    small = jnp.pad(small, ((0, 0), (0, 0), (0, LANES - small.shape[-1]))).astype(BF16)
    dff = w['w_ffn_out'].shape[1]
    lane_row = lambda v, at: jnp.pad(v, ((0, 0), (at, LANES - at - v.shape[1])))[:, None, :]
    return dict(
        w_ret=cut('ret'), w_gla=cut('gla'), w_gdn=cut('gdn'), w_gbr=cut('gbr'), w_small=small,
        w_gk=jnp.pad(w['w_gla_gk'], ((0, 0), (0, LANES - GLA_RANK), (0, 0))).astype(BF16),
        b_gk=w['b_gla_gk'][:, None, :],
        gla_norm=w['gla_norm'][:, None, :], gdn_norm=w['gdn_norm'][:, None, :],
        alog_row=lane_row(w['gdn_a_log'], SMALL_A0), dtb_row=lane_row(w['gdn_dt_bias'], SMALL_A0),
        gdn_conv_w=w['gdn_conv_w'], b_gate=w['b_gate'],
        w_branch=w['w_branch'].astype(BF16), w_out=w['w_out'].astype(BF16),
        w_xq=w['w_xq'].astype(BF16), w_xo=w['w_xo'].astype(BF16),
        w_ffn_g=w['w_ffn_in'][:, :, :dff].astype(BF16), w_ffn_u=w['w_ffn_in'][:, :, dff:].astype(BF16),
        ffn_conv_w=w['ffn_conv_w'], ffn_conv_b=w['ffn_conv_b'], w_ffn_out=w['w_ffn_out'].astype(BF16),
        ln_mix=w['ln_mix'], ln_xattn=w['ln_xattn'], ln_ffn=w['ln_ffn'], ln_final=w['ln_final'],
    )


def _tile(n, cap):
    t = min(n, cap)
    while n % t:
        t -= 1
    return t


def _trunk(x, offset, mem_k, mem_v, s_ret, s_gla, s_gdn, buf_gdn, buf_ffn, pw):
    b, t, d = x.shape
    m = b * t
    depth = s_ret.shape[0]
    x2 = x.reshape(m, d)
    tm = _tile(m, MM_TM)
    tm_res = _tile(m, RES_TM)
    tm_res_k = _tile(m, RES_TM_K)
    tb = _tile(t, SEQ_TB)
    tb_gdn = _tile(t, SEQ_TB_GDN)
    tq = _tile(t, XATTN_TQ)
    tm_ffn = _tile(t, FFN_TM)

    pos = offset + jnp.arange(t, dtype=F32)
    half = RET_DK // 2
    freqs = 1.0 / (ROPE_BASE ** (jnp.arange(half, dtype=F32) / half))
    ang = pos[:, None] * freqs[None, :]
    cos, sin = jnp.cos(ang), jnp.sin(ang)
    ld = jnp.log1p(-jnp.exp2(-5.0 - jnp.arange(RET_H, dtype=F32)))

    buf_gdn8 = _pad_rows_front(buf_gdn, SUBLANES)
    buf_ffn8 = _pad_rows_front(buf_ffn, SUBLANES)
    memk = mem_k.reshape(depth, b, mem_k.shape[2], d)
    memv = mem_v.reshape(depth, b, mem_v.shape[2], d)

    outs = ([], [], [], [], [])
    hn = _norm(x2, pw['ln_mix'][0], tm_res_k)
    y = None
    for l in range(depth):
        p_ret = _mm(hn, pw['w_ret'][l], F32, tm, MM_TN, "in_proj_ret")
        p_gla = _mm(hn, pw['w_gla'][l], F32, tm, MM_TN, "in_proj_gla")
        p_gdn = _mm(hn, pw['w_gdn'][l], F32, tm, MM_TN, "in_proj_gdn")
        g_br = _mm(hn, pw['w_gbr'][l], F32, tm, MM_TN, "in_proj_gates")
        p_small = _mm(hn, pw['w_small'][l], F32, tm, LANES, "in_proj_small")

        y_ret, sr = _retention(p_ret, cos, sin, ld, s_ret[l], t, tb)
        y_gla, sg = _gla(p_gla, p_small, pw['w_gk'][l], pw['b_gk'][l], pw['gla_norm'][l], s_gla[l], t, tb)
        y_gdn, sd, bg8 = _gdn(p_gdn, p_small, pw['gdn_conv_w'][l], buf_gdn8[l], pw['alog_row'][l],
                              pw['dtb_row'][l], pw['gdn_norm'][l], s_gdn[l], t, tb_gdn)

        merged = _merge((y_ret, y_gla, y_gdn), pw['w_branch'][l], g_br, pw['b_gate'][l], tm_res_k, MM_TN)
        x2, hx = _proj_res(merged, pw['w_out'][l], x2, pw['ln_xattn'][l], tm_res, d, False, "out_proj")

        q = _mm(hx, pw['w_xq'][l], BF16, tm, MM_TN, "xattn_q")
        o = _xattn(q, memk[l], memv[l], t, tq)
        x2, hf = _proj_res(o, pw['w_xo'][l], x2, pw['ln_ffn'][l], tm_res, d, False, "xattn_out")

        act, bf8 = _ffn_in(hf, pw['w_ffn_g'][l], pw['w_ffn_u'][l], pw['ffn_conv_w'][l], pw['ffn_conv_b'][l],
                           buf_ffn8[l], t, tm_ffn, FFN_TN)
        dff = act.shape[1]
        tk = _tile(dff, RES_TK)
        tmk = tm_res if tk == dff else tm_res_k
        if l + 1 < depth:
            x2, hn = _proj_res(act, pw['w_ffn_out'][l], x2, pw['ln_mix'][l + 1], tmk, tk, False, "ffn_out")
        else:
            y, = _proj_res(act, pw['w_ffn_out'][l], x2, pw['ln_final'], tmk, tk, True, "ffn_out_final")

        for lst, val in zip(outs, (sr, sg, sd, bg8[:, SUBLANES - (GDN_CONV - 1):], bf8[:, SUBLANES - (FFN_CONV - 1):])):
            lst.append(val)
    return (y.reshape(b, t, d),) + tuple(jnp.stack(lst) for lst in outs)


def kernel(x_prompt, x_sample, mem_prompt, state_ret, state_gla, state_gdn, state_gdn_conv, state_ffn_conv, cache_mem_k, cache_mem_v, ln_mix, w_in, w_gla_gk, b_gla_gk, gla_norm, gdn_conv_w, gdn_a_log, gdn_dt_bias, gdn_norm, b_gate, w_branch, w_out, ln_xattn, ln_mem, w_xq, w_xkv, w_xo, ln_ffn, w_ffn_in, ffn_conv_w, ffn_conv_b, w_ffn_out, ln_final):
    pw = _prep_weights(dict(
        w_in=w_in, w_gla_gk=w_gla_gk, b_gla_gk=b_gla_gk, gla_norm=gla_norm, gdn_conv_w=gdn_conv_w,
        gdn_a_log=gdn_a_log, gdn_dt_bias=gdn_dt_bias, gdn_norm=gdn_norm, b_gate=b_gate, w_branch=w_branch,
        w_out=w_out, w_xq=w_xq, w_xo=w_xo, w_ffn_in=w_ffn_in, ffn_conv_w=ffn_conv_w, ffn_conv_b=ffn_conv_b,
        w_ffn_out=w_ffn_out, ln_mix=ln_mix, ln_xattn=ln_xattn, ln_ffn=ln_ffn, ln_final=ln_final))
    depth = w_in.shape[0]
    bp, nmem, d = mem_prompt.shape
    dt = x_prompt.dtype

    mem2 = mem_prompt.reshape(bp * nmem, d)
    w_xkv_b = w_xkv.astype(BF16)
    tmem = _tile(bp * nmem, MM_TM)
    mk, mv = [], []
    for l in range(depth):
        hm = _norm(mem2, ln_mem[l], _tile(bp * nmem, RES_TM_K))
        mk.append(_mm(hm, w_xkv_b[l, :, :d], F32, tmem, MM_TN, "mem_k"))
        mv.append(_mm(hm, w_xkv_b[l, :, d:], F32, tmem, MM_TN, "mem_v"))
    dh = d // X_H
    mem_k_p = jnp.stack(mk).reshape(depth, bp, nmem, X_H, dh)
    mem_v_p = jnp.stack(mv).reshape(depth, bp, nmem, X_H, dh)

    zeros = lambda *s: jnp.zeros((depth, bp) + s, dt)
    y_p, ret_p, gla_p, gdn_p, gdn_conv_p, ffn_conv_p = _trunk(
        x_prompt, 0.0, mem_k_p, mem_v_p,
        zeros(RET_H, RET_DK, RET_DV), zeros(GLA_H, GLA_DK, GLA_DV), zeros(GDN_H, GDN_DK, GDN_DV),
        zeros(GDN_CONV - 1, state_gdn_conv.shape[-1]), zeros(FFN_CONV - 1, state_ffn_conv.shape[-1]), pw)

    past_len = 4096.0
    y_s, ret_s, gla_s, gdn_s, gdn_conv_s, ffn_conv_s = _trunk(
        x_sample, past_len, cache_mem_k, cache_mem_v,
        state_ret, state_gla, state_gdn, state_gdn_conv, state_ffn_conv, pw)

    return (y_p, y_s, ret_p, gla_p, gdn_p, gdn_conv_p, ffn_conv_p, mem_k_p, mem_v_p,
            ret_s, gla_s, gdn_s, gdn_conv_s, ffn_conv_s)
```

```python
import functools

import jax
import jax.numpy as jnp
from jax import lax
from jax.experimental import pallas as pl
from jax.experimental.pallas import tpu as pltpu

F32 = jnp.float32
BF16 = jnp.bfloat16

EPS = 1e-6
CHUNK = 64
RET_H, RET_DK, RET_DV = 4, 256, 256
GLA_H, GLA_DK, GLA_DV = 4, 128, 256
GLA_RANK = 16
GLA_NORMALIZER = 16.0
GDN_H, GDN_DK, GDN_DV = 8, 128, 128
GDN_CONV = 4
N_BRANCH = 3
X_H = 4
FFN_CONV = 3
ROPE_BASE = 10000.0

LANES = 128
SUBLANES = 8
SMALL_A0 = GLA_RANK
SMALL_B0 = GLA_RANK + GDN_H
VMEM_LIMIT = 52 * 1024 * 1024
MM_TM = 1024
MM_TN = 1024
RES_TM = 512
RES_TM_K = 512
FFO_TM = 256
ROW_SUB = 256
SEQ_TB = 512
SEQ_TB_GDN = 256
XATTN_TQ = 512
FFN_TM = 1024
FFN_TN = 512
FFN_SUB = 256


def _cp(*sem):
    return pltpu.CompilerParams(dimension_semantics=sem, vmem_limit_bytes=VMEM_LIMIT)


def _dot(a, b):
    return jnp.dot(a, b, preferred_element_type=F32)


def _dot_nt(a, b):
    return lax.dot_general(a, b, (((1,), (1,)), ((), ())), preferred_element_type=F32)


def _dot_tn(a, b):
    return lax.dot_general(a, b, (((0,), (0,)), ((), ())), preferred_element_type=F32)


def _sigmoid(x):
    return 1.0 / (1.0 + jnp.exp(-x))


def _silu(x):
    return x * _sigmoid(x)


def _softplus(x):
    return jnp.maximum(x, 0.0) + jnp.log1p(jnp.exp(-jnp.abs(x)))


def _rms(x, g=None):
    y = x * lax.rsqrt(jnp.mean(x * x, axis=-1, keepdims=True) + EPS)
    return y if g is None else y * g


def _split3(x):
    hi = x.astype(BF16)
    r = x - hi.astype(F32)
    mid = r.astype(BF16)
    lo = (r - mid.astype(F32)).astype(BF16)
    return hi, mid, lo


def _tri_cumsum(tri, x):
    hi, mid, lo = _split3(x)
    return _dot(tri, hi) + _dot(tri, mid) + _dot(tri, lo)


def _norm_kernel(x_ref, g_ref, o_ref):
    o_ref[...] = _rms(x_ref[...], g_ref[...]).astype(o_ref.dtype)


def _norm(x, g, tm):
    m, d = x.shape
    return pl.pallas_call(
        _norm_kernel,
        grid=(m // tm,),
        in_specs=[pl.BlockSpec((tm, d), lambda i: (i, 0)),
                  pl.BlockSpec((1, d), lambda i: (0, 0))],
        out_specs=pl.BlockSpec((tm, d), lambda i: (i, 0)),
        out_shape=jax.ShapeDtypeStruct((m, d), BF16),
        compiler_params=_cp("parallel"),
        name="rmsnorm",
    )(x, g.reshape(1, d))


CAST_ROWS = 256


def _cast_weight_tile(w_ref, wx_ref, wb_ref, shift):
    kk, tn = wb_ref.shape
    for r in range(0, kk, CAST_ROWS):
        rows = slice(r, min(r + CAST_ROWS, kk))
        if shift == 0:
            wb_ref[rows, :] = w_ref[0, rows, :].astype(BF16)
        else:
            wide = jnp.concatenate([w_ref[0, rows, :], wx_ref[0, rows, :]], axis=1)
            wb_ref[rows, :] = pltpu.roll(wide, wide.shape[1] - shift, 1)[:, :tn].astype(BF16)


def _mm_ws_kernel(a_ref, w_ref, *rest, shift):
    if shift:
        wx_ref, o_ref, wb_ref = rest
    else:
        wx_ref = None
        o_ref, wb_ref = rest

    @pl.when(pl.program_id(1) == 0)
    def _():
        _cast_weight_tile(w_ref, wx_ref, wb_ref, shift)

    o_ref[...] = _dot(a_ref[...], wb_ref[...]).astype(o_ref.dtype)


def _mm_ws(a, w, layer, blk0, n, shift, out_dtype, tm, tn, name):
    m, k = a.shape
    nj = n // tn
    in_specs = [pl.BlockSpec((tm, k), lambda j, i: (i, 0)),
                pl.BlockSpec((1, k, tn), lambda j, i: (layer, 0, blk0 + j))]
    args = [a, w]
    if shift:
        per = tn // LANES
        in_specs.append(pl.BlockSpec((1, k, LANES), lambda j, i: (layer, 0, (blk0 + j + 1) * per)))
        args.append(w)
    return pl.pallas_call(
        functools.partial(_mm_ws_kernel, shift=shift),
        grid=(nj, m // tm),
        in_specs=in_specs,
        out_specs=pl.BlockSpec((tm, tn), lambda j, i: (i, j)),
        out_shape=jax.ShapeDtypeStruct((m, n), out_dtype),
        scratch_shapes=[pltpu.VMEM((k, tn), BF16)],
        compiler_params=_cp("parallel", "arbitrary"),
        name=name,
    )(*args)


def _small_proj_kernel(a_ref, w0_ref, w1_ref, o_ref):
    a = a_ref[...]
    p0 = _dot(a, w0_ref[0].astype(BF16))
    p1 = _dot(a, w1_ref[0].astype(BF16))
    lane = lax.broadcasted_iota(jnp.int32, p0.shape, 1)
    o_ref[...] = jnp.where(lane < SMALL_A0, p0, p1)


def _small_proj(a, w, layer, blk_lr, blk_ab, tm):
    m, k = a.shape
    return pl.pallas_call(
        _small_proj_kernel,
        grid=(m // tm,),
        in_specs=[pl.BlockSpec((tm, k), lambda i: (i, 0)),
                  pl.BlockSpec((1, k, LANES), lambda i: (layer, 0, blk_lr)),
                  pl.BlockSpec((1, k, LANES), lambda i: (layer, 0, blk_ab))],
        out_specs=pl.BlockSpec((tm, LANES), lambda i: (i, 0)),
        out_shape=jax.ShapeDtypeStruct((m, LANES), F32),
        compiler_params=_cp("parallel"),
        name="in_proj_small",
    )(a, w, w)


def _proj_res_kernel(a_ref, w_ref, x_ref, g_ref, *outs, sub, final):
    tm = a_ref.shape[0]
    w = w_ref[...]
    g = g_ref[...]
    for r in range(tm // sub):
        rows = slice(r * sub, (r + 1) * sub)
        xn = x_ref[rows, :] + _dot(a_ref[rows, :], w)
        if final:
            outs[0][rows, :] = _rms(xn, g)
        else:
            outs[0][rows, :] = xn
            outs[1][rows, :] = _rms(xn, g).astype(outs[1].dtype)


def _proj_res(a, w, x, g, tm, sub, final, name):
    m, kk = a.shape
    d = w.shape[1]
    row = lambda i: (i, 0)
    out_specs = [pl.BlockSpec((tm, d), row)]
    out_shape = [jax.ShapeDtypeStruct((m, d), F32)]
    if not final:
        out_specs.append(pl.BlockSpec((tm, d), row))
        out_shape.append(jax.ShapeDtypeStruct((m, d), BF16))
    return pl.pallas_call(
        functools.partial(_proj_res_kernel, sub=min(sub, tm), final=final),
        grid=(m // tm,),
        in_specs=[pl.BlockSpec((tm, kk), row),
                  pl.BlockSpec((kk, d), lambda i: (0, 0), pipeline_mode=pl.Buffered(1)),
                  pl.BlockSpec((tm, d), row),
                  pl.BlockSpec((1, d), lambda i: (0, 0))],
        out_specs=out_specs,
        out_shape=out_shape,
        compiler_params=_cp("parallel"),
        name=name,
    )(a, w, x, g.reshape(1, d))


def _merge_kernel(y0_ref, y1_ref, y2_ref, w_ref, g0_ref, g1_ref, g2_ref, b_ref, o_ref, *, sub):
    tm = o_ref.shape[0]
    for r in range(tm // sub):
        rows = slice(r * sub, (r + 1) * sub)
        acc = None
        for n, (y_ref, g_ref) in enumerate(((y0_ref, g0_ref), (y1_ref, g1_ref), (y2_ref, g2_ref))):
            p = _dot(y_ref[rows, :], w_ref[n])
            t = _sigmoid(g_ref[rows, :] + b_ref[n:n + 1, :]) * p
            acc = t if acc is None else acc + t
        o_ref[rows, :] = acc.astype(o_ref.dtype)


def _merge(ys, w_branch, g_br, b_gate, tm, tn):
    m, bw = ys[0].shape
    d = w_branch.shape[2]
    nj = d // tn
    y_spec = pl.BlockSpec((tm, bw), lambda i, j: (i, 0))

    def g_spec(n):
        return pl.BlockSpec((tm, tn), lambda i, j: (i, n * nj + j))

    return pl.pallas_call(
        functools.partial(_merge_kernel, sub=min(tm, ROW_SUB)),
        grid=(m // tm, nj),
        in_specs=[y_spec, y_spec, y_spec,
                  pl.BlockSpec((N_BRANCH, bw, tn), lambda i, j: (0, 0, j)),
                  g_spec(0), g_spec(1), g_spec(2),
                  pl.BlockSpec((N_BRANCH, tn), lambda i, j: (0, j))],
        out_specs=pl.BlockSpec((tm, tn), lambda i, j: (i, j)),
        out_shape=jax.ShapeDtypeStruct((m, d), BF16),
        compiler_params=_cp("parallel", "arbitrary"),
        name="branch_merge",
    )(ys[0], ys[1], ys[2], w_branch, g_br, g_br, g_br, b_gate)


def _xattn_kernel(q_ref, k_ref, v_ref, o_ref, *, scale):
    q = q_ref[...]
    k = k_ref[0].astype(BF16)
    v = v_ref[0].astype(BF16)
    s = _dot_nt(q, k) * scale
    s = s - jnp.max(s, axis=-1, keepdims=True)
    e = jnp.exp(s)
    p = e / jnp.sum(e, axis=-1, keepdims=True)
    o_ref[...] = _dot(p.astype(BF16), v).astype(o_ref.dtype)


def _xattn(q, mem_k, mem_v, t, tq):
    m, d = q.shape
    b = m // t
    nmem = mem_k.shape[1]
    dh = d // X_H
    nq = t // tq
    kv_spec = pl.BlockSpec((1, nmem, dh), lambda bi, i, h: (bi, 0, h))
    qo_spec = pl.BlockSpec((tq, dh), lambda bi, i, h: (bi * nq + i, h))
    return pl.pallas_call(
        functools.partial(_xattn_kernel, scale=dh ** -0.5),
        grid=(b, nq, X_H),
        in_specs=[qo_spec, kv_spec, kv_spec],
        out_specs=qo_spec,
        out_shape=jax.ShapeDtypeStruct((m, d), BF16),
        compiler_params=_cp("parallel", "parallel", "arbitrary"),
        name="xattn",
    )(q, mem_k, mem_v)


def _gelu_tanh(x):
    c = 0.7978845608028654
    return x * (0.5 * (1.0 + jnp.tanh(c * (x + 0.044715 * (x * x * x)))))


def _shift_rows(x, prev8, s):
    xs = pltpu.roll(x, s, 0)
    ps = pltpu.roll(prev8, s, 0)
    r8 = lax.broadcasted_iota(jnp.int32, prev8.shape, 0)
    top = jnp.where(r8 < s, ps, xs[:SUBLANES])
    if x.shape[0] == SUBLANES:
        return top
    return jnp.concatenate([top, xs[SUBLANES:]], axis=0)


def _ffn_in_kernel(h_ref, wg_ref, wu_ref, cw_ref, cb_ref, buf_ref, act_ref, nbuf_ref,
                   wgb_ref, wub_ref, carry_ref, *, tiles_per_seq):
    i = pl.program_id(1)
    tm = h_ref.shape[0]
    sub = min(tm, FFN_SUB)

    @pl.when(i == 0)
    def _():
        _cast_weight_tile(wg_ref, None, wgb_ref, 0)
        _cast_weight_tile(wu_ref, None, wub_ref, 0)

    @pl.when(i % tiles_per_seq == 0)
    def _():
        carry_ref[...] = buf_ref[0]

    wg, wu = wgb_ref[...], wub_ref[...]
    w0, w1, w2, cb = cw_ref[0:1, :], cw_ref[1:2, :], cw_ref[2:3, :], cb_ref[...]
    prev = carry_ref[...]
    for r in range(tm // sub):
        rows = slice(r * sub, (r + 1) * sub)
        h = h_ref[rows, :]
        gate = _dot(h, wg)
        up = _dot(h, wu)
        y = _shift_rows(gate, prev, 2) * w0
        y = y + _shift_rows(gate, prev, 1) * w1
        y = y + gate * w2
        act_ref[rows, :] = (_gelu_tanh(y + cb) * up).astype(act_ref.dtype)
        prev = gate[sub - SUBLANES:, :]
    carry_ref[...] = prev
    nbuf_ref[0] = prev


def _ffn_in(h, w_ffn_in, layer, conv_w, conv_b, buf8, t, tm, tn):
    m, d = h.shape
    dff = w_ffn_in.shape[2] // 2
    ni, nj = m // tm, dff // tn
    tps = t // tm
    act, tails = pl.pallas_call(
        functools.partial(_ffn_in_kernel, tiles_per_seq=tps),
        grid=(nj, ni),
        in_specs=[pl.BlockSpec((tm, d), lambda j, i: (i, 0)),
                  pl.BlockSpec((1, d, tn), lambda j, i: (layer, 0, j)),
                  pl.BlockSpec((1, d, tn), lambda j, i: (layer, 0, nj + j)),
                  pl.BlockSpec((FFN_CONV, tn), lambda j, i: (0, j)),
                  pl.BlockSpec((1, tn), lambda j, i: (0, j)),
                  pl.BlockSpec((1, SUBLANES, tn), lambda j, i: (i // tps, 0, j))],
        out_specs=[pl.BlockSpec((tm, tn), lambda j, i: (i, j)),
                   pl.BlockSpec((1, SUBLANES, tn), lambda j, i: (i, 0, j))],
        out_shape=[jax.ShapeDtypeStruct((m, dff), BF16),
                   jax.ShapeDtypeStruct((ni, SUBLANES, dff), F32)],
        scratch_shapes=[pltpu.VMEM((d, tn), BF16), pltpu.VMEM((d, tn), BF16),
                        pltpu.VMEM((SUBLANES, tn), F32)],
        compiler_params=_cp("parallel", "arbitrary"),
        name="ffn_in",
    )(h, w_ffn_in, w_ffn_in, conv_w, conv_b.reshape(1, dff), buf8)
    return act, tails[tps - 1::tps]


def _ret_kernel(ld_ref, p_ref, cos_ref, sin_ref, s0_ref, y_ref, s_ref, *, c):
    t = pl.program_id(1)
    tb = p_ref.shape[0]
    half = RET_DK // 2
    hw = RET_H * RET_DK
    heads = range(RET_H)

    @pl.when(t == 0)
    def _():
        s_ref[...] = s0_ref[...]

    ri = lax.broadcasted_iota(jnp.int32, (c, c), 0)
    ci = lax.broadcasted_iota(jnp.int32, (c, c), 1)
    diff = (ri - ci).astype(F32)
    causal = ri >= ci
    idx = lax.broadcasted_iota(jnp.int32, (c, RET_DK), 0).astype(F32)
    lds = [ld_ref[h] for h in heads]
    dmask = [jnp.where(causal, jnp.exp(ld * jnp.where(causal, diff, 0.0)), 0.0) for ld in lds]
    q_dec = [jnp.exp(ld * (idx + 1.0)) for ld in lds]
    k_dec = [jnp.exp(ld * (c - 1.0 - idx)) for ld in lds]
    s_dec = [jnp.exp(jnp.full((1, RET_DV), ld * c, F32)) for ld in lds]

    def rot(x, cos, sin):
        x1, x2 = x[:, :half], x[:, half:]
        return jnp.concatenate([x1 * cos - x2 * sin, x1 * sin + x2 * cos], axis=-1)

    def body(j, carry):
        rows = pl.ds(pl.multiple_of(j * c, c), c)
        cos, sin = cos_ref[rows, :], sin_ref[rows, :]
        q = [rot(p_ref[rows, h * RET_DK:(h + 1) * RET_DK], cos, sin) for h in heads]
        k = [rot(p_ref[rows, hw + h * RET_DK:hw + (h + 1) * RET_DK], cos, sin) * (RET_DK ** -0.5) for h in heads]
        vb = [p_ref[rows, 2 * hw + h * RET_DV:2 * hw + (h + 1) * RET_DV].astype(BF16) for h in heads]
        qb = [x.astype(BF16) for x in q]
        att = [(_dot_nt(qb[h], k[h].astype(BF16)) * dmask[h]).astype(BF16) for h in heads]
        s = [s_ref[0, h] for h in heads]
        o = [_dot(att[h], vb[h]) + _dot(qb[h], s[h].astype(BF16)) * q_dec[h] for h in heads]
        for h in heads:
            s_ref[0, h] = s_dec[h] * s[h] + _dot_tn((k[h] * k_dec[h]).astype(BF16), vb[h])
        for h in heads:
            g = p_ref[rows, 3 * hw + h * RET_DV:3 * hw + (h + 1) * RET_DV]
            y_ref[rows, h * RET_DV:(h + 1) * RET_DV] = (_rms(o[h]) * _silu(g)).astype(y_ref.dtype)
        return carry

    lax.fori_loop(0, tb // c, body, 0)


def _retention(p_ret, cos, sin, ld, s0, t, tb):
    m, width = p_ret.shape
    b = m // t
    c = min(CHUNK, t)
    nt = t // tb
    st_spec = pl.BlockSpec((1, RET_H, RET_DK, RET_DV), lambda bi, ti: (bi, 0, 0, 0))
    rope_spec = pl.BlockSpec((tb, RET_DK // 2), lambda bi, ti: (ti, 0))
    return pl.pallas_call(
        functools.partial(_ret_kernel, c=c),
        grid=(b, nt),
        in_specs=[pl.BlockSpec(memory_space=pltpu.SMEM),
                  pl.BlockSpec((tb, width), lambda bi, ti: (bi * nt + ti, 0)),
                  rope_spec, rope_spec, st_spec],
        out_specs=[pl.BlockSpec((tb, RET_H * RET_DV), lambda bi, ti: (bi * nt + ti, 0)), st_spec],
        out_shape=[jax.ShapeDtypeStruct((m, RET_H * RET_DV), BF16),
                   jax.ShapeDtypeStruct((b, RET_H, RET_DK, RET_DV), F32)],
        compiler_params=_cp("parallel", "arbitrary"),
        name="retention",
    )(ld, p_ret, cos, sin, s0)


def _gla_kernel(p_ref, lr_ref, wgk_ref, bgk_ref, nrm_ref, s0_ref, y_ref, s_ref, st_ref, *, c, nt):
    t = pl.program_id(1)
    tb = p_ref.shape[0]
    hk = GLA_H * GLA_DK
    hv = GLA_H * GLA_DV
    heads = range(GLA_H)

    @pl.when(t == 0)
    def _():
        for h in heads:
            st_ref[h] = s0_ref[0, h].T

    ri = lax.broadcasted_iota(jnp.int32, (c, c), 0)
    ci = lax.broadcasted_iota(jnp.int32, (c, c), 1)
    causal = ri >= ci
    tri = jnp.where(causal, 1.0, 0.0).astype(BF16)
    wgk = wgk_ref[...]
    bgk = bgk_ref[...]
    nrm = nrm_ref[...]

    def body(j, carry):
        rows = pl.ds(pl.multiple_of(j * c, c), c)
        z = _dot(lr_ref[rows, :].astype(BF16), wgk) + bgk
        gk = -_softplus(-z) / GLA_NORMALIZER
        bc = _tri_cumsum(tri, gk)
        q = p_ref[rows, 0:hk] * (GLA_DK ** -0.5)
        k = p_ref[rows, hk:2 * hk]
        qe_all = (q * jnp.exp(bc)).astype(BF16)
        ke_all = (k * jnp.exp(-bc)).astype(BF16)
        b_last = bc[c - 1:c, :]
        kd_all = (k * jnp.exp(b_last - bc)).astype(BF16)
        eb_last = jnp.exp(b_last)
        hs = lambda x, h: x[:, h * GLA_DK:(h + 1) * GLA_DK]
        vb = [p_ref[rows, 2 * hk + h * GLA_DV:2 * hk + (h + 1) * GLA_DV].astype(BF16) for h in heads]
        att = [jnp.where(causal, _dot_nt(hs(qe_all, h), hs(ke_all, h)), 0.0).astype(BF16) for h in heads]
        st = [st_ref[h] for h in heads]
        o = [_dot(att[h], vb[h]) + _dot_nt(hs(qe_all, h), st[h].astype(BF16)) for h in heads]
        for h in heads:
            st_ref[h] = hs(eb_last, h) * st[h] + _dot_tn(vb[h], hs(kd_all, h))
        for h in heads:
            g = p_ref[rows, 2 * hk + hv + h * GLA_DV:2 * hk + hv + (h + 1) * GLA_DV]
            y_ref[rows, h * GLA_DV:(h + 1) * GLA_DV] = (_rms(o[h], nrm) * _silu(g)).astype(y_ref.dtype)
        return carry

    lax.fori_loop(0, tb // c, body, 0)

    @pl.when(t == nt - 1)
    def _():
        for h in heads:
            s_ref[0, h] = st_ref[h].T


def _gla(p_gla, p_small, wgk, bgk, nrm, s0, t, tb):
    m, width = p_gla.shape
    b = m // t
    c = min(CHUNK, t)
    nt = t // tb
    const = lambda bi, ti: (0, 0)
    st_spec = pl.BlockSpec((1, GLA_H, GLA_DK, GLA_DV), lambda bi, ti: (bi, 0, 0, 0))
    return pl.pallas_call(
        functools.partial(_gla_kernel, c=c, nt=nt),
        grid=(b, nt),
        in_specs=[pl.BlockSpec((tb, width), lambda bi, ti: (bi * nt + ti, 0)),
                  pl.BlockSpec((tb, LANES), lambda bi, ti: (bi * nt + ti, 0)),
                  pl.BlockSpec((LANES, GLA_H * GLA_DK), const),
                  pl.BlockSpec((1, GLA_H * GLA_DK), const),
                  pl.BlockSpec((1, GLA_DV), const),
                  st_spec],
        out_specs=[pl.BlockSpec((tb, GLA_H * GLA_DV), lambda bi, ti: (bi * nt + ti, 0)), st_spec],
        out_shape=[jax.ShapeDtypeStruct((m, GLA_H * GLA_DV), BF16),
                   jax.ShapeDtypeStruct((b, GLA_H, GLA_DK, GLA_DV), F32)],
        scratch_shapes=[pltpu.VMEM((GLA_H, GLA_DV, GLA_DK), F32)],
        compiler_params=_cp("parallel", "arbitrary"),
        name="gla",
    )(p_gla, p_small, wgk, bgk, nrm, s0)


def _unit_lower_inverse_minus_eye(mats, c):
    ri = lax.broadcasted_iota(jnp.int32, (c, c), 0)
    ci = lax.broadcasted_iota(jnp.int32, (c, c), 1)

    def level_mask(k):
        same = (ri ^ ci) < 2 * k
        return same & ((ri & k) != 0) & ((ci & k) == 0)

    m1 = level_mask(1)
    ns = [-jnp.where(m1, a, 0.0) for a in mats]
    k = 2
    while k < c:
        mk = level_mask(k)
        ls = [jnp.where(mk, a, 0.0) for a in mats]
        nbs = [n.astype(BF16) for n in ns]
        ys = [l + _dot(nb, l.astype(BF16)) for l, nb in zip(ls, nbs)]
        xs = [y + _dot(y.astype(BF16), nb) for y, nb in zip(ys, nbs)]
        ns = [n - x for n, x in zip(ns, xs)]
        k *= 2
    return ns


def _gdn_kernel(p_ref, sm_ref, cw_ref, buf_ref, alog_ref, dtb_ref, nrm_ref, s0_ref,
                y_ref, s_ref, nbuf_ref, carry_ref, *, c, nt):
    t = pl.program_id(1)
    tb = p_ref.shape[0]
    hq = GDN_H * GDN_DK
    hv = GDN_H * GDN_DV
    nconv = 2 * hq + hv

    @pl.when(t == 0)
    def _():
        s_ref[...] = s0_ref[...]
        carry_ref[...] = buf_ref[0]

    ri = lax.broadcasted_iota(jnp.int32, (c, c), 0)
    ci = lax.broadcasted_iota(jnp.int32, (c, c), 1)
    incl = ri >= ci
    strict = ri > ci
    tri = jnp.where(incl, 1.0, 0.0).astype(BF16)
    lane = lax.broadcasted_iota(jnp.int32, (1, LANES), 1)
    a_lanes = jnp.logical_and(lane >= SMALL_A0, lane < SMALL_A0 + GDN_H)
    sel = jnp.where(lax.broadcasted_iota(jnp.int32, (SUBLANES, LANES), 1)
                    == lax.broadcasted_iota(jnp.int32, (SUBLANES, LANES), 0) + SMALL_A0,
                    1.0, 0.0).astype(BF16)
    neg_a = -jnp.exp(alog_ref[...])
    dtb = dtb_ref[...]
    nrm = nrm_ref[...]
    w0, w1, w2, w3 = (cw_ref[i:i + 1, :] for i in range(GDN_CONV))

    def col(x, lane_idx):
        return x[:, lane_idx:lane_idx + 1]

    def body(j, carry):
        rows = pl.ds(pl.multiple_of(j * c, c), c)
        x = p_ref[rows, 0:nconv]
        prev = carry_ref[...]
        conv = _shift_rows(x, prev, 3) * w0
        conv = conv + _shift_rows(x, prev, 2) * w1
        conv = conv + _shift_rows(x, prev, 1) * w2
        conv = conv + x * w3
        carry_ref[...] = x[c - SUBLANES:, :]
        qkv = _silu(conv)

        sm = sm_ref[rows, :]
        g_all = jnp.where(a_lanes, neg_a * _softplus(sm + dtb), 0.0)
        gam = _tri_cumsum(tri, g_all)
        ghi, gmid, glo = _split3(gam)
        gam_rows = _dot_nt(sel, ghi) + _dot_nt(sel, gmid) + _dot_nt(sel, glo)
        beta_all = _sigmoid(sm)
        eg_all = jnp.exp(gam)
        g_last_all = gam[c - 1:c, :]
        egl_all = jnp.exp(g_last_all - gam)
        eg_last_all = jnp.exp(g_last_all)

        heads = range(GDN_H)

        def l2n(x):
            return x * lax.rsqrt(jnp.sum(x * x, axis=-1, keepdims=True) + EPS)

        q = [l2n(qkv[:, h * GDN_DK:(h + 1) * GDN_DK]) * (GDN_DK ** -0.5) for h in heads]
        k = [l2n(qkv[:, hq + h * GDN_DK:hq + (h + 1) * GDN_DK]) for h in heads]
        v = [qkv[:, 2 * hq + h * GDN_DV:2 * hq + (h + 1) * GDN_DV] for h in heads]
        beta = [col(beta_all, SMALL_B0 + h) for h in heads]
        eg = [col(eg_all, SMALL_A0 + h) for h in heads]
        decay = [jnp.where(incl, jnp.exp(jnp.where(incl, col(gam, SMALL_A0 + h) - gam_rows[h:h + 1, :], 0.0)), 0.0)
                 for h in heads]
        kb = [x.astype(BF16) for x in k]
        qk_kk = [_dot_nt(jnp.concatenate([q[h], k[h]], axis=0).astype(BF16), kb[h]) for h in heads]
        att = [(qk_kk[h][:c] * decay[h]).astype(BF16) for h in heads]
        a = [jnp.where(strict, beta[h] * qk_kk[h][c:] * decay[h], 0.0) for h in heads]
        n = _unit_lower_inverse_minus_eye(a, c)
        rhs = [jnp.concatenate([beta[h] * v[h], (beta[h] * eg[h]) * k[h]], axis=-1) for h in heads]
        sol = [rhs[h] + _dot(n[h].astype(BF16), rhs[h].astype(BF16)) for h in heads]
        s = [s_ref[0, h] for h in heads]
        wq_s = [_dot(jnp.concatenate([sol[h][:, GDN_DV:], q[h]], axis=0).astype(BF16), s[h].astype(BF16))
                for h in heads]
        wb = [(sol[h][:, :GDN_DV] - wq_s[h][:c]).astype(BF16) for h in heads]
        o = [eg[h] * wq_s[h][c:] + _dot(att[h], wb[h]) for h in heads]
        kd = [(k[h] * col(egl_all, SMALL_A0 + h)).astype(BF16) for h in heads]
        for h in heads:
            s_ref[0, h] = col(eg_last_all, SMALL_A0 + h) * s[h] + _dot_tn(kd[h], wb[h])
        for h in heads:
            z = p_ref[rows, nconv + h * GDN_DV:nconv + (h + 1) * GDN_DV]
            y = _rms(o[h], nrm) * _silu(z)
            y_ref[rows, h * GDN_DV:(h + 1) * GDN_DV] = y.astype(y_ref.dtype)
        return carry

    lax.fori_loop(0, tb // c, body, 0)

    @pl.when(t == nt - 1)
    def _():
        nbuf_ref[0] = carry_ref[...]


def _gdn(p_gdn, p_small, conv_w, buf8, alog_row, dtb_row, nrm, s0, t, tb):
    m, width = p_gdn.shape
    b = m // t
    c = min(CHUNK, t)
    nt = t // tb
    nconv = conv_w.shape[1]
    st_spec = pl.BlockSpec((1, GDN_H, GDN_DK, GDN_DV), lambda bi, ti: (bi, 0, 0, 0))
    buf_spec = pl.BlockSpec((1, SUBLANES, nconv), lambda bi, ti: (bi, 0, 0))
    const = lambda bi, ti: (0, 0)
    return pl.pallas_call(
        functools.partial(_gdn_kernel, c=c, nt=nt),
        grid=(b, nt),
        in_specs=[pl.BlockSpec((tb, width), lambda bi, ti: (bi * nt + ti, 0)),
                  pl.BlockSpec((tb, LANES), lambda bi, ti: (bi * nt + ti, 0)),
                  pl.BlockSpec((GDN_CONV, nconv), const),
                  buf_spec,
                  pl.BlockSpec((1, LANES), const),
                  pl.BlockSpec((1, LANES), const),
                  pl.BlockSpec((1, GDN_DV), const),
                  st_spec],
        out_specs=[pl.BlockSpec((tb, GDN_H * GDN_DV), lambda bi, ti: (bi * nt + ti, 0)),
                   st_spec, buf_spec],
        out_shape=[jax.ShapeDtypeStruct((m, GDN_H * GDN_DV), BF16),
                   jax.ShapeDtypeStruct((b, GDN_H, GDN_DK, GDN_DV), F32),
                   jax.ShapeDtypeStruct((b, SUBLANES, nconv), F32)],
        scratch_shapes=[pltpu.VMEM((SUBLANES, nconv), F32)],
        compiler_params=_cp("parallel", "arbitrary"),
        name="gated_deltanet",
    )(p_gdn, p_small, conv_w, buf8, alog_row, dtb_row, nrm, s0)


def _pad_rows_front(buf, rows):
    return jnp.pad(buf, ((0, 0), (0, 0), (rows - buf.shape[2], 0), (0, 0)))


def _in_proj_windows(d):
    sizes = dict(ret=4 * RET_H * RET_DK, gla=2 * GLA_H * GLA_DK + 2 * GLA_H * GLA_DV, lr=GLA_RANK,
                 gdn=2 * GDN_H * GDN_DK + 2 * GDN_H * GDN_DV, ab=2 * GDN_H, gbr=N_BRANCH * d)
    win, o = {}, 0
    for name in ('ret', 'gla', 'lr', 'gdn', 'ab', 'gbr'):
        win[name] = (o, sizes[name])
        o += sizes[name]
    assert win['lr'][0] % LANES == 0 and win['ab'][0] % LANES == SMALL_A0
    return win


def _prep_weights(w):
    lane_row = lambda v, at: jnp.pad(v, ((0, 0), (at, LANES - at - v.shape[1])))[:, None, :]
    return dict(
        w_in=w['w_in'], w_xq=w['w_xq'], w_ffn_in=w['w_ffn_in'],
        w_gk=jnp.pad(w['w_gla_gk'], ((0, 0), (0, LANES - GLA_RANK), (0, 0))).astype(BF16),
        b_gk=w['b_gla_gk'][:, None, :],
        gla_norm=w['gla_norm'][:, None, :], gdn_norm=w['gdn_norm'][:, None, :],
        alog_row=lane_row(w['gdn_a_log'], SMALL_A0), dtb_row=lane_row(w['gdn_dt_bias'], SMALL_A0),
        gdn_conv_w=w['gdn_conv_w'], b_gate=w['b_gate'],
        w_branch=w['w_branch'].astype(BF16), w_out=w['w_out'].astype(BF16), w_xo=w['w_xo'].astype(BF16),
        ffn_conv_w=w['ffn_conv_w'], ffn_conv_b=w['ffn_conv_b'], w_ffn_out=w['w_ffn_out'].astype(BF16),
        ln_mix=w['ln_mix'], ln_xattn=w['ln_xattn'], ln_ffn=w['ln_ffn'], ln_final=w['ln_final'],
    )


def _tile(n, cap):
    t = min(n, cap)
    while n % t:
        t -= 1
    return t


def _trunk(x, offset, mem_k, mem_v, s_ret, s_gla, s_gdn, buf_gdn, buf_ffn, pw):
    b, t, d = x.shape
    m = b * t
    depth = s_ret.shape[0]
    x2 = x.reshape(m, d)
    tm = _tile(m, MM_TM)
    tm_res = _tile(m, RES_TM)
    tm_res_k = _tile(m, RES_TM_K)
    tm_ffo = _tile(m, FFO_TM)
    tb = _tile(t, SEQ_TB)
    tb_gdn = _tile(t, SEQ_TB_GDN)
    tq = _tile(t, XATTN_TQ)
    tm_ffn = _tile(t, FFN_TM)

    pos = offset + jnp.arange(t, dtype=F32)
    half = RET_DK // 2
    freqs = 1.0 / (ROPE_BASE ** (jnp.arange(half, dtype=F32) / half))
    ang = pos[:, None] * freqs[None, :]
    cos, sin = jnp.cos(ang), jnp.sin(ang)
    ld = jnp.log1p(-jnp.exp2(-5.0 - jnp.arange(RET_H, dtype=F32)))

    buf_gdn8 = _pad_rows_front(buf_gdn, SUBLANES)
    buf_ffn8 = _pad_rows_front(buf_ffn, SUBLANES)
    win = _in_proj_windows(d)

    def in_proj(a, l, name):
        off, n = win[name]
        blk0 = off // MM_TN
        shift = off - blk0 * MM_TN
        assert shift < LANES and n % MM_TN == 0
        return _mm_ws(a, pw['w_in'], l, blk0, n, shift, F32, tm, MM_TN, "in_proj_" + name)

    outs = ([], [], [], [], [])
    hn = _norm(x2, pw['ln_mix'][0], tm_res_k)
    y = None
    for l in range(depth):
        p_ret = in_proj(hn, l, 'ret')
        p_gla = in_proj(hn, l, 'gla')
        p_gdn = in_proj(hn, l, 'gdn')
        g_br = in_proj(hn, l, 'gbr')
        p_small = _small_proj(hn, pw['w_in'], l, win['lr'][0] // LANES, win['ab'][0] // LANES, tm)

        y_ret, sr = _retention(p_ret, cos, sin, ld, s_ret[l], t, tb)
        y_gla, sg = _gla(p_gla, p_small, pw['w_gk'][l], pw['b_gk'][l], pw['gla_norm'][l], s_gla[l], t, tb)
        y_gdn, sd, bg8 = _gdn(p_gdn, p_small, pw['gdn_conv_w'][l], buf_gdn8[l], pw['alog_row'][l],
                              pw['dtb_row'][l], pw['gdn_norm'][l], s_gdn[l], t, tb_gdn)

        merged = _merge((y_ret, y_gla, y_gdn), pw['w_branch'][l], g_br, pw['b_gate'][l], tm_res_k, MM_TN)
        x2, hx = _proj_res(merged, pw['w_out'][l], x2, pw['ln_xattn'][l], tm_res, ROW_SUB, False, "out_proj")

        q = _mm_ws(hx, pw['w_xq'], l, 0, d, 0, BF16, tm, MM_TN, "xattn_q")
        o = _xattn(q, mem_k[l], mem_v[l], t, tq)
        x2, hf = _proj_res(o, pw['w_xo'][l], x2, pw['ln_ffn'][l], tm_res, ROW_SUB, False, "xattn_out")

        act, bf8 = _ffn_in(hf, pw['w_ffn_in'], l, pw['ffn_conv_w'][l], pw['ffn_conv_b'][l],
                           buf_ffn8[l], t, tm_ffn, FFN_TN)
        if l + 1 < depth:
            x2, hn = _proj_res(act, pw['w_ffn_out'][l], x2, pw['ln_mix'][l + 1], tm_ffo, ROW_SUB, False, "ffn_out")
        else:
            y, = _proj_res(act, pw['w_ffn_out'][l], x2, pw['ln_final'], tm_ffo, ROW_SUB, True, "ffn_out_final")

        for lst, val in zip(outs, (sr, sg, sd, bg8[:, SUBLANES - (GDN_CONV - 1):], bf8[:, SUBLANES - (FFN_CONV - 1):])):
            lst.append(val)
    return (y.reshape(b, t, d),) + tuple(jnp.stack(lst) for lst in outs)


def kernel(x_prompt, x_sample, mem_prompt, state_ret, state_gla, state_gdn, state_gdn_conv, state_ffn_conv, cache_mem_k, cache_mem_v, ln_mix, w_in, w_gla_gk, b_gla_gk, gla_norm, gdn_conv_w, gdn_a_log, gdn_dt_bias, gdn_norm, b_gate, w_branch, w_out, ln_xattn, ln_mem, w_xq, w_xkv, w_xo, ln_ffn, w_ffn_in, ffn_conv_w, ffn_conv_b, w_ffn_out, ln_final):
    pw = _prep_weights(dict(
        w_in=w_in, w_gla_gk=w_gla_gk, b_gla_gk=b_gla_gk, gla_norm=gla_norm, gdn_conv_w=gdn_conv_w,
        gdn_a_log=gdn_a_log, gdn_dt_bias=gdn_dt_bias, gdn_norm=gdn_norm, b_gate=b_gate, w_branch=w_branch,
        w_out=w_out, w_xq=w_xq, w_xo=w_xo, w_ffn_in=w_ffn_in, ffn_conv_w=ffn_conv_w, ffn_conv_b=ffn_conv_b,
        w_ffn_out=w_ffn_out, ln_mix=ln_mix, ln_xattn=ln_xattn, ln_ffn=ln_ffn, ln_final=ln_final))
    depth = w_in.shape[0]
    bp, nmem, d = mem_prompt.shape
    dt = x_prompt.dtype

    mem2 = mem_prompt.reshape(bp * nmem, d)
    tmem = _tile(bp * nmem, MM_TM)
    mk, mv = [], []
    for l in range(depth):
        hm = _norm(mem2, ln_mem[l], _tile(bp * nmem, RES_TM_K))
        mk.append(_mm_ws(hm, w_xkv, l, 0, d, 0, F32, tmem, MM_TN, "mem_k"))
        mv.append(_mm_ws(hm, w_xkv, l, d // MM_TN, d, 0, F32, tmem, MM_TN, "mem_v"))
    dh = d // X_H
    mem_k_p = jnp.stack(mk).reshape(depth, bp, nmem, X_H, dh)
    mem_v_p = jnp.stack(mv).reshape(depth, bp, nmem, X_H, dh)

    zeros = lambda *s: jnp.zeros((depth, bp) + s, dt)
    y_p, ret_p, gla_p, gdn_p, gdn_conv_p, ffn_conv_p = _trunk(
        x_prompt, 0.0, [a.reshape(bp, nmem, d) for a in mk], [a.reshape(bp, nmem, d) for a in mv],
        zeros(RET_H, RET_DK, RET_DV), zeros(GLA_H, GLA_DK, GLA_DV), zeros(GDN_H, GDN_DK, GDN_DV),
        zeros(GDN_CONV - 1, state_gdn_conv.shape[-1]), zeros(FFN_CONV - 1, state_ffn_conv.shape[-1]), pw)

    past_len = 4096.0
    bs = x_sample.shape[0]
    y_s, ret_s, gla_s, gdn_s, gdn_conv_s, ffn_conv_s = _trunk(
        x_sample, past_len,
        [cache_mem_k[l].reshape(bs, nmem, d) for l in range(depth)],
        [cache_mem_v[l].reshape(bs, nmem, d) for l in range(depth)],
        state_ret, state_gla, state_gdn, state_gdn_conv, state_ffn_conv, pw)

    return (y_p, y_s, ret_p, gla_p, gdn_p, gdn_conv_p, ffn_conv_p, mem_k_p, mem_v_p,
            ret_s, gla_s, gdn_s, gdn_conv_s, ffn_conv_s)
```

```python
import functools

import jax
import jax.numpy as jnp
from jax import lax
from jax.experimental import pallas as pl
from jax.experimental.pallas import tpu as pltpu

F32 = jnp.float32
BF16 = jnp.bfloat16

EPS = 1e-6
CHUNK = 64
RET_H, RET_DK, RET_DV = 4, 256, 256
GLA_H, GLA_DK, GLA_DV = 4, 128, 256
GLA_RANK = 16
GLA_NORMALIZER = 16.0
GDN_H, GDN_DK, GDN_DV = 8, 128, 128
GDN_CONV = 4
N_BRANCH = 3
X_H = 4
FFN_CONV = 3
ROPE_BASE = 10000.0

LANES = 128
SUBLANES = 8
SMALL_A0 = GLA_RANK
SMALL_B0 = GLA_RANK + GDN_H
VMEM_LIMIT = 52 * 1024 * 1024
MM_TM = 1024
MM_TN = 1024
RES_TM = 512
RES_TM_K = 512
FFO_TM = 256
ROW_SUB = 256
SEQ_TB = 512
SEQ_TB_GDN = 256
XATTN_TQ = 512
FFN_TM = 1024
FFN_TN = 512
FFN_SUB = 256


def _cp(*sem):
    return pltpu.CompilerParams(dimension_semantics=sem, vmem_limit_bytes=VMEM_LIMIT)


def _dot(a, b):
    return jnp.dot(a, b, preferred_element_type=F32)


def _dot_nt(a, b):
    return lax.dot_general(a, b, (((1,), (1,)), ((), ())), preferred_element_type=F32)


def _dot_tn(a, b):
    return lax.dot_general(a, b, (((0,), (0,)), ((), ())), preferred_element_type=F32)


def _sigmoid(x):
    return 1.0 / (1.0 + jnp.exp(-x))


def _silu(x):
    return x * _sigmoid(x)


def _softplus(x):
    return jnp.maximum(x, 0.0) + jnp.log1p(jnp.exp(-jnp.abs(x)))


def _rms(x, g=None):
    y = x * lax.rsqrt(jnp.mean(x * x, axis=-1, keepdims=True) + EPS)
    return y if g is None else y * g


def _split3(x):
    hi = x.astype(BF16)
    r = x - hi.astype(F32)
    mid = r.astype(BF16)
    lo = (r - mid.astype(F32)).astype(BF16)
    return hi, mid, lo


def _tri_cumsum(tri, x):
    hi, mid, lo = _split3(x)
    return _dot(tri, hi) + _dot(tri, mid) + _dot(tri, lo)


def _norm_kernel(x_ref, g_ref, o_ref):
    o_ref[...] = _rms(x_ref[...], g_ref[...]).astype(o_ref.dtype)


def _norm(x, g, tm):
    m, d = x.shape
    return pl.pallas_call(
        _norm_kernel,
        grid=(m // tm,),
        in_specs=[pl.BlockSpec((tm, d), lambda i: (i, 0)),
                  pl.BlockSpec((1, d), lambda i: (0, 0))],
        out_specs=pl.BlockSpec((tm, d), lambda i: (i, 0)),
        out_shape=jax.ShapeDtypeStruct((m, d), BF16),
        compiler_params=_cp("parallel"),
        name="rmsnorm",
    )(x, g.reshape(1, d))


CAST_ROWS = 256


def _cast_weight_tile(w_ref, wx_ref, wb_ref, shift):
    kk, tn = wb_ref.shape
    for r in range(0, kk, CAST_ROWS):
        rows = slice(r, min(r + CAST_ROWS, kk))
        if shift == 0:
            wb_ref[rows, :] = w_ref[0, rows, :].astype(BF16)
        else:
            wide = jnp.concatenate([w_ref[0, rows, :], wx_ref[0, rows, :]], axis=1)
            wb_ref[rows, :] = pltpu.roll(wide, wide.shape[1] - shift, 1)[:, :tn].astype(BF16)


def _cast_weight_tile_t(w_ref, wx_ref, wb_ref, shift):
    tn, _ = wb_ref.shape
    for r in range(0, tn, CAST_ROWS):
        lo, hi = r + shift, min(r + CAST_ROWS, tn) + shift
        if hi <= tn:
            src = w_ref[0, lo:hi, :]
        else:
            src = jnp.concatenate([w_ref[0, lo:tn, :], wx_ref[0, 0:hi - tn, :]], axis=0)
        wb_ref[r:min(r + CAST_ROWS, tn), :] = src.astype(BF16)


def _mm_ws_kernel(a_ref, w_ref, *rest, shift, transposed):
    if shift:
        wx_ref, o_ref, wb_ref = rest
    else:
        wx_ref = None
        o_ref, wb_ref = rest

    @pl.when(pl.program_id(1) == 0)
    def _():
        (_cast_weight_tile_t if transposed else _cast_weight_tile)(w_ref, wx_ref, wb_ref, shift)

    mm = _dot_nt if transposed else _dot
    o_ref[...] = mm(a_ref[...], wb_ref[...]).astype(o_ref.dtype)


def _mm_ws(a, w, layer, blk0, n, shift, out_dtype, tm, tn, name, transposed=False):
    m, k = a.shape
    nj = n // tn
    per = tn // LANES
    if transposed:
        assert shift % SUBLANES == 0
        w_spec = pl.BlockSpec((1, tn, k), lambda j, i: (layer, blk0 + j, 0))
        wx_spec = pl.BlockSpec((1, LANES, k), lambda j, i: (layer, (blk0 + j + 1) * per, 0))
    else:
        w_spec = pl.BlockSpec((1, k, tn), lambda j, i: (layer, 0, blk0 + j))
        wx_spec = pl.BlockSpec((1, k, LANES), lambda j, i: (layer, 0, (blk0 + j + 1) * per))
    in_specs = [pl.BlockSpec((tm, k), lambda j, i: (i, 0)), w_spec]
    args = [a, w]
    if shift:
        in_specs.append(wx_spec)
        args.append(w)
    return pl.pallas_call(
        functools.partial(_mm_ws_kernel, shift=shift, transposed=transposed),
        grid=(nj, m // tm),
        in_specs=in_specs,
        out_specs=pl.BlockSpec((tm, tn), lambda j, i: (i, j)),
        out_shape=jax.ShapeDtypeStruct((m, n), out_dtype),
        scratch_shapes=[pltpu.VMEM((tn, k) if transposed else (k, tn), BF16)],
        compiler_params=_cp("parallel", "arbitrary"),
        name=name,
    )(*args)


def _small_proj_kernel(a_ref, w0_ref, w1_ref, o_ref):
    a = a_ref[...]
    p0 = _dot_nt(a, w0_ref[0].astype(BF16))
    p1 = _dot_nt(a, w1_ref[0].astype(BF16))
    lane = lax.broadcasted_iota(jnp.int32, p0.shape, 1)
    o_ref[...] = jnp.where(lane < SMALL_A0, p0, p1)


def _small_proj(a, w_t, layer, blk_lr, blk_ab, tm):
    m, k = a.shape
    return pl.pallas_call(
        _small_proj_kernel,
        grid=(m // tm,),
        in_specs=[pl.BlockSpec((tm, k), lambda i: (i, 0)),
                  pl.BlockSpec((1, LANES, k), lambda i: (layer, blk_lr, 0)),
                  pl.BlockSpec((1, LANES, k), lambda i: (layer, blk_ab, 0))],
        out_specs=pl.BlockSpec((tm, LANES), lambda i: (i, 0)),
        out_shape=jax.ShapeDtypeStruct((m, LANES), F32),
        compiler_params=_cp("parallel"),
        name="in_proj_small",
    )(a, w_t, w_t)


def _proj_res_kernel(a_ref, w_ref, x_ref, g_ref, *outs, sub, final):
    tm = a_ref.shape[0]
    w = w_ref[...]
    g = g_ref[...]
    for r in range(tm // sub):
        rows = slice(r * sub, (r + 1) * sub)
        xn = x_ref[rows, :] + _dot(a_ref[rows, :], w)
        if final:
            outs[0][rows, :] = _rms(xn, g)
        else:
            outs[0][rows, :] = xn
            outs[1][rows, :] = _rms(xn, g).astype(outs[1].dtype)


def _proj_res(a, w, x, g, tm, sub, final, name):
    m, kk = a.shape
    d = w.shape[1]
    row = lambda i: (i, 0)
    out_specs = [pl.BlockSpec((tm, d), row)]
    out_shape = [jax.ShapeDtypeStruct((m, d), F32)]
    if not final:
        out_specs.append(pl.BlockSpec((tm, d), row))
        out_shape.append(jax.ShapeDtypeStruct((m, d), BF16))
    return pl.pallas_call(
        functools.partial(_proj_res_kernel, sub=min(sub, tm), final=final),
        grid=(m // tm,),
        in_specs=[pl.BlockSpec((tm, kk), row),
                  pl.BlockSpec((kk, d), lambda i: (0, 0), pipeline_mode=pl.Buffered(1)),
                  pl.BlockSpec((tm, d), row),
                  pl.BlockSpec((1, d), lambda i: (0, 0))],
        out_specs=out_specs,
        out_shape=out_shape,
        compiler_params=_cp("parallel"),
        name=name,
    )(a, w, x, g.reshape(1, d))


def _merge_kernel(y0_ref, y1_ref, y2_ref, w_ref, g_ref, b_ref, o_ref, *, sub):
    tm, d = o_ref.shape
    for r in range(tm // sub):
        rows = slice(r * sub, (r + 1) * sub)
        acc = None
        for n, y_ref in enumerate((y0_ref, y1_ref, y2_ref)):
            p = _dot(y_ref[rows, :], w_ref[n])
            g = g_ref[rows, n * d:(n + 1) * d].astype(F32)
            t = _sigmoid(g + b_ref[n:n + 1, :]) * p
            acc = t if acc is None else acc + t
        o_ref[rows, :] = acc.astype(o_ref.dtype)


def _merge(ys, w_branch, g_br, b_gate, tm):
    m, bw = ys[0].shape
    d = w_branch.shape[2]
    row = lambda i: (i, 0)
    y_spec = pl.BlockSpec((tm, bw), row)
    return pl.pallas_call(
        functools.partial(_merge_kernel, sub=min(tm, ROW_SUB)),
        grid=(m // tm,),
        in_specs=[y_spec, y_spec, y_spec,
                  pl.BlockSpec((N_BRANCH, bw, d), lambda i: (0, 0, 0), pipeline_mode=pl.Buffered(1)),
                  pl.BlockSpec((tm, N_BRANCH * d), row),
                  pl.BlockSpec((N_BRANCH, d), lambda i: (0, 0))],
        out_specs=pl.BlockSpec((tm, d), row),
        out_shape=jax.ShapeDtypeStruct((m, d), BF16),
        compiler_params=_cp("parallel"),
        name="branch_merge",
    )(ys[0], ys[1], ys[2], w_branch, g_br, b_gate)


def _xattn_kernel(q_ref, k_ref, v_ref, o_ref, *, scale):
    q = q_ref[...]
    k = k_ref[0].astype(BF16)
    v = v_ref[0].astype(BF16)
    s = _dot_nt(q, k) * scale
    s = s - jnp.max(s, axis=-1, keepdims=True)
    e = jnp.exp(s)
    p = e / jnp.sum(e, axis=-1, keepdims=True)
    o_ref[...] = _dot(p.astype(BF16), v).astype(o_ref.dtype)


def _xattn(q, mem_k, mem_v, t, tq):
    m, d = q.shape
    b = m // t
    nmem = mem_k.shape[1]
    dh = d // X_H
    nq = t // tq
    kv_spec = pl.BlockSpec((1, nmem, dh), lambda bi, i, h: (bi, 0, h))
    qo_spec = pl.BlockSpec((tq, dh), lambda bi, i, h: (bi * nq + i, h))
    return pl.pallas_call(
        functools.partial(_xattn_kernel, scale=dh ** -0.5),
        grid=(b, nq, X_H),
        in_specs=[qo_spec, kv_spec, kv_spec],
        out_specs=qo_spec,
        out_shape=jax.ShapeDtypeStruct((m, d), BF16),
        compiler_params=_cp("parallel", "parallel", "arbitrary"),
        name="xattn",
    )(q, mem_k, mem_v)


def _gelu_tanh(x):
    c = 0.7978845608028654
    return x * (0.5 * (1.0 + jnp.tanh(c * (x + 0.044715 * (x * x * x)))))


def _shift_rows(x, prev8, s):
    xs = pltpu.roll(x, s, 0)
    ps = pltpu.roll(prev8, s, 0)
    r8 = lax.broadcasted_iota(jnp.int32, prev8.shape, 0)
    top = jnp.where(r8 < s, ps, xs[:SUBLANES])
    if x.shape[0] == SUBLANES:
        return top
    return jnp.concatenate([top, xs[SUBLANES:]], axis=0)


def _ffn_in_kernel(h_ref, wg_ref, wu_ref, cw_ref, cb_ref, buf_ref, act_ref, nbuf_ref,
                   wgb_ref, wub_ref, carry_ref, *, tiles_per_seq, seq_rows):
    i = pl.program_id(1)
    tm = h_ref.shape[0]
    sub = min(tm, FFN_SUB)
    seg = sub if seq_rows is None else seq_rows

    @pl.when(i == 0)
    def _():
        _cast_weight_tile(wg_ref, None, wgb_ref, 0)
        _cast_weight_tile(wu_ref, None, wub_ref, 0)

    if seq_rows is None:
        @pl.when(i % tiles_per_seq == 0)
        def _():
            carry_ref[...] = buf_ref[0]

        prev = carry_ref[...]

    wg, wu = wgb_ref[...], wub_ref[...]
    w0, w1, w2, cb = cw_ref[0:1, :], cw_ref[1:2, :], cw_ref[2:3, :], cb_ref[...]
    for r in range(tm // sub):
        h = h_ref[r * sub:(r + 1) * sub, :]
        gate_all = _dot(h, wg)
        up_all = _dot(h, wu)
        for s in range(sub // seg):
            lo = r * sub + s * seg
            gate = gate_all[s * seg:(s + 1) * seg]
            if seq_rows is not None:
                prev = buf_ref[lo // seg]
            y = _shift_rows(gate, prev, 2) * w0
            y = y + _shift_rows(gate, prev, 1) * w1
            y = y + gate * w2
            act = _gelu_tanh(y + cb) * up_all[s * seg:(s + 1) * seg]
            act_ref[lo:lo + seg, :] = act.astype(act_ref.dtype)
            prev = gate[seg - SUBLANES:, :]
            if seq_rows is not None:
                nbuf_ref[lo // seg] = prev
    if seq_rows is None:
        carry_ref[...] = prev
        nbuf_ref[0] = prev


def _ffn_in(h, w_ffn_in, layer, conv_w, conv_b, buf8, t, tm, tn):
    m, d = h.shape
    dff = w_ffn_in.shape[2] // 2
    ni, nj = m // tm, dff // tn
    if t >= tm:
        tps, seq_rows, nb = t // tm, None, 1
        buf_ix = lambda j, i: (i // tps, 0, j)
    else:
        tps, seq_rows, nb = 1, t, tm // t
        buf_ix = lambda j, i: (i, 0, j)
    act, tails = pl.pallas_call(
        functools.partial(_ffn_in_kernel, tiles_per_seq=tps, seq_rows=seq_rows),
        grid=(nj, ni),
        in_specs=[pl.BlockSpec((tm, d), lambda j, i: (i, 0)),
                  pl.BlockSpec((1, d, tn), lambda j, i: (layer, 0, j)),
                  pl.BlockSpec((1, d, tn), lambda j, i: (layer, 0, nj + j)),
                  pl.BlockSpec((FFN_CONV, tn), lambda j, i: (0, j)),
                  pl.BlockSpec((1, tn), lambda j, i: (0, j)),
                  pl.BlockSpec((nb, SUBLANES, tn), buf_ix)],
        out_specs=[pl.BlockSpec((tm, tn), lambda j, i: (i, j)),
                   pl.BlockSpec((nb, SUBLANES, tn), lambda j, i: (i, 0, j))],
        out_shape=[jax.ShapeDtypeStruct((m, dff), BF16),
                   jax.ShapeDtypeStruct((ni * nb, SUBLANES, dff), F32)],
        scratch_shapes=[pltpu.VMEM((d, tn), BF16), pltpu.VMEM((d, tn), BF16),
                        pltpu.VMEM((SUBLANES, tn), F32)],
        compiler_params=_cp("parallel", "arbitrary"),
        name="ffn_in",
    )(h, w_ffn_in, w_ffn_in, conv_w, conv_b.reshape(1, dff), buf8)
    return act, tails[tps - 1::tps]


def _ret_kernel(ld_ref, p_ref, cos_ref, sin_ref, s0_ref, y_ref, s_ref, *, c):
    t = pl.program_id(1)
    tb = p_ref.shape[0]
    half = RET_DK // 2
    hw = RET_H * RET_DK
    heads = range(RET_H)

    @pl.when(t == 0)
    def _():
        s_ref[...] = s0_ref[...]

    ri = lax.broadcasted_iota(jnp.int32, (c, c), 0)
    ci = lax.broadcasted_iota(jnp.int32, (c, c), 1)
    diff = (ri - ci).astype(F32)
    causal = ri >= ci
    idx = lax.broadcasted_iota(jnp.int32, (c, RET_DK), 0).astype(F32)
    lds = [ld_ref[h] for h in heads]
    dmask = [jnp.where(causal, jnp.exp(ld * jnp.where(causal, diff, 0.0)), 0.0) for ld in lds]
    q_dec = [jnp.exp(ld * (idx + 1.0)) for ld in lds]
    k_dec = [jnp.exp(ld * (c - 1.0 - idx)) for ld in lds]
    s_dec = [jnp.exp(jnp.full((1, RET_DV), ld * c, F32)) for ld in lds]

    def rot(x, cos, sin):
        x1, x2 = x[:, :half], x[:, half:]
        return jnp.concatenate([x1 * cos - x2 * sin, x1 * sin + x2 * cos], axis=-1)

    def body(j, carry):
        rows = pl.ds(pl.multiple_of(j * c, c), c)
        cos, sin = cos_ref[rows, :], sin_ref[rows, :]
        q = [rot(p_ref[rows, h * RET_DK:(h + 1) * RET_DK], cos, sin) for h in heads]
        k = [rot(p_ref[rows, hw + h * RET_DK:hw + (h + 1) * RET_DK], cos, sin) * (RET_DK ** -0.5) for h in heads]
        vb = [p_ref[rows, 2 * hw + h * RET_DV:2 * hw + (h + 1) * RET_DV].astype(BF16) for h in heads]
        qb = [x.astype(BF16) for x in q]
        att = [(_dot_nt(qb[h], k[h].astype(BF16)) * dmask[h]).astype(BF16) for h in heads]
        s = [s_ref[0, h] for h in heads]
        o = [_dot(att[h], vb[h]) + _dot(qb[h], s[h].astype(BF16)) * q_dec[h] for h in heads]
        for h in heads:
            s_ref[0, h] = s_dec[h] * s[h] + _dot_tn((k[h] * k_dec[h]).astype(BF16), vb[h])
        for h in heads:
            g = p_ref[rows, 3 * hw + h * RET_DV:3 * hw + (h + 1) * RET_DV]
            y_ref[rows, h * RET_DV:(h + 1) * RET_DV] = (_rms(o[h]) * _silu(g)).astype(y_ref.dtype)
        return carry

    lax.fori_loop(0, tb // c, body, 0)


def _retention(p_ret, cos, sin, ld, s0, t, tb):
    m, width = p_ret.shape
    b = m // t
    c = min(CHUNK, t)
    nt = t // tb
    st_spec = pl.BlockSpec((1, RET_H, RET_DK, RET_DV), lambda bi, ti: (bi, 0, 0, 0))
    rope_spec = pl.BlockSpec((tb, RET_DK // 2), lambda bi, ti: (ti, 0))
    return pl.pallas_call(
        functools.partial(_ret_kernel, c=c),
        grid=(b, nt),
        in_specs=[pl.BlockSpec(memory_space=pltpu.SMEM),
                  pl.BlockSpec((tb, width), lambda bi, ti: (bi * nt + ti, 0)),
                  rope_spec, rope_spec, st_spec],
        out_specs=[pl.BlockSpec((tb, RET_H * RET_DV), lambda bi, ti: (bi * nt + ti, 0)), st_spec],
        out_shape=[jax.ShapeDtypeStruct((m, RET_H * RET_DV), BF16),
                   jax.ShapeDtypeStruct((b, RET_H, RET_DK, RET_DV), F32)],
        compiler_params=_cp("parallel", "arbitrary"),
        name="retention",
    )(ld, p_ret, cos, sin, s0)


def _gla_kernel(p_ref, lr_ref, wgk_ref, bgk_ref, nrm_ref, s0_ref, y_ref, s_ref, st_ref, *, c, nt):
    t = pl.program_id(1)
    tb = p_ref.shape[0]
    hk = GLA_H * GLA_DK
    hv = GLA_H * GLA_DV
    heads = range(GLA_H)

    @pl.when(t == 0)
    def _():
        for h in heads:
            st_ref[h] = s0_ref[0, h].T

    ri = lax.broadcasted_iota(jnp.int32, (c, c), 0)
    ci = lax.broadcasted_iota(jnp.int32, (c, c), 1)
    causal = ri >= ci
    tri = jnp.where(causal, 1.0, 0.0).astype(BF16)
    wgk = wgk_ref[...]
    bgk = bgk_ref[...]
    nrm = nrm_ref[...]

    def body(j, carry):
        rows = pl.ds(pl.multiple_of(j * c, c), c)
        z = _dot(lr_ref[rows, :].astype(BF16), wgk) + bgk
        gk = -_softplus(-z) / GLA_NORMALIZER
        bc = _tri_cumsum(tri, gk)
        q = p_ref[rows, 0:hk] * (GLA_DK ** -0.5)
        k = p_ref[rows, hk:2 * hk]
        qe_all = (q * jnp.exp(bc)).astype(BF16)
        ke_all = (k * jnp.exp(-bc)).astype(BF16)
        b_last = bc[c - 1:c, :]
        kd_all = (k * jnp.exp(b_last - bc)).astype(BF16)
        eb_last = jnp.exp(b_last)
        hs = lambda x, h: x[:, h * GLA_DK:(h + 1) * GLA_DK]
        vb = [p_ref[rows, 2 * hk + h * GLA_DV:2 * hk + (h + 1) * GLA_DV].astype(BF16) for h in heads]
        att = [jnp.where(causal, _dot_nt(hs(qe_all, h), hs(ke_all, h)), 0.0).astype(BF16) for h in heads]
        st = [st_ref[h] for h in heads]
        o = [_dot(att[h], vb[h]) + _dot_nt(hs(qe_all, h), st[h].astype(BF16)) for h in heads]
        for h in heads:
            st_ref[h] = hs(eb_last, h) * st[h] + _dot_tn(vb[h], hs(kd_all, h))
        for h in heads:
            g = p_ref[rows, 2 * hk + hv + h * GLA_DV:2 * hk + hv + (h + 1) * GLA_DV]
            y_ref[rows, h * GLA_DV:(h + 1) * GLA_DV] = (_rms(o[h], nrm) * _silu(g)).astype(y_ref.dtype)
        return carry

    lax.fori_loop(0, tb // c, body, 0)

    @pl.when(t == nt - 1)
    def _():
        for h in heads:
            s_ref[0, h] = st_ref[h].T


def _gla(p_gla, p_small, wgk, bgk, nrm, s0, t, tb):
    m, width = p_gla.shape
    b = m // t
    c = min(CHUNK, t)
    nt = t // tb
    const = lambda bi, ti: (0, 0)
    st_spec = pl.BlockSpec((1, GLA_H, GLA_DK, GLA_DV), lambda bi, ti: (bi, 0, 0, 0))
    return pl.pallas_call(
        functools.partial(_gla_kernel, c=c, nt=nt),
        grid=(b, nt),
        in_specs=[pl.BlockSpec((tb, width), lambda bi, ti: (bi * nt + ti, 0)),
                  pl.BlockSpec((tb, LANES), lambda bi, ti: (bi * nt + ti, 0)),
                  pl.BlockSpec((LANES, GLA_H * GLA_DK), const),
                  pl.BlockSpec((1, GLA_H * GLA_DK), const),
                  pl.BlockSpec((1, GLA_DV), const),
                  st_spec],
        out_specs=[pl.BlockSpec((tb, GLA_H * GLA_DV), lambda bi, ti: (bi * nt + ti, 0)), st_spec],
        out_shape=[jax.ShapeDtypeStruct((m, GLA_H * GLA_DV), BF16),
                   jax.ShapeDtypeStruct((b, GLA_H, GLA_DK, GLA_DV), F32)],
        scratch_shapes=[pltpu.VMEM((GLA_H, GLA_DV, GLA_DK), F32)],
        compiler_params=_cp("parallel", "arbitrary"),
        name="gla",
    )(p_gla, p_small, wgk, bgk, nrm, s0)


def _unit_lower_inverse_minus_eye(mats, c):
    ri = lax.broadcasted_iota(jnp.int32, (c, c), 0)
    ci = lax.broadcasted_iota(jnp.int32, (c, c), 1)

    def level_mask(k):
        same = (ri ^ ci) < 2 * k
        return same & ((ri & k) != 0) & ((ci & k) == 0)

    m1 = level_mask(1)
    ns = [-jnp.where(m1, a, 0.0) for a in mats]
    k = 2
    while k < c:
        mk = level_mask(k)
        ls = [jnp.where(mk, a, 0.0) for a in mats]
        nbs = [n.astype(BF16) for n in ns]
        ys = [l + _dot(nb, l.astype(BF16)) for l, nb in zip(ls, nbs)]
        xs = [y + _dot(y.astype(BF16), nb) for y, nb in zip(ys, nbs)]
        ns = [n - x for n, x in zip(ns, xs)]
        k *= 2
    return ns


def _gdn_kernel(p_ref, sm_ref, cw_ref, buf_ref, alog_ref, dtb_ref, nrm_ref, s0_ref,
                y_ref, s_ref, nbuf_ref, carry_ref, *, c, nt):
    t = pl.program_id(1)
    tb = p_ref.shape[0]
    hq = GDN_H * GDN_DK
    hv = GDN_H * GDN_DV
    nconv = 2 * hq + hv

    @pl.when(t == 0)
    def _():
        s_ref[...] = s0_ref[...]
        carry_ref[...] = buf_ref[0]

    ri = lax.broadcasted_iota(jnp.int32, (c, c), 0)
    ci = lax.broadcasted_iota(jnp.int32, (c, c), 1)
    incl = ri >= ci
    strict = ri > ci
    tri = jnp.where(incl, 1.0, 0.0).astype(BF16)
    lane = lax.broadcasted_iota(jnp.int32, (1, LANES), 1)
    a_lanes = jnp.logical_and(lane >= SMALL_A0, lane < SMALL_A0 + GDN_H)
    sel = jnp.where(lax.broadcasted_iota(jnp.int32, (SUBLANES, LANES), 1)
                    == lax.broadcasted_iota(jnp.int32, (SUBLANES, LANES), 0) + SMALL_A0,
                    1.0, 0.0).astype(BF16)
    neg_a = -jnp.exp(alog_ref[...])
    dtb = dtb_ref[...]
    nrm = nrm_ref[...]
    w0, w1, w2, w3 = (cw_ref[i:i + 1, :] for i in range(GDN_CONV))

    def col(x, lane_idx):
        return x[:, lane_idx:lane_idx + 1]

    def body(j, carry):
        rows = pl.ds(pl.multiple_of(j * c, c), c)
        x = p_ref[rows, 0:nconv]
        prev = carry_ref[...]
        conv = _shift_rows(x, prev, 3) * w0
        conv = conv + _shift_rows(x, prev, 2) * w1
        conv = conv + _shift_rows(x, prev, 1) * w2
        conv = conv + x * w3
        carry_ref[...] = x[c - SUBLANES:, :]
        qkv = _silu(conv)

        sm = sm_ref[rows, :]
        g_all = jnp.where(a_lanes, neg_a * _softplus(sm + dtb), 0.0)
        gam = _tri_cumsum(tri, g_all)
        ghi, gmid, glo = _split3(gam)
        gam_rows = _dot_nt(sel, ghi) + _dot_nt(sel, gmid) + _dot_nt(sel, glo)
        beta_all = _sigmoid(sm)
        eg_all = jnp.exp(gam)
        g_last_all = gam[c - 1:c, :]
        egl_all = jnp.exp(g_last_all - gam)
        eg_last_all = jnp.exp(g_last_all)

        heads = range(GDN_H)

        def l2n(x):
            return x * lax.rsqrt(jnp.sum(x * x, axis=-1, keepdims=True) + EPS)

        q = [l2n(qkv[:, h * GDN_DK:(h + 1) * GDN_DK]) * (GDN_DK ** -0.5) for h in heads]
        k = [l2n(qkv[:, hq + h * GDN_DK:hq + (h + 1) * GDN_DK]) for h in heads]
        v = [qkv[:, 2 * hq + h * GDN_DV:2 * hq + (h + 1) * GDN_DV] for h in heads]
        beta = [col(beta_all, SMALL_B0 + h) for h in heads]
        eg = [col(eg_all, SMALL_A0 + h) for h in heads]
        decay = [jnp.where(incl, jnp.exp(jnp.where(incl, col(gam, SMALL_A0 + h) - gam_rows[h:h + 1, :], 0.0)), 0.0)
                 for h in heads]
        kb = [x.astype(BF16) for x in k]
        qk_kk = [_dot_nt(jnp.concatenate([q[h], k[h]], axis=0).astype(BF16), kb[h]) for h in heads]
        att = [(qk_kk[h][:c] * decay[h]).astype(BF16) for h in heads]
        a = [jnp.where(strict, beta[h] * qk_kk[h][c:] * decay[h], 0.0) for h in heads]
        n = _unit_lower_inverse_minus_eye(a, c)
        rhs = [jnp.concatenate([beta[h] * v[h], (beta[h] * eg[h]) * k[h]], axis=-1) for h in heads]
        sol = [rhs[h] + _dot(n[h].astype(BF16), rhs[h].astype(BF16)) for h in heads]
        s = [s_ref[0, h] for h in heads]
        wq_s = [_dot(jnp.concatenate([sol[h][:, GDN_DV:], q[h]], axis=0).astype(BF16), s[h].astype(BF16))
                for h in heads]
        wb = [(sol[h][:, :GDN_DV] - wq_s[h][:c]).astype(BF16) for h in heads]
        o = [eg[h] * wq_s[h][c:] + _dot(att[h], wb[h]) for h in heads]
        kd = [(k[h] * col(egl_all, SMALL_A0 + h)).astype(BF16) for h in heads]
        for h in heads:
            s_ref[0, h] = col(eg_last_all, SMALL_A0 + h) * s[h] + _dot_tn(kd[h], wb[h])
        for h in heads:
            z = p_ref[rows, nconv + h * GDN_DV:nconv + (h + 1) * GDN_DV]
            y = _rms(o[h], nrm) * _silu(z)
            y_ref[rows, h * GDN_DV:(h + 1) * GDN_DV] = y.astype(y_ref.dtype)
        return carry

    lax.fori_loop(0, tb // c, body, 0)

    @pl.when(t == nt - 1)
    def _():
        nbuf_ref[0] = carry_ref[...]


def _gdn(p_gdn, p_small, conv_w, buf8, alog_row, dtb_row, nrm, s0, t, tb):
    m, width = p_gdn.shape
    b = m // t
    c = min(CHUNK, t)
    nt = t // tb
    nconv = conv_w.shape[1]
    st_spec = pl.BlockSpec((1, GDN_H, GDN_DK, GDN_DV), lambda bi, ti: (bi, 0, 0, 0))
    buf_spec = pl.BlockSpec((1, SUBLANES, nconv), lambda bi, ti: (bi, 0, 0))
    const = lambda bi, ti: (0, 0)
    return pl.pallas_call(
        functools.partial(_gdn_kernel, c=c, nt=nt),
        grid=(b, nt),
        in_specs=[pl.BlockSpec((tb, width), lambda bi, ti: (bi * nt + ti, 0)),
                  pl.BlockSpec((tb, LANES), lambda bi, ti: (bi * nt + ti, 0)),
                  pl.BlockSpec((GDN_CONV, nconv), const),
                  buf_spec,
                  pl.BlockSpec((1, LANES), const),
                  pl.BlockSpec((1, LANES), const),
                  pl.BlockSpec((1, GDN_DV), const),
                  st_spec],
        out_specs=[pl.BlockSpec((tb, GDN_H * GDN_DV), lambda bi, ti: (bi * nt + ti, 0)),
                   st_spec, buf_spec],
        out_shape=[jax.ShapeDtypeStruct((m, GDN_H * GDN_DV), BF16),
                   jax.ShapeDtypeStruct((b, GDN_H, GDN_DK, GDN_DV), F32),
                   jax.ShapeDtypeStruct((b, SUBLANES, nconv), F32)],
        scratch_shapes=[pltpu.VMEM((SUBLANES, nconv), F32)],
        compiler_params=_cp("parallel", "arbitrary"),
        name="gated_deltanet",
    )(p_gdn, p_small, conv_w, buf8, alog_row, dtb_row, nrm, s0)


def _pad_rows_front(buf, rows):
    return jnp.pad(buf, ((0, 0), (0, 0), (rows - buf.shape[2], 0), (0, 0)))


def _in_proj_windows(d):
    sizes = dict(ret=4 * RET_H * RET_DK, gla=2 * GLA_H * GLA_DK + 2 * GLA_H * GLA_DV, lr=GLA_RANK,
                 gdn=2 * GDN_H * GDN_DK + 2 * GDN_H * GDN_DV, ab=2 * GDN_H, gbr=N_BRANCH * d)
    win, o = {}, 0
    for name in ('ret', 'gla', 'lr', 'gdn', 'ab', 'gbr'):
        win[name] = (o, sizes[name])
        o += sizes[name]
    assert win['lr'][0] % LANES == 0 and win['ab'][0] % LANES == SMALL_A0
    return win


def _prep_weights(w):
    lane_row = lambda v, at: jnp.pad(v, ((0, 0), (at, LANES - at - v.shape[1])))[:, None, :]
    return dict(
        w_in_t=jnp.swapaxes(w['w_in'], 1, 2), w_xq=w['w_xq'], w_ffn_in=w['w_ffn_in'],
        w_gk=jnp.pad(w['w_gla_gk'], ((0, 0), (0, LANES - GLA_RANK), (0, 0))).astype(BF16),
        b_gk=w['b_gla_gk'][:, None, :],
        gla_norm=w['gla_norm'][:, None, :], gdn_norm=w['gdn_norm'][:, None, :],
        alog_row=lane_row(w['gdn_a_log'], SMALL_A0), dtb_row=lane_row(w['gdn_dt_bias'], SMALL_A0),
        gdn_conv_w=w['gdn_conv_w'], b_gate=w['b_gate'],
        w_branch=w['w_branch'].astype(BF16), w_out=w['w_out'].astype(BF16), w_xo=w['w_xo'].astype(BF16),
        ffn_conv_w=w['ffn_conv_w'], ffn_conv_b=w['ffn_conv_b'], w_ffn_out=w['w_ffn_out'].astype(BF16),
        ln_mix=w['ln_mix'], ln_xattn=w['ln_xattn'], ln_ffn=w['ln_ffn'], ln_final=w['ln_final'],
    )


def _tile(n, cap):
    t = min(n, cap)
    while n % t:
        t -= 1
    return t


def _trunk(x, offset, mem_k, mem_v, s_ret, s_gla, s_gdn, buf_gdn, buf_ffn, pw):
    b, t, d = x.shape
    m = b * t
    depth = s_ret.shape[0]
    x2 = x.reshape(m, d)
    tm = _tile(m, MM_TM)
    tm_res = _tile(m, RES_TM)
    tm_res_k = _tile(m, RES_TM_K)
    tm_ffo = _tile(m, FFO_TM)
    tb = _tile(t, SEQ_TB)
    tb_gdn = _tile(t, SEQ_TB_GDN)
    tq = _tile(t, XATTN_TQ)
    tm_ffn = _tile(t, FFN_TM) if t >= FFN_TM else t * _tile(b, max(1, FFN_TM // t))

    pos = offset + jnp.arange(t, dtype=F32)
    half = RET_DK // 2
    freqs = 1.0 / (ROPE_BASE ** (jnp.arange(half, dtype=F32) / half))
    ang = pos[:, None] * freqs[None, :]
    cos, sin = jnp.cos(ang), jnp.sin(ang)
    ld = jnp.log1p(-jnp.exp2(-5.0 - jnp.arange(RET_H, dtype=F32)))

    buf_gdn8 = _pad_rows_front(buf_gdn, SUBLANES)
    buf_ffn8 = _pad_rows_front(buf_ffn, SUBLANES)
    win = _in_proj_windows(d)

    def in_proj(a, l, name, out_dtype=F32):
        off, n = win[name]
        blk0 = off // MM_TN
        shift = off - blk0 * MM_TN
        assert shift < LANES and n % MM_TN == 0
        return _mm_ws(a, pw['w_in_t'], l, blk0, n, shift, out_dtype, tm, MM_TN, "in_proj_" + name,
                      transposed=True)

    outs = ([], [], [], [], [])
    hn = _norm(x2, pw['ln_mix'][0], tm_res_k)
    y = None
    for l in range(depth):
        p_ret = in_proj(hn, l, 'ret')
        p_gla = in_proj(hn, l, 'gla')
        p_gdn = in_proj(hn, l, 'gdn')
        g_br = in_proj(hn, l, 'gbr', BF16)
        p_small = _small_proj(hn, pw['w_in_t'], l, win['lr'][0] // LANES, win['ab'][0] // LANES, tm)

        y_ret, sr = _retention(p_ret, cos, sin, ld, s_ret[l], t, tb)
        y_gla, sg = _gla(p_gla, p_small, pw['w_gk'][l], pw['b_gk'][l], pw['gla_norm'][l], s_gla[l], t, tb)
        y_gdn, sd, bg8 = _gdn(p_gdn, p_small, pw['gdn_conv_w'][l], buf_gdn8[l], pw['alog_row'][l],
                              pw['dtb_row'][l], pw['gdn_norm'][l], s_gdn[l], t, tb_gdn)

        merged = _merge((y_ret, y_gla, y_gdn), pw['w_branch'][l], g_br, pw['b_gate'][l], tm_res_k)
        x2, hx = _proj_res(merged, pw['w_out'][l], x2, pw['ln_xattn'][l], tm_res, ROW_SUB, False, "out_proj")

        q = _mm_ws(hx, pw['w_xq'], l, 0, d, 0, BF16, tm, MM_TN, "xattn_q")
        o = _xattn(q, mem_k[l], mem_v[l], t, tq)
        x2, hf = _proj_res(o, pw['w_xo'][l], x2, pw['ln_ffn'][l], tm_res, ROW_SUB, False, "xattn_out")

        act, bf8 = _ffn_in(hf, pw['w_ffn_in'], l, pw['ffn_conv_w'][l], pw['ffn_conv_b'][l],
                           buf_ffn8[l], t, tm_ffn, FFN_TN)
        if l + 1 < depth:
            x2, hn = _proj_res(act, pw['w_ffn_out'][l], x2, pw['ln_mix'][l + 1], tm_ffo, ROW_SUB, False, "ffn_out")
        else:
            y, = _proj_res(act, pw['w_ffn_out'][l], x2, pw['ln_final'], tm_ffo, ROW_SUB, True, "ffn_out_final")

        for lst, val in zip(outs, (sr, sg, sd, bg8[:, SUBLANES - (GDN_CONV - 1):], bf8[:, SUBLANES - (FFN_CONV - 1):])):
            lst.append(val)
    return (y.reshape(b, t, d),) + tuple(jnp.stack(lst) for lst in outs)


def kernel(x_prompt, x_sample, mem_prompt, state_ret, state_gla, state_gdn, state_gdn_conv, state_ffn_conv, cache_mem_k, cache_mem_v, ln_mix, w_in, w_gla_gk, b_gla_gk, gla_norm, gdn_conv_w, gdn_a_log, gdn_dt_bias, gdn_norm, b_gate, w_branch, w_out, ln_xattn, ln_mem, w_xq, w_xkv, w_xo, ln_ffn, w_ffn_in, ffn_conv_w, ffn_conv_b, w_ffn_out, ln_final):
    pw = _prep_weights(dict(
        w_in=w_in, w_gla_gk=w_gla_gk, b_gla_gk=b_gla_gk, gla_norm=gla_norm, gdn_conv_w=gdn_conv_w,
        gdn_a_log=gdn_a_log, gdn_dt_bias=gdn_dt_bias, gdn_norm=gdn_norm, b_gate=b_gate, w_branch=w_branch,
        w_out=w_out, w_xq=w_xq, w_xo=w_xo, w_ffn_in=w_ffn_in, ffn_conv_w=ffn_conv_w, ffn_conv_b=ffn_conv_b,
        w_ffn_out=w_ffn_out, ln_mix=ln_mix, ln_xattn=ln_xattn, ln_ffn=ln_ffn, ln_final=ln_final))
    depth = w_in.shape[0]
    bp, nmem, d = mem_prompt.shape
    dt = x_prompt.dtype

    mem2 = mem_prompt.reshape(bp * nmem, d)
    tmem = _tile(bp * nmem, MM_TM)
    mk, mv = [], []
    for l in range(depth):
        hm = _norm(mem2, ln_mem[l], _tile(bp * nmem, RES_TM_K))
        mk.append(_mm_ws(hm, w_xkv, l, 0, d, 0, F32, tmem, MM_TN, "mem_k"))
        mv.append(_mm_ws(hm, w_xkv, l, d // MM_TN, d, 0, F32, tmem, MM_TN, "mem_v"))
    dh = d // X_H
    mem_k_p = jnp.stack(mk).reshape(depth, bp, nmem, X_H, dh)
    mem_v_p = jnp.stack(mv).reshape(depth, bp, nmem, X_H, dh)

    zeros = lambda *s: jnp.zeros((depth, bp) + s, dt)
    y_p, ret_p, gla_p, gdn_p, gdn_conv_p, ffn_conv_p = _trunk(
        x_prompt, 0.0, [a.reshape(bp, nmem, d) for a in mk], [a.reshape(bp, nmem, d) for a in mv],
        zeros(RET_H, RET_DK, RET_DV), zeros(GLA_H, GLA_DK, GLA_DV), zeros(GDN_H, GDN_DK, GDN_DV),
        zeros(GDN_CONV - 1, state_gdn_conv.shape[-1]), zeros(FFN_CONV - 1, state_ffn_conv.shape[-1]), pw)

    past_len = 4096.0
    bs = x_sample.shape[0]
    y_s, ret_s, gla_s, gdn_s, gdn_conv_s, ffn_conv_s = _trunk(
        x_sample, past_len,
        [cache_mem_k[l].reshape(bs, nmem, d) for l in range(depth)],
        [cache_mem_v[l].reshape(bs, nmem, d) for l in range(depth)],
        state_ret, state_gla, state_gdn, state_gdn_conv, state_ffn_conv, pw)

    return (y_p, y_s, ret_p, gla_p, gdn_p, gdn_conv_p, ffn_conv_p, mem_k_p, mem_v_p,
            ret_s, gla_s, gdn_s, gdn_conv_s, ffn_conv_s)
```

```python
import functools

import jax
import jax.numpy as jnp
from jax import lax
from jax.experimental import pallas as pl
from jax.experimental.pallas import tpu as pltpu

F32 = jnp.float32
BF16 = jnp.bfloat16

EPS = 1e-6
CHUNK = 64
RET_H, RET_DK, RET_DV = 4, 256, 256
GLA_H, GLA_DK, GLA_DV = 4, 128, 256
GLA_RANK = 16
GLA_NORMALIZER = 16.0
GDN_H, GDN_DK, GDN_DV = 8, 128, 128
GDN_CONV = 4
N_BRANCH = 3
X_H = 4
FFN_CONV = 3
ROPE_BASE = 10000.0

LANES = 128
SUBLANES = 8
SMALL_A0 = GLA_RANK
SMALL_B0 = GLA_RANK + GDN_H
VMEM_LIMIT = 52 * 1024 * 1024
MM_TM = 1024
MM_TN = 1024
RES_TM = 512
RES_TM_K = 512
FFO_TM = 256
ROW_SUB = 256
CONV_SUB = 256
SEQ_TB = 512
SEQ_TB_GDN = 256
GDN_CHUNKS_PER_STEP = 2
XATTN_TQ = 512
FFN_TM = 1024
FFN_TN = 512
FFN_SUB = 256


def _cp(*sem):
    return pltpu.CompilerParams(dimension_semantics=sem, vmem_limit_bytes=VMEM_LIMIT)


def _dot(a, b):
    return jnp.dot(a, b, preferred_element_type=F32)


def _dot_nt(a, b):
    return lax.dot_general(a, b, (((1,), (1,)), ((), ())), preferred_element_type=F32)


def _dot_tn(a, b):
    return lax.dot_general(a, b, (((0,), (0,)), ((), ())), preferred_element_type=F32)


def _sigmoid(x):
    return 1.0 / (1.0 + jnp.exp(-x))


def _silu(x):
    return x * _sigmoid(x)


def _softplus(x):
    return jnp.maximum(x, 0.0) + jnp.log1p(jnp.exp(-jnp.abs(x)))


def _rms(x, g=None):
    y = x * lax.rsqrt(jnp.mean(x * x, axis=-1, keepdims=True) + EPS)
    return y if g is None else y * g


def _split3(x):
    hi = x.astype(BF16)
    r = x - hi.astype(F32)
    mid = r.astype(BF16)
    lo = (r - mid.astype(F32)).astype(BF16)
    return hi, mid, lo


def _tri_cumsum(tri, x):
    hi, mid, lo = _split3(x)
    return _dot(tri, hi) + _dot(tri, mid) + _dot(tri, lo)


def _norm_kernel(x_ref, g_ref, o_ref):
    o_ref[...] = _rms(x_ref[...], g_ref[...]).astype(o_ref.dtype)


def _norm(x, g, tm):
    m, d = x.shape
    return pl.pallas_call(
        _norm_kernel,
        grid=(m // tm,),
        in_specs=[pl.BlockSpec((tm, d), lambda i: (i, 0)),
                  pl.BlockSpec((1, d), lambda i: (0, 0))],
        out_specs=pl.BlockSpec((tm, d), lambda i: (i, 0)),
        out_shape=jax.ShapeDtypeStruct((m, d), BF16),
        compiler_params=_cp("parallel"),
        name="rmsnorm",
    )(x, g.reshape(1, d))


CAST_ROWS = 256


def _cast_weight_tile(w_ref, wx_ref, wb_ref, shift):
    kk, tn = wb_ref.shape
    for r in range(0, kk, CAST_ROWS):
        rows = slice(r, min(r + CAST_ROWS, kk))
        if shift == 0:
            wb_ref[rows, :] = w_ref[0, rows, :].astype(BF16)
        else:
            wide = jnp.concatenate([w_ref[0, rows, :], wx_ref[0, rows, :]], axis=1)
            wb_ref[rows, :] = pltpu.roll(wide, wide.shape[1] - shift, 1)[:, :tn].astype(BF16)


def _cast_weight_tile_t(w_ref, wx_ref, wb_ref, shift):
    tn, _ = wb_ref.shape
    for r in range(0, tn, CAST_ROWS):
        lo, hi = r + shift, min(r + CAST_ROWS, tn) + shift
        if hi <= tn:
            src = w_ref[0, lo:hi, :]
        else:
            src = jnp.concatenate([w_ref[0, lo:tn, :], wx_ref[0, 0:hi - tn, :]], axis=0)
        wb_ref[r:min(r + CAST_ROWS, tn), :] = src.astype(BF16)


def _mm_ws_kernel(a_ref, w_ref, *rest, shift, transposed):
    if shift:
        wx_ref, o_ref, wb_ref = rest
    else:
        wx_ref = None
        o_ref, wb_ref = rest

    @pl.when(pl.program_id(1) == 0)
    def _():
        (_cast_weight_tile_t if transposed else _cast_weight_tile)(w_ref, wx_ref, wb_ref, shift)

    mm = _dot_nt if transposed else _dot
    o_ref[...] = mm(a_ref[...], wb_ref[...]).astype(o_ref.dtype)


def _mm_ws(a, w, layer, blk0, n, shift, out_dtype, tm, tn, name, transposed=False):
    m, k = a.shape
    nj = n // tn
    per = tn // LANES
    if transposed:
        assert shift % SUBLANES == 0
        w_spec = pl.BlockSpec((1, tn, k), lambda j, i: (layer, blk0 + j, 0))
        wx_spec = pl.BlockSpec((1, LANES, k), lambda j, i: (layer, (blk0 + j + 1) * per, 0))
    else:
        w_spec = pl.BlockSpec((1, k, tn), lambda j, i: (layer, 0, blk0 + j))
        wx_spec = pl.BlockSpec((1, k, LANES), lambda j, i: (layer, 0, (blk0 + j + 1) * per))
    in_specs = [pl.BlockSpec((tm, k), lambda j, i: (i, 0)), w_spec]
    args = [a, w]
    if shift:
        in_specs.append(wx_spec)
        args.append(w)
    return pl.pallas_call(
        functools.partial(_mm_ws_kernel, shift=shift, transposed=transposed),
        grid=(nj, m // tm),
        in_specs=in_specs,
        out_specs=pl.BlockSpec((tm, tn), lambda j, i: (i, j)),
        out_shape=jax.ShapeDtypeStruct((m, n), out_dtype),
        scratch_shapes=[pltpu.VMEM((tn, k) if transposed else (k, tn), BF16)],
        compiler_params=_cp("parallel", "arbitrary"),
        name=name,
    )(*args)


def _small_proj_kernel(a_ref, w0_ref, w1_ref, o_ref):
    a = a_ref[...]
    p0 = _dot_nt(a, w0_ref[0].astype(BF16))
    p1 = _dot_nt(a, w1_ref[0].astype(BF16))
    lane = lax.broadcasted_iota(jnp.int32, p0.shape, 1)
    o_ref[...] = jnp.where(lane < SMALL_A0, p0, p1)


def _small_proj(a, w_t, layer, blk_lr, blk_ab, tm):
    m, k = a.shape
    return pl.pallas_call(
        _small_proj_kernel,
        grid=(m // tm,),
        in_specs=[pl.BlockSpec((tm, k), lambda i: (i, 0)),
                  pl.BlockSpec((1, LANES, k), lambda i: (layer, blk_lr, 0)),
                  pl.BlockSpec((1, LANES, k), lambda i: (layer, blk_ab, 0))],
        out_specs=pl.BlockSpec((tm, LANES), lambda i: (i, 0)),
        out_shape=jax.ShapeDtypeStruct((m, LANES), F32),
        compiler_params=_cp("parallel"),
        name="in_proj_small",
    )(a, w_t, w_t)


def _shift_rows(x, prev8, s):
    xs = pltpu.roll(x, s, 0)
    ps = pltpu.roll(prev8, s, 0)
    r8 = lax.broadcasted_iota(jnp.int32, prev8.shape, 0)
    top = jnp.where(r8 < s, ps, xs[:SUBLANES])
    if x.shape[0] == SUBLANES:
        return top
    return jnp.concatenate([top, xs[SUBLANES:]], axis=0)


def _causal_conv(x, prev8, taps):
    width = len(taps)
    y = None
    for i, w in enumerate(taps):
        s = width - 1 - i
        xs = x if s == 0 else _shift_rows(x, prev8, s)
        y = xs * w if y is None else y + xs * w
    return y


def _conv_proj_kernel(a_ref, w_ref, *rest, shift, tiles_per_seq, seq_rows, head_dim, norm_scale):
    if shift:
        wx_ref, cw_ref, buf_ref, o_ref, tail_ref, wb_ref, carry_ref = rest
    else:
        wx_ref = None
        cw_ref, buf_ref, o_ref, tail_ref, wb_ref, carry_ref = rest
    i = pl.program_id(1)
    tm, tn = o_ref.shape
    sub = min(tm, CONV_SUB)
    seg = sub if seq_rows is None else seq_rows

    @pl.when(i == 0)
    def _():
        _cast_weight_tile_t(w_ref, wx_ref, wb_ref, shift)

    if seq_rows is None:
        @pl.when(i % tiles_per_seq == 0)
        def _():
            carry_ref[...] = buf_ref[0]

        prev = carry_ref[...]

    wb = wb_ref[...]
    taps = [cw_ref[n:n + 1, :] for n in range(cw_ref.shape[0])]
    for r in range(tm // sub):
        p_all = _dot_nt(a_ref[r * sub:(r + 1) * sub, :], wb)
        for s in range(sub // seg):
            lo = r * sub + s * seg
            x = p_all[s * seg:(s + 1) * seg]
            if seq_rows is not None:
                prev = buf_ref[lo // seg]
            y = _silu(_causal_conv(x, prev, taps))
            if norm_scale is None:
                o_ref[lo:lo + seg, :] = y
            else:
                for h in range(tn // head_dim):
                    hs = slice(h * head_dim, (h + 1) * head_dim)
                    yh = y[:, hs]
                    inv = lax.rsqrt(jnp.sum(yh * yh, axis=-1, keepdims=True) + EPS)
                    yh = yh * inv
                    o_ref[lo:lo + seg, hs] = yh if norm_scale == 1.0 else yh * norm_scale
            prev = x[seg - SUBLANES:, :]
            if seq_rows is not None:
                tail_ref[lo // seg] = prev
    if seq_rows is None:
        carry_ref[...] = prev
        tail_ref[0] = prev


def _conv_proj(a, w_t, layer, off, n, conv_w, buf8, coff, t, tm, tn, head_dim, norm_scale, name):
    m, k = a.shape
    nj, ni = n // tn, m // tm
    blk0 = off // tn
    shift = off - blk0 * tn
    assert shift < LANES and shift % SUBLANES == 0 and n % tn == 0 and coff % tn == 0
    cblk = coff // tn
    per = tn // LANES
    if t >= tm:
        tps, seq_rows, nb = t // tm, None, 1
        buf_ix = lambda j, i: (i // tps, 0, cblk + j)
    else:
        tps, seq_rows, nb = 1, t, tm // t
        buf_ix = lambda j, i: (i, 0, cblk + j)
    in_specs = [pl.BlockSpec((tm, k), lambda j, i: (i, 0)),
                pl.BlockSpec((1, tn, k), lambda j, i: (layer, blk0 + j, 0))]
    args = [a, w_t]
    if shift:
        in_specs.append(pl.BlockSpec((1, LANES, k), lambda j, i: (layer, (blk0 + j + 1) * per, 0)))
        args.append(w_t)
    in_specs += [pl.BlockSpec((conv_w.shape[0], tn), lambda j, i: (0, cblk + j)),
                 pl.BlockSpec((nb, SUBLANES, tn), buf_ix)]
    out, tails = pl.pallas_call(
        functools.partial(_conv_proj_kernel, shift=shift, tiles_per_seq=tps, seq_rows=seq_rows,
                          head_dim=head_dim, norm_scale=norm_scale),
        grid=(nj, ni),
        in_specs=in_specs,
        out_specs=[pl.BlockSpec((tm, tn), lambda j, i: (i, j)),
                   pl.BlockSpec((nb, SUBLANES, tn), lambda j, i: (i, 0, j))],
        out_shape=[jax.ShapeDtypeStruct((m, n), F32),
                   jax.ShapeDtypeStruct((ni * nb, SUBLANES, n), F32)],
        scratch_shapes=[pltpu.VMEM((tn, k), BF16), pltpu.VMEM((SUBLANES, tn), F32)],
        compiler_params=_cp("parallel", "arbitrary"),
        name=name,
    )(*args, conv_w, buf8)
    return out, tails[tps - 1::tps]


def _proj_res_kernel(a_ref, w_ref, x_ref, g_ref, *outs, sub, final):
    tm = a_ref.shape[0]
    w = w_ref[...]
    g = g_ref[...]
    for r in range(tm // sub):
        rows = slice(r * sub, (r + 1) * sub)
        xn = x_ref[rows, :] + _dot(a_ref[rows, :], w)
        if final:
            outs[0][rows, :] = _rms(xn, g)
        else:
            outs[0][rows, :] = xn
            outs[1][rows, :] = _rms(xn, g).astype(outs[1].dtype)


def _proj_res(a, w, x, g, tm, sub, final, name):
    m, kk = a.shape
    d = w.shape[1]
    row = lambda i: (i, 0)
    out_specs = [pl.BlockSpec((tm, d), row)]
    out_shape = [jax.ShapeDtypeStruct((m, d), F32)]
    if not final:
        out_specs.append(pl.BlockSpec((tm, d), row))
        out_shape.append(jax.ShapeDtypeStruct((m, d), BF16))
    return pl.pallas_call(
        functools.partial(_proj_res_kernel, sub=min(sub, tm), final=final),
        grid=(m // tm,),
        in_specs=[pl.BlockSpec((tm, kk), row),
                  pl.BlockSpec((kk, d), lambda i: (0, 0), pipeline_mode=pl.Buffered(1)),
                  pl.BlockSpec((tm, d), row),
                  pl.BlockSpec((1, d), lambda i: (0, 0))],
        out_specs=out_specs,
        out_shape=out_shape,
        compiler_params=_cp("parallel"),
        name=name,
    )(a, w, x, g.reshape(1, d))


def _merge_kernel(y0_ref, y1_ref, y2_ref, w_ref, g_ref, b_ref, o_ref, *, sub):
    tm, d = o_ref.shape
    for r in range(tm // sub):
        rows = slice(r * sub, (r + 1) * sub)
        acc = None
        for n, y_ref in enumerate((y0_ref, y1_ref, y2_ref)):
            p = _dot(y_ref[rows, :], w_ref[n])
            g = g_ref[rows, n * d:(n + 1) * d].astype(F32)
            t = _sigmoid(g + b_ref[n:n + 1, :]) * p
            acc = t if acc is None else acc + t
        o_ref[rows, :] = acc.astype(o_ref.dtype)


def _merge(ys, w_branch, g_br, b_gate, tm):
    m, bw = ys[0].shape
    d = w_branch.shape[2]
    row = lambda i: (i, 0)
    y_spec = pl.BlockSpec((tm, bw), row)
    return pl.pallas_call(
        functools.partial(_merge_kernel, sub=min(tm, ROW_SUB)),
        grid=(m // tm,),
        in_specs=[y_spec, y_spec, y_spec,
                  pl.BlockSpec((N_BRANCH, bw, d), lambda i: (0, 0, 0), pipeline_mode=pl.Buffered(1)),
                  pl.BlockSpec((tm, N_BRANCH * d), row),
                  pl.BlockSpec((N_BRANCH, d), lambda i: (0, 0))],
        out_specs=pl.BlockSpec((tm, d), row),
        out_shape=jax.ShapeDtypeStruct((m, d), BF16),
        compiler_params=_cp("parallel"),
        name="branch_merge",
    )(ys[0], ys[1], ys[2], w_branch, g_br, b_gate)


def _xattn_kernel(q_ref, k_ref, v_ref, o_ref, *, scale):
    q = q_ref[...]
    k = k_ref[0].astype(BF16)
    v = v_ref[0].astype(BF16)
    s = _dot_nt(q, k) * scale
    s = s - jnp.max(s, axis=-1, keepdims=True)
    e = jnp.exp(s)
    p = e / jnp.sum(e, axis=-1, keepdims=True)
    o_ref[...] = _dot(p.astype(BF16), v).astype(o_ref.dtype)


def _xattn(q, mem_k, mem_v, t, tq):
    m, d = q.shape
    b = m // t
    nmem = mem_k.shape[1]
    dh = d // X_H
    nq = t // tq
    kv_spec = pl.BlockSpec((1, nmem, dh), lambda bi, i, h: (bi, 0, h))
    qo_spec = pl.BlockSpec((tq, dh), lambda bi, i, h: (bi * nq + i, h))
    return pl.pallas_call(
        functools.partial(_xattn_kernel, scale=dh ** -0.5),
        grid=(b, nq, X_H),
        in_specs=[qo_spec, kv_spec, kv_spec],
        out_specs=qo_spec,
        out_shape=jax.ShapeDtypeStruct((m, d), BF16),
        compiler_params=_cp("parallel", "parallel", "arbitrary"),
        name="xattn",
    )(q, mem_k, mem_v)


def _gelu_tanh(x):
    c = 0.7978845608028654
    return x * (0.5 * (1.0 + jnp.tanh(c * (x + 0.044715 * (x * x * x)))))


def _ffn_in_kernel(h_ref, wg_ref, wu_ref, cw_ref, cb_ref, buf_ref, act_ref, nbuf_ref,
                   wgb_ref, wub_ref, carry_ref, *, tiles_per_seq, seq_rows):
    i = pl.program_id(1)
    tm = h_ref.shape[0]
    sub = min(tm, FFN_SUB)
    seg = sub if seq_rows is None else seq_rows

    @pl.when(i == 0)
    def _():
        _cast_weight_tile(wg_ref, None, wgb_ref, 0)
        _cast_weight_tile(wu_ref, None, wub_ref, 0)

    if seq_rows is None:
        @pl.when(i % tiles_per_seq == 0)
        def _():
            carry_ref[...] = buf_ref[0]

        prev = carry_ref[...]

    wg, wu = wgb_ref[...], wub_ref[...]
    taps, cb = [cw_ref[n:n + 1, :] for n in range(cw_ref.shape[0])], cb_ref[...]
    for r in range(tm // sub):
        h = h_ref[r * sub:(r + 1) * sub, :]
        gate_all = _dot(h, wg)
        up_all = _dot(h, wu)
        for s in range(sub // seg):
            lo = r * sub + s * seg
            gate = gate_all[s * seg:(s + 1) * seg]
            if seq_rows is not None:
                prev = buf_ref[lo // seg]
            y = _causal_conv(gate, prev, taps)
            act = _gelu_tanh(y + cb) * up_all[s * seg:(s + 1) * seg]
            act_ref[lo:lo + seg, :] = act.astype(act_ref.dtype)
            prev = gate[seg - SUBLANES:, :]
            if seq_rows is not None:
                nbuf_ref[lo // seg] = prev
    if seq_rows is None:
        carry_ref[...] = prev
        nbuf_ref[0] = prev


def _ffn_in(h, w_ffn_in, layer, conv_w, conv_b, buf8, t, tm, tn):
    m, d = h.shape
    dff = w_ffn_in.shape[2] // 2
    ni, nj = m // tm, dff // tn
    if t >= tm:
        tps, seq_rows, nb = t // tm, None, 1
        buf_ix = lambda j, i: (i // tps, 0, j)
    else:
        tps, seq_rows, nb = 1, t, tm // t
        buf_ix = lambda j, i: (i, 0, j)
    act, tails = pl.pallas_call(
        functools.partial(_ffn_in_kernel, tiles_per_seq=tps, seq_rows=seq_rows),
        grid=(nj, ni),
        in_specs=[pl.BlockSpec((tm, d), lambda j, i: (i, 0)),
                  pl.BlockSpec((1, d, tn), lambda j, i: (layer, 0, j)),
                  pl.BlockSpec((1, d, tn), lambda j, i: (layer, 0, nj + j)),
                  pl.BlockSpec((FFN_CONV, tn), lambda j, i: (0, j)),
                  pl.BlockSpec((1, tn), lambda j, i: (0, j)),
                  pl.BlockSpec((nb, SUBLANES, tn), buf_ix)],
        out_specs=[pl.BlockSpec((tm, tn), lambda j, i: (i, j)),
                   pl.BlockSpec((nb, SUBLANES, tn), lambda j, i: (i, 0, j))],
        out_shape=[jax.ShapeDtypeStruct((m, dff), BF16),
                   jax.ShapeDtypeStruct((ni * nb, SUBLANES, dff), F32)],
        scratch_shapes=[pltpu.VMEM((d, tn), BF16), pltpu.VMEM((d, tn), BF16),
                        pltpu.VMEM((SUBLANES, tn), F32)],
        compiler_params=_cp("parallel", "arbitrary"),
        name="ffn_in",
    )(h, w_ffn_in, w_ffn_in, conv_w, conv_b.reshape(1, dff), buf8)
    return act, tails[tps - 1::tps]


def _ret_kernel(ld_ref, p_ref, cos_ref, sin_ref, s0_ref, y_ref, s_ref, *, c):
    t = pl.program_id(1)
    tb = p_ref.shape[0]
    half = RET_DK // 2
    hw = RET_H * RET_DK
    heads = range(RET_H)

    @pl.when(t == 0)
    def _():
        s_ref[...] = s0_ref[...]

    ri = lax.broadcasted_iota(jnp.int32, (c, c), 0)
    ci = lax.broadcasted_iota(jnp.int32, (c, c), 1)
    diff = (ri - ci).astype(F32)
    causal = ri >= ci
    idx = lax.broadcasted_iota(jnp.int32, (c, RET_DK), 0).astype(F32)
    lds = [ld_ref[h] for h in heads]
    dmask = [jnp.where(causal, jnp.exp(ld * jnp.where(causal, diff, 0.0)), 0.0) for ld in lds]
    q_dec = [jnp.exp(ld * (idx + 1.0)) for ld in lds]
    k_dec = [jnp.exp(ld * (c - 1.0 - idx)) for ld in lds]
    s_dec = [jnp.exp(jnp.full((1, RET_DV), ld * c, F32)) for ld in lds]

    def rot(x, cos, sin):
        x1, x2 = x[:, :half], x[:, half:]
        return jnp.concatenate([x1 * cos - x2 * sin, x1 * sin + x2 * cos], axis=-1)

    def body(j, carry):
        rows = pl.ds(pl.multiple_of(j * c, c), c)
        cos, sin = cos_ref[rows, :], sin_ref[rows, :]
        q = [rot(p_ref[rows, h * RET_DK:(h + 1) * RET_DK], cos, sin) for h in heads]
        k = [rot(p_ref[rows, hw + h * RET_DK:hw + (h + 1) * RET_DK], cos, sin) * (RET_DK ** -0.5) for h in heads]
        vb = [p_ref[rows, 2 * hw + h * RET_DV:2 * hw + (h + 1) * RET_DV].astype(BF16) for h in heads]
        qb = [x.astype(BF16) for x in q]
        att = [(_dot_nt(qb[h], k[h].astype(BF16)) * dmask[h]).astype(BF16) for h in heads]
        s = [s_ref[0, h] for h in heads]
        o = [_dot(att[h], vb[h]) + _dot(qb[h], s[h].astype(BF16)) * q_dec[h] for h in heads]
        for h in heads:
            s_ref[0, h] = s_dec[h] * s[h] + _dot_tn((k[h] * k_dec[h]).astype(BF16), vb[h])
        for h in heads:
            g = p_ref[rows, 3 * hw + h * RET_DV:3 * hw + (h + 1) * RET_DV]
            y_ref[rows, h * RET_DV:(h + 1) * RET_DV] = (_rms(o[h]) * _silu(g)).astype(y_ref.dtype)
        return carry

    lax.fori_loop(0, tb // c, body, 0)


def _retention(p_ret, cos, sin, ld, s0, t, tb):
    m, width = p_ret.shape
    b = m // t
    c = min(CHUNK, t)
    nt = t // tb
    st_spec = pl.BlockSpec((1, RET_H, RET_DK, RET_DV), lambda bi, ti: (bi, 0, 0, 0))
    rope_spec = pl.BlockSpec((tb, RET_DK // 2), lambda bi, ti: (ti, 0))
    return pl.pallas_call(
        functools.partial(_ret_kernel, c=c),
        grid=(b, nt),
        in_specs=[pl.BlockSpec(memory_space=pltpu.SMEM),
                  pl.BlockSpec((tb, width), lambda bi, ti: (bi * nt + ti, 0)),
                  rope_spec, rope_spec, st_spec],
        out_specs=[pl.BlockSpec((tb, RET_H * RET_DV), lambda bi, ti: (bi * nt + ti, 0)), st_spec],
        out_shape=[jax.ShapeDtypeStruct((m, RET_H * RET_DV), BF16),
                   jax.ShapeDtypeStruct((b, RET_H, RET_DK, RET_DV), F32)],
        compiler_params=_cp("parallel", "arbitrary"),
        name="retention",
    )(ld, p_ret, cos, sin, s0)


def _gla_kernel(p_ref, lr_ref, wgk_ref, bgk_ref, nrm_ref, s0_ref, y_ref, s_ref, st_ref, *, c, nt):
    t = pl.program_id(1)
    tb = p_ref.shape[0]
    hk = GLA_H * GLA_DK
    hv = GLA_H * GLA_DV
    heads = range(GLA_H)

    @pl.when(t == 0)
    def _():
        for h in heads:
            st_ref[h] = s0_ref[0, h].T

    ri = lax.broadcasted_iota(jnp.int32, (c, c), 0)
    ci = lax.broadcasted_iota(jnp.int32, (c, c), 1)
    causal = ri >= ci
    tri = jnp.where(causal, 1.0, 0.0).astype(BF16)
    wgk = wgk_ref[...]
    bgk = bgk_ref[...]
    nrm = nrm_ref[...]

    def body(j, carry):
        rows = pl.ds(pl.multiple_of(j * c, c), c)
        z = _dot(lr_ref[rows, :].astype(BF16), wgk) + bgk
        gk = -_softplus(-z) / GLA_NORMALIZER
        bc = _tri_cumsum(tri, gk)
        q = p_ref[rows, 0:hk] * (GLA_DK ** -0.5)
        k = p_ref[rows, hk:2 * hk]
        qe_all = (q * jnp.exp(bc)).astype(BF16)
        ke_all = (k * jnp.exp(-bc)).astype(BF16)
        b_last = bc[c - 1:c, :]
        kd_all = (k * jnp.exp(b_last - bc)).astype(BF16)
        eb_last = jnp.exp(b_last)
        hs = lambda x, h: x[:, h * GLA_DK:(h + 1) * GLA_DK]
        vb = [p_ref[rows, 2 * hk + h * GLA_DV:2 * hk + (h + 1) * GLA_DV].astype(BF16) for h in heads]
        att = [jnp.where(causal, _dot_nt(hs(qe_all, h), hs(ke_all, h)), 0.0).astype(BF16) for h in heads]
        st = [st_ref[h] for h in heads]
        o = [_dot(att[h], vb[h]) + _dot_nt(hs(qe_all, h), st[h].astype(BF16)) for h in heads]
        for h in heads:
            st_ref[h] = hs(eb_last, h) * st[h] + _dot_tn(vb[h], hs(kd_all, h))
        for h in heads:
            g = p_ref[rows, 2 * hk + hv + h * GLA_DV:2 * hk + hv + (h + 1) * GLA_DV]
            y_ref[rows, h * GLA_DV:(h + 1) * GLA_DV] = (_rms(o[h], nrm) * _silu(g)).astype(y_ref.dtype)
        return carry

    lax.fori_loop(0, tb // c, body, 0)

    @pl.when(t == nt - 1)
    def _():
        for h in heads:
            s_ref[0, h] = st_ref[h].T


def _gla(p_gla, p_small, wgk, bgk, nrm, s0, t, tb):
    m, width = p_gla.shape
    b = m // t
    c = min(CHUNK, t)
    nt = t // tb
    const = lambda bi, ti: (0, 0)
    st_spec = pl.BlockSpec((1, GLA_H, GLA_DK, GLA_DV), lambda bi, ti: (bi, 0, 0, 0))
    return pl.pallas_call(
        functools.partial(_gla_kernel, c=c, nt=nt),
        grid=(b, nt),
        in_specs=[pl.BlockSpec((tb, width), lambda bi, ti: (bi * nt + ti, 0)),
                  pl.BlockSpec((tb, LANES), lambda bi, ti: (bi * nt + ti, 0)),
                  pl.BlockSpec((LANES, GLA_H * GLA_DK), const),
                  pl.BlockSpec((1, GLA_H * GLA_DK), const),
                  pl.BlockSpec((1, GLA_DV), const),
                  st_spec],
        out_specs=[pl.BlockSpec((tb, GLA_H * GLA_DV), lambda bi, ti: (bi * nt + ti, 0)), st_spec],
        out_shape=[jax.ShapeDtypeStruct((m, GLA_H * GLA_DV), BF16),
                   jax.ShapeDtypeStruct((b, GLA_H, GLA_DK, GLA_DV), F32)],
        scratch_shapes=[pltpu.VMEM((GLA_H, GLA_DV, GLA_DK), F32)],
        compiler_params=_cp("parallel", "arbitrary"),
        name="gla",
    )(p_gla, p_small, wgk, bgk, nrm, s0)


def _unit_lower_inverse_minus_eye(mats, c):
    ri = lax.broadcasted_iota(jnp.int32, (c, c), 0)
    ci = lax.broadcasted_iota(jnp.int32, (c, c), 1)

    def level_mask(k):
        same = (ri ^ ci) < 2 * k
        return same & ((ri & k) != 0) & ((ci & k) == 0)

    m1 = level_mask(1)
    ns = [-jnp.where(m1, a, 0.0) for a in mats]
    k = 2
    while k < c:
        mk = level_mask(k)
        ls = [jnp.where(mk, a, 0.0) for a in mats]
        nbs = [n.astype(BF16) for n in ns]
        ys = [l + _dot(nb, l.astype(BF16)) for l, nb in zip(ls, nbs)]
        xs = [y + _dot(y.astype(BF16), nb) for y, nb in zip(ys, nbs)]
        ns = [n - x for n, x in zip(ns, xs)]
        k *= 2
    return ns


def _gdn_kernel(q_ref, k_ref, v_ref, z_ref, sm_ref, alog_ref, dtb_ref, nrm_ref, s0_ref,
                y_ref, s_ref, *, c):
    t = pl.program_id(1)
    tb = q_ref.shape[0]

    @pl.when(t == 0)
    def _():
        s_ref[...] = s0_ref[...]

    ri = lax.broadcasted_iota(jnp.int32, (c, c), 0)
    ci = lax.broadcasted_iota(jnp.int32, (c, c), 1)
    incl = ri >= ci
    strict = ri > ci
    tri = jnp.where(incl, 1.0, 0.0).astype(BF16)
    lane = lax.broadcasted_iota(jnp.int32, (1, LANES), 1)
    a_lanes = jnp.logical_and(lane >= SMALL_A0, lane < SMALL_A0 + GDN_H)
    sel = jnp.where(lax.broadcasted_iota(jnp.int32, (SUBLANES, LANES), 1)
                    == lax.broadcasted_iota(jnp.int32, (SUBLANES, LANES), 0) + SMALL_A0,
                    1.0, 0.0).astype(BF16)
    neg_a = -jnp.exp(alog_ref[...])
    dtb = dtb_ref[...]
    nrm = nrm_ref[...]

    def col(x, lane_idx):
        return x[:, lane_idx:lane_idx + 1]

    heads = range(GDN_H)
    nchunk = tb // c
    nc = GDN_CHUNKS_PER_STEP if nchunk % GDN_CHUNKS_PER_STEP == 0 else 1

    def body(jj, carry):
        ch = []
        for ci in range(nc):
            rows = pl.ds(pl.multiple_of((jj * nc + ci) * c, c), c)
            sm = sm_ref[rows, :]
            ch.append(dict(
                rows=rows,
                g_all=jnp.where(a_lanes, neg_a * _softplus(sm + dtb), 0.0),
                beta_all=_sigmoid(sm),
                q=[q_ref[rows, h * GDN_DK:(h + 1) * GDN_DK] for h in heads],
                k=[k_ref[rows, h * GDN_DK:(h + 1) * GDN_DK] for h in heads],
                v=[v_ref[rows, h * GDN_DV:(h + 1) * GDN_DV] for h in heads]))
        for d in ch:
            d['gam'] = _tri_cumsum(tri, d['g_all'])
        for d in ch:
            ghi, gmid, glo = _split3(d['gam'])
            d['gam_rows'] = _dot_nt(sel, ghi) + _dot_nt(sel, gmid) + _dot_nt(sel, glo)
        for d in ch:
            gam = d['gam']
            g_last = gam[c - 1:c, :]
            d['eg_all'] = jnp.exp(gam)
            d['egl_all'] = jnp.exp(g_last - gam)
            d['eg_last_all'] = jnp.exp(g_last)
            d['beta'] = [col(d['beta_all'], SMALL_B0 + h) for h in heads]
            d['eg'] = [col(d['eg_all'], SMALL_A0 + h) for h in heads]
            d['decay'] = [jnp.where(incl, jnp.exp(jnp.where(
                incl, col(gam, SMALL_A0 + h) - d['gam_rows'][h:h + 1, :], 0.0)), 0.0) for h in heads]
        probs = [(d, h) for d in ch for h in heads]
        qk_kk = [_dot_nt(jnp.concatenate([d['q'][h], d['k'][h]], axis=0).astype(BF16), d['k'][h].astype(BF16))
                 for d, h in probs]
        att = [(r[:c] * d['decay'][h]).astype(BF16) for r, (d, h) in zip(qk_kk, probs)]
        a = [jnp.where(strict, d['beta'][h] * r[c:] * d['decay'][h], 0.0) for r, (d, h) in zip(qk_kk, probs)]
        n = _unit_lower_inverse_minus_eye(a, c)
        rhs = [jnp.concatenate([d['beta'][h] * d['v'][h], (d['beta'][h] * d['eg'][h]) * d['k'][h]], axis=-1)
               for d, h in probs]
        sol = [r + _dot(nn.astype(BF16), r.astype(BF16)) for r, nn in zip(rhs, n)]

        for ci, d in enumerate(ch):
            sl = sol[ci * GDN_H:(ci + 1) * GDN_H]
            at = att[ci * GDN_H:(ci + 1) * GDN_H]
            s = [s_ref[0, h] for h in heads]
            wq_s = [_dot(jnp.concatenate([sl[h][:, GDN_DV:], d['q'][h]], axis=0).astype(BF16), s[h].astype(BF16))
                    for h in heads]
            wb = [(sl[h][:, :GDN_DV] - wq_s[h][:c]).astype(BF16) for h in heads]
            o = [d['eg'][h] * wq_s[h][c:] + _dot(at[h], wb[h]) for h in heads]
            kd = [(d['k'][h] * col(d['egl_all'], SMALL_A0 + h)).astype(BF16) for h in heads]
            for h in heads:
                s_ref[0, h] = col(d['eg_last_all'], SMALL_A0 + h) * s[h] + _dot_tn(kd[h], wb[h])
            for h in heads:
                z = z_ref[d['rows'], h * GDN_DV:(h + 1) * GDN_DV]
                y = _rms(o[h], nrm) * _silu(z)
                y_ref[d['rows'], h * GDN_DV:(h + 1) * GDN_DV] = y.astype(y_ref.dtype)
        return carry

    lax.fori_loop(0, nchunk // nc, body, 0)


def _gdn(q, k, v, z, p_small, alog_row, dtb_row, nrm, s0, t, tb):
    m = q.shape[0]
    b = m // t
    c = min(CHUNK, t)
    nt = t // tb
    st_spec = pl.BlockSpec((1, GDN_H, GDN_DK, GDN_DV), lambda bi, ti: (bi, 0, 0, 0))
    const = lambda bi, ti: (0, 0)
    row = lambda bi, ti: (bi * nt + ti, 0)
    qk_spec = pl.BlockSpec((tb, GDN_H * GDN_DK), row)
    v_spec = pl.BlockSpec((tb, GDN_H * GDN_DV), row)
    return pl.pallas_call(
        functools.partial(_gdn_kernel, c=c),
        grid=(b, nt),
        in_specs=[qk_spec, qk_spec, v_spec, v_spec,
                  pl.BlockSpec((tb, LANES), row),
                  pl.BlockSpec((1, LANES), const),
                  pl.BlockSpec((1, LANES), const),
                  pl.BlockSpec((1, GDN_DV), const),
                  st_spec],
        out_specs=[v_spec, st_spec],
        out_shape=[jax.ShapeDtypeStruct((m, GDN_H * GDN_DV), BF16),
                   jax.ShapeDtypeStruct((b, GDN_H, GDN_DK, GDN_DV), F32)],
        compiler_params=_cp("parallel", "arbitrary"),
        name="gated_deltanet",
    )(q, k, v, z, p_small, alog_row, dtb_row, nrm, s0)


def _pad_rows_front(buf, rows):
    return jnp.pad(buf, ((0, 0), (0, 0), (rows - buf.shape[2], 0), (0, 0)))


def _in_proj_windows(d):
    sizes = dict(ret=4 * RET_H * RET_DK, gla=2 * GLA_H * GLA_DK + 2 * GLA_H * GLA_DV, lr=GLA_RANK,
                 gdn=2 * GDN_H * GDN_DK + 2 * GDN_H * GDN_DV, ab=2 * GDN_H, gbr=N_BRANCH * d)
    win, o = {}, 0
    for name in ('ret', 'gla', 'lr', 'gdn', 'ab', 'gbr'):
        win[name] = (o, sizes[name])
        o += sizes[name]
    assert win['lr'][0] % LANES == 0 and win['ab'][0] % LANES == SMALL_A0
    return win


def _prep_weights(w):
    lane_row = lambda v, at: jnp.pad(v, ((0, 0), (at, LANES - at - v.shape[1])))[:, None, :]
    return dict(
        w_in_t=jnp.swapaxes(w['w_in'], 1, 2), w_xq=w['w_xq'], w_ffn_in=w['w_ffn_in'],
        w_gk=jnp.pad(w['w_gla_gk'], ((0, 0), (0, LANES - GLA_RANK), (0, 0))).astype(BF16),
        b_gk=w['b_gla_gk'][:, None, :],
        gla_norm=w['gla_norm'][:, None, :], gdn_norm=w['gdn_norm'][:, None, :],
        alog_row=lane_row(w['gdn_a_log'], SMALL_A0), dtb_row=lane_row(w['gdn_dt_bias'], SMALL_A0),
        gdn_conv_w=w['gdn_conv_w'], b_gate=w['b_gate'],
        w_branch=w['w_branch'].astype(BF16), w_out=w['w_out'].astype(BF16), w_xo=w['w_xo'].astype(BF16),
        ffn_conv_w=w['ffn_conv_w'], ffn_conv_b=w['ffn_conv_b'], w_ffn_out=w['w_ffn_out'].astype(BF16),
        ln_mix=w['ln_mix'], ln_xattn=w['ln_xattn'], ln_ffn=w['ln_ffn'], ln_final=w['ln_final'],
    )


def _tile(n, cap):
    t = min(n, cap)
    while n % t:
        t -= 1
    return t


def _trunk(x, offset, mem_k, mem_v, s_ret, s_gla, s_gdn, buf_gdn, buf_ffn, pw):
    b, t, d = x.shape
    m = b * t
    depth = s_ret.shape[0]
    x2 = x.reshape(m, d)
    tm = _tile(m, MM_TM)
    tm_res = _tile(m, RES_TM)
    tm_res_k = _tile(m, RES_TM_K)
    tm_ffo = _tile(m, FFO_TM)
    tb = _tile(t, SEQ_TB)
    tb_gdn = _tile(t, SEQ_TB_GDN)
    tq = _tile(t, XATTN_TQ)
    tm_ffn = _tile(t, FFN_TM) if t >= FFN_TM else t * _tile(b, max(1, FFN_TM // t))

    pos = offset + jnp.arange(t, dtype=F32)
    half = RET_DK // 2
    freqs = 1.0 / (ROPE_BASE ** (jnp.arange(half, dtype=F32) / half))
    ang = pos[:, None] * freqs[None, :]
    cos, sin = jnp.cos(ang), jnp.sin(ang)
    ld = jnp.log1p(-jnp.exp2(-5.0 - jnp.arange(RET_H, dtype=F32)))

    buf_gdn8 = _pad_rows_front(buf_gdn, SUBLANES)
    buf_ffn8 = _pad_rows_front(buf_ffn, SUBLANES)
    win = _in_proj_windows(d)

    def in_proj(a, l, name, out_dtype=F32):
        off, n = win[name]
        blk0 = off // MM_TN
        shift = off - blk0 * MM_TN
        assert shift < LANES and n % MM_TN == 0
        return _mm_ws(a, pw['w_in_t'], l, blk0, n, shift, out_dtype, tm, MM_TN, "in_proj_" + name,
                      transposed=True)

    outs = ([], [], [], [], [])
    hn = _norm(x2, pw['ln_mix'][0], tm_res_k)
    y = None
    for l in range(depth):
        p_ret = in_proj(hn, l, 'ret')
        p_gla = in_proj(hn, l, 'gla')
        g_off = win['gdn'][0]
        hq, hv = GDN_H * GDN_DK, GDN_H * GDN_DV
        conv_args = (pw['gdn_conv_w'][l], buf_gdn8[l])
        d_q, tq8 = _conv_proj(hn, pw['w_in_t'], l, g_off, hq, *conv_args, 0, t, tm, MM_TN,
                              GDN_DK, GDN_DK ** -0.5, "in_proj_gdn_q")
        d_k, tk8 = _conv_proj(hn, pw['w_in_t'], l, g_off + hq, hq, *conv_args, hq, t, tm, MM_TN,
                              GDN_DK, 1.0, "in_proj_gdn_k")
        d_v, tv8 = _conv_proj(hn, pw['w_in_t'], l, g_off + 2 * hq, hv, *conv_args, 2 * hq, t, tm, MM_TN,
                              GDN_DV, None, "in_proj_gdn_v")
        zo = g_off + 2 * hq + hv
        d_z = _mm_ws(hn, pw['w_in_t'], l, zo // MM_TN, hv, zo % MM_TN, F32, tm, MM_TN, "in_proj_gdn_z",
                     transposed=True)
        bg8 = jnp.concatenate([tq8, tk8, tv8], axis=-1)
        g_br = in_proj(hn, l, 'gbr', BF16)
        p_small = _small_proj(hn, pw['w_in_t'], l, win['lr'][0] // LANES, win['ab'][0] // LANES, tm)

        y_ret, sr = _retention(p_ret, cos, sin, ld, s_ret[l], t, tb)
        y_gla, sg = _gla(p_gla, p_small, pw['w_gk'][l], pw['b_gk'][l], pw['gla_norm'][l], s_gla[l], t, tb)
        y_gdn, sd = _gdn(d_q, d_k, d_v, d_z, p_small, pw['alog_row'][l], pw['dtb_row'][l],
                         pw['gdn_norm'][l], s_gdn[l], t, tb_gdn)

        merged = _merge((y_ret, y_gla, y_gdn), pw['w_branch'][l], g_br, pw['b_gate'][l], tm_res_k)
        x2, hx = _proj_res(merged, pw['w_out'][l], x2, pw['ln_xattn'][l], tm_res, ROW_SUB, False, "out_proj")

        q = _mm_ws(hx, pw['w_xq'], l, 0, d, 0, BF16, tm, MM_TN, "xattn_q")
        o = _xattn(q, mem_k[l], mem_v[l], t, tq)
        x2, hf = _proj_res(o, pw['w_xo'][l], x2, pw['ln_ffn'][l], tm_res, ROW_SUB, False, "xattn_out")

        act, bf8 = _ffn_in(hf, pw['w_ffn_in'], l, pw['ffn_conv_w'][l], pw['ffn_conv_b'][l],
                           buf_ffn8[l], t, tm_ffn, FFN_TN)
        if l + 1 < depth:
            x2, hn = _proj_res(act, pw['w_ffn_out'][l], x2, pw['ln_mix'][l + 1], tm_ffo, ROW_SUB, False, "ffn_out")
        else:
            y, = _proj_res(act, pw['w_ffn_out'][l], x2, pw['ln_final'], tm_ffo, ROW_SUB, True, "ffn_out_final")

        for lst, val in zip(outs, (sr, sg, sd, bg8[:, SUBLANES - (GDN_CONV - 1):], bf8[:, SUBLANES - (FFN_CONV - 1):])):
            lst.append(val)
    return (y.reshape(b, t, d),) + tuple(jnp.stack(lst) for lst in outs)


def kernel(x_prompt, x_sample, mem_prompt, state_ret, state_gla, state_gdn, state_gdn_conv, state_ffn_conv, cache_mem_k, cache_mem_v, ln_mix, w_in, w_gla_gk, b_gla_gk, gla_norm, gdn_conv_w, gdn_a_log, gdn_dt_bias, gdn_norm, b_gate, w_branch, w_out, ln_xattn, ln_mem, w_xq, w_xkv, w_xo, ln_ffn, w_ffn_in, ffn_conv_w, ffn_conv_b, w_ffn_out, ln_final):
    pw = _prep_weights(dict(
        w_in=w_in, w_gla_gk=w_gla_gk, b_gla_gk=b_gla_gk, gla_norm=gla_norm, gdn_conv_w=gdn_conv_w,
        gdn_a_log=gdn_a_log, gdn_dt_bias=gdn_dt_bias, gdn_norm=gdn_norm, b_gate=b_gate, w_branch=w_branch,
        w_out=w_out, w_xq=w_xq, w_xo=w_xo, w_ffn_in=w_ffn_in, ffn_conv_w=ffn_conv_w, ffn_conv_b=ffn_conv_b,
        w_ffn_out=w_ffn_out, ln_mix=ln_mix, ln_xattn=ln_xattn, ln_ffn=ln_ffn, ln_final=ln_final))
    depth = w_in.shape[0]
    bp, nmem, d = mem_prompt.shape
    dt = x_prompt.dtype

    mem2 = mem_prompt.reshape(bp * nmem, d)
    tmem = _tile(bp * nmem, MM_TM)
    mk, mv = [], []
    for l in range(depth):
        hm = _norm(mem2, ln_mem[l], _tile(bp * nmem, RES_TM_K))
        mk.append(_mm_ws(hm, w_xkv, l, 0, d, 0, F32, tmem, MM_TN, "mem_k"))
        mv.append(_mm_ws(hm, w_xkv, l, d // MM_TN, d, 0, F32, tmem, MM_TN, "mem_v"))
    dh = d // X_H
    mem_k_p = jnp.stack(mk).reshape(depth, bp, nmem, X_H, dh)
    mem_v_p = jnp.stack(mv).reshape(depth, bp, nmem, X_H, dh)

    zeros = lambda *s: jnp.zeros((depth, bp) + s, dt)
    y_p, ret_p, gla_p, gdn_p, gdn_conv_p, ffn_conv_p = _trunk(
        x_prompt, 0.0, [a.reshape(bp, nmem, d) for a in mk], [a.reshape(bp, nmem, d) for a in mv],
        zeros(RET_H, RET_DK, RET_DV), zeros(GLA_H, GLA_DK, GLA_DV), zeros(GDN_H, GDN_DK, GDN_DV),
        zeros(GDN_CONV - 1, state_gdn_conv.shape[-1]), zeros(FFN_CONV - 1, state_ffn_conv.shape[-1]), pw)

    past_len = 4096.0
    bs = x_sample.shape[0]
    y_s, ret_s, gla_s, gdn_s, gdn_conv_s, ffn_conv_s = _trunk(
        x_sample, past_len,
        [cache_mem_k[l].reshape(bs, nmem, d) for l in range(depth)],
        [cache_mem_v[l].reshape(bs, nmem, d) for l in range(depth)],
        state_ret, state_gla, state_gdn, state_gdn_conv, state_ffn_conv, pw)

    return (y_p, y_s, ret_p, gla_p, gdn_p, gdn_conv_p, ffn_conv_p, mem_k_p, mem_v_p,
            ret_s, gla_s, gdn_s, gdn_conv_s, ffn_conv_s)
```

```python
import functools

import jax
import jax.numpy as jnp
from jax import lax
from jax.experimental import pallas as pl
from jax.experimental.pallas import tpu as pltpu

F32 = jnp.float32
BF16 = jnp.bfloat16

EPS = 1e-6
CHUNK = 64
RET_H, RET_DK, RET_DV = 4, 256, 256
GLA_H, GLA_DK, GLA_DV = 4, 128, 256
GLA_RANK = 16
GLA_NORMALIZER = 16.0
GDN_H, GDN_DK, GDN_DV = 8, 128, 128
GDN_CONV = 4
N_BRANCH = 3
X_H = 4
FFN_CONV = 3
ROPE_BASE = 10000.0

LANES = 128
SUBLANES = 8
SMALL_A0 = GLA_RANK
SMALL_B0 = GLA_RANK + GDN_H
VMEM_LIMIT = 52 * 1024 * 1024
MM_TM = 1024
MM_TN = 1024
RES_TM = 512
RES_TM_K = 512
FFO_TM = 256
ROW_SUB = 256
CONV_SUB = 256
SEQ_TB = 512
SEQ_TB_GDN = 256
GDN_CHUNKS_PER_STEP = 2
GLA_CHUNKS_PER_STEP = 2
XATTN_TQ = 512
FFN_TM = 1024
FFN_TN = 512
FFN_SUB = 256


def _cp(*sem):
    return pltpu.CompilerParams(dimension_semantics=sem, vmem_limit_bytes=VMEM_LIMIT)


def _dot(a, b):
    return jnp.dot(a, b, preferred_element_type=F32)


def _dot_nt(a, b):
    return lax.dot_general(a, b, (((1,), (1,)), ((), ())), preferred_element_type=F32)


def _dot_tn(a, b):
    return lax.dot_general(a, b, (((0,), (0,)), ((), ())), preferred_element_type=F32)


def _sigmoid(x):
    return 1.0 / (1.0 + jnp.exp(-x))


def _silu(x):
    return x * _sigmoid(x)


def _softplus(x):
    return jnp.maximum(x, 0.0) + jnp.log1p(jnp.exp(-jnp.abs(x)))


def _rms(x, g=None):
    y = x * lax.rsqrt(jnp.mean(x * x, axis=-1, keepdims=True) + EPS)
    return y if g is None else y * g


def _split3(x):
    hi = x.astype(BF16)
    r = x - hi.astype(F32)
    mid = r.astype(BF16)
    lo = (r - mid.astype(F32)).astype(BF16)
    return hi, mid, lo


def _tri_cumsum(tri, x):
    hi, mid, lo = _split3(x)
    return _dot(tri, hi) + _dot(tri, mid) + _dot(tri, lo)


def _norm_kernel(x_ref, g_ref, o_ref):
    o_ref[...] = _rms(x_ref[...], g_ref[...]).astype(o_ref.dtype)


def _norm(x, g, tm):
    m, d = x.shape
    return pl.pallas_call(
        _norm_kernel,
        grid=(m // tm,),
        in_specs=[pl.BlockSpec((tm, d), lambda i: (i, 0)),
                  pl.BlockSpec((1, d), lambda i: (0, 0))],
        out_specs=pl.BlockSpec((tm, d), lambda i: (i, 0)),
        out_shape=jax.ShapeDtypeStruct((m, d), BF16),
        compiler_params=_cp("parallel"),
        name="rmsnorm",
    )(x, g.reshape(1, d))


CAST_ROWS = 256


def _cast_weight_tile(w_ref, wx_ref, wb_ref, shift):
    kk, tn = wb_ref.shape
    for r in range(0, kk, CAST_ROWS):
        rows = slice(r, min(r + CAST_ROWS, kk))
        if shift == 0:
            wb_ref[rows, :] = w_ref[0, rows, :].astype(BF16)
        else:
            wide = jnp.concatenate([w_ref[0, rows, :], wx_ref[0, rows, :]], axis=1)
            wb_ref[rows, :] = pltpu.roll(wide, wide.shape[1] - shift, 1)[:, :tn].astype(BF16)


def _cast_weight_tile_t(w_ref, wx_ref, wb_ref, shift):
    tn, _ = wb_ref.shape
    for r in range(0, tn, CAST_ROWS):
        lo, hi = r + shift, min(r + CAST_ROWS, tn) + shift
        if hi <= tn:
            src = w_ref[0, lo:hi, :]
        else:
            src = jnp.concatenate([w_ref[0, lo:tn, :], wx_ref[0, 0:hi - tn, :]], axis=0)
        wb_ref[r:min(r + CAST_ROWS, tn), :] = src.astype(BF16)


def _mm_ws_kernel(a_ref, w_ref, *rest, shift, transposed):
    if shift:
        wx_ref, o_ref, wb_ref = rest
    else:
        wx_ref = None
        o_ref, wb_ref = rest

    @pl.when(pl.program_id(1) == 0)
    def _():
        (_cast_weight_tile_t if transposed else _cast_weight_tile)(w_ref, wx_ref, wb_ref, shift)

    mm = _dot_nt if transposed else _dot
    o_ref[...] = mm(a_ref[...], wb_ref[...]).astype(o_ref.dtype)


def _mm_ws(a, w, layer, blk0, n, shift, out_dtype, tm, tn, name, transposed=False):
    m, k = a.shape
    nj = n // tn
    per = tn // LANES
    if transposed:
        assert shift % SUBLANES == 0
        w_spec = pl.BlockSpec((1, tn, k), lambda j, i: (layer, blk0 + j, 0))
        wx_spec = pl.BlockSpec((1, LANES, k), lambda j, i: (layer, (blk0 + j + 1) * per, 0))
    else:
        w_spec = pl.BlockSpec((1, k, tn), lambda j, i: (layer, 0, blk0 + j))
        wx_spec = pl.BlockSpec((1, k, LANES), lambda j, i: (layer, 0, (blk0 + j + 1) * per))
    in_specs = [pl.BlockSpec((tm, k), lambda j, i: (i, 0)), w_spec]
    args = [a, w]
    if shift:
        in_specs.append(wx_spec)
        args.append(w)
    return pl.pallas_call(
        functools.partial(_mm_ws_kernel, shift=shift, transposed=transposed),
        grid=(nj, m // tm),
        in_specs=in_specs,
        out_specs=pl.BlockSpec((tm, tn), lambda j, i: (i, j)),
        out_shape=jax.ShapeDtypeStruct((m, n), out_dtype),
        scratch_shapes=[pltpu.VMEM((tn, k) if transposed else (k, tn), BF16)],
        compiler_params=_cp("parallel", "arbitrary"),
        name=name,
    )(*args)


def _small_proj_kernel(a_ref, w0_ref, w1_ref, o_ref):
    a = a_ref[...]
    p0 = _dot_nt(a, w0_ref[0].astype(BF16))
    p1 = _dot_nt(a, w1_ref[0].astype(BF16))
    lane = lax.broadcasted_iota(jnp.int32, p0.shape, 1)
    o_ref[...] = jnp.where(lane < SMALL_A0, p0, p1)


def _small_proj(a, w_t, layer, blk_lr, blk_ab, tm):
    m, k = a.shape
    return pl.pallas_call(
        _small_proj_kernel,
        grid=(m // tm,),
        in_specs=[pl.BlockSpec((tm, k), lambda i: (i, 0)),
                  pl.BlockSpec((1, LANES, k), lambda i: (layer, blk_lr, 0)),
                  pl.BlockSpec((1, LANES, k), lambda i: (layer, blk_ab, 0))],
        out_specs=pl.BlockSpec((tm, LANES), lambda i: (i, 0)),
        out_shape=jax.ShapeDtypeStruct((m, LANES), F32),
        compiler_params=_cp("parallel"),
        name="in_proj_small",
    )(a, w_t, w_t)


def _shift_rows(x, prev8, s):
    xs = pltpu.roll(x, s, 0)
    ps = pltpu.roll(prev8, s, 0)
    r8 = lax.broadcasted_iota(jnp.int32, prev8.shape, 0)
    top = jnp.where(r8 < s, ps, xs[:SUBLANES])
    if x.shape[0] == SUBLANES:
        return top
    return jnp.concatenate([top, xs[SUBLANES:]], axis=0)


def _causal_conv(x, prev8, taps):
    width = len(taps)
    y = None
    for i, w in enumerate(taps):
        s = width - 1 - i
        xs = x if s == 0 else _shift_rows(x, prev8, s)
        y = xs * w if y is None else y + xs * w
    return y


def _conv_proj_kernel(a_ref, w_ref, *rest, shift, tiles_per_seq, seq_rows, head_dim, norm_scale):
    if shift:
        wx_ref, cw_ref, buf_ref, o_ref, tail_ref, wb_ref, carry_ref = rest
    else:
        wx_ref = None
        cw_ref, buf_ref, o_ref, tail_ref, wb_ref, carry_ref = rest
    i = pl.program_id(1)
    tm, tn = o_ref.shape
    sub = min(tm, CONV_SUB)
    seg = sub if seq_rows is None else seq_rows

    @pl.when(i == 0)
    def _():
        _cast_weight_tile_t(w_ref, wx_ref, wb_ref, shift)

    if seq_rows is None:
        @pl.when(i % tiles_per_seq == 0)
        def _():
            carry_ref[...] = buf_ref[0]

        prev = carry_ref[...]

    wb = wb_ref[...]
    taps = [cw_ref[n:n + 1, :] for n in range(cw_ref.shape[0])]
    for r in range(tm // sub):
        p_all = _dot_nt(a_ref[r * sub:(r + 1) * sub, :], wb)
        for s in range(sub // seg):
            lo = r * sub + s * seg
            x = p_all[s * seg:(s + 1) * seg]
            if seq_rows is not None:
                prev = buf_ref[lo // seg]
            y = _silu(_causal_conv(x, prev, taps))
            if norm_scale is None:
                o_ref[lo:lo + seg, :] = y
            else:
                for h in range(tn // head_dim):
                    hs = slice(h * head_dim, (h + 1) * head_dim)
                    yh = y[:, hs]
                    inv = lax.rsqrt(jnp.sum(yh * yh, axis=-1, keepdims=True) + EPS)
                    yh = yh * inv
                    o_ref[lo:lo + seg, hs] = yh if norm_scale == 1.0 else yh * norm_scale
            prev = x[seg - SUBLANES:, :]
            if seq_rows is not None:
                tail_ref[lo // seg] = prev
    if seq_rows is None:
        carry_ref[...] = prev
        tail_ref[0] = prev


def _conv_proj(a, w_t, layer, off, n, conv_w, buf8, coff, t, tm, tn, head_dim, norm_scale, name):
    m, k = a.shape
    nj, ni = n // tn, m // tm
    blk0 = off // tn
    shift = off - blk0 * tn
    assert shift < LANES and shift % SUBLANES == 0 and n % tn == 0 and coff % tn == 0
    cblk = coff // tn
    per = tn // LANES
    if t >= tm:
        tps, seq_rows, nb = t // tm, None, 1
        buf_ix = lambda j, i: (i // tps, 0, cblk + j)
    else:
        tps, seq_rows, nb = 1, t, tm // t
        buf_ix = lambda j, i: (i, 0, cblk + j)
    in_specs = [pl.BlockSpec((tm, k), lambda j, i: (i, 0)),
                pl.BlockSpec((1, tn, k), lambda j, i: (layer, blk0 + j, 0))]
    args = [a, w_t]
    if shift:
        in_specs.append(pl.BlockSpec((1, LANES, k), lambda j, i: (layer, (blk0 + j + 1) * per, 0)))
        args.append(w_t)
    in_specs += [pl.BlockSpec((conv_w.shape[0], tn), lambda j, i: (0, cblk + j)),
                 pl.BlockSpec((nb, SUBLANES, tn), buf_ix)]
    out, tails = pl.pallas_call(
        functools.partial(_conv_proj_kernel, shift=shift, tiles_per_seq=tps, seq_rows=seq_rows,
                          head_dim=head_dim, norm_scale=norm_scale),
        grid=(nj, ni),
        in_specs=in_specs,
        out_specs=[pl.BlockSpec((tm, tn), lambda j, i: (i, j)),
                   pl.BlockSpec((nb, SUBLANES, tn), lambda j, i: (i, 0, j))],
        out_shape=[jax.ShapeDtypeStruct((m, n), F32),
                   jax.ShapeDtypeStruct((ni * nb, SUBLANES, n), F32)],
        scratch_shapes=[pltpu.VMEM((tn, k), BF16), pltpu.VMEM((SUBLANES, tn), F32)],
        compiler_params=_cp("parallel", "arbitrary"),
        name=name,
    )(*args, conv_w, buf8)
    return out, tails[tps - 1::tps]


def _proj_res_kernel(a_ref, w_ref, x_ref, g_ref, *outs, sub, final):
    tm = a_ref.shape[0]
    w = w_ref[...]
    g = g_ref[...]
    for r in range(tm // sub):
        rows = slice(r * sub, (r + 1) * sub)
        xn = x_ref[rows, :] + _dot(a_ref[rows, :], w)
        if final:
            outs[0][rows, :] = _rms(xn, g)
        else:
            outs[0][rows, :] = xn
            outs[1][rows, :] = _rms(xn, g).astype(outs[1].dtype)


def _proj_res(a, w, x, g, tm, sub, final, name):
    m, kk = a.shape
    d = w.shape[1]
    row = lambda i: (i, 0)
    out_specs = [pl.BlockSpec((tm, d), row)]
    out_shape = [jax.ShapeDtypeStruct((m, d), F32)]
    if not final:
        out_specs.append(pl.BlockSpec((tm, d), row))
        out_shape.append(jax.ShapeDtypeStruct((m, d), BF16))
    return pl.pallas_call(
        functools.partial(_proj_res_kernel, sub=min(sub, tm), final=final),
        grid=(m // tm,),
        in_specs=[pl.BlockSpec((tm, kk), row),
                  pl.BlockSpec((kk, d), lambda i: (0, 0), pipeline_mode=pl.Buffered(1)),
                  pl.BlockSpec((tm, d), row),
                  pl.BlockSpec((1, d), lambda i: (0, 0))],
        out_specs=out_specs,
        out_shape=out_shape,
        compiler_params=_cp("parallel"),
        name=name,
    )(a, w, x, g.reshape(1, d))


def _merge_kernel(y0_ref, y1_ref, y2_ref, w_ref, g_ref, b_ref, o_ref, *, sub):
    tm, d = o_ref.shape
    for r in range(tm // sub):
        rows = slice(r * sub, (r + 1) * sub)
        acc = None
        for n, y_ref in enumerate((y0_ref, y1_ref, y2_ref)):
            p = _dot(y_ref[rows, :], w_ref[n])
            g = g_ref[rows, n * d:(n + 1) * d].astype(F32)
            t = _sigmoid(g + b_ref[n:n + 1, :]) * p
            acc = t if acc is None else acc + t
        o_ref[rows, :] = acc.astype(o_ref.dtype)


def _merge(ys, w_branch, g_br, b_gate, tm):
    m, bw = ys[0].shape
    d = w_branch.shape[2]
    row = lambda i: (i, 0)
    y_spec = pl.BlockSpec((tm, bw), row)
    return pl.pallas_call(
        functools.partial(_merge_kernel, sub=min(tm, ROW_SUB)),
        grid=(m // tm,),
        in_specs=[y_spec, y_spec, y_spec,
                  pl.BlockSpec((N_BRANCH, bw, d), lambda i: (0, 0, 0), pipeline_mode=pl.Buffered(1)),
                  pl.BlockSpec((tm, N_BRANCH * d), row),
                  pl.BlockSpec((N_BRANCH, d), lambda i: (0, 0))],
        out_specs=pl.BlockSpec((tm, d), row),
        out_shape=jax.ShapeDtypeStruct((m, d), BF16),
        compiler_params=_cp("parallel"),
        name="branch_merge",
    )(ys[0], ys[1], ys[2], w_branch, g_br, b_gate)


def _xattn_kernel(q_ref, k_ref, v_ref, o_ref, *, scale):
    q = q_ref[...]
    k = k_ref[0].astype(BF16)
    v = v_ref[0].astype(BF16)
    s = _dot_nt(q, k) * scale
    s = s - jnp.max(s, axis=-1, keepdims=True)
    e = jnp.exp(s)
    p = e / jnp.sum(e, axis=-1, keepdims=True)
    o_ref[...] = _dot(p.astype(BF16), v).astype(o_ref.dtype)


def _xattn(q, mem_k, mem_v, t, tq):
    m, d = q.shape
    b = m // t
    nmem = mem_k.shape[1]
    dh = d // X_H
    nq = t // tq
    kv_spec = pl.BlockSpec((1, nmem, dh), lambda bi, h, i: (bi, 0, h))
    qo_spec = pl.BlockSpec((tq, dh), lambda bi, h, i: (bi * nq + i, h))
    return pl.pallas_call(
        functools.partial(_xattn_kernel, scale=dh ** -0.5),
        grid=(b, X_H, nq),
        in_specs=[qo_spec, kv_spec, kv_spec],
        out_specs=qo_spec,
        out_shape=jax.ShapeDtypeStruct((m, d), BF16),
        compiler_params=_cp("parallel", "parallel", "arbitrary"),
        name="xattn",
    )(q, mem_k, mem_v)


def _gelu_tanh(x):
    c = 0.7978845608028654
    return x * (0.5 * (1.0 + jnp.tanh(c * (x + 0.044715 * (x * x * x)))))


def _ffn_in_kernel(h_ref, wg_ref, wu_ref, cw_ref, cb_ref, buf_ref, act_ref, nbuf_ref,
                   wgb_ref, wub_ref, carry_ref, *, tiles_per_seq, seq_rows):
    i = pl.program_id(1)
    tm = h_ref.shape[0]
    sub = min(tm, FFN_SUB)
    seg = sub if seq_rows is None else seq_rows

    @pl.when(i == 0)
    def _():
        _cast_weight_tile(wg_ref, None, wgb_ref, 0)
        _cast_weight_tile(wu_ref, None, wub_ref, 0)

    if seq_rows is None:
        @pl.when(i % tiles_per_seq == 0)
        def _():
            carry_ref[...] = buf_ref[0]

        prev = carry_ref[...]

    wg, wu = wgb_ref[...], wub_ref[...]
    taps, cb = [cw_ref[n:n + 1, :] for n in range(cw_ref.shape[0])], cb_ref[...]
    for r in range(tm // sub):
        h = h_ref[r * sub:(r + 1) * sub, :]
        gate_all = _dot(h, wg)
        up_all = _dot(h, wu)
        for s in range(sub // seg):
            lo = r * sub + s * seg
            gate = gate_all[s * seg:(s + 1) * seg]
            if seq_rows is not None:
                prev = buf_ref[lo // seg]
            y = _causal_conv(gate, prev, taps)
            act = _gelu_tanh(y + cb) * up_all[s * seg:(s + 1) * seg]
            act_ref[lo:lo + seg, :] = act.astype(act_ref.dtype)
            prev = gate[seg - SUBLANES:, :]
            if seq_rows is not None:
                nbuf_ref[lo // seg] = prev
    if seq_rows is None:
        carry_ref[...] = prev
        nbuf_ref[0] = prev


def _ffn_in(h, w_ffn_in, layer, conv_w, conv_b, buf8, t, tm, tn):
    m, d = h.shape
    dff = w_ffn_in.shape[2] // 2
    ni, nj = m // tm, dff // tn
    if t >= tm:
        tps, seq_rows, nb = t // tm, None, 1
        buf_ix = lambda j, i: (i // tps, 0, j)
    else:
        tps, seq_rows, nb = 1, t, tm // t
        buf_ix = lambda j, i: (i, 0, j)
    act, tails = pl.pallas_call(
        functools.partial(_ffn_in_kernel, tiles_per_seq=tps, seq_rows=seq_rows),
        grid=(nj, ni),
        in_specs=[pl.BlockSpec((tm, d), lambda j, i: (i, 0)),
                  pl.BlockSpec((1, d, tn), lambda j, i: (layer, 0, j)),
                  pl.BlockSpec((1, d, tn), lambda j, i: (layer, 0, nj + j)),
                  pl.BlockSpec((FFN_CONV, tn), lambda j, i: (0, j)),
                  pl.BlockSpec((1, tn), lambda j, i: (0, j)),
                  pl.BlockSpec((nb, SUBLANES, tn), buf_ix)],
        out_specs=[pl.BlockSpec((tm, tn), lambda j, i: (i, j)),
                   pl.BlockSpec((nb, SUBLANES, tn), lambda j, i: (i, 0, j))],
        out_shape=[jax.ShapeDtypeStruct((m, dff), BF16),
                   jax.ShapeDtypeStruct((ni * nb, SUBLANES, dff), F32)],
        scratch_shapes=[pltpu.VMEM((d, tn), BF16), pltpu.VMEM((d, tn), BF16),
                        pltpu.VMEM((SUBLANES, tn), F32)],
        compiler_params=_cp("parallel", "arbitrary"),
        name="ffn_in",
    )(h, w_ffn_in, w_ffn_in, conv_w, conv_b.reshape(1, dff), buf8)
    return act, tails[tps - 1::tps]


def _ret_kernel(ld_ref, p_ref, cos_ref, sin_ref, s0_ref, y_ref, s_ref, *, c):
    t = pl.program_id(1)
    tb = p_ref.shape[0]
    half = RET_DK // 2
    hw = RET_H * RET_DK
    heads = range(RET_H)

    @pl.when(t == 0)
    def _():
        s_ref[...] = s0_ref[...]

    ri = lax.broadcasted_iota(jnp.int32, (c, c), 0)
    ci = lax.broadcasted_iota(jnp.int32, (c, c), 1)
    diff = (ri - ci).astype(F32)
    causal = ri >= ci
    idx = lax.broadcasted_iota(jnp.int32, (c, RET_DK), 0).astype(F32)
    lds = [ld_ref[h] for h in heads]
    dmask = [jnp.where(causal, jnp.exp(ld * jnp.where(causal, diff, 0.0)), 0.0) for ld in lds]
    q_dec = [jnp.exp(ld * (idx + 1.0)) for ld in lds]
    k_dec = [jnp.exp(ld * (c - 1.0 - idx)) for ld in lds]
    s_dec = [jnp.exp(jnp.full((1, RET_DV), ld * c, F32)) for ld in lds]

    def rot(x, cos, sin):
        x1, x2 = x[:, :half], x[:, half:]
        return jnp.concatenate([x1 * cos - x2 * sin, x1 * sin + x2 * cos], axis=-1)

    def body(j, carry):
        rows = pl.ds(pl.multiple_of(j * c, c), c)
        cos, sin = cos_ref[rows, :], sin_ref[rows, :]
        q = [rot(p_ref[rows, h * RET_DK:(h + 1) * RET_DK], cos, sin) for h in heads]
        k = [rot(p_ref[rows, hw + h * RET_DK:hw + (h + 1) * RET_DK], cos, sin) * (RET_DK ** -0.5) for h in heads]
        vb = [p_ref[rows, 2 * hw + h * RET_DV:2 * hw + (h + 1) * RET_DV].astype(BF16) for h in heads]
        qb = [x.astype(BF16) for x in q]
        att = [(_dot_nt(qb[h], k[h].astype(BF16)) * dmask[h]).astype(BF16) for h in heads]
        s = [s_ref[0, h] for h in heads]
        o = [_dot(att[h], vb[h]) + _dot(qb[h], s[h].astype(BF16)) * q_dec[h] for h in heads]
        for h in heads:
            s_ref[0, h] = s_dec[h] * s[h] + _dot_tn((k[h] * k_dec[h]).astype(BF16), vb[h])
        for h in heads:
            g = p_ref[rows, 3 * hw + h * RET_DV:3 * hw + (h + 1) * RET_DV]
            y_ref[rows, h * RET_DV:(h + 1) * RET_DV] = (_rms(o[h]) * _silu(g)).astype(y_ref.dtype)
        return carry

    lax.fori_loop(0, tb // c, body, 0)


def _retention(p_ret, cos, sin, ld, s0, t, tb):
    m, width = p_ret.shape
    b = m // t
    c = min(CHUNK, t)
    nt = t // tb
    st_spec = pl.BlockSpec((1, RET_H, RET_DK, RET_DV), lambda bi, ti: (bi, 0, 0, 0))
    rope_spec = pl.BlockSpec((tb, RET_DK // 2), lambda bi, ti: (ti, 0))
    return pl.pallas_call(
        functools.partial(_ret_kernel, c=c),
        grid=(b, nt),
        in_specs=[pl.BlockSpec(memory_space=pltpu.SMEM),
                  pl.BlockSpec((tb, width), lambda bi, ti: (bi * nt + ti, 0)),
                  rope_spec, rope_spec, st_spec],
        out_specs=[pl.BlockSpec((tb, RET_H * RET_DV), lambda bi, ti: (bi * nt + ti, 0)), st_spec],
        out_shape=[jax.ShapeDtypeStruct((m, RET_H * RET_DV), BF16),
                   jax.ShapeDtypeStruct((b, RET_H, RET_DK, RET_DV), F32)],
        compiler_params=_cp("parallel", "arbitrary"),
        name="retention",
    )(ld, p_ret, cos, sin, s0)


def _gla_kernel(p_ref, lr_ref, wgk_ref, bgk_ref, nrm_ref, s0_ref, y_ref, s_ref, st_ref, *, c, nt):
    t = pl.program_id(1)
    tb = p_ref.shape[0]
    hk = GLA_H * GLA_DK
    hv = GLA_H * GLA_DV
    heads = range(GLA_H)

    @pl.when(t == 0)
    def _():
        for h in heads:
            st_ref[h] = s0_ref[0, h].T

    ri = lax.broadcasted_iota(jnp.int32, (c, c), 0)
    ci = lax.broadcasted_iota(jnp.int32, (c, c), 1)
    causal = ri >= ci
    tri = jnp.where(causal, 1.0, 0.0).astype(BF16)
    wgk = wgk_ref[...]
    bgk = bgk_ref[...]
    nrm = nrm_ref[...]

    nchunk = tb // c
    nc = GLA_CHUNKS_PER_STEP if nchunk % GLA_CHUNKS_PER_STEP == 0 else 1
    hs = lambda x, h: x[:, h * GLA_DK:(h + 1) * GLA_DK]

    def body(jj, carry):
        rows = [pl.ds(pl.multiple_of((jj * nc + ci) * c, c), c) for ci in range(nc)]
        z = [_dot(lr_ref[r, :].astype(BF16), wgk) + bgk for r in rows]
        bc = [_tri_cumsum(tri, -_softplus(-zz) / GLA_NORMALIZER) for zz in z]
        ch = []
        for r, b_ in zip(rows, bc):
            q = p_ref[r, 0:hk] * (GLA_DK ** -0.5)
            k = p_ref[r, hk:2 * hk]
            b_last = b_[c - 1:c, :]
            ch.append(dict(
                rows=r, qe=(q * jnp.exp(b_)).astype(BF16), ke=(k * jnp.exp(-b_)).astype(BF16),
                kd=(k * jnp.exp(b_last - b_)).astype(BF16), eb_last=jnp.exp(b_last),
                vb=[p_ref[r, 2 * hk + h * GLA_DV:2 * hk + (h + 1) * GLA_DV].astype(BF16) for h in heads]))
        probs = [(d, h) for d in ch for h in heads]
        att = [jnp.where(causal, _dot_nt(hs(d['qe'], h), hs(d['ke'], h)), 0.0).astype(BF16) for d, h in probs]
        av = [_dot(a, d['vb'][h]) for a, (d, h) in zip(att, probs)]

        for ci, d in enumerate(ch):
            st = [st_ref[h] for h in heads]
            o = [av[ci * GLA_H + h] + _dot_nt(hs(d['qe'], h), st[h].astype(BF16)) for h in heads]
            for h in heads:
                st_ref[h] = hs(d['eb_last'], h) * st[h] + _dot_tn(d['vb'][h], hs(d['kd'], h))
            for h in heads:
                g = p_ref[d['rows'], 2 * hk + hv + h * GLA_DV:2 * hk + hv + (h + 1) * GLA_DV]
                y_ref[d['rows'], h * GLA_DV:(h + 1) * GLA_DV] = (_rms(o[h], nrm) * _silu(g)).astype(y_ref.dtype)
        return carry

    lax.fori_loop(0, nchunk // nc, body, 0)

    @pl.when(t == nt - 1)
    def _():
        for h in heads:
            s_ref[0, h] = st_ref[h].T


def _gla(p_gla, p_small, wgk, bgk, nrm, s0, t, tb):
    m, width = p_gla.shape
    b = m // t
    c = min(CHUNK, t)
    nt = t // tb
    const = lambda bi, ti: (0, 0)
    st_spec = pl.BlockSpec((1, GLA_H, GLA_DK, GLA_DV), lambda bi, ti: (bi, 0, 0, 0))
    return pl.pallas_call(
        functools.partial(_gla_kernel, c=c, nt=nt),
        grid=(b, nt),
        in_specs=[pl.BlockSpec((tb, width), lambda bi, ti: (bi * nt + ti, 0)),
                  pl.BlockSpec((tb, LANES), lambda bi, ti: (bi * nt + ti, 0)),
                  pl.BlockSpec((LANES, GLA_H * GLA_DK), const),
                  pl.BlockSpec((1, GLA_H * GLA_DK), const),
                  pl.BlockSpec((1, GLA_DV), const),
                  st_spec],
        out_specs=[pl.BlockSpec((tb, GLA_H * GLA_DV), lambda bi, ti: (bi * nt + ti, 0)), st_spec],
        out_shape=[jax.ShapeDtypeStruct((m, GLA_H * GLA_DV), BF16),
                   jax.ShapeDtypeStruct((b, GLA_H, GLA_DK, GLA_DV), F32)],
        scratch_shapes=[pltpu.VMEM((GLA_H, GLA_DV, GLA_DK), F32)],
        compiler_params=_cp("parallel", "arbitrary"),
        name="gla",
    )(p_gla, p_small, wgk, bgk, nrm, s0)


def _unit_lower_inverse_minus_eye(mats, c):
    ri = lax.broadcasted_iota(jnp.int32, (c, c), 0)
    ci = lax.broadcasted_iota(jnp.int32, (c, c), 1)

    def level_mask(k):
        same = (ri ^ ci) < 2 * k
        return same & ((ri & k) != 0) & ((ci & k) == 0)

    m1 = level_mask(1)
    ns = [-jnp.where(m1, a, 0.0) for a in mats]
    k = 2
    while k < c:
        mk = level_mask(k)
        ls = [jnp.where(mk, a, 0.0) for a in mats]
        nbs = [n.astype(BF16) for n in ns]
        ys = [l + _dot(nb, l.astype(BF16)) for l, nb in zip(ls, nbs)]
        xs = [y + _dot(y.astype(BF16), nb) for y, nb in zip(ys, nbs)]
        ns = [n - x for n, x in zip(ns, xs)]
        k *= 2
    return ns


def _gdn_kernel(q_ref, k_ref, v_ref, z_ref, sm_ref, alog_ref, dtb_ref, nrm_ref, s0_ref,
                y_ref, s_ref, *, c):
    t = pl.program_id(1)
    tb = q_ref.shape[0]

    @pl.when(t == 0)
    def _():
        s_ref[...] = s0_ref[...]

    ri = lax.broadcasted_iota(jnp.int32, (c, c), 0)
    ci = lax.broadcasted_iota(jnp.int32, (c, c), 1)
    incl = ri >= ci
    strict = ri > ci
    tri = jnp.where(incl, 1.0, 0.0).astype(BF16)
    lane = lax.broadcasted_iota(jnp.int32, (1, LANES), 1)
    a_lanes = jnp.logical_and(lane >= SMALL_A0, lane < SMALL_A0 + GDN_H)
    sel = jnp.where(lax.broadcasted_iota(jnp.int32, (SUBLANES, LANES), 1)
                    == lax.broadcasted_iota(jnp.int32, (SUBLANES, LANES), 0) + SMALL_A0,
                    1.0, 0.0).astype(BF16)
    neg_a = -jnp.exp(alog_ref[...])
    dtb = dtb_ref[...]
    nrm = nrm_ref[...]

    def col(x, lane_idx):
        return x[:, lane_idx:lane_idx + 1]

    heads = range(GDN_H)
    nchunk = tb // c
    nc = GDN_CHUNKS_PER_STEP if nchunk % GDN_CHUNKS_PER_STEP == 0 else 1

    def body(jj, carry):
        ch = []
        for ci in range(nc):
            rows = pl.ds(pl.multiple_of((jj * nc + ci) * c, c), c)
            sm = sm_ref[rows, :]
            ch.append(dict(
                rows=rows,
                g_all=jnp.where(a_lanes, neg_a * _softplus(sm + dtb), 0.0),
                beta_all=_sigmoid(sm),
                q=[q_ref[rows, h * GDN_DK:(h + 1) * GDN_DK] for h in heads],
                k=[k_ref[rows, h * GDN_DK:(h + 1) * GDN_DK] for h in heads],
                v=[v_ref[rows, h * GDN_DV:(h + 1) * GDN_DV] for h in heads]))
        for d in ch:
            d['gam'] = _tri_cumsum(tri, d['g_all'])
        for d in ch:
            ghi, gmid, glo = _split3(d['gam'])
            d['gam_rows'] = _dot_nt(sel, ghi) + _dot_nt(sel, gmid) + _dot_nt(sel, glo)
        for d in ch:
            gam = d['gam']
            g_last = gam[c - 1:c, :]
            d['eg_all'] = jnp.exp(gam)
            d['egl_all'] = jnp.exp(g_last - gam)
            d['eg_last_all'] = jnp.exp(g_last)
            d['beta'] = [col(d['beta_all'], SMALL_B0 + h) for h in heads]
            d['eg'] = [col(d['eg_all'], SMALL_A0 + h) for h in heads]
            d['decay'] = [jnp.where(incl, jnp.exp(jnp.where(
                incl, col(gam, SMALL_A0 + h) - d['gam_rows'][h:h + 1, :], 0.0)), 0.0) for h in heads]
        probs = [(d, h) for d in ch for h in heads]
        qk_kk = [_dot_nt(jnp.concatenate([d['q'][h], d['k'][h]], axis=0).astype(BF16), d['k'][h].astype(BF16))
                 for d, h in probs]
        att = [(r[:c] * d['decay'][h]).astype(BF16) for r, (d, h) in zip(qk_kk, probs)]
        a = [jnp.where(strict, d['beta'][h] * r[c:] * d['decay'][h], 0.0) for r, (d, h) in zip(qk_kk, probs)]
        n = _unit_lower_inverse_minus_eye(a, c)
        rhs = [jnp.concatenate([d['beta'][h] * d['v'][h], (d['beta'][h] * d['eg'][h]) * d['k'][h]], axis=-1)
               for d, h in probs]
        sol = [r + _dot(nn.astype(BF16), r.astype(BF16)) for r, nn in zip(rhs, n)]

        for ci, d in enumerate(ch):
            sl = sol[ci * GDN_H:(ci + 1) * GDN_H]
            at = att[ci * GDN_H:(ci + 1) * GDN_H]
            s = [s_ref[0, h] for h in heads]
            wq_s = [_dot(jnp.concatenate([sl[h][:, GDN_DV:], d['q'][h]], axis=0).astype(BF16), s[h].astype(BF16))
                    for h in heads]
            wb = [(sl[h][:, :GDN_DV] - wq_s[h][:c]).astype(BF16) for h in heads]
            o = [d['eg'][h] * wq_s[h][c:] + _dot(at[h], wb[h]) for h in heads]
            kd = [(d['k'][h] * col(d['egl_all'], SMALL_A0 + h)).astype(BF16) for h in heads]
            for h in heads:
                s_ref[0, h] = col(d['eg_last_all'], SMALL_A0 + h) * s[h] + _dot_tn(kd[h], wb[h])
            for h in heads:
                z = z_ref[d['rows'], h * GDN_DV:(h + 1) * GDN_DV]
                y = _rms(o[h], nrm) * _silu(z)
                y_ref[d['rows'], h * GDN_DV:(h + 1) * GDN_DV] = y.astype(y_ref.dtype)
        return carry

    lax.fori_loop(0, nchunk // nc, body, 0)


def _gdn(q, k, v, z, p_small, alog_row, dtb_row, nrm, s0, t, tb):
    m = q.shape[0]
    b = m // t
    c = min(CHUNK, t)
    nt = t // tb
    st_spec = pl.BlockSpec((1, GDN_H, GDN_DK, GDN_DV), lambda bi, ti: (bi, 0, 0, 0))
    const = lambda bi, ti: (0, 0)
    row = lambda bi, ti: (bi * nt + ti, 0)
    qk_spec = pl.BlockSpec((tb, GDN_H * GDN_DK), row)
    v_spec = pl.BlockSpec((tb, GDN_H * GDN_DV), row)
    return pl.pallas_call(
        functools.partial(_gdn_kernel, c=c),
        grid=(b, nt),
        in_specs=[qk_spec, qk_spec, v_spec, v_spec,
                  pl.BlockSpec((tb, LANES), row),
                  pl.BlockSpec((1, LANES), const),
                  pl.BlockSpec((1, LANES), const),
                  pl.BlockSpec((1, GDN_DV), const),
                  st_spec],
        out_specs=[v_spec, st_spec],
        out_shape=[jax.ShapeDtypeStruct((m, GDN_H * GDN_DV), BF16),
                   jax.ShapeDtypeStruct((b, GDN_H, GDN_DK, GDN_DV), F32)],
        compiler_params=_cp("parallel", "arbitrary"),
        name="gated_deltanet",
    )(q, k, v, z, p_small, alog_row, dtb_row, nrm, s0)


def _pad_rows_front(buf, rows):
    return jnp.pad(buf, ((0, 0), (0, 0), (rows - buf.shape[2], 0), (0, 0)))


def _in_proj_windows(d):
    sizes = dict(ret=4 * RET_H * RET_DK, gla=2 * GLA_H * GLA_DK + 2 * GLA_H * GLA_DV, lr=GLA_RANK,
                 gdn=2 * GDN_H * GDN_DK + 2 * GDN_H * GDN_DV, ab=2 * GDN_H, gbr=N_BRANCH * d)
    win, o = {}, 0
    for name in ('ret', 'gla', 'lr', 'gdn', 'ab', 'gbr'):
        win[name] = (o, sizes[name])
        o += sizes[name]
    assert win['lr'][0] % LANES == 0 and win['ab'][0] % LANES == SMALL_A0
    return win


def _prep_weights(w):
    lane_row = lambda v, at: jnp.pad(v, ((0, 0), (at, LANES - at - v.shape[1])))[:, None, :]
    return dict(
        w_in_t=jnp.swapaxes(w['w_in'], 1, 2), w_xq=w['w_xq'], w_ffn_in=w['w_ffn_in'],
        w_gk=jnp.pad(w['w_gla_gk'], ((0, 0), (0, LANES - GLA_RANK), (0, 0))).astype(BF16),
        b_gk=w['b_gla_gk'][:, None, :],
        gla_norm=w['gla_norm'][:, None, :], gdn_norm=w['gdn_norm'][:, None, :],
        alog_row=lane_row(w['gdn_a_log'], SMALL_A0), dtb_row=lane_row(w['gdn_dt_bias'], SMALL_A0),
        gdn_conv_w=w['gdn_conv_w'], b_gate=w['b_gate'],
        w_branch=w['w_branch'].astype(BF16), w_out=w['w_out'].astype(BF16), w_xo=w['w_xo'].astype(BF16),
        ffn_conv_w=w['ffn_conv_w'], ffn_conv_b=w['ffn_conv_b'], w_ffn_out=w['w_ffn_out'].astype(BF16),
        ln_mix=w['ln_mix'], ln_xattn=w['ln_xattn'], ln_ffn=w['ln_ffn'], ln_final=w['ln_final'],
    )


def _tile(n, cap):
    t = min(n, cap)
    while n % t:
        t -= 1
    return t


def _trunk(x, offset, mem_k, mem_v, s_ret, s_gla, s_gdn, buf_gdn, buf_ffn, pw):
    b, t, d = x.shape
    m = b * t
    depth = s_ret.shape[0]
    x2 = x.reshape(m, d)
    tm = _tile(m, MM_TM)
    tm_res = _tile(m, RES_TM)
    tm_res_k = _tile(m, RES_TM_K)
    tm_ffo = _tile(m, FFO_TM)
    tb = _tile(t, SEQ_TB)
    tb_gdn = _tile(t, SEQ_TB_GDN)
    tq = _tile(t, XATTN_TQ)
    tm_ffn = _tile(t, FFN_TM) if t >= FFN_TM else t * _tile(b, max(1, FFN_TM // t))

    pos = offset + jnp.arange(t, dtype=F32)
    half = RET_DK // 2
    freqs = 1.0 / (ROPE_BASE ** (jnp.arange(half, dtype=F32) / half))
    ang = pos[:, None] * freqs[None, :]
    cos, sin = jnp.cos(ang), jnp.sin(ang)
    ld = jnp.log1p(-jnp.exp2(-5.0 - jnp.arange(RET_H, dtype=F32)))

    buf_gdn8 = _pad_rows_front(buf_gdn, SUBLANES)
    buf_ffn8 = _pad_rows_front(buf_ffn, SUBLANES)
    win = _in_proj_windows(d)

    def in_proj(a, l, name, out_dtype=F32):
        off, n = win[name]
        blk0 = off // MM_TN
        shift = off - blk0 * MM_TN
        assert shift < LANES and n % MM_TN == 0
        return _mm_ws(a, pw['w_in_t'], l, blk0, n, shift, out_dtype, tm, MM_TN, "in_proj_" + name,
                      transposed=True)

    outs = ([], [], [], [], [])
    hn = _norm(x2, pw['ln_mix'][0], tm_res_k)
    y = None
    for l in range(depth):
        p_ret = in_proj(hn, l, 'ret')
        p_gla = in_proj(hn, l, 'gla')
        g_off = win['gdn'][0]
        hq, hv = GDN_H * GDN_DK, GDN_H * GDN_DV
        conv_args = (pw['gdn_conv_w'][l], buf_gdn8[l])
        d_q, tq8 = _conv_proj(hn, pw['w_in_t'], l, g_off, hq, *conv_args, 0, t, tm, MM_TN,
                              GDN_DK, GDN_DK ** -0.5, "in_proj_gdn_q")
        d_k, tk8 = _conv_proj(hn, pw['w_in_t'], l, g_off + hq, hq, *conv_args, hq, t, tm, MM_TN,
                              GDN_DK, 1.0, "in_proj_gdn_k")
        d_v, tv8 = _conv_proj(hn, pw['w_in_t'], l, g_off + 2 * hq, hv, *conv_args, 2 * hq, t, tm, MM_TN,
                              GDN_DV, None, "in_proj_gdn_v")
        zo = g_off + 2 * hq + hv
        d_z = _mm_ws(hn, pw['w_in_t'], l, zo // MM_TN, hv, zo % MM_TN, F32, tm, MM_TN, "in_proj_gdn_z",
                     transposed=True)
        bg8 = jnp.concatenate([tq8, tk8, tv8], axis=-1)
        g_br = in_proj(hn, l, 'gbr', BF16)
        p_small = _small_proj(hn, pw['w_in_t'], l, win['lr'][0] // LANES, win['ab'][0] // LANES, tm)

        y_ret, sr = _retention(p_ret, cos, sin, ld, s_ret[l], t, tb)
        y_gla, sg = _gla(p_gla, p_small, pw['w_gk'][l], pw['b_gk'][l], pw['gla_norm'][l], s_gla[l], t, tb)
        y_gdn, sd = _gdn(d_q, d_k, d_v, d_z, p_small, pw['alog_row'][l], pw['dtb_row'][l],
                         pw['gdn_norm'][l], s_gdn[l], t, tb_gdn)

        merged = _merge((y_ret, y_gla, y_gdn), pw['w_branch'][l], g_br, pw['b_gate'][l], tm_res_k)
        x2, hx = _proj_res(merged, pw['w_out'][l], x2, pw['ln_xattn'][l], tm_res, ROW_SUB, False, "out_proj")

        q = _mm_ws(hx, pw['w_xq'], l, 0, d, 0, BF16, tm, MM_TN, "xattn_q")
        o = _xattn(q, mem_k[l], mem_v[l], t, tq)
        x2, hf = _proj_res(o, pw['w_xo'][l], x2, pw['ln_ffn'][l], tm_res, ROW_SUB, False, "xattn_out")

        act, bf8 = _ffn_in(hf, pw['w_ffn_in'], l, pw['ffn_conv_w'][l], pw['ffn_conv_b'][l],
                           buf_ffn8[l], t, tm_ffn, FFN_TN)
        if l + 1 < depth:
            x2, hn = _proj_res(act, pw['w_ffn_out'][l], x2, pw['ln_mix'][l + 1], tm_ffo, ROW_SUB, False, "ffn_out")
        else:
            y, = _proj_res(act, pw['w_ffn_out'][l], x2, pw['ln_final'], tm_ffo, ROW_SUB, True, "ffn_out_final")

        for lst, val in zip(outs, (sr, sg, sd, bg8[:, SUBLANES - (GDN_CONV - 1):], bf8[:, SUBLANES - (FFN_CONV - 1):])):
            lst.append(val)
    return (y.reshape(b, t, d),) + tuple(jnp.stack(lst) for lst in outs)


def kernel(x_prompt, x_sample, mem_prompt, state_ret, state_gla, state_gdn, state_gdn_conv, state_ffn_conv, cache_mem_k, cache_mem_v, ln_mix, w_in, w_gla_gk, b_gla_gk, gla_norm, gdn_conv_w, gdn_a_log, gdn_dt_bias, gdn_norm, b_gate, w_branch, w_out, ln_xattn, ln_mem, w_xq, w_xkv, w_xo, ln_ffn, w_ffn_in, ffn_conv_w, ffn_conv_b, w_ffn_out, ln_final):
    pw = _prep_weights(dict(
        w_in=w_in, w_gla_gk=w_gla_gk, b_gla_gk=b_gla_gk, gla_norm=gla_norm, gdn_conv_w=gdn_conv_w,
        gdn_a_log=gdn_a_log, gdn_dt_bias=gdn_dt_bias, gdn_norm=gdn_norm, b_gate=b_gate, w_branch=w_branch,
        w_out=w_out, w_xq=w_xq, w_xo=w_xo, w_ffn_in=w_ffn_in, ffn_conv_w=ffn_conv_w, ffn_conv_b=ffn_conv_b,
        w_ffn_out=w_ffn_out, ln_mix=ln_mix, ln_xattn=ln_xattn, ln_ffn=ln_ffn, ln_final=ln_final))
    depth = w_in.shape[0]
    bp, nmem, d = mem_prompt.shape
    dt = x_prompt.dtype

    mem2 = mem_prompt.reshape(bp * nmem, d)
    tmem = _tile(bp * nmem, MM_TM)
    mk, mv = [], []
    for l in range(depth):
        hm = _norm(mem2, ln_mem[l], _tile(bp * nmem, RES_TM_K))
        mk.append(_mm_ws(hm, w_xkv, l, 0, d, 0, F32, tmem, MM_TN, "mem_k"))
        mv.append(_mm_ws(hm, w_xkv, l, d // MM_TN, d, 0, F32, tmem, MM_TN, "mem_v"))
    dh = d // X_H
    mem_k_p = jnp.stack(mk).reshape(depth, bp, nmem, X_H, dh)
    mem_v_p = jnp.stack(mv).reshape(depth, bp, nmem, X_H, dh)

    zeros = lambda *s: jnp.zeros((depth, bp) + s, dt)
    y_p, ret_p, gla_p, gdn_p, gdn_conv_p, ffn_conv_p = _trunk(
        x_prompt, 0.0, [a.reshape(bp, nmem, d) for a in mk], [a.reshape(bp, nmem, d) for a in mv],
        zeros(RET_H, RET_DK, RET_DV), zeros(GLA_H, GLA_DK, GLA_DV), zeros(GDN_H, GDN_DK, GDN_DV),
        zeros(GDN_CONV - 1, state_gdn_conv.shape[-1]), zeros(FFN_CONV - 1, state_ffn_conv.shape[-1]), pw)

    past_len = 4096.0
    bs = x_sample.shape[0]
    y_s, ret_s, gla_s, gdn_s, gdn_conv_s, ffn_conv_s = _trunk(
        x_sample, past_len,
        [cache_mem_k[l].reshape(bs, nmem, d) for l in range(depth)],
        [cache_mem_v[l].reshape(bs, nmem, d) for l in range(depth)],
        state_ret, state_gla, state_gdn, state_gdn_conv, state_ffn_conv, pw)

    return (y_p, y_s, ret_p, gla_p, gdn_p, gdn_conv_p, ffn_conv_p, mem_k_p, mem_v_p,
            ret_s, gla_s, gdn_s, gdn_conv_s, ffn_conv_s)
```

```python
import functools

import jax
import jax.numpy as jnp
from jax import lax
from jax.experimental import pallas as pl
from jax.experimental.pallas import tpu as pltpu

F32 = jnp.float32
BF16 = jnp.bfloat16

EPS = 1e-6
CHUNK = 64
RET_H, RET_DK, RET_DV = 4, 256, 256
GLA_H, GLA_DK, GLA_DV = 4, 128, 256
GLA_RANK = 16
GLA_NORMALIZER = 16.0
GDN_H, GDN_DK, GDN_DV = 8, 128, 128
GDN_CONV = 4
N_BRANCH = 3
X_H = 4
FFN_CONV = 3
ROPE_BASE = 10000.0

LANES = 128
SUBLANES = 8
SMALL_A0 = GLA_RANK
SMALL_B0 = GLA_RANK + GDN_H
VMEM_LIMIT = 52 * 1024 * 1024
MM_TM = 1024
MM_TN = 1024
RES_TM = 512
RES_TM_K = 512
FFO_TM = 256
ROW_SUB = 256
CONV_SUB = 256
SEQ_TB = 512
SEQ_TB_GDN = 256
GDN_CHUNKS_PER_STEP = 2
GLA_CHUNKS_PER_STEP = 2
RET_CHUNKS_PER_STEP = 2
XATTN_TQ = 512
FFN_TM = 1024
FFN_TN = 512
FFN_SUB = 256


def _cp(*sem):
    return pltpu.CompilerParams(dimension_semantics=sem, vmem_limit_bytes=VMEM_LIMIT)


def _dot(a, b):
    return jnp.dot(a, b, preferred_element_type=F32)


def _dot_nt(a, b):
    return lax.dot_general(a, b, (((1,), (1,)), ((), ())), preferred_element_type=F32)


def _dot_tn(a, b):
    return lax.dot_general(a, b, (((0,), (0,)), ((), ())), preferred_element_type=F32)


def _sigmoid(x):
    return 1.0 / (1.0 + jnp.exp(-x))


def _silu(x):
    return x * _sigmoid(x)


def _softplus(x):
    return jnp.maximum(x, 0.0) + jnp.log1p(jnp.exp(-jnp.abs(x)))


def _rms(x, g=None):
    y = x * lax.rsqrt(jnp.mean(x * x, axis=-1, keepdims=True) + EPS)
    return y if g is None else y * g


def _split3(x):
    hi = x.astype(BF16)
    r = x - hi.astype(F32)
    mid = r.astype(BF16)
    lo = (r - mid.astype(F32)).astype(BF16)
    return hi, mid, lo


def _tri_cumsum(tri, x):
    hi, mid, lo = _split3(x)
    return _dot(tri, hi) + _dot(tri, mid) + _dot(tri, lo)


def _norm_kernel(x_ref, g_ref, o_ref):
    o_ref[...] = _rms(x_ref[...], g_ref[...]).astype(o_ref.dtype)


def _norm(x, g, tm):
    m, d = x.shape
    return pl.pallas_call(
        _norm_kernel,
        grid=(m // tm,),
        in_specs=[pl.BlockSpec((tm, d), lambda i: (i, 0)),
                  pl.BlockSpec((1, d), lambda i: (0, 0))],
        out_specs=pl.BlockSpec((tm, d), lambda i: (i, 0)),
        out_shape=jax.ShapeDtypeStruct((m, d), BF16),
        compiler_params=_cp("parallel"),
        name="rmsnorm",
    )(x, g.reshape(1, d))


CAST_ROWS = 256


def _cast_weight_tile(w_ref, wx_ref, wb_ref, shift):
    kk, tn = wb_ref.shape
    for r in range(0, kk, CAST_ROWS):
        rows = slice(r, min(r + CAST_ROWS, kk))
        if shift == 0:
            wb_ref[rows, :] = w_ref[0, rows, :].astype(BF16)
        else:
            wide = jnp.concatenate([w_ref[0, rows, :], wx_ref[0, rows, :]], axis=1)
            wb_ref[rows, :] = pltpu.roll(wide, wide.shape[1] - shift, 1)[:, :tn].astype(BF16)


def _cast_weight_tile_t(w_ref, wx_ref, wb_ref, shift):
    tn, _ = wb_ref.shape
    for r in range(0, tn, CAST_ROWS):
        lo, hi = r + shift, min(r + CAST_ROWS, tn) + shift
        if hi <= tn:
            src = w_ref[0, lo:hi, :]
        else:
            src = jnp.concatenate([w_ref[0, lo:tn, :], wx_ref[0, 0:hi - tn, :]], axis=0)
        wb_ref[r:min(r + CAST_ROWS, tn), :] = src.astype(BF16)


def _mm_ws_kernel(a_ref, w_ref, *rest, shift, transposed):
    if shift:
        wx_ref, o_ref, wb_ref = rest
    else:
        wx_ref = None
        o_ref, wb_ref = rest

    @pl.when(pl.program_id(1) == 0)
    def _():
        (_cast_weight_tile_t if transposed else _cast_weight_tile)(w_ref, wx_ref, wb_ref, shift)

    mm = _dot_nt if transposed else _dot
    o_ref[...] = mm(a_ref[...], wb_ref[...]).astype(o_ref.dtype)


def _mm_ws(a, w, layer, blk0, n, shift, out_dtype, tm, tn, name, transposed=False):
    m, k = a.shape
    nj = n // tn
    per = tn // LANES
    if transposed:
        assert shift % SUBLANES == 0
        w_spec = pl.BlockSpec((1, tn, k), lambda j, i: (layer, blk0 + j, 0))
        wx_spec = pl.BlockSpec((1, LANES, k), lambda j, i: (layer, (blk0 + j + 1) * per, 0))
    else:
        w_spec = pl.BlockSpec((1, k, tn), lambda j, i: (layer, 0, blk0 + j))
        wx_spec = pl.BlockSpec((1, k, LANES), lambda j, i: (layer, 0, (blk0 + j + 1) * per))
    in_specs = [pl.BlockSpec((tm, k), lambda j, i: (i, 0)), w_spec]
    args = [a, w]
    if shift:
        in_specs.append(wx_spec)
        args.append(w)
    return pl.pallas_call(
        functools.partial(_mm_ws_kernel, shift=shift, transposed=transposed),
        grid=(nj, m // tm),
        in_specs=in_specs,
        out_specs=pl.BlockSpec((tm, tn), lambda j, i: (i, j)),
        out_shape=jax.ShapeDtypeStruct((m, n), out_dtype),
        scratch_shapes=[pltpu.VMEM((tn, k) if transposed else (k, tn), BF16)],
        compiler_params=_cp("parallel", "arbitrary"),
        name=name,
    )(*args)


def _small_proj_kernel(a_ref, w0_ref, w1_ref, o_ref):
    a = a_ref[...]
    p0 = _dot_nt(a, w0_ref[0].astype(BF16))
    p1 = _dot_nt(a, w1_ref[0].astype(BF16))
    lane = lax.broadcasted_iota(jnp.int32, p0.shape, 1)
    o_ref[...] = jnp.where(lane < SMALL_A0, p0, p1)


def _small_proj(a, w_t, layer, blk_lr, blk_ab, tm):
    m, k = a.shape
    return pl.pallas_call(
        _small_proj_kernel,
        grid=(m // tm,),
        in_specs=[pl.BlockSpec((tm, k), lambda i: (i, 0)),
                  pl.BlockSpec((1, LANES, k), lambda i: (layer, blk_lr, 0)),
                  pl.BlockSpec((1, LANES, k), lambda i: (layer, blk_ab, 0))],
        out_specs=pl.BlockSpec((tm, LANES), lambda i: (i, 0)),
        out_shape=jax.ShapeDtypeStruct((m, LANES), F32),
        compiler_params=_cp("parallel"),
        name="in_proj_small",
    )(a, w_t, w_t)


def _shift_rows(x, prev8, s):
    xs = pltpu.roll(x, s, 0)
    ps = pltpu.roll(prev8, s, 0)
    r8 = lax.broadcasted_iota(jnp.int32, prev8.shape, 0)
    top = jnp.where(r8 < s, ps, xs[:SUBLANES])
    if x.shape[0] == SUBLANES:
        return top
    return jnp.concatenate([top, xs[SUBLANES:]], axis=0)


def _causal_conv(x, prev8, taps):
    width = len(taps)
    y = None
    for i, w in enumerate(taps):
        s = width - 1 - i
        xs = x if s == 0 else _shift_rows(x, prev8, s)
        y = xs * w if y is None else y + xs * w
    return y


def _conv_proj_kernel(a_ref, w_ref, *rest, shift, tiles_per_seq, seq_rows, head_dim, norm_scale):
    if shift:
        wx_ref, cw_ref, buf_ref, o_ref, tail_ref, wb_ref, carry_ref = rest
    else:
        wx_ref = None
        cw_ref, buf_ref, o_ref, tail_ref, wb_ref, carry_ref = rest
    i = pl.program_id(1)
    tm, tn = o_ref.shape
    sub = min(tm, CONV_SUB)
    seg = sub if seq_rows is None else seq_rows

    @pl.when(i == 0)
    def _():
        _cast_weight_tile_t(w_ref, wx_ref, wb_ref, shift)

    if seq_rows is None:
        @pl.when(i % tiles_per_seq == 0)
        def _():
            carry_ref[...] = buf_ref[0]

        prev = carry_ref[...]

    wb = wb_ref[...]
    taps = [cw_ref[n:n + 1, :] for n in range(cw_ref.shape[0])]
    for r in range(tm // sub):
        p_all = _dot_nt(a_ref[r * sub:(r + 1) * sub, :], wb)
        for s in range(sub // seg):
            lo = r * sub + s * seg
            x = p_all[s * seg:(s + 1) * seg]
            if seq_rows is not None:
                prev = buf_ref[lo // seg]
            y = _silu(_causal_conv(x, prev, taps))
            if norm_scale is None:
                o_ref[lo:lo + seg, :] = y
            else:
                for h in range(tn // head_dim):
                    hs = slice(h * head_dim, (h + 1) * head_dim)
                    yh = y[:, hs]
                    inv = lax.rsqrt(jnp.sum(yh * yh, axis=-1, keepdims=True) + EPS)
                    yh = yh * inv
                    o_ref[lo:lo + seg, hs] = yh if norm_scale == 1.0 else yh * norm_scale
            prev = x[seg - SUBLANES:, :]
            if seq_rows is not None:
                tail_ref[lo // seg] = prev
    if seq_rows is None:
        carry_ref[...] = prev
        tail_ref[0] = prev


def _conv_proj(a, w_t, layer, off, n, conv_w, buf8, coff, t, tm, tn, head_dim, norm_scale, name):
    m, k = a.shape
    nj, ni = n // tn, m // tm
    blk0 = off // tn
    shift = off - blk0 * tn
    assert shift < LANES and shift % SUBLANES == 0 and n % tn == 0 and coff % tn == 0
    cblk = coff // tn
    per = tn // LANES
    if t >= tm:
        tps, seq_rows, nb = t // tm, None, 1
        buf_ix = lambda j, i: (i // tps, 0, cblk + j)
    else:
        tps, seq_rows, nb = 1, t, tm // t
        buf_ix = lambda j, i: (i, 0, cblk + j)
    in_specs = [pl.BlockSpec((tm, k), lambda j, i: (i, 0)),
                pl.BlockSpec((1, tn, k), lambda j, i: (layer, blk0 + j, 0))]
    args = [a, w_t]
    if shift:
        in_specs.append(pl.BlockSpec((1, LANES, k), lambda j, i: (layer, (blk0 + j + 1) * per, 0)))
        args.append(w_t)
    in_specs += [pl.BlockSpec((conv_w.shape[0], tn), lambda j, i: (0, cblk + j)),
                 pl.BlockSpec((nb, SUBLANES, tn), buf_ix)]
    out, tails = pl.pallas_call(
        functools.partial(_conv_proj_kernel, shift=shift, tiles_per_seq=tps, seq_rows=seq_rows,
                          head_dim=head_dim, norm_scale=norm_scale),
        grid=(nj, ni),
        in_specs=in_specs,
        out_specs=[pl.BlockSpec((tm, tn), lambda j, i: (i, j)),
                   pl.BlockSpec((nb, SUBLANES, tn), lambda j, i: (i, 0, j))],
        out_shape=[jax.ShapeDtypeStruct((m, n), F32),
                   jax.ShapeDtypeStruct((ni * nb, SUBLANES, n), F32)],
        scratch_shapes=[pltpu.VMEM((tn, k), BF16), pltpu.VMEM((SUBLANES, tn), F32)],
        compiler_params=_cp("parallel", "arbitrary"),
        name=name,
    )(*args, conv_w, buf8)
    return out, tails[tps - 1::tps]


def _proj_res_kernel(a_ref, w_ref, x_ref, g_ref, *outs, sub, final):
    tm = a_ref.shape[0]
    w = w_ref[...]
    g = g_ref[...]
    for r in range(tm // sub):
        rows = slice(r * sub, (r + 1) * sub)
        xn = x_ref[rows, :] + _dot(a_ref[rows, :], w)
        if final:
            outs[0][rows, :] = _rms(xn, g)
        else:
            outs[0][rows, :] = xn
            outs[1][rows, :] = _rms(xn, g).astype(outs[1].dtype)


def _proj_res(a, w, x, g, tm, sub, final, name):
    m, kk = a.shape
    d = w.shape[1]
    row = lambda i: (i, 0)
    out_specs = [pl.BlockSpec((tm, d), row)]
    out_shape = [jax.ShapeDtypeStruct((m, d), F32)]
    if not final:
        out_specs.append(pl.BlockSpec((tm, d), row))
        out_shape.append(jax.ShapeDtypeStruct((m, d), BF16))
    return pl.pallas_call(
        functools.partial(_proj_res_kernel, sub=min(sub, tm), final=final),
        grid=(m // tm,),
        in_specs=[pl.BlockSpec((tm, kk), row),
                  pl.BlockSpec((kk, d), lambda i: (0, 0), pipeline_mode=pl.Buffered(1)),
                  pl.BlockSpec((tm, d), row),
                  pl.BlockSpec((1, d), lambda i: (0, 0))],
        out_specs=out_specs,
        out_shape=out_shape,
        compiler_params=_cp("parallel"),
        name=name,
    )(a, w, x, g.reshape(1, d))


def _merge_kernel(y0_ref, y1_ref, y2_ref, w_ref, g_ref, b_ref, o_ref, *, sub):
    tm, d = o_ref.shape
    for r in range(tm // sub):
        rows = slice(r * sub, (r + 1) * sub)
        acc = None
        for n, y_ref in enumerate((y0_ref, y1_ref, y2_ref)):
            p = _dot(y_ref[rows, :], w_ref[n])
            g = g_ref[rows, n * d:(n + 1) * d].astype(F32)
            t = _sigmoid(g + b_ref[n:n + 1, :]) * p
            acc = t if acc is None else acc + t
        o_ref[rows, :] = acc.astype(o_ref.dtype)


def _merge(ys, w_branch, g_br, b_gate, tm):
    m, bw = ys[0].shape
    d = w_branch.shape[2]
    row = lambda i: (i, 0)
    y_spec = pl.BlockSpec((tm, bw), row)
    return pl.pallas_call(
        functools.partial(_merge_kernel, sub=min(tm, ROW_SUB)),
        grid=(m // tm,),
        in_specs=[y_spec, y_spec, y_spec,
                  pl.BlockSpec((N_BRANCH, bw, d), lambda i: (0, 0, 0), pipeline_mode=pl.Buffered(1)),
                  pl.BlockSpec((tm, N_BRANCH * d), row),
                  pl.BlockSpec((N_BRANCH, d), lambda i: (0, 0))],
        out_specs=pl.BlockSpec((tm, d), row),
        out_shape=jax.ShapeDtypeStruct((m, d), BF16),
        compiler_params=_cp("parallel"),
        name="branch_merge",
    )(ys[0], ys[1], ys[2], w_branch, g_br, b_gate)


def _xattn_kernel(q_ref, k_ref, v_ref, o_ref, *, scale, dh):
    heads = range(q_ref.shape[1] // dh)
    hs = lambda h: slice(h * dh, (h + 1) * dh)
    s = [_dot_nt(q_ref[:, hs(h)], k_ref[0, :, hs(h)].astype(BF16)) * scale for h in heads]
    e = [jnp.exp(x - jnp.max(x, axis=-1, keepdims=True)) for x in s]
    p = [(x / jnp.sum(x, axis=-1, keepdims=True)).astype(BF16) for x in e]
    for h in heads:
        o_ref[:, hs(h)] = _dot(p[h], v_ref[0, :, hs(h)].astype(BF16)).astype(o_ref.dtype)


def _xattn(q, mem_k, mem_v, t, tq):
    m, d = q.shape
    b = m // t
    nmem = mem_k.shape[1]
    dh = d // X_H
    nq = t // tq
    kv_spec = pl.BlockSpec((1, nmem, d), lambda bi, i: (bi, 0, 0))
    qo_spec = pl.BlockSpec((tq, d), lambda bi, i: (bi * nq + i, 0))
    return pl.pallas_call(
        functools.partial(_xattn_kernel, scale=dh ** -0.5, dh=dh),
        grid=(b, nq),
        in_specs=[qo_spec, kv_spec, kv_spec],
        out_specs=qo_spec,
        out_shape=jax.ShapeDtypeStruct((m, d), BF16),
        compiler_params=_cp("parallel", "arbitrary"),
        name="xattn",
    )(q, mem_k, mem_v)


def _gelu_tanh(x):
    c = 0.7978845608028654
    return x * (0.5 * (1.0 + jnp.tanh(c * (x + 0.044715 * (x * x * x)))))


def _ffn_in_kernel(h_ref, wg_ref, wu_ref, cw_ref, cb_ref, buf_ref, act_ref, nbuf_ref,
                   wgb_ref, wub_ref, carry_ref, *, tiles_per_seq, seq_rows):
    i = pl.program_id(1)
    tm = h_ref.shape[0]
    sub = min(tm, FFN_SUB)
    seg = sub if seq_rows is None else seq_rows

    @pl.when(i == 0)
    def _():
        _cast_weight_tile(wg_ref, None, wgb_ref, 0)
        _cast_weight_tile(wu_ref, None, wub_ref, 0)

    if seq_rows is None:
        @pl.when(i % tiles_per_seq == 0)
        def _():
            carry_ref[...] = buf_ref[0]

        prev = carry_ref[...]

    wg, wu = wgb_ref[...], wub_ref[...]
    taps, cb = [cw_ref[n:n + 1, :] for n in range(cw_ref.shape[0])], cb_ref[...]
    for r in range(tm // sub):
        h = h_ref[r * sub:(r + 1) * sub, :]
        gate_all = _dot(h, wg)
        up_all = _dot(h, wu)
        for s in range(sub // seg):
            lo = r * sub + s * seg
            gate = gate_all[s * seg:(s + 1) * seg]
            if seq_rows is not None:
                prev = buf_ref[lo // seg]
            y = _causal_conv(gate, prev, taps)
            act = _gelu_tanh(y + cb) * up_all[s * seg:(s + 1) * seg]
            act_ref[lo:lo + seg, :] = act.astype(act_ref.dtype)
            prev = gate[seg - SUBLANES:, :]
            if seq_rows is not None:
                nbuf_ref[lo // seg] = prev
    if seq_rows is None:
        carry_ref[...] = prev
        nbuf_ref[0] = prev


def _ffn_in(h, w_ffn_in, layer, conv_w, conv_b, buf8, t, tm, tn):
    m, d = h.shape
    dff = w_ffn_in.shape[2] // 2
    ni, nj = m // tm, dff // tn
    if t >= tm:
        tps, seq_rows, nb = t // tm, None, 1
        buf_ix = lambda j, i: (i // tps, 0, j)
    else:
        tps, seq_rows, nb = 1, t, tm // t
        buf_ix = lambda j, i: (i, 0, j)
    act, tails = pl.pallas_call(
        functools.partial(_ffn_in_kernel, tiles_per_seq=tps, seq_rows=seq_rows),
        grid=(nj, ni),
        in_specs=[pl.BlockSpec((tm, d), lambda j, i: (i, 0)),
                  pl.BlockSpec((1, d, tn), lambda j, i: (layer, 0, j)),
                  pl.BlockSpec((1, d, tn), lambda j, i: (layer, 0, nj + j)),
                  pl.BlockSpec((FFN_CONV, tn), lambda j, i: (0, j)),
                  pl.BlockSpec((1, tn), lambda j, i: (0, j)),
                  pl.BlockSpec((nb, SUBLANES, tn), buf_ix)],
        out_specs=[pl.BlockSpec((tm, tn), lambda j, i: (i, j)),
                   pl.BlockSpec((nb, SUBLANES, tn), lambda j, i: (i, 0, j))],
        out_shape=[jax.ShapeDtypeStruct((m, dff), BF16),
                   jax.ShapeDtypeStruct((ni * nb, SUBLANES, dff), F32)],
        scratch_shapes=[pltpu.VMEM((d, tn), BF16), pltpu.VMEM((d, tn), BF16),
                        pltpu.VMEM((SUBLANES, tn), F32)],
        compiler_params=_cp("parallel", "arbitrary"),
        name="ffn_in",
    )(h, w_ffn_in, w_ffn_in, conv_w, conv_b.reshape(1, dff), buf8)
    return act, tails[tps - 1::tps]


def _ret_kernel(ld_ref, p_ref, cos_ref, sin_ref, s0_ref, y_ref, s_ref, *, c):
    t = pl.program_id(1)
    tb = p_ref.shape[0]
    half = RET_DK // 2
    hw = RET_H * RET_DK
    heads = range(RET_H)

    @pl.when(t == 0)
    def _():
        s_ref[...] = s0_ref[...]

    ri = lax.broadcasted_iota(jnp.int32, (c, c), 0)
    ci = lax.broadcasted_iota(jnp.int32, (c, c), 1)
    diff = (ri - ci).astype(F32)
    causal = ri >= ci
    idx = lax.broadcasted_iota(jnp.int32, (c, RET_DK), 0).astype(F32)
    lds = [ld_ref[h] for h in heads]
    dmask = [jnp.where(causal, jnp.exp(ld * jnp.where(causal, diff, 0.0)), 0.0) for ld in lds]
    q_dec = [jnp.exp(ld * (idx + 1.0)) for ld in lds]
    k_dec = [jnp.exp(ld * (c - 1.0 - idx)) for ld in lds]
    s_dec = [jnp.exp(jnp.full((1, RET_DV), ld * c, F32)) for ld in lds]

    def rot(x, cos, sin):
        x1, x2 = x[:, :half], x[:, half:]
        return jnp.concatenate([x1 * cos - x2 * sin, x1 * sin + x2 * cos], axis=-1)

    nchunk = tb // c
    nc = RET_CHUNKS_PER_STEP if nchunk % RET_CHUNKS_PER_STEP == 0 else 1

    def body(jj, carry):
        ch = []
        for ci in range(nc):
            rows = pl.ds(pl.multiple_of((jj * nc + ci) * c, c), c)
            cos, sin = cos_ref[rows, :], sin_ref[rows, :]
            k = [rot(p_ref[rows, hw + h * RET_DK:hw + (h + 1) * RET_DK], cos, sin) * (RET_DK ** -0.5)
                 for h in heads]
            ch.append(dict(
                rows=rows, k=k,
                qb=[rot(p_ref[rows, h * RET_DK:(h + 1) * RET_DK], cos, sin).astype(BF16) for h in heads],
                vb=[p_ref[rows, 2 * hw + h * RET_DV:2 * hw + (h + 1) * RET_DV].astype(BF16) for h in heads]))
        probs = [(d, h) for d in ch for h in heads]
        att = [(_dot_nt(d['qb'][h], d['k'][h].astype(BF16)) * dmask[h]).astype(BF16) for d, h in probs]
        av = [_dot(a, d['vb'][h]) for a, (d, h) in zip(att, probs)]
        kv = [_dot_tn((d['k'][h] * k_dec[h]).astype(BF16), d['vb'][h]) for d, h in probs]

        for ci, d in enumerate(ch):
            s = [s_ref[0, h] for h in heads]
            o = [av[ci * RET_H + h] + _dot(d['qb'][h], s[h].astype(BF16)) * q_dec[h] for h in heads]
            for h in heads:
                s_ref[0, h] = s_dec[h] * s[h] + kv[ci * RET_H + h]
            for h in heads:
                g = p_ref[d['rows'], 3 * hw + h * RET_DV:3 * hw + (h + 1) * RET_DV]
                y_ref[d['rows'], h * RET_DV:(h + 1) * RET_DV] = (_rms(o[h]) * _silu(g)).astype(y_ref.dtype)
        return carry

    lax.fori_loop(0, nchunk // nc, body, 0)


def _retention(p_ret, cos, sin, ld, s0, t, tb):
    m, width = p_ret.shape
    b = m // t
    c = min(CHUNK, t)
    nt = t // tb
    st_spec = pl.BlockSpec((1, RET_H, RET_DK, RET_DV), lambda bi, ti: (bi, 0, 0, 0))
    rope_spec = pl.BlockSpec((tb, RET_DK // 2), lambda bi, ti: (ti, 0))
    return pl.pallas_call(
        functools.partial(_ret_kernel, c=c),
        grid=(b, nt),
        in_specs=[pl.BlockSpec(memory_space=pltpu.SMEM),
                  pl.BlockSpec((tb, width), lambda bi, ti: (bi * nt + ti, 0)),
                  rope_spec, rope_spec, st_spec],
        out_specs=[pl.BlockSpec((tb, RET_H * RET_DV), lambda bi, ti: (bi * nt + ti, 0)), st_spec],
        out_shape=[jax.ShapeDtypeStruct((m, RET_H * RET_DV), BF16),
                   jax.ShapeDtypeStruct((b, RET_H, RET_DK, RET_DV), F32)],
        compiler_params=_cp("parallel", "arbitrary"),
        name="retention",
    )(ld, p_ret, cos, sin, s0)


def _gla_kernel(p_ref, lr_ref, wgk_ref, bgk_ref, nrm_ref, s0_ref, y_ref, s_ref, st_ref, *, c, nt):
    t = pl.program_id(1)
    tb = p_ref.shape[0]
    hk = GLA_H * GLA_DK
    hv = GLA_H * GLA_DV
    heads = range(GLA_H)

    @pl.when(t == 0)
    def _():
        for h in heads:
            st_ref[h] = s0_ref[0, h].T

    ri = lax.broadcasted_iota(jnp.int32, (c, c), 0)
    ci = lax.broadcasted_iota(jnp.int32, (c, c), 1)
    causal = ri >= ci
    tri = jnp.where(causal, 1.0, 0.0).astype(BF16)
    wgk = wgk_ref[...]
    bgk = bgk_ref[...]
    nrm = nrm_ref[...]

    nchunk = tb // c
    nc = GLA_CHUNKS_PER_STEP if nchunk % GLA_CHUNKS_PER_STEP == 0 else 1
    hs = lambda x, h: x[:, h * GLA_DK:(h + 1) * GLA_DK]

    def body(jj, carry):
        rows = [pl.ds(pl.multiple_of((jj * nc + ci) * c, c), c) for ci in range(nc)]
        z = [_dot(lr_ref[r, :].astype(BF16), wgk) + bgk for r in rows]
        bc = [_tri_cumsum(tri, -_softplus(-zz) / GLA_NORMALIZER) for zz in z]
        ch = []
        for r, b_ in zip(rows, bc):
            q = p_ref[r, 0:hk] * (GLA_DK ** -0.5)
            k = p_ref[r, hk:2 * hk]
            b_last = b_[c - 1:c, :]
            ch.append(dict(
                rows=r, qe=(q * jnp.exp(b_)).astype(BF16), ke=(k * jnp.exp(-b_)).astype(BF16),
                kd=(k * jnp.exp(b_last - b_)).astype(BF16), eb_last=jnp.exp(b_last),
                vb=[p_ref[r, 2 * hk + h * GLA_DV:2 * hk + (h + 1) * GLA_DV].astype(BF16) for h in heads]))
        probs = [(d, h) for d in ch for h in heads]
        att = [jnp.where(causal, _dot_nt(hs(d['qe'], h), hs(d['ke'], h)), 0.0).astype(BF16) for d, h in probs]
        av = [_dot(a, d['vb'][h]) for a, (d, h) in zip(att, probs)]

        for ci, d in enumerate(ch):
            st = [st_ref[h] for h in heads]
            o = [av[ci * GLA_H + h] + _dot_nt(hs(d['qe'], h), st[h].astype(BF16)) for h in heads]
            for h in heads:
                st_ref[h] = hs(d['eb_last'], h) * st[h] + _dot_tn(d['vb'][h], hs(d['kd'], h))
            for h in heads:
                g = p_ref[d['rows'], 2 * hk + hv + h * GLA_DV:2 * hk + hv + (h + 1) * GLA_DV]
                y_ref[d['rows'], h * GLA_DV:(h + 1) * GLA_DV] = (_rms(o[h], nrm) * _silu(g)).astype(y_ref.dtype)
        return carry

    lax.fori_loop(0, nchunk // nc, body, 0)

    @pl.when(t == nt - 1)
    def _():
        for h in heads:
            s_ref[0, h] = st_ref[h].T


def _gla(p_gla, p_small, wgk, bgk, nrm, s0, t, tb):
    m, width = p_gla.shape
    b = m // t
    c = min(CHUNK, t)
    nt = t // tb
    const = lambda bi, ti: (0, 0)
    st_spec = pl.BlockSpec((1, GLA_H, GLA_DK, GLA_DV), lambda bi, ti: (bi, 0, 0, 0))
    return pl.pallas_call(
        functools.partial(_gla_kernel, c=c, nt=nt),
        grid=(b, nt),
        in_specs=[pl.BlockSpec((tb, width), lambda bi, ti: (bi * nt + ti, 0)),
                  pl.BlockSpec((tb, LANES), lambda bi, ti: (bi * nt + ti, 0)),
                  pl.BlockSpec((LANES, GLA_H * GLA_DK), const),
                  pl.BlockSpec((1, GLA_H * GLA_DK), const),
                  pl.BlockSpec((1, GLA_DV), const),
                  st_spec],
        out_specs=[pl.BlockSpec((tb, GLA_H * GLA_DV), lambda bi, ti: (bi * nt + ti, 0)), st_spec],
        out_shape=[jax.ShapeDtypeStruct((m, GLA_H * GLA_DV), BF16),
                   jax.ShapeDtypeStruct((b, GLA_H, GLA_DK, GLA_DV), F32)],
        scratch_shapes=[pltpu.VMEM((GLA_H, GLA_DV, GLA_DK), F32)],
        compiler_params=_cp("parallel", "arbitrary"),
        name="gla",
    )(p_gla, p_small, wgk, bgk, nrm, s0)


def _unit_lower_inverse_minus_eye(mats, c):
    ri = lax.broadcasted_iota(jnp.int32, (c, c), 0)
    ci = lax.broadcasted_iota(jnp.int32, (c, c), 1)

    def level_mask(k):
        same = (ri ^ ci) < 2 * k
        return same & ((ri & k) != 0) & ((ci & k) == 0)

    m1 = level_mask(1)
    ns = [-jnp.where(m1, a, 0.0) for a in mats]
    k = 2
    while k < c:
        mk = level_mask(k)
        ls = [jnp.where(mk, a, 0.0) for a in mats]
        nbs = [n.astype(BF16) for n in ns]
        ys = [l + _dot(nb, l.astype(BF16)) for l, nb in zip(ls, nbs)]
        xs = [y + _dot(y.astype(BF16), nb) for y, nb in zip(ys, nbs)]
        ns = [n - x for n, x in zip(ns, xs)]
        k *= 2
    return ns


def _gdn_kernel(q_ref, k_ref, v_ref, z_ref, sm_ref, alog_ref, dtb_ref, nrm_ref, s0_ref,
                y_ref, s_ref, *, c):
    t = pl.program_id(1)
    tb = q_ref.shape[0]

    @pl.when(t == 0)
    def _():
        s_ref[...] = s0_ref[...]

    ri = lax.broadcasted_iota(jnp.int32, (c, c), 0)
    ci = lax.broadcasted_iota(jnp.int32, (c, c), 1)
    incl = ri >= ci
    strict = ri > ci
    tri = jnp.where(incl, 1.0, 0.0).astype(BF16)
    lane = lax.broadcasted_iota(jnp.int32, (1, LANES), 1)
    a_lanes = jnp.logical_and(lane >= SMALL_A0, lane < SMALL_A0 + GDN_H)
    sel = jnp.where(lax.broadcasted_iota(jnp.int32, (SUBLANES, LANES), 1)
                    == lax.broadcasted_iota(jnp.int32, (SUBLANES, LANES), 0) + SMALL_A0,
                    1.0, 0.0).astype(BF16)
    neg_a = -jnp.exp(alog_ref[...])
    dtb = dtb_ref[...]
    nrm = nrm_ref[...]

    def col(x, lane_idx):
        return x[:, lane_idx:lane_idx + 1]

    heads = range(GDN_H)
    nchunk = tb // c
    nc = GDN_CHUNKS_PER_STEP if nchunk % GDN_CHUNKS_PER_STEP == 0 else 1

    def body(jj, carry):
        ch = []
        for ci in range(nc):
            rows = pl.ds(pl.multiple_of((jj * nc + ci) * c, c), c)
            sm = sm_ref[rows, :]
            ch.append(dict(
                rows=rows,
                g_all=jnp.where(a_lanes, neg_a * _softplus(sm + dtb), 0.0),
                beta_all=_sigmoid(sm),
                q=[q_ref[rows, h * GDN_DK:(h + 1) * GDN_DK] for h in heads],
                k=[k_ref[rows, h * GDN_DK:(h + 1) * GDN_DK] for h in heads],
                v=[v_ref[rows, h * GDN_DV:(h + 1) * GDN_DV] for h in heads]))
        for d in ch:
            d['gam'] = _tri_cumsum(tri, d['g_all'])
        for d in ch:
            ghi, gmid, glo = _split3(d['gam'])
            d['gam_rows'] = _dot_nt(sel, ghi) + _dot_nt(sel, gmid) + _dot_nt(sel, glo)
        for d in ch:
            gam = d['gam']
            g_last = gam[c - 1:c, :]
            d['eg_all'] = jnp.exp(gam)
            d['egl_all'] = jnp.exp(g_last - gam)
            d['eg_last_all'] = jnp.exp(g_last)
            d['beta'] = [col(d['beta_all'], SMALL_B0 + h) for h in heads]
            d['eg'] = [col(d['eg_all'], SMALL_A0 + h) for h in heads]
            d['decay'] = [jnp.where(incl, jnp.exp(jnp.where(
                incl, col(gam, SMALL_A0 + h) - d['gam_rows'][h:h + 1, :], 0.0)), 0.0) for h in heads]
        probs = [(d, h) for d in ch for h in heads]
        qk_kk = [_dot_nt(jnp.concatenate([d['q'][h], d['k'][h]], axis=0).astype(BF16), d['k'][h].astype(BF16))
                 for d, h in probs]
        att = [(r[:c] * d['decay'][h]).astype(BF16) for r, (d, h) in zip(qk_kk, probs)]
        a = [jnp.where(strict, d['beta'][h] * r[c:] * d['decay'][h], 0.0) for r, (d, h) in zip(qk_kk, probs)]
        n = _unit_lower_inverse_minus_eye(a, c)
        rhs = [jnp.concatenate([d['beta'][h] * d['v'][h], (d['beta'][h] * d['eg'][h]) * d['k'][h]], axis=-1)
               for d, h in probs]
        sol = [r + _dot(nn.astype(BF16), r.astype(BF16)) for r, nn in zip(rhs, n)]

        for ci, d in enumerate(ch):
            sl = sol[ci * GDN_H:(ci + 1) * GDN_H]
            at = att[ci * GDN_H:(ci + 1) * GDN_H]
            s = [s_ref[0, h] for h in heads]
            wq_s = [_dot(jnp.concatenate([sl[h][:, GDN_DV:], d['q'][h]], axis=0).astype(BF16), s[h].astype(BF16))
                    for h in heads]
            wb = [(sl[h][:, :GDN_DV] - wq_s[h][:c]).astype(BF16) for h in heads]
            o = [d['eg'][h] * wq_s[h][c:] + _dot(at[h], wb[h]) for h in heads]
            kd = [(d['k'][h] * col(d['egl_all'], SMALL_A0 + h)).astype(BF16) for h in heads]
            for h in heads:
                s_ref[0, h] = col(d['eg_last_all'], SMALL_A0 + h) * s[h] + _dot_tn(kd[h], wb[h])
            for h in heads:
                z = z_ref[d['rows'], h * GDN_DV:(h + 1) * GDN_DV]
                y = _rms(o[h], nrm) * _silu(z)
                y_ref[d['rows'], h * GDN_DV:(h + 1) * GDN_DV] = y.astype(y_ref.dtype)
        return carry

    lax.fori_loop(0, nchunk // nc, body, 0)


def _gdn(q, k, v, z, p_small, alog_row, dtb_row, nrm, s0, t, tb):
    m = q.shape[0]
    b = m // t
    c = min(CHUNK, t)
    nt = t // tb
    st_spec = pl.BlockSpec((1, GDN_H, GDN_DK, GDN_DV), lambda bi, ti: (bi, 0, 0, 0))
    const = lambda bi, ti: (0, 0)
    row = lambda bi, ti: (bi * nt + ti, 0)
    qk_spec = pl.BlockSpec((tb, GDN_H * GDN_DK), row)
    v_spec = pl.BlockSpec((tb, GDN_H * GDN_DV), row)
    return pl.pallas_call(
        functools.partial(_gdn_kernel, c=c),
        grid=(b, nt),
        in_specs=[qk_spec, qk_spec, v_spec, v_spec,
                  pl.BlockSpec((tb, LANES), row),
                  pl.BlockSpec((1, LANES), const),
                  pl.BlockSpec((1, LANES), const),
                  pl.BlockSpec((1, GDN_DV), const),
                  st_spec],
        out_specs=[v_spec, st_spec],
        out_shape=[jax.ShapeDtypeStruct((m, GDN_H * GDN_DV), BF16),
                   jax.ShapeDtypeStruct((b, GDN_H, GDN_DK, GDN_DV), F32)],
        compiler_params=_cp("parallel", "arbitrary"),
        name="gated_deltanet",
    )(q, k, v, z, p_small, alog_row, dtb_row, nrm, s0)


def _pad_rows_front(buf, rows):
    return jnp.pad(buf, ((0, 0), (0, 0), (rows - buf.shape[2], 0), (0, 0)))


def _in_proj_windows(d):
    sizes = dict(ret=4 * RET_H * RET_DK, gla=2 * GLA_H * GLA_DK + 2 * GLA_H * GLA_DV, lr=GLA_RANK,
                 gdn=2 * GDN_H * GDN_DK + 2 * GDN_H * GDN_DV, ab=2 * GDN_H, gbr=N_BRANCH * d)
    win, o = {}, 0
    for name in ('ret', 'gla', 'lr', 'gdn', 'ab', 'gbr'):
        win[name] = (o, sizes[name])
        o += sizes[name]
    assert win['lr'][0] % LANES == 0 and win['ab'][0] % LANES == SMALL_A0
    return win


def _prep_weights(w):
    lane_row = lambda v, at: jnp.pad(v, ((0, 0), (at, LANES - at - v.shape[1])))[:, None, :]
    return dict(
        w_in_t=jnp.swapaxes(w['w_in'], 1, 2), w_xq=w['w_xq'], w_ffn_in=w['w_ffn_in'],
        w_gk=jnp.pad(w['w_gla_gk'], ((0, 0), (0, LANES - GLA_RANK), (0, 0))).astype(BF16),
        b_gk=w['b_gla_gk'][:, None, :],
        gla_norm=w['gla_norm'][:, None, :], gdn_norm=w['gdn_norm'][:, None, :],
        alog_row=lane_row(w['gdn_a_log'], SMALL_A0), dtb_row=lane_row(w['gdn_dt_bias'], SMALL_A0),
        gdn_conv_w=w['gdn_conv_w'], b_gate=w['b_gate'],
        w_branch=w['w_branch'].astype(BF16), w_out=w['w_out'].astype(BF16), w_xo=w['w_xo'].astype(BF16),
        ffn_conv_w=w['ffn_conv_w'], ffn_conv_b=w['ffn_conv_b'], w_ffn_out=w['w_ffn_out'].astype(BF16),
        ln_mix=w['ln_mix'], ln_xattn=w['ln_xattn'], ln_ffn=w['ln_ffn'], ln_final=w['ln_final'],
    )


def _tile(n, cap):
    t = min(n, cap)
    while n % t:
        t -= 1
    return t


def _trunk(x, offset, mem_k, mem_v, s_ret, s_gla, s_gdn, buf_gdn, buf_ffn, pw):
    b, t, d = x.shape
    m = b * t
    depth = s_ret.shape[0]
    x2 = x.reshape(m, d)
    tm = _tile(m, MM_TM)
    tm_res = _tile(m, RES_TM)
    tm_res_k = _tile(m, RES_TM_K)
    tm_ffo = _tile(m, FFO_TM)
    tb = _tile(t, SEQ_TB)
    tb_gdn = _tile(t, SEQ_TB_GDN)
    tq = _tile(t, XATTN_TQ)
    tm_ffn = _tile(t, FFN_TM) if t >= FFN_TM else t * _tile(b, max(1, FFN_TM // t))

    pos = offset + jnp.arange(t, dtype=F32)
    half = RET_DK // 2
    freqs = 1.0 / (ROPE_BASE ** (jnp.arange(half, dtype=F32) / half))
    ang = pos[:, None] * freqs[None, :]
    cos, sin = jnp.cos(ang), jnp.sin(ang)
    ld = jnp.log1p(-jnp.exp2(-5.0 - jnp.arange(RET_H, dtype=F32)))

    buf_gdn8 = _pad_rows_front(buf_gdn, SUBLANES)
    buf_ffn8 = _pad_rows_front(buf_ffn, SUBLANES)
    win = _in_proj_windows(d)

    def in_proj(a, l, name, out_dtype=F32):
        off, n = win[name]
        blk0 = off // MM_TN
        shift = off - blk0 * MM_TN
        assert shift < LANES and n % MM_TN == 0
        return _mm_ws(a, pw['w_in_t'], l, blk0, n, shift, out_dtype, tm, MM_TN, "in_proj_" + name,
                      transposed=True)

    outs = ([], [], [], [], [])
    hn = _norm(x2, pw['ln_mix'][0], tm_res_k)
    y = None
    for l in range(depth):
        p_ret = in_proj(hn, l, 'ret')
        p_gla = in_proj(hn, l, 'gla')
        g_off = win['gdn'][0]
        hq, hv = GDN_H * GDN_DK, GDN_H * GDN_DV
        conv_args = (pw['gdn_conv_w'][l], buf_gdn8[l])
        d_q, tq8 = _conv_proj(hn, pw['w_in_t'], l, g_off, hq, *conv_args, 0, t, tm, MM_TN,
                              GDN_DK, GDN_DK ** -0.5, "in_proj_gdn_q")
        d_k, tk8 = _conv_proj(hn, pw['w_in_t'], l, g_off + hq, hq, *conv_args, hq, t, tm, MM_TN,
                              GDN_DK, 1.0, "in_proj_gdn_k")
        d_v, tv8 = _conv_proj(hn, pw['w_in_t'], l, g_off + 2 * hq, hv, *conv_args, 2 * hq, t, tm, MM_TN,
                              GDN_DV, None, "in_proj_gdn_v")
        zo = g_off + 2 * hq + hv
        d_z = _mm_ws(hn, pw['w_in_t'], l, zo // MM_TN, hv, zo % MM_TN, F32, tm, MM_TN, "in_proj_gdn_z",
                     transposed=True)
        bg8 = jnp.concatenate([tq8, tk8, tv8], axis=-1)
        g_br = in_proj(hn, l, 'gbr', BF16)
        p_small = _small_proj(hn, pw['w_in_t'], l, win['lr'][0] // LANES, win['ab'][0] // LANES, tm)

        y_ret, sr = _retention(p_ret, cos, sin, ld, s_ret[l], t, tb)
        y_gla, sg = _gla(p_gla, p_small, pw['w_gk'][l], pw['b_gk'][l], pw['gla_norm'][l], s_gla[l], t, tb)
        y_gdn, sd = _gdn(d_q, d_k, d_v, d_z, p_small, pw['alog_row'][l], pw['dtb_row'][l],
                         pw['gdn_norm'][l], s_gdn[l], t, tb_gdn)

        merged = _merge((y_ret, y_gla, y_gdn), pw['w_branch'][l], g_br, pw['b_gate'][l], tm_res_k)
        x2, hx = _proj_res(merged, pw['w_out'][l], x2, pw['ln_xattn'][l], tm_res, ROW_SUB, False, "out_proj")

        q = _mm_ws(hx, pw['w_xq'], l, 0, d, 0, BF16, tm, MM_TN, "xattn_q")
        o = _xattn(q, mem_k[l], mem_v[l], t, tq)
        x2, hf = _proj_res(o, pw['w_xo'][l], x2, pw['ln_ffn'][l], tm_res, ROW_SUB, False, "xattn_out")

        act, bf8 = _ffn_in(hf, pw['w_ffn_in'], l, pw['ffn_conv_w'][l], pw['ffn_conv_b'][l],
                           buf_ffn8[l], t, tm_ffn, FFN_TN)
        if l + 1 < depth:
            x2, hn = _proj_res(act, pw['w_ffn_out'][l], x2, pw['ln_mix'][l + 1], tm_ffo, ROW_SUB, False, "ffn_out")
        else:
            y, = _proj_res(act, pw['w_ffn_out'][l], x2, pw['ln_final'], tm_ffo, ROW_SUB, True, "ffn_out_final")

        for lst, val in zip(outs, (sr, sg, sd, bg8[:, SUBLANES - (GDN_CONV - 1):], bf8[:, SUBLANES - (FFN_CONV - 1):])):
            lst.append(val)
    return (y.reshape(b, t, d),) + tuple(jnp.stack(lst) for lst in outs)


def kernel(x_prompt, x_sample, mem_prompt, state_ret, state_gla, state_gdn, state_gdn_conv, state_ffn_conv, cache_mem_k, cache_mem_v, ln_mix, w_in, w_gla_gk, b_gla_gk, gla_norm, gdn_conv_w, gdn_a_log, gdn_dt_bias, gdn_norm, b_gate, w_branch, w_out, ln_xattn, ln_mem, w_xq, w_xkv, w_xo, ln_ffn, w_ffn_in, ffn_conv_w, ffn_conv_b, w_ffn_out, ln_final):
    pw = _prep_weights(dict(
        w_in=w_in, w_gla_gk=w_gla_gk, b_gla_gk=b_gla_gk, gla_norm=gla_norm, gdn_conv_w=gdn_conv_w,
        gdn_a_log=gdn_a_log, gdn_dt_bias=gdn_dt_bias, gdn_norm=gdn_norm, b_gate=b_gate, w_branch=w_branch,
        w_out=w_out, w_xq=w_xq, w_xo=w_xo, w_ffn_in=w_ffn_in, ffn_conv_w=ffn_conv_w, ffn_conv_b=ffn_conv_b,
        w_ffn_out=w_ffn_out, ln_mix=ln_mix, ln_xattn=ln_xattn, ln_ffn=ln_ffn, ln_final=ln_final))
    depth = w_in.shape[0]
    bp, nmem, d = mem_prompt.shape
    dt = x_prompt.dtype

    mem2 = mem_prompt.reshape(bp * nmem, d)
    tmem = _tile(bp * nmem, MM_TM)
    mk, mv = [], []
    for l in range(depth):
        hm = _norm(mem2, ln_mem[l], _tile(bp * nmem, RES_TM_K))
        mk.append(_mm_ws(hm, w_xkv, l, 0, d, 0, F32, tmem, MM_TN, "mem_k"))
        mv.append(_mm_ws(hm, w_xkv, l, d // MM_TN, d, 0, F32, tmem, MM_TN, "mem_v"))
    dh = d // X_H
    mem_k_p = jnp.stack(mk).reshape(depth, bp, nmem, X_H, dh)
    mem_v_p = jnp.stack(mv).reshape(depth, bp, nmem, X_H, dh)

    zeros = lambda *s: jnp.zeros((depth, bp) + s, dt)
    y_p, ret_p, gla_p, gdn_p, gdn_conv_p, ffn_conv_p = _trunk(
        x_prompt, 0.0, [a.reshape(bp, nmem, d) for a in mk], [a.reshape(bp, nmem, d) for a in mv],
        zeros(RET_H, RET_DK, RET_DV), zeros(GLA_H, GLA_DK, GLA_DV), zeros(GDN_H, GDN_DK, GDN_DV),
        zeros(GDN_CONV - 1, state_gdn_conv.shape[-1]), zeros(FFN_CONV - 1, state_ffn_conv.shape[-1]), pw)

    past_len = 4096.0
    bs = x_sample.shape[0]
    y_s, ret_s, gla_s, gdn_s, gdn_conv_s, ffn_conv_s = _trunk(
        x_sample, past_len,
        [cache_mem_k[l].reshape(bs, nmem, d) for l in range(depth)],
        [cache_mem_v[l].reshape(bs, nmem, d) for l in range(depth)],
        state_ret, state_gla, state_gdn, state_gdn_conv, state_ffn_conv, pw)

    return (y_p, y_s, ret_p, gla_p, gdn_p, gdn_conv_p, ffn_conv_p, mem_k_p, mem_v_p,
            ret_s, gla_s, gdn_s, gdn_conv_s, ffn_conv_s)
```

```python
import functools

import jax
import jax.numpy as jnp
from jax import lax
from jax.experimental import pallas as pl
from jax.experimental.pallas import tpu as pltpu

F32 = jnp.float32
BF16 = jnp.bfloat16

EPS = 1e-6
CHUNK = 64
RET_H, RET_DK, RET_DV = 4, 256, 256
GLA_H, GLA_DK, GLA_DV = 4, 128, 256
GLA_RANK = 16
GLA_NORMALIZER = 16.0
GDN_H, GDN_DK, GDN_DV = 8, 128, 128
GDN_CONV = 4
N_BRANCH = 3
X_H = 4
FFN_CONV = 3
ROPE_BASE = 10000.0

LANES = 128
SUBLANES = 8
SMALL_A0 = GLA_RANK
SMALL_B0 = GLA_RANK + GDN_H
VMEM_LIMIT = 52 * 1024 * 1024
MM_TM = 1024
MM_TN = 1024
RES_TM = 512
RES_TM_K = 512
FFO_TM = 256
ROW_SUB = 256
CONV_SUB = 256
SEQ_TB = 512
SEQ_TB_GDN = 256
GDN_CHUNKS_PER_STEP = 2
GLA_CHUNKS_PER_STEP = 2
RET_CHUNKS_PER_STEP = 2
XATTN_TQ = 512
FFN_TM = 1024
FFN_TN = 512
FFN_SUB = 256


def _cp(*sem):
    return pltpu.CompilerParams(dimension_semantics=sem, vmem_limit_bytes=VMEM_LIMIT)


def _dot(a, b):
    return jnp.dot(a, b, preferred_element_type=F32)


def _dot_nt(a, b):
    return lax.dot_general(a, b, (((1,), (1,)), ((), ())), preferred_element_type=F32)


def _dot_tn(a, b):
    return lax.dot_general(a, b, (((0,), (0,)), ((), ())), preferred_element_type=F32)


def _sigmoid(x):
    return 1.0 / (1.0 + jnp.exp(-x))


def _silu(x):
    return x * _sigmoid(x)


def _softplus(x):
    return jnp.maximum(x, 0.0) + jnp.log1p(jnp.exp(-jnp.abs(x)))


def _rms(x, g=None):
    y = x * lax.rsqrt(jnp.mean(x * x, axis=-1, keepdims=True) + EPS)
    return y if g is None else y * g


def _split3(x):
    hi = x.astype(BF16)
    r = x - hi.astype(F32)
    mid = r.astype(BF16)
    lo = (r - mid.astype(F32)).astype(BF16)
    return hi, mid, lo


def _tri_cumsum(tri, x):
    hi, mid, lo = _split3(x)
    return _dot(tri, hi) + _dot(tri, mid) + _dot(tri, lo)


def _norm_kernel(x_ref, g_ref, o_ref):
    o_ref[...] = _rms(x_ref[...], g_ref[...]).astype(o_ref.dtype)


def _norm(x, g, tm):
    m, d = x.shape
    return pl.pallas_call(
        _norm_kernel,
        grid=(m // tm,),
        in_specs=[pl.BlockSpec((tm, d), lambda i: (i, 0)),
                  pl.BlockSpec((1, d), lambda i: (0, 0))],
        out_specs=pl.BlockSpec((tm, d), lambda i: (i, 0)),
        out_shape=jax.ShapeDtypeStruct((m, d), BF16),
        compiler_params=_cp("parallel"),
        name="rmsnorm",
    )(x, g.reshape(1, d))


CAST_ROWS = 256


def _cast_weight_tile(w_ref, wx_ref, wb_ref, shift):
    kk, tn = wb_ref.shape
    for r in range(0, kk, CAST_ROWS):
        rows = slice(r, min(r + CAST_ROWS, kk))
        if shift == 0:
            wb_ref[rows, :] = w_ref[0, rows, :].astype(BF16)
        else:
            wide = jnp.concatenate([w_ref[0, rows, :], wx_ref[0, rows, :]], axis=1)
            wb_ref[rows, :] = pltpu.roll(wide, wide.shape[1] - shift, 1)[:, :tn].astype(BF16)


def _cast_weight_tile_t(w_ref, wx_ref, wb_ref, shift):
    tn, _ = wb_ref.shape
    for r in range(0, tn, CAST_ROWS):
        lo, hi = r + shift, min(r + CAST_ROWS, tn) + shift
        if hi <= tn:
            src = w_ref[0, lo:hi, :]
        else:
            src = jnp.concatenate([w_ref[0, lo:tn, :], wx_ref[0, 0:hi - tn, :]], axis=0)
        wb_ref[r:min(r + CAST_ROWS, tn), :] = src.astype(BF16)


def _mm_ws_kernel(a_ref, w_ref, *rest, shift, transposed):
    if shift:
        wx_ref, o_ref, wb_ref = rest
    else:
        wx_ref = None
        o_ref, wb_ref = rest

    @pl.when(pl.program_id(1) == 0)
    def _():
        (_cast_weight_tile_t if transposed else _cast_weight_tile)(w_ref, wx_ref, wb_ref, shift)

    mm = _dot_nt if transposed else _dot
    o_ref[...] = mm(a_ref[...], wb_ref[...]).astype(o_ref.dtype)


def _mm_ws(a, w, layer, blk0, n, shift, out_dtype, tm, tn, name, transposed=False):
    m, k = a.shape
    nj = n // tn
    per = tn // LANES
    if transposed:
        assert shift % SUBLANES == 0
        w_spec = pl.BlockSpec((1, tn, k), lambda j, i: (layer, blk0 + j, 0))
        wx_spec = pl.BlockSpec((1, LANES, k), lambda j, i: (layer, (blk0 + j + 1) * per, 0))
    else:
        w_spec = pl.BlockSpec((1, k, tn), lambda j, i: (layer, 0, blk0 + j))
        wx_spec = pl.BlockSpec((1, k, LANES), lambda j, i: (layer, 0, (blk0 + j + 1) * per))
    in_specs = [pl.BlockSpec((tm, k), lambda j, i: (i, 0)), w_spec]
    args = [a, w]
    if shift:
        in_specs.append(wx_spec)
        args.append(w)
    return pl.pallas_call(
        functools.partial(_mm_ws_kernel, shift=shift, transposed=transposed),
        grid=(nj, m // tm),
        in_specs=in_specs,
        out_specs=pl.BlockSpec((tm, tn), lambda j, i: (i, j)),
        out_shape=jax.ShapeDtypeStruct((m, n), out_dtype),
        scratch_shapes=[pltpu.VMEM((tn, k) if transposed else (k, tn), BF16)],
        compiler_params=_cp("parallel", "arbitrary"),
        name=name,
    )(*args)


def _small_proj_kernel(a_ref, w0_ref, w1_ref, o_ref):
    a = a_ref[...]
    p0 = _dot_nt(a, w0_ref[0].astype(BF16))
    p1 = _dot_nt(a, w1_ref[0].astype(BF16))
    lane = lax.broadcasted_iota(jnp.int32, p0.shape, 1)
    o_ref[...] = jnp.where(lane < SMALL_A0, p0, p1)


def _small_proj(a, w_t, layer, blk_lr, blk_ab, tm):
    m, k = a.shape
    return pl.pallas_call(
        _small_proj_kernel,
        grid=(m // tm,),
        in_specs=[pl.BlockSpec((tm, k), lambda i: (i, 0)),
                  pl.BlockSpec((1, LANES, k), lambda i: (layer, blk_lr, 0)),
                  pl.BlockSpec((1, LANES, k), lambda i: (layer, blk_ab, 0))],
        out_specs=pl.BlockSpec((tm, LANES), lambda i: (i, 0)),
        out_shape=jax.ShapeDtypeStruct((m, LANES), F32),
        compiler_params=_cp("parallel"),
        name="in_proj_small",
    )(a, w_t, w_t)


def _shift_rows(x, prev8, s):
    xs = pltpu.roll(x, s, 0)
    ps = pltpu.roll(prev8, s, 0)
    r8 = lax.broadcasted_iota(jnp.int32, prev8.shape, 0)
    top = jnp.where(r8 < s, ps, xs[:SUBLANES])
    if x.shape[0] == SUBLANES:
        return top
    return jnp.concatenate([top, xs[SUBLANES:]], axis=0)


def _causal_conv(x, prev8, taps):
    width = len(taps)
    y = None
    for i, w in enumerate(taps):
        s = width - 1 - i
        xs = x if s == 0 else _shift_rows(x, prev8, s)
        y = xs * w if y is None else y + xs * w
    return y


def _conv_proj_kernel(a_ref, w_ref, *rest, shift, tiles_per_seq, seq_rows, head_dim, norm_scale):
    if shift:
        wx_ref, cw_ref, buf_ref, o_ref, tail_ref, wb_ref, carry_ref = rest
    else:
        wx_ref = None
        cw_ref, buf_ref, o_ref, tail_ref, wb_ref, carry_ref = rest
    i = pl.program_id(1)
    tm, tn = o_ref.shape
    sub = min(tm, CONV_SUB)
    seg = sub if seq_rows is None else seq_rows

    @pl.when(i == 0)
    def _():
        _cast_weight_tile_t(w_ref, wx_ref, wb_ref, shift)

    if seq_rows is None:
        @pl.when(i % tiles_per_seq == 0)
        def _():
            carry_ref[...] = buf_ref[0]

        prev = carry_ref[...]

    wb = wb_ref[...]
    taps = [cw_ref[n:n + 1, :] for n in range(cw_ref.shape[0])]
    for r in range(tm // sub):
        p_all = _dot_nt(a_ref[r * sub:(r + 1) * sub, :], wb)
        for s in range(sub // seg):
            lo = r * sub + s * seg
            x = p_all[s * seg:(s + 1) * seg]
            if seq_rows is not None:
                prev = buf_ref[lo // seg]
            for h in range(tn // head_dim):
                hs = slice(h * head_dim, (h + 1) * head_dim)
                yh = _silu(_causal_conv(x[:, hs], prev[:, hs], [w[:, hs] for w in taps]))
                if norm_scale is not None:
                    yh = yh * lax.rsqrt(jnp.sum(yh * yh, axis=-1, keepdims=True) + EPS)
                    if norm_scale != 1.0:
                        yh = yh * norm_scale
                o_ref[lo:lo + seg, hs] = yh
            prev = x[seg - SUBLANES:, :]
            if seq_rows is not None:
                tail_ref[lo // seg] = prev
    if seq_rows is None:
        carry_ref[...] = prev
        tail_ref[0] = prev


def _conv_proj(a, w_t, layer, off, n, conv_w, buf8, coff, t, tm, tn, head_dim, norm_scale, name):
    m, k = a.shape
    nj, ni = n // tn, m // tm
    blk0 = off // tn
    shift = off - blk0 * tn
    assert shift < LANES and shift % SUBLANES == 0 and n % tn == 0 and coff % tn == 0
    cblk = coff // tn
    per = tn // LANES
    if t >= tm:
        tps, seq_rows, nb = t // tm, None, 1
        buf_ix = lambda j, i: (i // tps, 0, cblk + j)
    else:
        tps, seq_rows, nb = 1, t, tm // t
        buf_ix = lambda j, i: (i, 0, cblk + j)
    in_specs = [pl.BlockSpec((tm, k), lambda j, i: (i, 0)),
                pl.BlockSpec((1, tn, k), lambda j, i: (layer, blk0 + j, 0))]
    args = [a, w_t]
    if shift:
        in_specs.append(pl.BlockSpec((1, LANES, k), lambda j, i: (layer, (blk0 + j + 1) * per, 0)))
        args.append(w_t)
    in_specs += [pl.BlockSpec((conv_w.shape[0], tn), lambda j, i: (0, cblk + j)),
                 pl.BlockSpec((nb, SUBLANES, tn), buf_ix)]
    out, tails = pl.pallas_call(
        functools.partial(_conv_proj_kernel, shift=shift, tiles_per_seq=tps, seq_rows=seq_rows,
                          head_dim=head_dim, norm_scale=norm_scale),
        grid=(nj, ni),
        in_specs=in_specs,
        out_specs=[pl.BlockSpec((tm, tn), lambda j, i: (i, j)),
                   pl.BlockSpec((nb, SUBLANES, tn), lambda j, i: (i, 0, j))],
        out_shape=[jax.ShapeDtypeStruct((m, n), F32),
                   jax.ShapeDtypeStruct((ni * nb, SUBLANES, n), F32)],
        scratch_shapes=[pltpu.VMEM((tn, k), BF16), pltpu.VMEM((SUBLANES, tn), F32)],
        compiler_params=_cp("parallel", "arbitrary"),
        name=name,
    )(*args, conv_w, buf8)
    return out, tails[tps - 1::tps]


def _proj_res_kernel(a_ref, w_ref, x_ref, g_ref, *outs, sub, final):
    tm = a_ref.shape[0]
    w = w_ref[...]
    g = g_ref[...]
    for r in range(tm // sub):
        rows = slice(r * sub, (r + 1) * sub)
        xn = x_ref[rows, :] + _dot(a_ref[rows, :], w)
        if final:
            outs[0][rows, :] = _rms(xn, g)
        else:
            outs[0][rows, :] = xn
            outs[1][rows, :] = _rms(xn, g).astype(outs[1].dtype)


def _proj_res(a, w, x, g, tm, sub, final, name):
    m, kk = a.shape
    d = w.shape[1]
    row = lambda i: (i, 0)
    out_specs = [pl.BlockSpec((tm, d), row)]
    out_shape = [jax.ShapeDtypeStruct((m, d), F32)]
    if not final:
        out_specs.append(pl.BlockSpec((tm, d), row))
        out_shape.append(jax.ShapeDtypeStruct((m, d), BF16))
    return pl.pallas_call(
        functools.partial(_proj_res_kernel, sub=min(sub, tm), final=final),
        grid=(m // tm,),
        in_specs=[pl.BlockSpec((tm, kk), row),
                  pl.BlockSpec((kk, d), lambda i: (0, 0), pipeline_mode=pl.Buffered(1)),
                  pl.BlockSpec((tm, d), row),
                  pl.BlockSpec((1, d), lambda i: (0, 0))],
        out_specs=out_specs,
        out_shape=out_shape,
        compiler_params=_cp("parallel"),
        name=name,
    )(a, w, x, g.reshape(1, d))


def _merge_kernel(y0_ref, y1_ref, y2_ref, w_ref, g_ref, b_ref, o_ref, *, sub):
    tm, d = o_ref.shape
    for r in range(tm // sub):
        rows = slice(r * sub, (r + 1) * sub)
        acc = None
        for n, y_ref in enumerate((y0_ref, y1_ref, y2_ref)):
            p = _dot(y_ref[rows, :], w_ref[n])
            g = g_ref[rows, n * d:(n + 1) * d].astype(F32)
            t = _sigmoid(g + b_ref[n:n + 1, :]) * p
            acc = t if acc is None else acc + t
        o_ref[rows, :] = acc.astype(o_ref.dtype)


def _merge(ys, w_branch, g_br, b_gate, tm):
    m, bw = ys[0].shape
    d = w_branch.shape[2]
    row = lambda i: (i, 0)
    y_spec = pl.BlockSpec((tm, bw), row)
    return pl.pallas_call(
        functools.partial(_merge_kernel, sub=min(tm, ROW_SUB)),
        grid=(m // tm,),
        in_specs=[y_spec, y_spec, y_spec,
                  pl.BlockSpec((N_BRANCH, bw, d), lambda i: (0, 0, 0), pipeline_mode=pl.Buffered(1)),
                  pl.BlockSpec((tm, N_BRANCH * d), row),
                  pl.BlockSpec((N_BRANCH, d), lambda i: (0, 0))],
        out_specs=pl.BlockSpec((tm, d), row),
        out_shape=jax.ShapeDtypeStruct((m, d), BF16),
        compiler_params=_cp("parallel"),
        name="branch_merge",
    )(ys[0], ys[1], ys[2], w_branch, g_br, b_gate)


def _xattn_kernel(q_ref, k_ref, v_ref, o_ref, *scratch, scale, dh):
    heads = range(q_ref.shape[1] // dh)
    hs = lambda h: slice(h * dh, (h + 1) * dh)
    if scratch:
        kb_ref, vb_ref = scratch

        @pl.when(pl.program_id(1) == 0)
        def _():
            for h in heads:
                kb_ref[:, hs(h)] = k_ref[0, 0, :, h, :].astype(BF16)
                vb_ref[:, hs(h)] = v_ref[0, 0, :, h, :].astype(BF16)
    else:
        kb_ref, vb_ref = k_ref.at[0], v_ref.at[0]

    s = [_dot_nt(q_ref[:, hs(h)], kb_ref[:, hs(h)]) * scale for h in heads]
    e = [jnp.exp(x - jnp.max(x, axis=-1, keepdims=True)) for x in s]
    p = [(x / jnp.sum(x, axis=-1, keepdims=True)).astype(BF16) for x in e]
    for h in heads:
        o_ref[:, hs(h)] = _dot(p[h], vb_ref[:, hs(h)]).astype(o_ref.dtype)


def _xattn(q, mem_k, mem_v, layer, t, tq):
    m, d = q.shape
    b = m // t
    dh = d // X_H
    nq = t // tq
    if mem_k.ndim == 5:
        nmem = mem_k.shape[2]
        kv_spec = pl.BlockSpec((1, 1, nmem, X_H, dh), lambda bi, i: (layer, bi, 0, 0, 0))
        scratch = [pltpu.VMEM((nmem, d), BF16), pltpu.VMEM((nmem, d), BF16)]
    else:
        nmem = mem_k.shape[1] // b
        kv_spec = pl.BlockSpec((1, nmem, d), lambda bi, i: (layer, bi, 0))
        scratch = []
    qo_spec = pl.BlockSpec((tq, d), lambda bi, i: (bi * nq + i, 0))
    return pl.pallas_call(
        functools.partial(_xattn_kernel, scale=dh ** -0.5, dh=dh),
        grid=(b, nq),
        in_specs=[qo_spec, kv_spec, kv_spec],
        out_specs=qo_spec,
        out_shape=jax.ShapeDtypeStruct((m, d), BF16),
        scratch_shapes=scratch,
        compiler_params=_cp("parallel", "arbitrary"),
        name="xattn",
    )(q, mem_k, mem_v)


def _mem_proj_kernel(x_ref, g_ref, w_ref, o_ref, ob_ref, wb_ref):
    @pl.when(pl.program_id(1) == 0)
    def _():
        _cast_weight_tile(w_ref, None, wb_ref, 0)

    _, nb, nmem, nh, dh = o_ref.shape
    res = _dot(_rms(x_ref[...], g_ref[0]).astype(BF16), wb_ref[...])
    ob_ref[0] = res.astype(BF16)
    for bb in range(nb):
        for h in range(nh):
            o_ref[0, bb, :, h, :] = res[bb * nmem:(bb + 1) * nmem, h * dh:(h + 1) * dh]


def _mem_proj(mem2, ln_mem, w_xkv, half, bsz, nmem, name):
    depth, d, _ = w_xkv.shape
    dh = d // X_H
    nb = 1
    return pl.pallas_call(
        _mem_proj_kernel,
        grid=(depth, bsz // nb),
        in_specs=[pl.BlockSpec((nb * nmem, d), lambda l, i: (i, 0)),
                  pl.BlockSpec((1, 1, d), lambda l, i: (l, 0, 0)),
                  pl.BlockSpec((1, d, d), lambda l, i: (l, 0, half), pipeline_mode=pl.Buffered(1))],
        out_specs=[pl.BlockSpec((1, nb, nmem, X_H, dh), lambda l, i: (l, i, 0, 0, 0)),
                   pl.BlockSpec((1, nb * nmem, d), lambda l, i: (l, i, 0))],
        out_shape=[jax.ShapeDtypeStruct((depth, bsz, nmem, X_H, dh), F32),
                   jax.ShapeDtypeStruct((depth, bsz * nmem, d), BF16)],
        scratch_shapes=[pltpu.VMEM((d, d), BF16)],
        compiler_params=_cp("arbitrary", "arbitrary"),
        name=name,
    )(mem2, ln_mem[:, None, :], w_xkv)


def _gelu_tanh(x):
    c = 0.7978845608028654
    return x * (0.5 * (1.0 + jnp.tanh(c * (x + 0.044715 * (x * x * x)))))


def _ffn_in_kernel(h_ref, wg_ref, wu_ref, cw_ref, cb_ref, buf_ref, act_ref, nbuf_ref,
                   wgb_ref, wub_ref, carry_ref, *, tiles_per_seq, seq_rows):
    i = pl.program_id(1)
    tm = h_ref.shape[0]
    sub = min(tm, FFN_SUB)
    seg = sub if seq_rows is None else seq_rows

    @pl.when(i == 0)
    def _():
        _cast_weight_tile(wg_ref, None, wgb_ref, 0)
        _cast_weight_tile(wu_ref, None, wub_ref, 0)

    if seq_rows is None:
        @pl.when(i % tiles_per_seq == 0)
        def _():
            carry_ref[...] = buf_ref[0]

        prev = carry_ref[...]

    wg, wu = wgb_ref[...], wub_ref[...]
    taps, cb = [cw_ref[n:n + 1, :] for n in range(cw_ref.shape[0])], cb_ref[...]
    for r in range(tm // sub):
        h = h_ref[r * sub:(r + 1) * sub, :]
        gate_all = _dot(h, wg)
        up_all = _dot(h, wu)
        for s in range(sub // seg):
            lo = r * sub + s * seg
            gate = gate_all[s * seg:(s + 1) * seg]
            if seq_rows is not None:
                prev = buf_ref[lo // seg]
            y = _causal_conv(gate, prev, taps)
            act = _gelu_tanh(y + cb) * up_all[s * seg:(s + 1) * seg]
            act_ref[lo:lo + seg, :] = act.astype(act_ref.dtype)
            prev = gate[seg - SUBLANES:, :]
            if seq_rows is not None:
                nbuf_ref[lo // seg] = prev
    if seq_rows is None:
        carry_ref[...] = prev
        nbuf_ref[0] = prev


def _ffn_in(h, w_ffn_in, layer, conv_w, conv_b, buf8, t, tm, tn):
    m, d = h.shape
    dff = w_ffn_in.shape[2] // 2
    ni, nj = m // tm, dff // tn
    if t >= tm:
        tps, seq_rows, nb = t // tm, None, 1
        buf_ix = lambda j, i: (i // tps, 0, j)
    else:
        tps, seq_rows, nb = 1, t, tm // t
        buf_ix = lambda j, i: (i, 0, j)
    act, tails = pl.pallas_call(
        functools.partial(_ffn_in_kernel, tiles_per_seq=tps, seq_rows=seq_rows),
        grid=(nj, ni),
        in_specs=[pl.BlockSpec((tm, d), lambda j, i: (i, 0)),
                  pl.BlockSpec((1, d, tn), lambda j, i: (layer, 0, j)),
                  pl.BlockSpec((1, d, tn), lambda j, i: (layer, 0, nj + j)),
                  pl.BlockSpec((FFN_CONV, tn), lambda j, i: (0, j)),
                  pl.BlockSpec((1, tn), lambda j, i: (0, j)),
                  pl.BlockSpec((nb, SUBLANES, tn), buf_ix)],
        out_specs=[pl.BlockSpec((tm, tn), lambda j, i: (i, j)),
                   pl.BlockSpec((nb, SUBLANES, tn), lambda j, i: (i, 0, j))],
        out_shape=[jax.ShapeDtypeStruct((m, dff), BF16),
                   jax.ShapeDtypeStruct((ni * nb, SUBLANES, dff), F32)],
        scratch_shapes=[pltpu.VMEM((d, tn), BF16), pltpu.VMEM((d, tn), BF16),
                        pltpu.VMEM((SUBLANES, tn), F32)],
        compiler_params=_cp("parallel", "arbitrary"),
        name="ffn_in",
    )(h, w_ffn_in, w_ffn_in, conv_w, conv_b.reshape(1, dff), buf8)
    return act, tails[tps - 1::tps]


def _ret_kernel(ld_ref, p_ref, cos_ref, sin_ref, s0_ref, y_ref, s_ref, *, c):
    t = pl.program_id(1)
    tb = p_ref.shape[0]
    half = RET_DK // 2
    hw = RET_H * RET_DK
    heads = range(RET_H)

    @pl.when(t == 0)
    def _():
        s_ref[...] = s0_ref[...]

    ri = lax.broadcasted_iota(jnp.int32, (c, c), 0)
    ci = lax.broadcasted_iota(jnp.int32, (c, c), 1)
    diff = (ri - ci).astype(F32)
    causal = ri >= ci
    idx = lax.broadcasted_iota(jnp.int32, (c, RET_DK), 0).astype(F32)
    lds = [ld_ref[h] for h in heads]
    dmask = [jnp.where(causal, jnp.exp(ld * jnp.where(causal, diff, 0.0)), 0.0) for ld in lds]
    q_dec = [jnp.exp(ld * (idx + 1.0)) for ld in lds]
    k_dec = [jnp.exp(ld * (c - 1.0 - idx)) for ld in lds]
    s_dec = [jnp.exp(jnp.full((1, RET_DV), ld * c, F32)) for ld in lds]

    def rot(x, cos, sin):
        x1, x2 = x[:, :half], x[:, half:]
        return jnp.concatenate([x1 * cos - x2 * sin, x1 * sin + x2 * cos], axis=-1)

    nchunk = tb // c
    nc = RET_CHUNKS_PER_STEP if nchunk % RET_CHUNKS_PER_STEP == 0 else 1

    def body(jj, carry):
        ch = []
        for ci in range(nc):
            rows = pl.ds(pl.multiple_of((jj * nc + ci) * c, c), c)
            cos, sin = cos_ref[rows, :], sin_ref[rows, :]
            k = [rot(p_ref[rows, hw + h * RET_DK:hw + (h + 1) * RET_DK], cos, sin) * (RET_DK ** -0.5)
                 for h in heads]
            ch.append(dict(
                rows=rows, k=k,
                qb=[rot(p_ref[rows, h * RET_DK:(h + 1) * RET_DK], cos, sin).astype(BF16) for h in heads],
                vb=[p_ref[rows, 2 * hw + h * RET_DV:2 * hw + (h + 1) * RET_DV].astype(BF16) for h in heads]))
        probs = [(d, h) for d in ch for h in heads]
        att = [(_dot_nt(d['qb'][h], d['k'][h].astype(BF16)) * dmask[h]).astype(BF16) for d, h in probs]
        av = [_dot(a, d['vb'][h]) for a, (d, h) in zip(att, probs)]
        kv = [_dot_tn((d['k'][h] * k_dec[h]).astype(BF16), d['vb'][h]) for d, h in probs]

        for ci, d in enumerate(ch):
            s = [s_ref[0, h] for h in heads]
            o = [av[ci * RET_H + h] + _dot(d['qb'][h], s[h].astype(BF16)) * q_dec[h] for h in heads]
            for h in heads:
                s_ref[0, h] = s_dec[h] * s[h] + kv[ci * RET_H + h]
            for h in heads:
                g = p_ref[d['rows'], 3 * hw + h * RET_DV:3 * hw + (h + 1) * RET_DV]
                y_ref[d['rows'], h * RET_DV:(h + 1) * RET_DV] = (_rms(o[h]) * _silu(g)).astype(y_ref.dtype)
        return carry

    lax.fori_loop(0, nchunk // nc, body, 0)


def _retention(p_ret, cos, sin, ld, s0, t, tb):
    m, width = p_ret.shape
    b = m // t
    c = min(CHUNK, t)
    nt = t // tb
    st_spec = pl.BlockSpec((1, RET_H, RET_DK, RET_DV), lambda bi, ti: (bi, 0, 0, 0))
    rope_spec = pl.BlockSpec((tb, RET_DK // 2), lambda bi, ti: (ti, 0))
    return pl.pallas_call(
        functools.partial(_ret_kernel, c=c),
        grid=(b, nt),
        in_specs=[pl.BlockSpec(memory_space=pltpu.SMEM),
                  pl.BlockSpec((tb, width), lambda bi, ti: (bi * nt + ti, 0)),
                  rope_spec, rope_spec, st_spec],
        out_specs=[pl.BlockSpec((tb, RET_H * RET_DV), lambda bi, ti: (bi * nt + ti, 0)), st_spec],
        out_shape=[jax.ShapeDtypeStruct((m, RET_H * RET_DV), BF16),
                   jax.ShapeDtypeStruct((b, RET_H, RET_DK, RET_DV), F32)],
        compiler_params=_cp("parallel", "arbitrary"),
        name="retention",
    )(ld, p_ret, cos, sin, s0)


def _gla_kernel(p_ref, lr_ref, wgk_ref, bgk_ref, nrm_ref, s0_ref, y_ref, s_ref, st_ref, *, c, nt):
    t = pl.program_id(1)
    tb = p_ref.shape[0]
    hk = GLA_H * GLA_DK
    hv = GLA_H * GLA_DV
    heads = range(GLA_H)

    @pl.when(t == 0)
    def _():
        for h in heads:
            st_ref[h] = s0_ref[0, h].T

    ri = lax.broadcasted_iota(jnp.int32, (c, c), 0)
    ci = lax.broadcasted_iota(jnp.int32, (c, c), 1)
    causal = ri >= ci
    tri = jnp.where(causal, 1.0, 0.0).astype(BF16)
    wgk = wgk_ref[...]
    bgk = bgk_ref[...]
    nrm = nrm_ref[...]

    nchunk = tb // c
    nc = GLA_CHUNKS_PER_STEP if nchunk % GLA_CHUNKS_PER_STEP == 0 else 1
    hs = lambda x, h: x[:, h * GLA_DK:(h + 1) * GLA_DK]

    def body(jj, carry):
        rows = [pl.ds(pl.multiple_of((jj * nc + ci) * c, c), c) for ci in range(nc)]
        z = [_dot(lr_ref[r, :].astype(BF16), wgk) + bgk for r in rows]
        bc = [_tri_cumsum(tri, -_softplus(-zz) / GLA_NORMALIZER) for zz in z]
        ch = []
        for r, b_ in zip(rows, bc):
            q = p_ref[r, 0:hk] * (GLA_DK ** -0.5)
            k = p_ref[r, hk:2 * hk]
            b_last = b_[c - 1:c, :]
            ch.append(dict(
                rows=r, qe=(q * jnp.exp(b_)).astype(BF16), ke=(k * jnp.exp(-b_)).astype(BF16),
                kd=(k * jnp.exp(b_last - b_)).astype(BF16), eb_last=jnp.exp(b_last),
                vb=[p_ref[r, 2 * hk + h * GLA_DV:2 * hk + (h + 1) * GLA_DV].astype(BF16) for h in heads]))
        probs = [(d, h) for d in ch for h in heads]
        att = [jnp.where(causal, _dot_nt(hs(d['qe'], h), hs(d['ke'], h)), 0.0).astype(BF16) for d, h in probs]
        av = [_dot(a, d['vb'][h]) for a, (d, h) in zip(att, probs)]

        for ci, d in enumerate(ch):
            st = [st_ref[h] for h in heads]
            o = [av[ci * GLA_H + h] + _dot_nt(hs(d['qe'], h), st[h].astype(BF16)) for h in heads]
            for h in heads:
                st_ref[h] = hs(d['eb_last'], h) * st[h] + _dot_tn(d['vb'][h], hs(d['kd'], h))
            for h in heads:
                g = p_ref[d['rows'], 2 * hk + hv + h * GLA_DV:2 * hk + hv + (h + 1) * GLA_DV]
                y_ref[d['rows'], h * GLA_DV:(h + 1) * GLA_DV] = (_rms(o[h], nrm) * _silu(g)).astype(y_ref.dtype)
        return carry

    lax.fori_loop(0, nchunk // nc, body, 0)

    @pl.when(t == nt - 1)
    def _():
        for h in heads:
            s_ref[0, h] = st_ref[h].T


def _gla(p_gla, p_small, wgk, bgk, nrm, s0, t, tb):
    m, width = p_gla.shape
    b = m // t
    c = min(CHUNK, t)
    nt = t // tb
    const = lambda bi, ti: (0, 0)
    st_spec = pl.BlockSpec((1, GLA_H, GLA_DK, GLA_DV), lambda bi, ti: (bi, 0, 0, 0))
    return pl.pallas_call(
        functools.partial(_gla_kernel, c=c, nt=nt),
        grid=(b, nt),
        in_specs=[pl.BlockSpec((tb, width), lambda bi, ti: (bi * nt + ti, 0)),
                  pl.BlockSpec((tb, LANES), lambda bi, ti: (bi * nt + ti, 0)),
                  pl.BlockSpec((LANES, GLA_H * GLA_DK), const),
                  pl.BlockSpec((1, GLA_H * GLA_DK), const),
                  pl.BlockSpec((1, GLA_DV), const),
                  st_spec],
        out_specs=[pl.BlockSpec((tb, GLA_H * GLA_DV), lambda bi, ti: (bi * nt + ti, 0)), st_spec],
        out_shape=[jax.ShapeDtypeStruct((m, GLA_H * GLA_DV), BF16),
                   jax.ShapeDtypeStruct((b, GLA_H, GLA_DK, GLA_DV), F32)],
        scratch_shapes=[pltpu.VMEM((GLA_H, GLA_DV, GLA_DK), F32)],
        compiler_params=_cp("parallel", "arbitrary"),
        name="gla",
    )(p_gla, p_small, wgk, bgk, nrm, s0)


def _unit_lower_inverse_minus_eye(mats, c):
    ri = lax.broadcasted_iota(jnp.int32, (c, c), 0)
    ci = lax.broadcasted_iota(jnp.int32, (c, c), 1)

    def level_mask(k):
        same = (ri ^ ci) < 2 * k
        return same & ((ri & k) != 0) & ((ci & k) == 0)

    m1 = level_mask(1)
    ns = [-jnp.where(m1, a, 0.0) for a in mats]
    k = 2
    while k < c:
        mk = level_mask(k)
        ls = [jnp.where(mk, a, 0.0) for a in mats]
        nbs = [n.astype(BF16) for n in ns]
        ys = [l + _dot(nb, l.astype(BF16)) for l, nb in zip(ls, nbs)]
        xs = [y + _dot(y.astype(BF16), nb) for y, nb in zip(ys, nbs)]
        ns = [n - x for n, x in zip(ns, xs)]
        k *= 2
    return ns


def _gdn_kernel(q_ref, k_ref, v_ref, z_ref, sm_ref, alog_ref, dtb_ref, nrm_ref, s0_ref,
                y_ref, s_ref, *, c):
    t = pl.program_id(1)
    tb = q_ref.shape[0]

    @pl.when(t == 0)
    def _():
        s_ref[...] = s0_ref[...]

    ri = lax.broadcasted_iota(jnp.int32, (c, c), 0)
    ci = lax.broadcasted_iota(jnp.int32, (c, c), 1)
    incl = ri >= ci
    strict = ri > ci
    tri = jnp.where(incl, 1.0, 0.0).astype(BF16)
    lane = lax.broadcasted_iota(jnp.int32, (1, LANES), 1)
    a_lanes = jnp.logical_and(lane >= SMALL_A0, lane < SMALL_A0 + GDN_H)
    sel = jnp.where(lax.broadcasted_iota(jnp.int32, (SUBLANES, LANES), 1)
                    == lax.broadcasted_iota(jnp.int32, (SUBLANES, LANES), 0) + SMALL_A0,
                    1.0, 0.0).astype(BF16)
    neg_a = -jnp.exp(alog_ref[...])
    dtb = dtb_ref[...]
    nrm = nrm_ref[...]

    def col(x, lane_idx):
        return x[:, lane_idx:lane_idx + 1]

    heads = range(GDN_H)
    nchunk = tb // c
    nc = GDN_CHUNKS_PER_STEP if nchunk % GDN_CHUNKS_PER_STEP == 0 else 1

    def body(jj, carry):
        ch = []
        for ci in range(nc):
            rows = pl.ds(pl.multiple_of((jj * nc + ci) * c, c), c)
            sm = sm_ref[rows, :]
            ch.append(dict(
                rows=rows,
                g_all=jnp.where(a_lanes, neg_a * _softplus(sm + dtb), 0.0),
                beta_all=_sigmoid(sm),
                q=[q_ref[rows, h * GDN_DK:(h + 1) * GDN_DK] for h in heads],
                k=[k_ref[rows, h * GDN_DK:(h + 1) * GDN_DK] for h in heads],
                v=[v_ref[rows, h * GDN_DV:(h + 1) * GDN_DV] for h in heads]))
        for d in ch:
            d['gam'] = _tri_cumsum(tri, d['g_all'])
        for d in ch:
            ghi, gmid, glo = _split3(d['gam'])
            d['gam_rows'] = _dot_nt(sel, ghi) + _dot_nt(sel, gmid) + _dot_nt(sel, glo)
        for d in ch:
            gam = d['gam']
            g_last = gam[c - 1:c, :]
            d['eg_all'] = jnp.exp(gam)
            d['egl_all'] = jnp.exp(g_last - gam)
            d['eg_last_all'] = jnp.exp(g_last)
            d['beta'] = [col(d['beta_all'], SMALL_B0 + h) for h in heads]
            d['eg'] = [col(d['eg_all'], SMALL_A0 + h) for h in heads]
            d['decay'] = [jnp.where(incl, jnp.exp(jnp.where(
                incl, col(gam, SMALL_A0 + h) - d['gam_rows'][h:h + 1, :], 0.0)), 0.0) for h in heads]
        probs = [(d, h) for d in ch for h in heads]
        qk_kk = [_dot_nt(jnp.concatenate([d['q'][h], d['k'][h]], axis=0).astype(BF16), d['k'][h].astype(BF16))
                 for d, h in probs]
        att = [(r[:c] * d['decay'][h]).astype(BF16) for r, (d, h) in zip(qk_kk, probs)]
        a = [jnp.where(strict, d['beta'][h] * r[c:] * d['decay'][h], 0.0) for r, (d, h) in zip(qk_kk, probs)]
        n = _unit_lower_inverse_minus_eye(a, c)
        rhs = [jnp.concatenate([d['beta'][h] * d['v'][h], (d['beta'][h] * d['eg'][h]) * d['k'][h]], axis=-1)
               for d, h in probs]
        sol = [r + _dot(nn.astype(BF16), r.astype(BF16)) for r, nn in zip(rhs, n)]

        for ci, d in enumerate(ch):
            sl = sol[ci * GDN_H:(ci + 1) * GDN_H]
            at = att[ci * GDN_H:(ci + 1) * GDN_H]
            s = [s_ref[0, h] for h in heads]
            wq_s = [_dot(jnp.concatenate([sl[h][:, GDN_DV:], d['q'][h]], axis=0).astype(BF16), s[h].astype(BF16))
                    for h in heads]
            wb = [(sl[h][:, :GDN_DV] - wq_s[h][:c]).astype(BF16) for h in heads]
            o = [d['eg'][h] * wq_s[h][c:] + _dot(at[h], wb[h]) for h in heads]
            kd = [(d['k'][h] * col(d['egl_all'], SMALL_A0 + h)).astype(BF16) for h in heads]
            for h in heads:
                s_ref[0, h] = col(d['eg_last_all'], SMALL_A0 + h) * s[h] + _dot_tn(kd[h], wb[h])
            for h in heads:
                z = z_ref[d['rows'], h * GDN_DV:(h + 1) * GDN_DV]
                y = _rms(o[h], nrm) * _silu(z)
                y_ref[d['rows'], h * GDN_DV:(h + 1) * GDN_DV] = y.astype(y_ref.dtype)
        return carry

    lax.fori_loop(0, nchunk // nc, body, 0)


def _gdn(q, k, v, z, p_small, alog_row, dtb_row, nrm, s0, t, tb):
    m = q.shape[0]
    b = m // t
    c = min(CHUNK, t)
    nt = t // tb
    st_spec = pl.BlockSpec((1, GDN_H, GDN_DK, GDN_DV), lambda bi, ti: (bi, 0, 0, 0))
    const = lambda bi, ti: (0, 0)
    row = lambda bi, ti: (bi * nt + ti, 0)
    qk_spec = pl.BlockSpec((tb, GDN_H * GDN_DK), row)
    v_spec = pl.BlockSpec((tb, GDN_H * GDN_DV), row)
    return pl.pallas_call(
        functools.partial(_gdn_kernel, c=c),
        grid=(b, nt),
        in_specs=[qk_spec, qk_spec, v_spec, v_spec,
                  pl.BlockSpec((tb, LANES), row),
                  pl.BlockSpec((1, LANES), const),
                  pl.BlockSpec((1, LANES), const),
                  pl.BlockSpec((1, GDN_DV), const),
                  st_spec],
        out_specs=[v_spec, st_spec],
        out_shape=[jax.ShapeDtypeStruct((m, GDN_H * GDN_DV), BF16),
                   jax.ShapeDtypeStruct((b, GDN_H, GDN_DK, GDN_DV), F32)],
        compiler_params=_cp("parallel", "arbitrary"),
        name="gated_deltanet",
    )(q, k, v, z, p_small, alog_row, dtb_row, nrm, s0)


def _pad_rows_front(buf, rows):
    return jnp.pad(buf, ((0, 0), (0, 0), (rows - buf.shape[2], 0), (0, 0)))


def _in_proj_windows(d):
    sizes = dict(ret=4 * RET_H * RET_DK, gla=2 * GLA_H * GLA_DK + 2 * GLA_H * GLA_DV, lr=GLA_RANK,
                 gdn=2 * GDN_H * GDN_DK + 2 * GDN_H * GDN_DV, ab=2 * GDN_H, gbr=N_BRANCH * d)
    win, o = {}, 0
    for name in ('ret', 'gla', 'lr', 'gdn', 'ab', 'gbr'):
        win[name] = (o, sizes[name])
        o += sizes[name]
    assert win['lr'][0] % LANES == 0 and win['ab'][0] % LANES == SMALL_A0
    return win


def _prep_weights(w):
    lane_row = lambda v, at: jnp.pad(v, ((0, 0), (at, LANES - at - v.shape[1])))[:, None, :]
    return dict(
        w_in_t=jnp.swapaxes(w['w_in'], 1, 2), w_xq=w['w_xq'], w_ffn_in=w['w_ffn_in'],
        w_gk=jnp.pad(w['w_gla_gk'], ((0, 0), (0, LANES - GLA_RANK), (0, 0))).astype(BF16),
        b_gk=w['b_gla_gk'][:, None, :],
        gla_norm=w['gla_norm'][:, None, :], gdn_norm=w['gdn_norm'][:, None, :],
        alog_row=lane_row(w['gdn_a_log'], SMALL_A0), dtb_row=lane_row(w['gdn_dt_bias'], SMALL_A0),
        gdn_conv_w=w['gdn_conv_w'], b_gate=w['b_gate'],
        w_branch=w['w_branch'].astype(BF16), w_out=w['w_out'].astype(BF16), w_xo=w['w_xo'].astype(BF16),
        ffn_conv_w=w['ffn_conv_w'], ffn_conv_b=w['ffn_conv_b'], w_ffn_out=w['w_ffn_out'].astype(BF16),
        ln_mix=w['ln_mix'], ln_xattn=w['ln_xattn'], ln_ffn=w['ln_ffn'], ln_final=w['ln_final'],
    )


def _tile(n, cap):
    t = min(n, cap)
    while n % t:
        t -= 1
    return t


def _trunk(x, offset, mem_k, mem_v, s_ret, s_gla, s_gdn, buf_gdn, buf_ffn, pw):
    b, t, d = x.shape
    m = b * t
    depth = s_ret.shape[0]
    x2 = x.reshape(m, d)
    tm = _tile(m, MM_TM)
    tm_res = _tile(m, RES_TM)
    tm_res_k = _tile(m, RES_TM_K)
    tm_ffo = _tile(m, FFO_TM)
    tb = _tile(t, SEQ_TB)
    tb_gdn = _tile(t, SEQ_TB_GDN)
    tq = _tile(t, XATTN_TQ)
    tm_ffn = _tile(t, FFN_TM) if t >= FFN_TM else t * _tile(b, max(1, FFN_TM // t))

    pos = offset + jnp.arange(t, dtype=F32)
    half = RET_DK // 2
    freqs = 1.0 / (ROPE_BASE ** (jnp.arange(half, dtype=F32) / half))
    ang = pos[:, None] * freqs[None, :]
    cos, sin = jnp.cos(ang), jnp.sin(ang)
    ld = jnp.log1p(-jnp.exp2(-5.0 - jnp.arange(RET_H, dtype=F32)))

    buf_gdn8 = _pad_rows_front(buf_gdn, SUBLANES)
    buf_ffn8 = _pad_rows_front(buf_ffn, SUBLANES)
    win = _in_proj_windows(d)

    def in_proj(a, l, name, out_dtype=F32):
        off, n = win[name]
        blk0 = off // MM_TN
        shift = off - blk0 * MM_TN
        assert shift < LANES and n % MM_TN == 0
        return _mm_ws(a, pw['w_in_t'], l, blk0, n, shift, out_dtype, tm, MM_TN, "in_proj_" + name,
                      transposed=True)

    outs = ([], [], [], [], [])
    hn = _norm(x2, pw['ln_mix'][0], tm_res_k)
    y = None
    for l in range(depth):
        p_ret = in_proj(hn, l, 'ret')
        p_gla = in_proj(hn, l, 'gla')
        g_off = win['gdn'][0]
        hq, hv = GDN_H * GDN_DK, GDN_H * GDN_DV
        conv_args = (pw['gdn_conv_w'][l], buf_gdn8[l])
        d_q, tq8 = _conv_proj(hn, pw['w_in_t'], l, g_off, hq, *conv_args, 0, t, tm, MM_TN,
                              GDN_DK, GDN_DK ** -0.5, "in_proj_gdn_q")
        d_k, tk8 = _conv_proj(hn, pw['w_in_t'], l, g_off + hq, hq, *conv_args, hq, t, tm, MM_TN,
                              GDN_DK, 1.0, "in_proj_gdn_k")
        d_v, tv8 = _conv_proj(hn, pw['w_in_t'], l, g_off + 2 * hq, hv, *conv_args, 2 * hq, t, tm, MM_TN,
                              GDN_DV, None, "in_proj_gdn_v")
        zo = g_off + 2 * hq + hv
        d_z = _mm_ws(hn, pw['w_in_t'], l, zo // MM_TN, hv, zo % MM_TN, F32, tm, MM_TN, "in_proj_gdn_z",
                     transposed=True)
        bg8 = jnp.concatenate([tq8, tk8, tv8], axis=-1)
        g_br = in_proj(hn, l, 'gbr', BF16)
        p_small = _small_proj(hn, pw['w_in_t'], l, win['lr'][0] // LANES, win['ab'][0] // LANES, tm)

        y_ret, sr = _retention(p_ret, cos, sin, ld, s_ret[l], t, tb)
        y_gla, sg = _gla(p_gla, p_small, pw['w_gk'][l], pw['b_gk'][l], pw['gla_norm'][l], s_gla[l], t, tb)
        y_gdn, sd = _gdn(d_q, d_k, d_v, d_z, p_small, pw['alog_row'][l], pw['dtb_row'][l],
                         pw['gdn_norm'][l], s_gdn[l], t, tb_gdn)

        merged = _merge((y_ret, y_gla, y_gdn), pw['w_branch'][l], g_br, pw['b_gate'][l], tm_res_k)
        x2, hx = _proj_res(merged, pw['w_out'][l], x2, pw['ln_xattn'][l], tm_res, ROW_SUB, False, "out_proj")

        q = _mm_ws(hx, pw['w_xq'], l, 0, d, 0, BF16, tm, MM_TN, "xattn_q")
        o = _xattn(q, mem_k, mem_v, l, t, tq)
        x2, hf = _proj_res(o, pw['w_xo'][l], x2, pw['ln_ffn'][l], tm_res, ROW_SUB, False, "xattn_out")

        act, bf8 = _ffn_in(hf, pw['w_ffn_in'], l, pw['ffn_conv_w'][l], pw['ffn_conv_b'][l],
                           buf_ffn8[l], t, tm_ffn, FFN_TN)
        if l + 1 < depth:
            x2, hn = _proj_res(act, pw['w_ffn_out'][l], x2, pw['ln_mix'][l + 1], tm_ffo, ROW_SUB, False, "ffn_out")
        else:
            y, = _proj_res(act, pw['w_ffn_out'][l], x2, pw['ln_final'], tm_ffo, ROW_SUB, True, "ffn_out_final")

        for lst, val in zip(outs, (sr, sg, sd, bg8[:, SUBLANES - (GDN_CONV - 1):], bf8[:, SUBLANES - (FFN_CONV - 1):])):
            lst.append(val)
    return (y.reshape(b, t, d),) + tuple(jnp.stack(lst) for lst in outs)


def kernel(x_prompt, x_sample, mem_prompt, state_ret, state_gla, state_gdn, state_gdn_conv, state_ffn_conv, cache_mem_k, cache_mem_v, ln_mix, w_in, w_gla_gk, b_gla_gk, gla_norm, gdn_conv_w, gdn_a_log, gdn_dt_bias, gdn_norm, b_gate, w_branch, w_out, ln_xattn, ln_mem, w_xq, w_xkv, w_xo, ln_ffn, w_ffn_in, ffn_conv_w, ffn_conv_b, w_ffn_out, ln_final):
    pw = _prep_weights(dict(
        w_in=w_in, w_gla_gk=w_gla_gk, b_gla_gk=b_gla_gk, gla_norm=gla_norm, gdn_conv_w=gdn_conv_w,
        gdn_a_log=gdn_a_log, gdn_dt_bias=gdn_dt_bias, gdn_norm=gdn_norm, b_gate=b_gate, w_branch=w_branch,
        w_out=w_out, w_xq=w_xq, w_xo=w_xo, w_ffn_in=w_ffn_in, ffn_conv_w=ffn_conv_w, ffn_conv_b=ffn_conv_b,
        w_ffn_out=w_ffn_out, ln_mix=ln_mix, ln_xattn=ln_xattn, ln_ffn=ln_ffn, ln_final=ln_final))
    depth = w_in.shape[0]
    bp, nmem, d = mem_prompt.shape
    dt = x_prompt.dtype

    mem2 = mem_prompt.reshape(bp * nmem, d)
    mem_k_p, mem_kb = _mem_proj(mem2, ln_mem, w_xkv, 0, bp, nmem, "mem_k")
    mem_v_p, mem_vb = _mem_proj(mem2, ln_mem, w_xkv, 1, bp, nmem, "mem_v")

    zeros = lambda *s: jnp.zeros((depth, bp) + s, dt)
    y_p, ret_p, gla_p, gdn_p, gdn_conv_p, ffn_conv_p = _trunk(
        x_prompt, 0.0, mem_kb, mem_vb,
        zeros(RET_H, RET_DK, RET_DV), zeros(GLA_H, GLA_DK, GLA_DV), zeros(GDN_H, GDN_DK, GDN_DV),
        zeros(GDN_CONV - 1, state_gdn_conv.shape[-1]), zeros(FFN_CONV - 1, state_ffn_conv.shape[-1]), pw)

    past_len = 4096.0
    y_s, ret_s, gla_s, gdn_s, gdn_conv_s, ffn_conv_s = _trunk(
        x_sample, past_len, cache_mem_k, cache_mem_v,
        state_ret, state_gla, state_gdn, state_gdn_conv, state_ffn_conv, pw)

    return (y_p, y_s, ret_p, gla_p, gdn_p, gdn_conv_p, ffn_conv_p, mem_k_p, mem_v_p,
            ret_s, gla_s, gdn_s, gdn_conv_s, ffn_conv_s)
```

```python
import functools

import jax
import jax.numpy as jnp
from jax import lax
from jax.experimental import pallas as pl
from jax.experimental.pallas import tpu as pltpu

F32 = jnp.float32
BF16 = jnp.bfloat16

EPS = 1e-6
CHUNK = 64
RET_H, RET_DK, RET_DV = 4, 256, 256
GLA_H, GLA_DK, GLA_DV = 4, 128, 256
GLA_RANK = 16
GLA_NORMALIZER = 16.0
GDN_H, GDN_DK, GDN_DV = 8, 128, 128
GDN_CONV = 4
N_BRANCH = 3
X_H = 4
FFN_CONV = 3
ROPE_BASE = 10000.0

LANES = 128
SUBLANES = 8
SMALL_A0 = GLA_RANK
SMALL_B0 = GLA_RANK + GDN_H
VMEM_LIMIT = 52 * 1024 * 1024
MM_TM = 1024
MM_TN = 1024
RES_TM = 512
RES_TM_K = 512
FFO_TM = 256
ROW_SUB = 256
CONV_SUB = 256
SEQ_TB = 512
SEQ_TB_GDN = 256
GDN_CHUNKS_PER_STEP = 4
GLA_CHUNKS_PER_STEP = 2
RET_CHUNKS_PER_STEP = 2
XATTN_TQ = 512
FFN_TM = 1024
FFN_TN = 512
FFN_SUB = 256


def _cp(*sem):
    return pltpu.CompilerParams(dimension_semantics=sem, vmem_limit_bytes=VMEM_LIMIT)


def _dot(a, b):
    return jnp.dot(a, b, preferred_element_type=F32)


def _dot_nt(a, b):
    return lax.dot_general(a, b, (((1,), (1,)), ((), ())), preferred_element_type=F32)


def _dot_tn(a, b):
    return lax.dot_general(a, b, (((0,), (0,)), ((), ())), preferred_element_type=F32)


def _sigmoid(x):
    return 1.0 / (1.0 + jnp.exp(-x))


def _silu(x):
    return x * _sigmoid(x)


def _softplus(x):
    return jnp.maximum(x, 0.0) + jnp.log1p(jnp.exp(-jnp.abs(x)))


def _rms(x, g=None):
    y = x * lax.rsqrt(jnp.mean(x * x, axis=-1, keepdims=True) + EPS)
    return y if g is None else y * g


def _split3(x):
    hi = x.astype(BF16)
    r = x - hi.astype(F32)
    mid = r.astype(BF16)
    lo = (r - mid.astype(F32)).astype(BF16)
    return hi, mid, lo


def _tri_cumsum(tri, x):
    hi, mid, lo = _split3(x)
    return _dot(tri, hi) + _dot(tri, mid) + _dot(tri, lo)


def _norm_kernel(x_ref, g_ref, o_ref):
    o_ref[...] = _rms(x_ref[...], g_ref[...]).astype(o_ref.dtype)


def _norm(x, g, tm):
    m, d = x.shape
    return pl.pallas_call(
        _norm_kernel,
        grid=(m // tm,),
        in_specs=[pl.BlockSpec((tm, d), lambda i: (i, 0)),
                  pl.BlockSpec((1, d), lambda i: (0, 0))],
        out_specs=pl.BlockSpec((tm, d), lambda i: (i, 0)),
        out_shape=jax.ShapeDtypeStruct((m, d), BF16),
        compiler_params=_cp("parallel"),
        name="rmsnorm",
    )(x, g.reshape(1, d))


CAST_ROWS = 256


def _cast_weight_tile(w_ref, wx_ref, wb_ref, shift):
    kk, tn = wb_ref.shape
    for r in range(0, kk, CAST_ROWS):
        rows = slice(r, min(r + CAST_ROWS, kk))
        if shift == 0:
            wb_ref[rows, :] = w_ref[0, rows, :].astype(BF16)
        else:
            wide = jnp.concatenate([w_ref[0, rows, :], wx_ref[0, rows, :]], axis=1)
            wb_ref[rows, :] = pltpu.roll(wide, wide.shape[1] - shift, 1)[:, :tn].astype(BF16)


def _cast_weight_tile_t(w_ref, wx_ref, wb_ref, shift):
    tn, _ = wb_ref.shape
    for r in range(0, tn, CAST_ROWS):
        lo, hi = r + shift, min(r + CAST_ROWS, tn) + shift
        if hi <= tn:
            src = w_ref[0, lo:hi, :]
        else:
            src = jnp.concatenate([w_ref[0, lo:tn, :], wx_ref[0, 0:hi - tn, :]], axis=0)
        wb_ref[r:min(r + CAST_ROWS, tn), :] = src.astype(BF16)


def _mm_ws_kernel(a_ref, w_ref, *rest, shift, transposed):
    if shift:
        wx_ref, o_ref, wb_ref = rest
    else:
        wx_ref = None
        o_ref, wb_ref = rest

    @pl.when(pl.program_id(1) == 0)
    def _():
        (_cast_weight_tile_t if transposed else _cast_weight_tile)(w_ref, wx_ref, wb_ref, shift)

    mm = _dot_nt if transposed else _dot
    o_ref[...] = mm(a_ref[...], wb_ref[...]).astype(o_ref.dtype)


def _mm_ws(a, w, layer, blk0, n, shift, out_dtype, tm, tn, name, transposed=False):
    m, k = a.shape
    nj = n // tn
    per = tn // LANES
    if transposed:
        assert shift % SUBLANES == 0
        w_spec = pl.BlockSpec((1, tn, k), lambda j, i: (layer, blk0 + j, 0))
        wx_spec = pl.BlockSpec((1, LANES, k), lambda j, i: (layer, (blk0 + j + 1) * per, 0))
    else:
        w_spec = pl.BlockSpec((1, k, tn), lambda j, i: (layer, 0, blk0 + j))
        wx_spec = pl.BlockSpec((1, k, LANES), lambda j, i: (layer, 0, (blk0 + j + 1) * per))
    in_specs = [pl.BlockSpec((tm, k), lambda j, i: (i, 0)), w_spec]
    args = [a, w]
    if shift:
        in_specs.append(wx_spec)
        args.append(w)
    return pl.pallas_call(
        functools.partial(_mm_ws_kernel, shift=shift, transposed=transposed),
        grid=(nj, m // tm),
        in_specs=in_specs,
        out_specs=pl.BlockSpec((tm, tn), lambda j, i: (i, j)),
        out_shape=jax.ShapeDtypeStruct((m, n), out_dtype),
        scratch_shapes=[pltpu.VMEM((tn, k) if transposed else (k, tn), BF16)],
        compiler_params=_cp("parallel", "arbitrary"),
        name=name,
    )(*args)


def _small_proj_kernel(a_ref, w0_ref, w1_ref, o_ref):
    a = a_ref[...]
    p0 = _dot_nt(a, w0_ref[0].astype(BF16))
    p1 = _dot_nt(a, w1_ref[0].astype(BF16))
    lane = lax.broadcasted_iota(jnp.int32, p0.shape, 1)
    o_ref[...] = jnp.where(lane < SMALL_A0, p0, p1)


def _small_proj(a, w_t, layer, blk_lr, blk_ab, tm):
    m, k = a.shape
    return pl.pallas_call(
        _small_proj_kernel,
        grid=(m // tm,),
        in_specs=[pl.BlockSpec((tm, k), lambda i: (i, 0)),
                  pl.BlockSpec((1, LANES, k), lambda i: (layer, blk_lr, 0)),
                  pl.BlockSpec((1, LANES, k), lambda i: (layer, blk_ab, 0))],
        out_specs=pl.BlockSpec((tm, LANES), lambda i: (i, 0)),
        out_shape=jax.ShapeDtypeStruct((m, LANES), F32),
        compiler_params=_cp("parallel"),
        name="in_proj_small",
    )(a, w_t, w_t)


def _shift_rows(x, prev8, s):
    xs = pltpu.roll(x, s, 0)
    ps = pltpu.roll(prev8, s, 0)
    r8 = lax.broadcasted_iota(jnp.int32, prev8.shape, 0)
    top = jnp.where(r8 < s, ps, xs[:SUBLANES])
    if x.shape[0] == SUBLANES:
        return top
    return jnp.concatenate([top, xs[SUBLANES:]], axis=0)


def _causal_conv(x, prev8, taps):
    width = len(taps)
    y = None
    for i, w in enumerate(taps):
        s = width - 1 - i
        xs = x if s == 0 else _shift_rows(x, prev8, s)
        y = xs * w if y is None else y + xs * w
    return y


def _conv_proj_kernel(a_ref, w_ref, *rest, shift, tiles_per_seq, seq_rows, head_dim, norm_scale):
    if shift:
        wx_ref, cw_ref, buf_ref, o_ref, tail_ref, wb_ref, carry_ref = rest
    else:
        wx_ref = None
        cw_ref, buf_ref, o_ref, tail_ref, wb_ref, carry_ref = rest
    i = pl.program_id(1)
    tm, tn = o_ref.shape
    sub = min(tm, CONV_SUB)
    seg = sub if seq_rows is None else seq_rows

    @pl.when(i == 0)
    def _():
        _cast_weight_tile_t(w_ref, wx_ref, wb_ref, shift)

    if seq_rows is None:
        @pl.when(i % tiles_per_seq == 0)
        def _():
            carry_ref[...] = buf_ref[0]

        prev = carry_ref[...]

    wb = wb_ref[...]
    taps = [cw_ref[n:n + 1, :] for n in range(cw_ref.shape[0])]
    for r in range(tm // sub):
        p_all = _dot_nt(a_ref[r * sub:(r + 1) * sub, :], wb)
        for s in range(sub // seg):
            lo = r * sub + s * seg
            x = p_all[s * seg:(s + 1) * seg]
            if seq_rows is not None:
                prev = buf_ref[lo // seg]
            for h in range(tn // head_dim):
                hs = slice(h * head_dim, (h + 1) * head_dim)
                yh = _silu(_causal_conv(x[:, hs], prev[:, hs], [w[:, hs] for w in taps]))
                if norm_scale is not None:
                    yh = yh * lax.rsqrt(jnp.sum(yh * yh, axis=-1, keepdims=True) + EPS)
                    if norm_scale != 1.0:
                        yh = yh * norm_scale
                o_ref[lo:lo + seg, hs] = yh
            prev = x[seg - SUBLANES:, :]
            if seq_rows is not None:
                tail_ref[lo // seg] = prev
    if seq_rows is None:
        carry_ref[...] = prev
        tail_ref[0] = prev


def _conv_proj(a, w_t, layer, off, n, conv_w, buf8, coff, t, tm, tn, head_dim, norm_scale, name):
    m, k = a.shape
    nj, ni = n // tn, m // tm
    blk0 = off // tn
    shift = off - blk0 * tn
    assert shift < LANES and shift % SUBLANES == 0 and n % tn == 0 and coff % tn == 0
    cblk = coff // tn
    per = tn // LANES
    if t >= tm:
        tps, seq_rows, nb = t // tm, None, 1
        buf_ix = lambda j, i: (i // tps, 0, cblk + j)
    else:
        tps, seq_rows, nb = 1, t, tm // t
        buf_ix = lambda j, i: (i, 0, cblk + j)
    in_specs = [pl.BlockSpec((tm, k), lambda j, i: (i, 0)),
                pl.BlockSpec((1, tn, k), lambda j, i: (layer, blk0 + j, 0))]
    args = [a, w_t]
    if shift:
        in_specs.append(pl.BlockSpec((1, LANES, k), lambda j, i: (layer, (blk0 + j + 1) * per, 0)))
        args.append(w_t)
    in_specs += [pl.BlockSpec((conv_w.shape[0], tn), lambda j, i: (0, cblk + j)),
                 pl.BlockSpec((nb, SUBLANES, tn), buf_ix)]
    out, tails = pl.pallas_call(
        functools.partial(_conv_proj_kernel, shift=shift, tiles_per_seq=tps, seq_rows=seq_rows,
                          head_dim=head_dim, norm_scale=norm_scale),
        grid=(nj, ni),
        in_specs=in_specs,
        out_specs=[pl.BlockSpec((tm, tn), lambda j, i: (i, j)),
                   pl.BlockSpec((nb, SUBLANES, tn), lambda j, i: (i, 0, j))],
        out_shape=[jax.ShapeDtypeStruct((m, n), F32),
                   jax.ShapeDtypeStruct((ni * nb, SUBLANES, n), F32)],
        scratch_shapes=[pltpu.VMEM((tn, k), BF16), pltpu.VMEM((SUBLANES, tn), F32)],
        compiler_params=_cp("parallel", "arbitrary"),
        name=name,
    )(*args, conv_w, buf8)
    return out, tails[tps - 1::tps]


def _proj_res_kernel(a_ref, w_ref, x_ref, g_ref, *outs, sub, final):
    tm = a_ref.shape[0]
    w = w_ref[...]
    g = g_ref[...]
    for r in range(tm // sub):
        rows = slice(r * sub, (r + 1) * sub)
        xn = x_ref[rows, :] + _dot(a_ref[rows, :], w)
        if final:
            outs[0][rows, :] = _rms(xn, g)
        else:
            outs[0][rows, :] = xn
            outs[1][rows, :] = _rms(xn, g).astype(outs[1].dtype)


def _proj_res(a, w, x, g, tm, sub, final, name):
    m, kk = a.shape
    d = w.shape[1]
    row = lambda i: (i, 0)
    out_specs = [pl.BlockSpec((tm, d), row)]
    out_shape = [jax.ShapeDtypeStruct((m, d), F32)]
    if not final:
        out_specs.append(pl.BlockSpec((tm, d), row))
        out_shape.append(jax.ShapeDtypeStruct((m, d), BF16))
    return pl.pallas_call(
        functools.partial(_proj_res_kernel, sub=min(sub, tm), final=final),
        grid=(m // tm,),
        in_specs=[pl.BlockSpec((tm, kk), row),
                  pl.BlockSpec((kk, d), lambda i: (0, 0), pipeline_mode=pl.Buffered(1)),
                  pl.BlockSpec((tm, d), row),
                  pl.BlockSpec((1, d), lambda i: (0, 0))],
        out_specs=out_specs,
        out_shape=out_shape,
        compiler_params=_cp("parallel"),
        name=name,
    )(a, w, x, g.reshape(1, d))


def _merge_kernel(y0_ref, y1_ref, y2_ref, w_ref, g_ref, b_ref, o_ref, *, sub):
    tm, d = o_ref.shape
    for r in range(tm // sub):
        rows = slice(r * sub, (r + 1) * sub)
        acc = None
        for n, y_ref in enumerate((y0_ref, y1_ref, y2_ref)):
            p = _dot(y_ref[rows, :], w_ref[n])
            g = g_ref[rows, n * d:(n + 1) * d].astype(F32)
            t = _sigmoid(g + b_ref[n:n + 1, :]) * p
            acc = t if acc is None else acc + t
        o_ref[rows, :] = acc.astype(o_ref.dtype)


def _merge(ys, w_branch, g_br, b_gate, tm):
    m, bw = ys[0].shape
    d = w_branch.shape[2]
    row = lambda i: (i, 0)
    y_spec = pl.BlockSpec((tm, bw), row)
    return pl.pallas_call(
        functools.partial(_merge_kernel, sub=min(tm, ROW_SUB)),
        grid=(m // tm,),
        in_specs=[y_spec, y_spec, y_spec,
                  pl.BlockSpec((N_BRANCH, bw, d), lambda i: (0, 0, 0), pipeline_mode=pl.Buffered(1)),
                  pl.BlockSpec((tm, N_BRANCH * d), row),
                  pl.BlockSpec((N_BRANCH, d), lambda i: (0, 0))],
        out_specs=pl.BlockSpec((tm, d), row),
        out_shape=jax.ShapeDtypeStruct((m, d), BF16),
        compiler_params=_cp("parallel"),
        name="branch_merge",
    )(ys[0], ys[1], ys[2], w_branch, g_br, b_gate)


def _xattn_kernel(q_ref, k_ref, v_ref, o_ref, *scratch, scale, dh):
    heads = range(q_ref.shape[1] // dh)
    hs = lambda h: slice(h * dh, (h + 1) * dh)
    if scratch:
        kb_ref, vb_ref = scratch

        @pl.when(pl.program_id(1) == 0)
        def _():
            for h in heads:
                kb_ref[:, hs(h)] = k_ref[0, 0, :, h, :].astype(BF16)
                vb_ref[:, hs(h)] = v_ref[0, 0, :, h, :].astype(BF16)
    else:
        kb_ref, vb_ref = k_ref.at[0], v_ref.at[0]

    s = [_dot_nt(q_ref[:, hs(h)], kb_ref[:, hs(h)]) * scale for h in heads]
    e = [jnp.exp(x - jnp.max(x, axis=-1, keepdims=True)) for x in s]
    p = [(x / jnp.sum(x, axis=-1, keepdims=True)).astype(BF16) for x in e]
    for h in heads:
        o_ref[:, hs(h)] = _dot(p[h], vb_ref[:, hs(h)]).astype(o_ref.dtype)


def _xattn(q, mem_k, mem_v, layer, t, tq):
    m, d = q.shape
    b = m // t
    dh = d // X_H
    nq = t // tq
    if mem_k.ndim == 5:
        nmem = mem_k.shape[2]
        kv_spec = pl.BlockSpec((1, 1, nmem, X_H, dh), lambda bi, i: (layer, bi, 0, 0, 0))
        scratch = [pltpu.VMEM((nmem, d), BF16), pltpu.VMEM((nmem, d), BF16)]
    else:
        nmem = mem_k.shape[1] // b
        kv_spec = pl.BlockSpec((1, nmem, d), lambda bi, i: (layer, bi, 0))
        scratch = []
    qo_spec = pl.BlockSpec((tq, d), lambda bi, i: (bi * nq + i, 0))
    return pl.pallas_call(
        functools.partial(_xattn_kernel, scale=dh ** -0.5, dh=dh),
        grid=(b, nq),
        in_specs=[qo_spec, kv_spec, kv_spec],
        out_specs=qo_spec,
        out_shape=jax.ShapeDtypeStruct((m, d), BF16),
        scratch_shapes=scratch,
        compiler_params=_cp("parallel", "arbitrary"),
        name="xattn",
    )(q, mem_k, mem_v)


def _mem_proj_kernel(x_ref, g_ref, w_ref, o_ref, ob_ref, wb_ref):
    @pl.when(pl.program_id(1) == 0)
    def _():
        _cast_weight_tile(w_ref, None, wb_ref, 0)

    _, nb, nmem, nh, dh = o_ref.shape
    res = _dot(_rms(x_ref[...], g_ref[0]).astype(BF16), wb_ref[...])
    ob_ref[0] = res.astype(BF16)
    for bb in range(nb):
        for h in range(nh):
            o_ref[0, bb, :, h, :] = res[bb * nmem:(bb + 1) * nmem, h * dh:(h + 1) * dh]


def _mem_proj(mem2, ln_mem, w_xkv, half, bsz, nmem, name):
    depth, d, _ = w_xkv.shape
    dh = d // X_H
    nb = 1
    return pl.pallas_call(
        _mem_proj_kernel,
        grid=(depth, bsz // nb),
        in_specs=[pl.BlockSpec((nb * nmem, d), lambda l, i: (i, 0)),
                  pl.BlockSpec((1, 1, d), lambda l, i: (l, 0, 0)),
                  pl.BlockSpec((1, d, d), lambda l, i: (l, 0, half), pipeline_mode=pl.Buffered(1))],
        out_specs=[pl.BlockSpec((1, nb, nmem, X_H, dh), lambda l, i: (l, i, 0, 0, 0)),
                   pl.BlockSpec((1, nb * nmem, d), lambda l, i: (l, i, 0))],
        out_shape=[jax.ShapeDtypeStruct((depth, bsz, nmem, X_H, dh), F32),
                   jax.ShapeDtypeStruct((depth, bsz * nmem, d), BF16)],
        scratch_shapes=[pltpu.VMEM((d, d), BF16)],
        compiler_params=_cp("arbitrary", "arbitrary"),
        name=name,
    )(mem2, ln_mem[:, None, :], w_xkv)


def _gelu_tanh(x):
    c = 0.7978845608028654
    return x * (0.5 * (1.0 + jnp.tanh(c * (x + 0.044715 * (x * x * x)))))


def _ffn_in_kernel(h_ref, wg_ref, wu_ref, cw_ref, cb_ref, buf_ref, act_ref, nbuf_ref,
                   wgb_ref, wub_ref, carry_ref, *, tiles_per_seq, seq_rows):
    i = pl.program_id(1)
    tm = h_ref.shape[0]
    sub = min(tm, FFN_SUB)
    seg = sub if seq_rows is None else seq_rows

    @pl.when(i == 0)
    def _():
        _cast_weight_tile(wg_ref, None, wgb_ref, 0)
        _cast_weight_tile(wu_ref, None, wub_ref, 0)

    if seq_rows is None:
        @pl.when(i % tiles_per_seq == 0)
        def _():
            carry_ref[...] = buf_ref[0]

        prev = carry_ref[...]

    wg, wu = wgb_ref[...], wub_ref[...]
    taps, cb = [cw_ref[n:n + 1, :] for n in range(cw_ref.shape[0])], cb_ref[...]
    for r in range(tm // sub):
        h = h_ref[r * sub:(r + 1) * sub, :]
        gate_all = _dot(h, wg)
        up_all = _dot(h, wu)
        for s in range(sub // seg):
            lo = r * sub + s * seg
            gate = gate_all[s * seg:(s + 1) * seg]
            if seq_rows is not None:
                prev = buf_ref[lo // seg]
            y = _causal_conv(gate, prev, taps)
            act = _gelu_tanh(y + cb) * up_all[s * seg:(s + 1) * seg]
            act_ref[lo:lo + seg, :] = act.astype(act_ref.dtype)
            prev = gate[seg - SUBLANES:, :]
            if seq_rows is not None:
                nbuf_ref[lo // seg] = prev
    if seq_rows is None:
        carry_ref[...] = prev
        nbuf_ref[0] = prev


def _ffn_in(h, w_ffn_in, layer, conv_w, conv_b, buf8, t, tm, tn):
    m, d = h.shape
    dff = w_ffn_in.shape[2] // 2
    ni, nj = m // tm, dff // tn
    if t >= tm:
        tps, seq_rows, nb = t // tm, None, 1
        buf_ix = lambda j, i: (i // tps, 0, j)
    else:
        tps, seq_rows, nb = 1, t, tm // t
        buf_ix = lambda j, i: (i, 0, j)
    act, tails = pl.pallas_call(
        functools.partial(_ffn_in_kernel, tiles_per_seq=tps, seq_rows=seq_rows),
        grid=(nj, ni),
        in_specs=[pl.BlockSpec((tm, d), lambda j, i: (i, 0)),
                  pl.BlockSpec((1, d, tn), lambda j, i: (layer, 0, j)),
                  pl.BlockSpec((1, d, tn), lambda j, i: (layer, 0, nj + j)),
                  pl.BlockSpec((FFN_CONV, tn), lambda j, i: (0, j)),
                  pl.BlockSpec((1, tn), lambda j, i: (0, j)),
                  pl.BlockSpec((nb, SUBLANES, tn), buf_ix)],
        out_specs=[pl.BlockSpec((tm, tn), lambda j, i: (i, j)),
                   pl.BlockSpec((nb, SUBLANES, tn), lambda j, i: (i, 0, j))],
        out_shape=[jax.ShapeDtypeStruct((m, dff), BF16),
                   jax.ShapeDtypeStruct((ni * nb, SUBLANES, dff), F32)],
        scratch_shapes=[pltpu.VMEM((d, tn), BF16), pltpu.VMEM((d, tn), BF16),
                        pltpu.VMEM((SUBLANES, tn), F32)],
        compiler_params=_cp("parallel", "arbitrary"),
        name="ffn_in",
    )(h, w_ffn_in, w_ffn_in, conv_w, conv_b.reshape(1, dff), buf8)
    return act, tails[tps - 1::tps]


def _ret_kernel(ld_ref, p_ref, cos_ref, sin_ref, s0_ref, y_ref, s_ref, *, c):
    t = pl.program_id(1)
    tb = p_ref.shape[0]
    half = RET_DK // 2
    hw = RET_H * RET_DK
    heads = range(RET_H)

    @pl.when(t == 0)
    def _():
        s_ref[...] = s0_ref[...]

    ri = lax.broadcasted_iota(jnp.int32, (c, c), 0)
    ci = lax.broadcasted_iota(jnp.int32, (c, c), 1)
    diff = (ri - ci).astype(F32)
    causal = ri >= ci
    idx = lax.broadcasted_iota(jnp.int32, (c, RET_DK), 0).astype(F32)
    lds = [ld_ref[h] for h in heads]
    dmask = [jnp.where(causal, jnp.exp(ld * jnp.where(causal, diff, 0.0)), 0.0) for ld in lds]
    q_dec = [jnp.exp(ld * (idx + 1.0)) for ld in lds]
    k_dec = [jnp.exp(ld * (c - 1.0 - idx)) for ld in lds]
    s_dec = [jnp.exp(jnp.full((1, RET_DV), ld * c, F32)) for ld in lds]

    def rot(x, cos, sin):
        x1, x2 = x[:, :half], x[:, half:]
        return jnp.concatenate([x1 * cos - x2 * sin, x1 * sin + x2 * cos], axis=-1)

    nchunk = tb // c
    nc = RET_CHUNKS_PER_STEP if nchunk % RET_CHUNKS_PER_STEP == 0 else 1

    def body(jj, carry):
        ch = []
        for ci in range(nc):
            rows = pl.ds(pl.multiple_of((jj * nc + ci) * c, c), c)
            cos, sin = cos_ref[rows, :], sin_ref[rows, :]
            k = [rot(p_ref[rows, hw + h * RET_DK:hw + (h + 1) * RET_DK], cos, sin) * (RET_DK ** -0.5)
                 for h in heads]
            ch.append(dict(
                rows=rows, k=k,
                qb=[rot(p_ref[rows, h * RET_DK:(h + 1) * RET_DK], cos, sin).astype(BF16) for h in heads],
                vb=[p_ref[rows, 2 * hw + h * RET_DV:2 * hw + (h + 1) * RET_DV].astype(BF16) for h in heads]))
        probs = [(d, h) for d in ch for h in heads]
        att = [(_dot_nt(d['qb'][h], d['k'][h].astype(BF16)) * dmask[h]).astype(BF16) for d, h in probs]
        av = [_dot(a, d['vb'][h]) for a, (d, h) in zip(att, probs)]
        kv = [_dot_tn((d['k'][h] * k_dec[h]).astype(BF16), d['vb'][h]) for d, h in probs]

        for ci, d in enumerate(ch):
            s = [s_ref[0, h] for h in heads]
            o = [av[ci * RET_H + h] + _dot(d['qb'][h], s[h].astype(BF16)) * q_dec[h] for h in heads]
            for h in heads:
                s_ref[0, h] = s_dec[h] * s[h] + kv[ci * RET_H + h]
            for h in heads:
                g = p_ref[d['rows'], 3 * hw + h * RET_DV:3 * hw + (h + 1) * RET_DV]
                y_ref[d['rows'], h * RET_DV:(h + 1) * RET_DV] = (_rms(o[h]) * _silu(g)).astype(y_ref.dtype)
        return carry

    lax.fori_loop(0, nchunk // nc, body, 0)


def _retention(p_ret, cos, sin, ld, s0, t, tb):
    m, width = p_ret.shape
    b = m // t
    c = min(CHUNK, t)
    nt = t // tb
    st_spec = pl.BlockSpec((1, RET_H, RET_DK, RET_DV), lambda bi, ti: (bi, 0, 0, 0))
    rope_spec = pl.BlockSpec((tb, RET_DK // 2), lambda bi, ti: (ti, 0))
    return pl.pallas_call(
        functools.partial(_ret_kernel, c=c),
        grid=(b, nt),
        in_specs=[pl.BlockSpec(memory_space=pltpu.SMEM),
                  pl.BlockSpec((tb, width), lambda bi, ti: (bi * nt + ti, 0)),
                  rope_spec, rope_spec, st_spec],
        out_specs=[pl.BlockSpec((tb, RET_H * RET_DV), lambda bi, ti: (bi * nt + ti, 0)), st_spec],
        out_shape=[jax.ShapeDtypeStruct((m, RET_H * RET_DV), BF16),
                   jax.ShapeDtypeStruct((b, RET_H, RET_DK, RET_DV), F32)],
        compiler_params=_cp("parallel", "arbitrary"),
        name="retention",
    )(ld, p_ret, cos, sin, s0)


def _gla_kernel(p_ref, lr_ref, wgk_ref, bgk_ref, nrm_ref, s0_ref, y_ref, s_ref, st_ref, *, c, nt):
    t = pl.program_id(1)
    tb = p_ref.shape[0]
    hk = GLA_H * GLA_DK
    hv = GLA_H * GLA_DV
    heads = range(GLA_H)

    @pl.when(t == 0)
    def _():
        for h in heads:
            st_ref[h] = s0_ref[0, h].T

    ri = lax.broadcasted_iota(jnp.int32, (c, c), 0)
    ci = lax.broadcasted_iota(jnp.int32, (c, c), 1)
    causal = ri >= ci
    tri = jnp.where(causal, 1.0, 0.0).astype(BF16)
    wgk = wgk_ref[...]
    bgk = bgk_ref[...]
    nrm = nrm_ref[...]

    nchunk = tb // c
    nc = GLA_CHUNKS_PER_STEP if nchunk % GLA_CHUNKS_PER_STEP == 0 else 1
    hs = lambda x, h: x[:, h * GLA_DK:(h + 1) * GLA_DK]

    def body(jj, carry):
        rows = [pl.ds(pl.multiple_of((jj * nc + ci) * c, c), c) for ci in range(nc)]
        z = [_dot(lr_ref[r, :].astype(BF16), wgk) + bgk for r in rows]
        bc = [_tri_cumsum(tri, -_softplus(-zz) / GLA_NORMALIZER) for zz in z]
        ch = []
        for r, b_ in zip(rows, bc):
            q = p_ref[r, 0:hk] * (GLA_DK ** -0.5)
            k = p_ref[r, hk:2 * hk]
            b_last = b_[c - 1:c, :]
            ch.append(dict(
                rows=r, qe=(q * jnp.exp(b_)).astype(BF16), ke=(k * jnp.exp(-b_)).astype(BF16),
                kd=(k * jnp.exp(b_last - b_)).astype(BF16), eb_last=jnp.exp(b_last),
                vb=[p_ref[r, 2 * hk + h * GLA_DV:2 * hk + (h + 1) * GLA_DV].astype(BF16) for h in heads]))
        probs = [(d, h) for d in ch for h in heads]
        att = [jnp.where(causal, _dot_nt(hs(d['qe'], h), hs(d['ke'], h)), 0.0).astype(BF16) for d, h in probs]
        av = [_dot(a, d['vb'][h]) for a, (d, h) in zip(att, probs)]

        for ci, d in enumerate(ch):
            st = [st_ref[h] for h in heads]
            o = [av[ci * GLA_H + h] + _dot_nt(hs(d['qe'], h), st[h].astype(BF16)) for h in heads]
            for h in heads:
                st_ref[h] = hs(d['eb_last'], h) * st[h] + _dot_tn(d['vb'][h], hs(d['kd'], h))
            for h in heads:
                g = p_ref[d['rows'], 2 * hk + hv + h * GLA_DV:2 * hk + hv + (h + 1) * GLA_DV]
                y_ref[d['rows'], h * GLA_DV:(h + 1) * GLA_DV] = (_rms(o[h], nrm) * _silu(g)).astype(y_ref.dtype)
        return carry

    lax.fori_loop(0, nchunk // nc, body, 0)

    @pl.when(t == nt - 1)
    def _():
        for h in heads:
            s_ref[0, h] = st_ref[h].T


def _gla(p_gla, p_small, wgk, bgk, nrm, s0, t, tb):
    m, width = p_gla.shape
    b = m // t
    c = min(CHUNK, t)
    nt = t // tb
    const = lambda bi, ti: (0, 0)
    st_spec = pl.BlockSpec((1, GLA_H, GLA_DK, GLA_DV), lambda bi, ti: (bi, 0, 0, 0))
    return pl.pallas_call(
        functools.partial(_gla_kernel, c=c, nt=nt),
        grid=(b, nt),
        in_specs=[pl.BlockSpec((tb, width), lambda bi, ti: (bi * nt + ti, 0)),
                  pl.BlockSpec((tb, LANES), lambda bi, ti: (bi * nt + ti, 0)),
                  pl.BlockSpec((LANES, GLA_H * GLA_DK), const),
                  pl.BlockSpec((1, GLA_H * GLA_DK), const),
                  pl.BlockSpec((1, GLA_DV), const),
                  st_spec],
        out_specs=[pl.BlockSpec((tb, GLA_H * GLA_DV), lambda bi, ti: (bi * nt + ti, 0)), st_spec],
        out_shape=[jax.ShapeDtypeStruct((m, GLA_H * GLA_DV), BF16),
                   jax.ShapeDtypeStruct((b, GLA_H, GLA_DK, GLA_DV), F32)],
        scratch_shapes=[pltpu.VMEM((GLA_H, GLA_DV, GLA_DK), F32)],
        compiler_params=_cp("parallel", "arbitrary"),
        name="gla",
    )(p_gla, p_small, wgk, bgk, nrm, s0)


def _unit_lower_inverse_minus_eye(mats, c):
    ri = lax.broadcasted_iota(jnp.int32, (c, c), 0)
    ci = lax.broadcasted_iota(jnp.int32, (c, c), 1)

    def level_mask(k):
        same = (ri ^ ci) < 2 * k
        return same & ((ri & k) != 0) & ((ci & k) == 0)

    m1 = level_mask(1)
    ns = [-jnp.where(m1, a, 0.0) for a in mats]
    k = 2
    while k < c:
        mk = level_mask(k)
        ls = [jnp.where(mk, a, 0.0) for a in mats]
        nbs = [n.astype(BF16) for n in ns]
        ys = [l + _dot(nb, l.astype(BF16)) for l, nb in zip(ls, nbs)]
        xs = [y + _dot(y.astype(BF16), nb) for y, nb in zip(ys, nbs)]
        ns = [n - x for n, x in zip(ns, xs)]
        k *= 2
    return ns


def _blockdiag2(xp, left):
    return jnp.concatenate([jnp.where(left, xp, 0.0), jnp.where(left, 0.0, xp)], axis=0)


def _unit_lower_inverse_minus_eye_packed(mats, c):
    ri = lax.broadcasted_iota(jnp.int32, (c, 2 * c), 0)
    cl = lax.broadcasted_iota(jnp.int32, (c, 2 * c), 1)
    left = cl < c
    cm = cl & (c - 1)

    def level_mask(k):
        return ((ri ^ cm) < 2 * k) & ((ri & k) != 0) & ((cm & k) == 0)

    m1 = level_mask(1)
    ns = [-jnp.where(m1, a, 0.0) for a in mats]
    k = 2
    while k < c:
        mk = level_mask(k)
        ls = [jnp.where(mk, a, 0.0) for a in mats]
        ys = [l + _dot(n.astype(BF16), _blockdiag2(l, left).astype(BF16)) for l, n in zip(ls, ns)]
        xs = [y + _dot(y.astype(BF16), _blockdiag2(n, left).astype(BF16)) for y, n in zip(ys, ns)]
        ns = [n - x for n, x in zip(ns, xs)]
        k *= 2
    return ns


def _gdn_kernel(q_ref, k_ref, v_ref, z_ref, sm_ref, alog_ref, dtb_ref, nrm_ref, s0_ref,
                y_ref, s_ref, *, c):
    t = pl.program_id(1)
    tb = q_ref.shape[0]

    @pl.when(t == 0)
    def _():
        s_ref[...] = s0_ref[...]

    tri = jnp.where(lax.broadcasted_iota(jnp.int32, (c, c), 0) >= lax.broadcasted_iota(jnp.int32, (c, c), 1),
                    1.0, 0.0).astype(BF16)
    ri = lax.broadcasted_iota(jnp.int32, (c, 2 * c), 0)
    cl = lax.broadcasted_iota(jnp.int32, (c, 2 * c), 1)
    left = cl < c
    cm = cl & (c - 1)
    incl = ri >= cm
    strict = ri > cm
    lane = lax.broadcasted_iota(jnp.int32, (1, LANES), 1)
    a_lanes = jnp.logical_and(lane >= SMALL_A0, lane < SMALL_A0 + GDN_H)
    sel2 = jnp.where(lax.broadcasted_iota(jnp.int32, (SUBLANES, LANES), 1)
                     == 2 * lax.broadcasted_iota(jnp.int32, (SUBLANES, LANES), 0) + SMALL_A0,
                     1.0, 0.0).astype(BF16)
    neg_a = -jnp.exp(alog_ref[...])
    dtb = dtb_ref[...]
    nrm = nrm_ref[...]

    def col(x, lane_idx):
        return x[:, lane_idx:lane_idx + 1]

    assert GDN_DK == LANES and GDN_DV == LANES and GDN_H % 2 == 0
    heads = range(GDN_H)
    npair = GDN_H // 2
    pairs = range(npair)
    nchunk = tb // c
    nc = GDN_CHUNKS_PER_STEP if nchunk % GDN_CHUNKS_PER_STEP == 0 else 1

    def lanes2(x, y):
        return jnp.concatenate([x, y], axis=1)

    def body(jj, carry):
        ch = []
        for ci in range(nc):
            rows = pl.ds(pl.multiple_of((jj * nc + ci) * c, c), c)
            sm = sm_ref[rows, :]
            ch.append(dict(
                rows=rows,
                g_all=jnp.where(a_lanes, neg_a * _softplus(sm + dtb), 0.0),
                beta_all=_sigmoid(sm),
                q=[q_ref[rows, h * GDN_DK:(h + 1) * GDN_DK] for h in heads],
                k=[k_ref[rows, h * GDN_DK:(h + 1) * GDN_DK] for h in heads],
                v=[v_ref[rows, h * GDN_DV:(h + 1) * GDN_DV] for h in heads]))
        for d in ch:
            d['gam'] = _tri_cumsum(tri, d['g_all'])
        for d in ch:
            g2 = jnp.concatenate([d['gam'], pltpu.roll(d['gam'], LANES - 1, 1)], axis=0)
            ghi, gmid, glo = _split3(g2)
            d['gam_rows'] = _dot_nt(sel2, ghi) + _dot_nt(sel2, gmid) + _dot_nt(sel2, glo)

        def packed_cols(x, base, p):
            return jnp.where(left, col(x, base + 2 * p), col(x, base + 2 * p + 1))

        for d in ch:
            gam = d['gam']
            g_last = gam[c - 1:c, :]
            d['eg_all'] = jnp.exp(gam)
            d['egl_all'] = jnp.exp(g_last - gam)
            d['eg_last_all'] = jnp.exp(g_last)
            d['beta'] = [col(d['beta_all'], SMALL_B0 + h) for h in heads]
            d['eg'] = [col(d['eg_all'], SMALL_A0 + h) for h in heads]
            d['beta_p'] = [packed_cols(d['beta_all'], SMALL_B0, p) for p in pairs]
            d['decay_p'] = [jnp.where(incl, jnp.exp(jnp.where(
                incl, packed_cols(gam, SMALL_A0, p) - d['gam_rows'][p:p + 1, :], 0.0)), 0.0) for p in pairs]
        probs = [(d, p) for d in ch for p in pairs]
        zk = jnp.zeros((c, GDN_DK), F32)
        qk_kk = [_dot_nt(
            jnp.concatenate([lanes2(d['q'][2 * p], d['q'][2 * p + 1]),
                             lanes2(d['k'][2 * p], d['k'][2 * p + 1])], axis=0).astype(BF16),
            jnp.concatenate([lanes2(d['k'][2 * p], zk), lanes2(zk, d['k'][2 * p + 1])], axis=0).astype(BF16))
            for d, p in probs]
        att = [(r[:c] * d['decay_p'][p]).astype(BF16) for r, (d, p) in zip(qk_kk, probs)]
        a = [jnp.where(strict, d['beta_p'][p] * r[c:] * d['decay_p'][p], 0.0) for r, (d, p) in zip(qk_kk, probs)]
        n = _unit_lower_inverse_minus_eye_packed(a, c)
        zr = jnp.zeros((c, GDN_DV + GDN_DK), F32)
        sol = []
        for (d, p), nn in zip(probs, n):
            r1, r2 = (jnp.concatenate([d['beta'][h] * d['v'][h], (d['beta'][h] * d['eg'][h]) * d['k'][h]], axis=-1)
                      for h in (2 * p, 2 * p + 1))
            bd = jnp.concatenate([lanes2(r1, zr), lanes2(zr, r2)], axis=0).astype(BF16)
            sol.append(lanes2(r1, r2) + _dot(nn.astype(BF16), bd))

        zs = jnp.zeros((GDN_DK, GDN_DV), F32)
        zw = jnp.zeros((c, GDN_DV), F32)
        for ci, d in enumerate(ch):
            sl = sol[ci * npair:(ci + 1) * npair]
            at = att[ci * npair:(ci + 1) * npair]
            s = [s_ref[0, h] for h in heads]
            wq_s = [_dot(
                jnp.concatenate([lanes2(sl[p][:, LANES:2 * LANES], sl[p][:, 3 * LANES:]),
                                 lanes2(d['q'][2 * p], d['q'][2 * p + 1])], axis=0).astype(BF16),
                jnp.concatenate([lanes2(s[2 * p], zs), lanes2(zs, s[2 * p + 1])], axis=0).astype(BF16))
                for p in pairs]
            bdw = [jnp.concatenate([lanes2(sl[p][:, :LANES] - wq_s[p][:c, :LANES], zw),
                                    lanes2(zw, sl[p][:, 2 * LANES:3 * LANES] - wq_s[p][:c, LANES:])],
                                   axis=0).astype(BF16) for p in pairs]
            aw = [_dot(at[p], bdw[p]) for p in pairs]
            kw = [_dot_tn(jnp.concatenate([d['k'][h] * col(d['egl_all'], SMALL_A0 + h)
                                           for h in (2 * p, 2 * p + 1)], axis=0).astype(BF16), bdw[p])
                  for p in pairs]
            for h in heads:
                p, half = h // 2, slice((h % 2) * LANES, (h % 2 + 1) * LANES)
                s_ref[0, h] = col(d['eg_last_all'], SMALL_A0 + h) * s[h] + kw[p][:, half]
            for h in heads:
                p, half = h // 2, slice((h % 2) * LANES, (h % 2 + 1) * LANES)
                o = d['eg'][h] * wq_s[p][c:, half] + aw[p][:, half]
                z = z_ref[d['rows'], h * GDN_DV:(h + 1) * GDN_DV]
                y = _rms(o, nrm) * _silu(z)
                y_ref[d['rows'], h * GDN_DV:(h + 1) * GDN_DV] = y.astype(y_ref.dtype)
        return carry

    lax.fori_loop(0, nchunk // nc, body, 0)


def _gdn(q, k, v, z, p_small, alog_row, dtb_row, nrm, s0, t, tb):
    m = q.shape[0]
    b = m // t
    c = min(CHUNK, t)
    nt = t // tb
    st_spec = pl.BlockSpec((1, GDN_H, GDN_DK, GDN_DV), lambda bi, ti: (bi, 0, 0, 0))
    const = lambda bi, ti: (0, 0)
    row = lambda bi, ti: (bi * nt + ti, 0)
    qk_spec = pl.BlockSpec((tb, GDN_H * GDN_DK), row)
    v_spec = pl.BlockSpec((tb, GDN_H * GDN_DV), row)
    return pl.pallas_call(
        functools.partial(_gdn_kernel, c=c),
        grid=(b, nt),
        in_specs=[qk_spec, qk_spec, v_spec, v_spec,
                  pl.BlockSpec((tb, LANES), row),
                  pl.BlockSpec((1, LANES), const),
                  pl.BlockSpec((1, LANES), const),
                  pl.BlockSpec((1, GDN_DV), const),
                  st_spec],
        out_specs=[v_spec, st_spec],
        out_shape=[jax.ShapeDtypeStruct((m, GDN_H * GDN_DV), BF16),
                   jax.ShapeDtypeStruct((b, GDN_H, GDN_DK, GDN_DV), F32)],
        compiler_params=_cp("parallel", "arbitrary"),
        name="gated_deltanet",
    )(q, k, v, z, p_small, alog_row, dtb_row, nrm, s0)


def _pad_rows_front(buf, rows):
    return jnp.pad(buf, ((0, 0), (0, 0), (rows - buf.shape[2], 0), (0, 0)))


def _in_proj_windows(d):
    sizes = dict(ret=4 * RET_H * RET_DK, gla=2 * GLA_H * GLA_DK + 2 * GLA_H * GLA_DV, lr=GLA_RANK,
                 gdn=2 * GDN_H * GDN_DK + 2 * GDN_H * GDN_DV, ab=2 * GDN_H, gbr=N_BRANCH * d)
    win, o = {}, 0
    for name in ('ret', 'gla', 'lr', 'gdn', 'ab', 'gbr'):
        win[name] = (o, sizes[name])
        o += sizes[name]
    assert win['lr'][0] % LANES == 0 and win['ab'][0] % LANES == SMALL_A0
    return win


def _prep_weights(w):
    lane_row = lambda v, at: jnp.pad(v, ((0, 0), (at, LANES - at - v.shape[1])))[:, None, :]
    return dict(
        w_in_t=jnp.swapaxes(w['w_in'], 1, 2), w_xq=w['w_xq'], w_ffn_in=w['w_ffn_in'],
        w_gk=jnp.pad(w['w_gla_gk'], ((0, 0), (0, LANES - GLA_RANK), (0, 0))).astype(BF16),
        b_gk=w['b_gla_gk'][:, None, :],
        gla_norm=w['gla_norm'][:, None, :], gdn_norm=w['gdn_norm'][:, None, :],
        alog_row=lane_row(w['gdn_a_log'], SMALL_A0), dtb_row=lane_row(w['gdn_dt_bias'], SMALL_A0),
        gdn_conv_w=w['gdn_conv_w'], b_gate=w['b_gate'],
        w_branch=w['w_branch'].astype(BF16), w_out=w['w_out'].astype(BF16), w_xo=w['w_xo'].astype(BF16),
        ffn_conv_w=w['ffn_conv_w'], ffn_conv_b=w['ffn_conv_b'], w_ffn_out=w['w_ffn_out'].astype(BF16),
        ln_mix=w['ln_mix'], ln_xattn=w['ln_xattn'], ln_ffn=w['ln_ffn'], ln_final=w['ln_final'],
    )


def _tile(n, cap):
    t = min(n, cap)
    while n % t:
        t -= 1
    return t


def _trunk(x, offset, mem_k, mem_v, s_ret, s_gla, s_gdn, buf_gdn, buf_ffn, pw):
    b, t, d = x.shape
    m = b * t
    depth = s_ret.shape[0]
    x2 = x.reshape(m, d)
    tm = _tile(m, MM_TM)
    tm_res = _tile(m, RES_TM)
    tm_res_k = _tile(m, RES_TM_K)
    tm_ffo = _tile(m, FFO_TM)
    tb = _tile(t, SEQ_TB)
    tb_gdn = _tile(t, SEQ_TB_GDN)
    tq = _tile(t, XATTN_TQ)
    tm_ffn = _tile(t, FFN_TM) if t >= FFN_TM else t * _tile(b, max(1, FFN_TM // t))

    pos = offset + jnp.arange(t, dtype=F32)
    half = RET_DK // 2
    freqs = 1.0 / (ROPE_BASE ** (jnp.arange(half, dtype=F32) / half))
    ang = pos[:, None] * freqs[None, :]
    cos, sin = jnp.cos(ang), jnp.sin(ang)
    ld = jnp.log1p(-jnp.exp2(-5.0 - jnp.arange(RET_H, dtype=F32)))

    buf_gdn8 = _pad_rows_front(buf_gdn, SUBLANES)
    buf_ffn8 = _pad_rows_front(buf_ffn, SUBLANES)
    win = _in_proj_windows(d)

    def in_proj(a, l, name, out_dtype=F32):
        off, n = win[name]
        blk0 = off // MM_TN
        shift = off - blk0 * MM_TN
        assert shift < LANES and n % MM_TN == 0
        return _mm_ws(a, pw['w_in_t'], l, blk0, n, shift, out_dtype, tm, MM_TN, "in_proj_" + name,
                      transposed=True)

    outs = ([], [], [], [], [])
    hn = _norm(x2, pw['ln_mix'][0], tm_res_k)
    y = None
    for l in range(depth):
        p_ret = in_proj(hn, l, 'ret')
        p_gla = in_proj(hn, l, 'gla')
        g_off = win['gdn'][0]
        hq, hv = GDN_H * GDN_DK, GDN_H * GDN_DV
        conv_args = (pw['gdn_conv_w'][l], buf_gdn8[l])
        d_q, tq8 = _conv_proj(hn, pw['w_in_t'], l, g_off, hq, *conv_args, 0, t, tm, MM_TN,
                              GDN_DK, GDN_DK ** -0.5, "in_proj_gdn_q")
        d_k, tk8 = _conv_proj(hn, pw['w_in_t'], l, g_off + hq, hq, *conv_args, hq, t, tm, MM_TN,
                              GDN_DK, 1.0, "in_proj_gdn_k")
        d_v, tv8 = _conv_proj(hn, pw['w_in_t'], l, g_off + 2 * hq, hv, *conv_args, 2 * hq, t, tm, MM_TN,
                              GDN_DV, None, "in_proj_gdn_v")
        zo = g_off + 2 * hq + hv
        d_z = _mm_ws(hn, pw['w_in_t'], l, zo // MM_TN, hv, zo % MM_TN, F32, tm, MM_TN, "in_proj_gdn_z",
                     transposed=True)
        bg8 = jnp.concatenate([tq8, tk8, tv8], axis=-1)
        g_br = in_proj(hn, l, 'gbr', BF16)
        p_small = _small_proj(hn, pw['w_in_t'], l, win['lr'][0] // LANES, win['ab'][0] // LANES, tm)

        y_ret, sr = _retention(p_ret, cos, sin, ld, s_ret[l], t, tb)
        y_gla, sg = _gla(p_gla, p_small, pw['w_gk'][l], pw['b_gk'][l], pw['gla_norm'][l], s_gla[l], t, tb)
        y_gdn, sd = _gdn(d_q, d_k, d_v, d_z, p_small, pw['alog_row'][l], pw['dtb_row'][l],
                         pw['gdn_norm'][l], s_gdn[l], t, tb_gdn)

        merged = _merge((y_ret, y_gla, y_gdn), pw['w_branch'][l], g_br, pw['b_gate'][l], tm_res_k)
        x2, hx = _proj_res(merged, pw['w_out'][l], x2, pw['ln_xattn'][l], tm_res, ROW_SUB, False, "out_proj")

        q = _mm_ws(hx, pw['w_xq'], l, 0, d, 0, BF16, tm, MM_TN, "xattn_q")
        o = _xattn(q, mem_k, mem_v, l, t, tq)
        x2, hf = _proj_res(o, pw['w_xo'][l], x2, pw['ln_ffn'][l], tm_res, ROW_SUB, False, "xattn_out")

        act, bf8 = _ffn_in(hf, pw['w_ffn_in'], l, pw['ffn_conv_w'][l], pw['ffn_conv_b'][l],
                           buf_ffn8[l], t, tm_ffn, FFN_TN)
        if l + 1 < depth:
            x2, hn = _proj_res(act, pw['w_ffn_out'][l], x2, pw['ln_mix'][l + 1], tm_ffo, ROW_SUB, False, "ffn_out")
        else:
            y, = _proj_res(act, pw['w_ffn_out'][l], x2, pw['ln_final'], tm_ffo, ROW_SUB, True, "ffn_out_final")

        for lst, val in zip(outs, (sr, sg, sd, bg8[:, SUBLANES - (GDN_CONV - 1):], bf8[:, SUBLANES - (FFN_CONV - 1):])):
            lst.append(val)
    return (y.reshape(b, t, d),) + tuple(jnp.stack(lst) for lst in outs)


def kernel(x_prompt, x_sample, mem_prompt, state_ret, state_gla, state_gdn, state_gdn_conv, state_ffn_conv, cache_mem_k, cache_mem_v, ln_mix, w_in, w_gla_gk, b_gla_gk, gla_norm, gdn_conv_w, gdn_a_log, gdn_dt_bias, gdn_norm, b_gate, w_branch, w_out, ln_xattn, ln_mem, w_xq, w_xkv, w_xo, ln_ffn, w_ffn_in, ffn_conv_w, ffn_conv_b, w_ffn_out, ln_final):
    pw = _prep_weights(dict(
        w_in=w_in, w_gla_gk=w_gla_gk, b_gla_gk=b_gla_gk, gla_norm=gla_norm, gdn_conv_w=gdn_conv_w,
        gdn_a_log=gdn_a_log, gdn_dt_bias=gdn_dt_bias, gdn_norm=gdn_norm, b_gate=b_gate, w_branch=w_branch,
        w_out=w_out, w_xq=w_xq, w_xo=w_xo, w_ffn_in=w_ffn_in, ffn_conv_w=ffn_conv_w, ffn_conv_b=ffn_conv_b,
        w_ffn_out=w_ffn_out, ln_mix=ln_mix, ln_xattn=ln_xattn, ln_ffn=ln_ffn, ln_final=ln_final))
    depth = w_in.shape[0]
    bp, nmem, d = mem_prompt.shape
    dt = x_prompt.dtype

    mem2 = mem_prompt.reshape(bp * nmem, d)
    mem_k_p, mem_kb = _mem_proj(mem2, ln_mem, w_xkv, 0, bp, nmem, "mem_k")
    mem_v_p, mem_vb = _mem_proj(mem2, ln_mem, w_xkv, 1, bp, nmem, "mem_v")

    zeros = lambda *s: jnp.zeros((depth, bp) + s, dt)
    y_p, ret_p, gla_p, gdn_p, gdn_conv_p, ffn_conv_p = _trunk(
        x_prompt, 0.0, mem_kb, mem_vb,
        zeros(RET_H, RET_DK, RET_DV), zeros(GLA_H, GLA_DK, GLA_DV), zeros(GDN_H, GDN_DK, GDN_DV),
        zeros(GDN_CONV - 1, state_gdn_conv.shape[-1]), zeros(FFN_CONV - 1, state_ffn_conv.shape[-1]), pw)

    past_len = 4096.0
    y_s, ret_s, gla_s, gdn_s, gdn_conv_s, ffn_conv_s = _trunk(
        x_sample, past_len, cache_mem_k, cache_mem_v,
        state_ret, state_gla, state_gdn, state_gdn_conv, state_ffn_conv, pw)

    return (y_p, y_s, ret_p, gla_p, gdn_p, gdn_conv_p, ffn_conv_p, mem_k_p, mem_v_p,
            ret_s, gla_s, gdn_s, gdn_conv_s, ffn_conv_s)
```

```python
import functools

import jax
import jax.numpy as jnp
from jax import lax
from jax.experimental import pallas as pl
from jax.experimental.pallas import tpu as pltpu

F32 = jnp.float32
BF16 = jnp.bfloat16

EPS = 1e-6
CHUNK = 64
RET_H, RET_DK, RET_DV = 4, 256, 256
GLA_H, GLA_DK, GLA_DV = 4, 128, 256
GLA_RANK = 16
GLA_NORMALIZER = 16.0
GDN_H, GDN_DK, GDN_DV = 8, 128, 128
GDN_CONV = 4
N_BRANCH = 3
X_H = 4
FFN_CONV = 3
ROPE_BASE = 10000.0

LANES = 128
SUBLANES = 8
SMALL_A0 = GLA_RANK
SMALL_B0 = GLA_RANK + GDN_H
VMEM_LIMIT = 52 * 1024 * 1024
MM_TM = 1024
MM_TN = 1024
RES_TM = 512
RES_TM_K = 512
FFO_TM = 256
ROW_SUB = 256
CONV_SUB = 256
SEQ_TB = 512
SEQ_TB_GDN = 256
GDN_CHUNKS_PER_STEP = 4
GLA_CHUNKS_PER_STEP = 2
RET_CHUNKS_PER_STEP = 2
XATTN_TQ = 512
FFN_TM = 1024
FFN_TN = 512
FFN_SUB = 256


def _cp(*sem):
    return pltpu.CompilerParams(dimension_semantics=sem, vmem_limit_bytes=VMEM_LIMIT)


def _dot(a, b):
    return jnp.dot(a, b, preferred_element_type=F32)


def _dot_nt(a, b):
    return lax.dot_general(a, b, (((1,), (1,)), ((), ())), preferred_element_type=F32)


def _dot_tn(a, b):
    return lax.dot_general(a, b, (((0,), (0,)), ((), ())), preferred_element_type=F32)


def _sigmoid(x):
    return 1.0 / (1.0 + jnp.exp(-x))


def _silu(x):
    return x * _sigmoid(x)


def _softplus(x):
    return jnp.maximum(x, 0.0) + jnp.log1p(jnp.exp(-jnp.abs(x)))


def _rms(x, g=None):
    y = x * lax.rsqrt(jnp.mean(x * x, axis=-1, keepdims=True) + EPS)
    return y if g is None else y * g


def _split3(x):
    hi = x.astype(BF16)
    r = x - hi.astype(F32)
    mid = r.astype(BF16)
    lo = (r - mid.astype(F32)).astype(BF16)
    return hi, mid, lo


def _tri_cumsum(tri, x):
    hi, mid, lo = _split3(x)
    return _dot(tri, hi) + _dot(tri, mid) + _dot(tri, lo)


def _norm_kernel(x_ref, g_ref, o_ref):
    o_ref[...] = _rms(x_ref[...], g_ref[...]).astype(o_ref.dtype)


def _norm(x, g, tm):
    m, d = x.shape
    return pl.pallas_call(
        _norm_kernel,
        grid=(m // tm,),
        in_specs=[pl.BlockSpec((tm, d), lambda i: (i, 0)),
                  pl.BlockSpec((1, d), lambda i: (0, 0))],
        out_specs=pl.BlockSpec((tm, d), lambda i: (i, 0)),
        out_shape=jax.ShapeDtypeStruct((m, d), BF16),
        compiler_params=_cp("parallel"),
        name="rmsnorm",
    )(x, g.reshape(1, d))


CAST_ROWS = 256


def _cast_weight_tile(w_ref, wx_ref, wb_ref, shift):
    kk, tn = wb_ref.shape
    for r in range(0, kk, CAST_ROWS):
        rows = slice(r, min(r + CAST_ROWS, kk))
        if shift == 0:
            wb_ref[rows, :] = w_ref[0, rows, :].astype(BF16)
        else:
            wide = jnp.concatenate([w_ref[0, rows, :], wx_ref[0, rows, :]], axis=1)
            wb_ref[rows, :] = pltpu.roll(wide, wide.shape[1] - shift, 1)[:, :tn].astype(BF16)


def _cast_weight_tile_t(w_ref, wx_ref, wb_ref, shift):
    tn, _ = wb_ref.shape
    for r in range(0, tn, CAST_ROWS):
        lo, hi = r + shift, min(r + CAST_ROWS, tn) + shift
        if hi <= tn:
            src = w_ref[0, lo:hi, :]
        else:
            src = jnp.concatenate([w_ref[0, lo:tn, :], wx_ref[0, 0:hi - tn, :]], axis=0)
        wb_ref[r:min(r + CAST_ROWS, tn), :] = src.astype(BF16)


def _group_steps(ms, tms):
    nis = [m // t for m, t in zip(ms, tms)]
    starts = [0]
    for n in nis:
        starts.append(starts[-1] + n)

    def idx(g):
        return lambda i: jnp.clip(i - starts[g], 0, nis[g] - 1)

    return starts, [idx(g) for g in range(len(ms))]


def _on_group_turn(i, starts, g, fn):
    if len(starts) == 2:
        fn()
    else:
        pl.when(jnp.logical_and(i >= starts[g], i < starts[g + 1]))(fn)


def _mm_ws_kernel(*refs, shift, transposed, starts):
    ng = len(starts) - 1
    a_refs, w_ref, rest = refs[:ng], refs[ng], refs[ng + 1:]
    wx_ref = None
    if shift:
        wx_ref, rest = rest[0], rest[1:]
    o_refs, wb_ref = rest[:ng], rest[ng]
    i = pl.program_id(1)

    @pl.when(i == 0)
    def _():
        (_cast_weight_tile_t if transposed else _cast_weight_tile)(w_ref, wx_ref, wb_ref, shift)

    mm = _dot_nt if transposed else _dot
    for g in range(ng):
        def run(a_ref=a_refs[g], o_ref=o_refs[g]):
            o_ref[...] = mm(a_ref[...], wb_ref[...]).astype(o_ref.dtype)
        _on_group_turn(i, starts, g, run)


def _mm_ws(a, w, layer, blk0, n, shift, out_dtype, tm, tn, name, transposed=False):
    single = not isinstance(a, (tuple, list))
    a_s, tms = ((a,), (tm,)) if single else (tuple(a), tuple(tm))
    k = a_s[0].shape[1]
    nj = n // tn
    per = tn // LANES
    starts, idx = _group_steps([x.shape[0] for x in a_s], tms)
    if transposed:
        assert shift % SUBLANES == 0
        w_spec = pl.BlockSpec((1, tn, k), lambda j, i: (layer, blk0 + j, 0))
        wx_spec = pl.BlockSpec((1, LANES, k), lambda j, i: (layer, (blk0 + j + 1) * per, 0))
    else:
        w_spec = pl.BlockSpec((1, k, tn), lambda j, i: (layer, 0, blk0 + j))
        wx_spec = pl.BlockSpec((1, k, LANES), lambda j, i: (layer, 0, (blk0 + j + 1) * per))
    in_specs = [pl.BlockSpec((t_, k), functools.partial(lambda j, i, f: (f(i), 0), f=f))
                for t_, f in zip(tms, idx)] + [w_spec]
    args = list(a_s) + [w]
    if shift:
        in_specs.append(wx_spec)
        args.append(w)
    outs = pl.pallas_call(
        functools.partial(_mm_ws_kernel, shift=shift, transposed=transposed, starts=tuple(starts)),
        grid=(nj, starts[-1]),
        in_specs=in_specs,
        out_specs=[pl.BlockSpec((t_, tn), functools.partial(lambda j, i, f: (f(i), j), f=f))
                   for t_, f in zip(tms, idx)],
        out_shape=[jax.ShapeDtypeStruct((x.shape[0], n), out_dtype) for x in a_s],
        scratch_shapes=[pltpu.VMEM((tn, k) if transposed else (k, tn), BF16)],
        compiler_params=_cp("parallel", "arbitrary"),
        name=name,
    )(*args)
    return outs[0] if single else tuple(outs)


def _small_proj_kernel(a_ref, w0_ref, w1_ref, o_ref):
    a = a_ref[...]
    p0 = _dot_nt(a, w0_ref[0].astype(BF16))
    p1 = _dot_nt(a, w1_ref[0].astype(BF16))
    lane = lax.broadcasted_iota(jnp.int32, p0.shape, 1)
    o_ref[...] = jnp.where(lane < SMALL_A0, p0, p1)


def _small_proj(a, w_t, layer, blk_lr, blk_ab, tm):
    m, k = a.shape
    return pl.pallas_call(
        _small_proj_kernel,
        grid=(m // tm,),
        in_specs=[pl.BlockSpec((tm, k), lambda i: (i, 0)),
                  pl.BlockSpec((1, LANES, k), lambda i: (layer, blk_lr, 0)),
                  pl.BlockSpec((1, LANES, k), lambda i: (layer, blk_ab, 0))],
        out_specs=pl.BlockSpec((tm, LANES), lambda i: (i, 0)),
        out_shape=jax.ShapeDtypeStruct((m, LANES), F32),
        compiler_params=_cp("parallel"),
        name="in_proj_small",
    )(a, w_t, w_t)


def _shift_rows(x, prev8, s):
    xs = pltpu.roll(x, s, 0)
    ps = pltpu.roll(prev8, s, 0)
    r8 = lax.broadcasted_iota(jnp.int32, prev8.shape, 0)
    top = jnp.where(r8 < s, ps, xs[:SUBLANES])
    if x.shape[0] == SUBLANES:
        return top
    return jnp.concatenate([top, xs[SUBLANES:]], axis=0)


def _causal_conv(x, prev8, taps):
    width = len(taps)
    y = None
    for i, w in enumerate(taps):
        s = width - 1 - i
        xs = x if s == 0 else _shift_rows(x, prev8, s)
        y = xs * w if y is None else y + xs * w
    return y


def _conv_proj_kernel(a_ref, w_ref, *rest, shift, tiles_per_seq, seq_rows, head_dim, norm_scale):
    if shift:
        wx_ref, cw_ref, buf_ref, o_ref, tail_ref, wb_ref, carry_ref = rest
    else:
        wx_ref = None
        cw_ref, buf_ref, o_ref, tail_ref, wb_ref, carry_ref = rest
    i = pl.program_id(1)
    tm, tn = o_ref.shape
    sub = min(tm, CONV_SUB)
    seg = sub if seq_rows is None else seq_rows

    @pl.when(i == 0)
    def _():
        _cast_weight_tile_t(w_ref, wx_ref, wb_ref, shift)

    if seq_rows is None:
        @pl.when(i % tiles_per_seq == 0)
        def _():
            carry_ref[...] = buf_ref[0]

        prev = carry_ref[...]

    wb = wb_ref[...]
    taps = [cw_ref[n:n + 1, :] for n in range(cw_ref.shape[0])]
    for r in range(tm // sub):
        p_all = _dot_nt(a_ref[r * sub:(r + 1) * sub, :], wb)
        for s in range(sub // seg):
            lo = r * sub + s * seg
            x = p_all[s * seg:(s + 1) * seg]
            if seq_rows is not None:
                prev = buf_ref[lo // seg]
            for h in range(tn // head_dim):
                hs = slice(h * head_dim, (h + 1) * head_dim)
                yh = _silu(_causal_conv(x[:, hs], prev[:, hs], [w[:, hs] for w in taps]))
                if norm_scale is not None:
                    yh = yh * lax.rsqrt(jnp.sum(yh * yh, axis=-1, keepdims=True) + EPS)
                    if norm_scale != 1.0:
                        yh = yh * norm_scale
                o_ref[lo:lo + seg, hs] = yh
            prev = x[seg - SUBLANES:, :]
            if seq_rows is not None:
                tail_ref[lo // seg] = prev
    if seq_rows is None:
        carry_ref[...] = prev
        tail_ref[0] = prev


def _conv_proj(a, w_t, layer, off, n, conv_w, buf8, coff, t, tm, tn, head_dim, norm_scale, name):
    m, k = a.shape
    nj, ni = n // tn, m // tm
    blk0 = off // tn
    shift = off - blk0 * tn
    assert shift < LANES and shift % SUBLANES == 0 and n % tn == 0 and coff % tn == 0
    cblk = coff // tn
    per = tn // LANES
    if t >= tm:
        tps, seq_rows, nb = t // tm, None, 1
        buf_ix = lambda j, i: (i // tps, 0, cblk + j)
    else:
        tps, seq_rows, nb = 1, t, tm // t
        buf_ix = lambda j, i: (i, 0, cblk + j)
    in_specs = [pl.BlockSpec((tm, k), lambda j, i: (i, 0)),
                pl.BlockSpec((1, tn, k), lambda j, i: (layer, blk0 + j, 0))]
    args = [a, w_t]
    if shift:
        in_specs.append(pl.BlockSpec((1, LANES, k), lambda j, i: (layer, (blk0 + j + 1) * per, 0)))
        args.append(w_t)
    in_specs += [pl.BlockSpec((conv_w.shape[0], tn), lambda j, i: (0, cblk + j)),
                 pl.BlockSpec((nb, SUBLANES, tn), buf_ix)]
    out, tails = pl.pallas_call(
        functools.partial(_conv_proj_kernel, shift=shift, tiles_per_seq=tps, seq_rows=seq_rows,
                          head_dim=head_dim, norm_scale=norm_scale),
        grid=(nj, ni),
        in_specs=in_specs,
        out_specs=[pl.BlockSpec((tm, tn), lambda j, i: (i, j)),
                   pl.BlockSpec((nb, SUBLANES, tn), lambda j, i: (i, 0, j))],
        out_shape=[jax.ShapeDtypeStruct((m, n), F32),
                   jax.ShapeDtypeStruct((ni * nb, SUBLANES, n), F32)],
        scratch_shapes=[pltpu.VMEM((tn, k), BF16), pltpu.VMEM((SUBLANES, tn), F32)],
        compiler_params=_cp("parallel", "arbitrary"),
        name=name,
    )(*args, conv_w, buf8)
    return out, tails[tps - 1::tps]


def _proj_res_kernel(*refs, sub, final, starts):
    ng = len(starts) - 1
    nout = 1 if final else 2
    a_refs, x_refs = refs[0:2 * ng:2], refs[1:2 * ng:2]
    w_ref, g_ref = refs[2 * ng], refs[2 * ng + 1]
    outs = refs[2 * ng + 2:]
    i = pl.program_id(0)
    for gi in range(ng):
        def run(a_ref=a_refs[gi], x_ref=x_refs[gi], o=outs[gi * nout:(gi + 1) * nout]):
            tm = a_ref.shape[0]
            sb = min(sub, tm)
            w = w_ref[...]
            g = g_ref[...]
            for r in range(tm // sb):
                rows = slice(r * sb, (r + 1) * sb)
                xn = x_ref[rows, :] + _dot(a_ref[rows, :], w)
                if final:
                    o[0][rows, :] = _rms(xn, g)
                else:
                    o[0][rows, :] = xn
                    o[1][rows, :] = _rms(xn, g).astype(o[1].dtype)
        _on_group_turn(i, starts, gi, run)


def _proj_res(a_s, w, x_s, g, tms, sub, final, name):
    kk, d = w.shape
    starts, idx = _group_steps([a.shape[0] for a in a_s], tms)
    in_specs, args, out_specs, out_shape = [], [], [], []
    for a, x, tm, f in zip(a_s, x_s, tms, idx):
        row = functools.partial(lambda i, f: (f(i), 0), f=f)
        in_specs += [pl.BlockSpec((tm, kk), row), pl.BlockSpec((tm, d), row)]
        args += [a, x]
        out_specs.append(pl.BlockSpec((tm, d), row))
        out_shape.append(jax.ShapeDtypeStruct((a.shape[0], d), F32))
        if not final:
            out_specs.append(pl.BlockSpec((tm, d), row))
            out_shape.append(jax.ShapeDtypeStruct((a.shape[0], d), BF16))
    outs = pl.pallas_call(
        functools.partial(_proj_res_kernel, sub=sub, final=final, starts=tuple(starts)),
        grid=(starts[-1],),
        in_specs=in_specs + [pl.BlockSpec((kk, d), lambda i: (0, 0), pipeline_mode=pl.Buffered(1)),
                             pl.BlockSpec((1, d), lambda i: (0, 0))],
        out_specs=out_specs,
        out_shape=out_shape,
        compiler_params=_cp("arbitrary"),
        name=name,
    )(*args, w, g.reshape(1, d))
    nout = 1 if final else 2
    return [tuple(outs[gi * nout:(gi + 1) * nout]) for gi in range(len(a_s))]


def _merge_kernel(*refs, sub, starts):
    ng = len(starts) - 1
    per = N_BRANCH + 1
    w_ref, b_ref = refs[per * ng], refs[per * ng + 1]
    o_refs = refs[per * ng + 2:]
    i = pl.program_id(0)
    for gi in range(ng):
        def run(y_refs=refs[gi * per:gi * per + N_BRANCH], g_ref=refs[gi * per + N_BRANCH], o_ref=o_refs[gi]):
            tm, d = o_ref.shape
            sb = min(sub, tm)
            for r in range(tm // sb):
                rows = slice(r * sb, (r + 1) * sb)
                acc = None
                for n, y_ref in enumerate(y_refs):
                    p = _dot(y_ref[rows, :], w_ref[n])
                    g = g_ref[rows, n * d:(n + 1) * d].astype(F32)
                    t = _sigmoid(g + b_ref[n:n + 1, :]) * p
                    acc = t if acc is None else acc + t
                o_ref[rows, :] = acc.astype(o_ref.dtype)
        _on_group_turn(i, starts, gi, run)


def _merge(ys_s, w_branch, g_br_s, b_gate, tms):
    _, bw, d = w_branch.shape
    starts, idx = _group_steps([g.shape[0] for g in g_br_s], tms)
    in_specs, args, out_specs, out_shape = [], [], [], []
    for ys, g_br, tm, f in zip(ys_s, g_br_s, tms, idx):
        row = functools.partial(lambda i, f: (f(i), 0), f=f)
        in_specs += [pl.BlockSpec((tm, bw), row)] * N_BRANCH + [pl.BlockSpec((tm, N_BRANCH * d), row)]
        args += list(ys) + [g_br]
        out_specs.append(pl.BlockSpec((tm, d), row))
        out_shape.append(jax.ShapeDtypeStruct((g_br.shape[0], d), BF16))
    return pl.pallas_call(
        functools.partial(_merge_kernel, sub=ROW_SUB, starts=tuple(starts)),
        grid=(starts[-1],),
        in_specs=in_specs + [pl.BlockSpec((N_BRANCH, bw, d), lambda i: (0, 0, 0), pipeline_mode=pl.Buffered(1)),
                             pl.BlockSpec((N_BRANCH, d), lambda i: (0, 0))],
        out_specs=out_specs,
        out_shape=out_shape,
        compiler_params=_cp("arbitrary"),
        name="branch_merge",
    )(*args, w_branch, b_gate)


def _xattn_kernel(q_ref, k_ref, v_ref, o_ref, *scratch, scale, dh):
    heads = range(q_ref.shape[1] // dh)
    hs = lambda h: slice(h * dh, (h + 1) * dh)
    if scratch:
        kb_ref, vb_ref = scratch

        @pl.when(pl.program_id(1) == 0)
        def _():
            for h in heads:
                kb_ref[:, hs(h)] = k_ref[0, 0, :, h, :].astype(BF16)
                vb_ref[:, hs(h)] = v_ref[0, 0, :, h, :].astype(BF16)
    else:
        kb_ref, vb_ref = k_ref.at[0], v_ref.at[0]

    s = [_dot_nt(q_ref[:, hs(h)], kb_ref[:, hs(h)]) * scale for h in heads]
    e = [jnp.exp(x - jnp.max(x, axis=-1, keepdims=True)) for x in s]
    p = [(x / jnp.sum(x, axis=-1, keepdims=True)).astype(BF16) for x in e]
    for h in heads:
        o_ref[:, hs(h)] = _dot(p[h], vb_ref[:, hs(h)]).astype(o_ref.dtype)


def _xattn(q, mem_k, mem_v, layer, t, tq):
    m, d = q.shape
    b = m // t
    dh = d // X_H
    nq = t // tq
    if mem_k.ndim == 5:
        nmem = mem_k.shape[2]
        kv_spec = pl.BlockSpec((1, 1, nmem, X_H, dh), lambda bi, i: (layer, bi, 0, 0, 0))
        scratch = [pltpu.VMEM((nmem, d), BF16), pltpu.VMEM((nmem, d), BF16)]
    else:
        nmem = mem_k.shape[1] // b
        kv_spec = pl.BlockSpec((1, nmem, d), lambda bi, i: (layer, bi, 0))
        scratch = []
    qo_spec = pl.BlockSpec((tq, d), lambda bi, i: (bi * nq + i, 0))
    return pl.pallas_call(
        functools.partial(_xattn_kernel, scale=dh ** -0.5, dh=dh),
        grid=(b, nq),
        in_specs=[qo_spec, kv_spec, kv_spec],
        out_specs=qo_spec,
        out_shape=jax.ShapeDtypeStruct((m, d), BF16),
        scratch_shapes=scratch,
        compiler_params=_cp("parallel", "arbitrary"),
        name="xattn",
    )(q, mem_k, mem_v)


def _mem_proj_kernel(x_ref, g_ref, w_ref, o_ref, ob_ref, wb_ref):
    @pl.when(pl.program_id(1) == 0)
    def _():
        _cast_weight_tile(w_ref, None, wb_ref, 0)

    _, nb, nmem, nh, dh = o_ref.shape
    res = _dot(_rms(x_ref[...], g_ref[0]).astype(BF16), wb_ref[...])
    ob_ref[0] = res.astype(BF16)
    for bb in range(nb):
        for h in range(nh):
            o_ref[0, bb, :, h, :] = res[bb * nmem:(bb + 1) * nmem, h * dh:(h + 1) * dh]


def _mem_proj(mem2, ln_mem, w_xkv, half, bsz, nmem, name):
    depth, d, _ = w_xkv.shape
    dh = d // X_H
    nb = 1
    return pl.pallas_call(
        _mem_proj_kernel,
        grid=(depth, bsz // nb),
        in_specs=[pl.BlockSpec((nb * nmem, d), lambda l, i: (i, 0)),
                  pl.BlockSpec((1, 1, d), lambda l, i: (l, 0, 0)),
                  pl.BlockSpec((1, d, d), lambda l, i: (l, 0, half), pipeline_mode=pl.Buffered(1))],
        out_specs=[pl.BlockSpec((1, nb, nmem, X_H, dh), lambda l, i: (l, i, 0, 0, 0)),
                   pl.BlockSpec((1, nb * nmem, d), lambda l, i: (l, i, 0))],
        out_shape=[jax.ShapeDtypeStruct((depth, bsz, nmem, X_H, dh), F32),
                   jax.ShapeDtypeStruct((depth, bsz * nmem, d), BF16)],
        scratch_shapes=[pltpu.VMEM((d, d), BF16)],
        compiler_params=_cp("arbitrary", "arbitrary"),
        name=name,
    )(mem2, ln_mem[:, None, :], w_xkv)


def _gelu_tanh(x):
    c = 0.7978845608028654
    return x * (0.5 * (1.0 + jnp.tanh(c * (x + 0.044715 * (x * x * x)))))


def _ffn_in_kernel(h_ref, wg_ref, wu_ref, cw_ref, cb_ref, buf_ref, act_ref, nbuf_ref,
                   wgb_ref, wub_ref, carry_ref, *, tiles_per_seq, seq_rows):
    i = pl.program_id(1)
    tm = h_ref.shape[0]
    sub = min(tm, FFN_SUB)
    seg = sub if seq_rows is None else seq_rows

    @pl.when(i == 0)
    def _():
        _cast_weight_tile(wg_ref, None, wgb_ref, 0)
        _cast_weight_tile(wu_ref, None, wub_ref, 0)

    if seq_rows is None:
        @pl.when(i % tiles_per_seq == 0)
        def _():
            carry_ref[...] = buf_ref[0]

        prev = carry_ref[...]

    wg, wu = wgb_ref[...], wub_ref[...]
    taps, cb = [cw_ref[n:n + 1, :] for n in range(cw_ref.shape[0])], cb_ref[...]
    for r in range(tm // sub):
        h = h_ref[r * sub:(r + 1) * sub, :]
        gate_all = _dot(h, wg)
        up_all = _dot(h, wu)
        for s in range(sub // seg):
            lo = r * sub + s * seg
            gate = gate_all[s * seg:(s + 1) * seg]
            if seq_rows is not None:
                prev = buf_ref[lo // seg]
            y = _causal_conv(gate, prev, taps)
            act = _gelu_tanh(y + cb) * up_all[s * seg:(s + 1) * seg]
            act_ref[lo:lo + seg, :] = act.astype(act_ref.dtype)
            prev = gate[seg - SUBLANES:, :]
            if seq_rows is not None:
                nbuf_ref[lo // seg] = prev
    if seq_rows is None:
        carry_ref[...] = prev
        nbuf_ref[0] = prev


def _ffn_in(h, w_ffn_in, layer, conv_w, conv_b, buf8, t, tm, tn):
    m, d = h.shape
    dff = w_ffn_in.shape[2] // 2
    ni, nj = m // tm, dff // tn
    if t >= tm:
        tps, seq_rows, nb = t // tm, None, 1
        buf_ix = lambda j, i: (i // tps, 0, j)
    else:
        tps, seq_rows, nb = 1, t, tm // t
        buf_ix = lambda j, i: (i, 0, j)
    act, tails = pl.pallas_call(
        functools.partial(_ffn_in_kernel, tiles_per_seq=tps, seq_rows=seq_rows),
        grid=(nj, ni),
        in_specs=[pl.BlockSpec((tm, d), lambda j, i: (i, 0)),
                  pl.BlockSpec((1, d, tn), lambda j, i: (layer, 0, j)),
                  pl.BlockSpec((1, d, tn), lambda j, i: (layer, 0, nj + j)),
                  pl.BlockSpec((FFN_CONV, tn), lambda j, i: (0, j)),
                  pl.BlockSpec((1, tn), lambda j, i: (0, j)),
                  pl.BlockSpec((nb, SUBLANES, tn), buf_ix)],
        out_specs=[pl.BlockSpec((tm, tn), lambda j, i: (i, j)),
                   pl.BlockSpec((nb, SUBLANES, tn), lambda j, i: (i, 0, j))],
        out_shape=[jax.ShapeDtypeStruct((m, dff), BF16),
                   jax.ShapeDtypeStruct((ni * nb, SUBLANES, dff), F32)],
        scratch_shapes=[pltpu.VMEM((d, tn), BF16), pltpu.VMEM((d, tn), BF16),
                        pltpu.VMEM((SUBLANES, tn), F32)],
        compiler_params=_cp("parallel", "arbitrary"),
        name="ffn_in",
    )(h, w_ffn_in, w_ffn_in, conv_w, conv_b.reshape(1, dff), buf8)
    return act, tails[tps - 1::tps]


def _ret_kernel(ld_ref, p_ref, cos_ref, sin_ref, s0_ref, y_ref, s_ref, *, c):
    t = pl.program_id(1)
    tb = p_ref.shape[0]
    half = RET_DK // 2
    hw = RET_H * RET_DK
    heads = range(RET_H)

    @pl.when(t == 0)
    def _():
        s_ref[...] = s0_ref[...]

    ri = lax.broadcasted_iota(jnp.int32, (c, c), 0)
    ci = lax.broadcasted_iota(jnp.int32, (c, c), 1)
    diff = (ri - ci).astype(F32)
    causal = ri >= ci
    idx = lax.broadcasted_iota(jnp.int32, (c, RET_DK), 0).astype(F32)
    lds = [ld_ref[h] for h in heads]
    dmask = [jnp.where(causal, jnp.exp(ld * jnp.where(causal, diff, 0.0)), 0.0) for ld in lds]
    q_dec = [jnp.exp(ld * (idx + 1.0)) for ld in lds]
    k_dec = [jnp.exp(ld * (c - 1.0 - idx)) for ld in lds]
    s_dec = [jnp.exp(jnp.full((1, RET_DV), ld * c, F32)) for ld in lds]

    def rot(x, cos, sin):
        x1, x2 = x[:, :half], x[:, half:]
        return jnp.concatenate([x1 * cos - x2 * sin, x1 * sin + x2 * cos], axis=-1)

    nchunk = tb // c
    nc = RET_CHUNKS_PER_STEP if nchunk % RET_CHUNKS_PER_STEP == 0 else 1

    def body(jj, carry):
        ch = []
        for ci in range(nc):
            rows = pl.ds(pl.multiple_of((jj * nc + ci) * c, c), c)
            cos, sin = cos_ref[rows, :], sin_ref[rows, :]
            k = [rot(p_ref[rows, hw + h * RET_DK:hw + (h + 1) * RET_DK], cos, sin) * (RET_DK ** -0.5)
                 for h in heads]
            ch.append(dict(
                rows=rows, k=k,
                qb=[rot(p_ref[rows, h * RET_DK:(h + 1) * RET_DK], cos, sin).astype(BF16) for h in heads],
                vb=[p_ref[rows, 2 * hw + h * RET_DV:2 * hw + (h + 1) * RET_DV].astype(BF16) for h in heads]))
        probs = [(d, h) for d in ch for h in heads]
        att = [(_dot_nt(d['qb'][h], d['k'][h].astype(BF16)) * dmask[h]).astype(BF16) for d, h in probs]
        av = [_dot(a, d['vb'][h]) for a, (d, h) in zip(att, probs)]
        kv = [_dot_tn((d['k'][h] * k_dec[h]).astype(BF16), d['vb'][h]) for d, h in probs]

        for ci, d in enumerate(ch):
            s = [s_ref[0, h] for h in heads]
            o = [av[ci * RET_H + h] + _dot(d['qb'][h], s[h].astype(BF16)) * q_dec[h] for h in heads]
            for h in heads:
                s_ref[0, h] = s_dec[h] * s[h] + kv[ci * RET_H + h]
            for h in heads:
                g = p_ref[d['rows'], 3 * hw + h * RET_DV:3 * hw + (h + 1) * RET_DV]
                y_ref[d['rows'], h * RET_DV:(h + 1) * RET_DV] = (_rms(o[h]) * _silu(g)).astype(y_ref.dtype)
        return carry

    lax.fori_loop(0, nchunk // nc, body, 0)


def _retention(p_ret, cos, sin, ld, s0, t, tb):
    m, width = p_ret.shape
    b = m // t
    c = min(CHUNK, t)
    nt = t // tb
    st_spec = pl.BlockSpec((1, RET_H, RET_DK, RET_DV), lambda bi, ti: (bi, 0, 0, 0))
    rope_spec = pl.BlockSpec((tb, RET_DK // 2), lambda bi, ti: (ti, 0))
    return pl.pallas_call(
        functools.partial(_ret_kernel, c=c),
        grid=(b, nt),
        in_specs=[pl.BlockSpec(memory_space=pltpu.SMEM),
                  pl.BlockSpec((tb, width), lambda bi, ti: (bi * nt + ti, 0)),
                  rope_spec, rope_spec, st_spec],
        out_specs=[pl.BlockSpec((tb, RET_H * RET_DV), lambda bi, ti: (bi * nt + ti, 0)), st_spec],
        out_shape=[jax.ShapeDtypeStruct((m, RET_H * RET_DV), BF16),
                   jax.ShapeDtypeStruct((b, RET_H, RET_DK, RET_DV), F32)],
        compiler_params=_cp("parallel", "arbitrary"),
        name="retention",
    )(ld, p_ret, cos, sin, s0)


def _gla_kernel(p_ref, lr_ref, wgk_ref, bgk_ref, nrm_ref, s0_ref, y_ref, s_ref, st_ref, *, c, nt):
    t = pl.program_id(1)
    tb = p_ref.shape[0]
    hk = GLA_H * GLA_DK
    hv = GLA_H * GLA_DV
    heads = range(GLA_H)

    @pl.when(t == 0)
    def _():
        for h in heads:
            st_ref[h] = s0_ref[0, h].T

    ri = lax.broadcasted_iota(jnp.int32, (c, c), 0)
    ci = lax.broadcasted_iota(jnp.int32, (c, c), 1)
    causal = ri >= ci
    tri = jnp.where(causal, 1.0, 0.0).astype(BF16)
    wgk = wgk_ref[...]
    bgk = bgk_ref[...]
    nrm = nrm_ref[...]

    nchunk = tb // c
    nc = GLA_CHUNKS_PER_STEP if nchunk % GLA_CHUNKS_PER_STEP == 0 else 1
    hs = lambda x, h: x[:, h * GLA_DK:(h + 1) * GLA_DK]

    def body(jj, carry):
        rows = [pl.ds(pl.multiple_of((jj * nc + ci) * c, c), c) for ci in range(nc)]
        z = [_dot(lr_ref[r, :].astype(BF16), wgk) + bgk for r in rows]
        bc = [_tri_cumsum(tri, -_softplus(-zz) / GLA_NORMALIZER) for zz in z]
        ch = []
        for r, b_ in zip(rows, bc):
            q = p_ref[r, 0:hk] * (GLA_DK ** -0.5)
            k = p_ref[r, hk:2 * hk]
            b_last = b_[c - 1:c, :]
            ch.append(dict(
                rows=r, qe=(q * jnp.exp(b_)).astype(BF16), ke=(k * jnp.exp(-b_)).astype(BF16),
                kd=(k * jnp.exp(b_last - b_)).astype(BF16), eb_last=jnp.exp(b_last),
                vb=[p_ref[r, 2 * hk + h * GLA_DV:2 * hk + (h + 1) * GLA_DV].astype(BF16) for h in heads]))
        probs = [(d, h) for d in ch for h in heads]
        att = [jnp.where(causal, _dot_nt(hs(d['qe'], h), hs(d['ke'], h)), 0.0).astype(BF16) for d, h in probs]
        av = [_dot(a, d['vb'][h]) for a, (d, h) in zip(att, probs)]

        for ci, d in enumerate(ch):
            st = [st_ref[h] for h in heads]
            o = [av[ci * GLA_H + h] + _dot_nt(hs(d['qe'], h), st[h].astype(BF16)) for h in heads]
            for h in heads:
                st_ref[h] = hs(d['eb_last'], h) * st[h] + _dot_tn(d['vb'][h], hs(d['kd'], h))
            for h in heads:
                g = p_ref[d['rows'], 2 * hk + hv + h * GLA_DV:2 * hk + hv + (h + 1) * GLA_DV]
                y_ref[d['rows'], h * GLA_DV:(h + 1) * GLA_DV] = (_rms(o[h], nrm) * _silu(g)).astype(y_ref.dtype)
        return carry

    lax.fori_loop(0, nchunk // nc, body, 0)

    @pl.when(t == nt - 1)
    def _():
        for h in heads:
            s_ref[0, h] = st_ref[h].T


def _gla(p_gla, p_small, wgk, bgk, nrm, s0, t, tb):
    m, width = p_gla.shape
    b = m // t
    c = min(CHUNK, t)
    nt = t // tb
    const = lambda bi, ti: (0, 0)
    st_spec = pl.BlockSpec((1, GLA_H, GLA_DK, GLA_DV), lambda bi, ti: (bi, 0, 0, 0))
    return pl.pallas_call(
        functools.partial(_gla_kernel, c=c, nt=nt),
        grid=(b, nt),
        in_specs=[pl.BlockSpec((tb, width), lambda bi, ti: (bi * nt + ti, 0)),
                  pl.BlockSpec((tb, LANES), lambda bi, ti: (bi * nt + ti, 0)),
                  pl.BlockSpec((LANES, GLA_H * GLA_DK), const),
                  pl.BlockSpec((1, GLA_H * GLA_DK), const),
                  pl.BlockSpec((1, GLA_DV), const),
                  st_spec],
        out_specs=[pl.BlockSpec((tb, GLA_H * GLA_DV), lambda bi, ti: (bi * nt + ti, 0)), st_spec],
        out_shape=[jax.ShapeDtypeStruct((m, GLA_H * GLA_DV), BF16),
                   jax.ShapeDtypeStruct((b, GLA_H, GLA_DK, GLA_DV), F32)],
        scratch_shapes=[pltpu.VMEM((GLA_H, GLA_DV, GLA_DK), F32)],
        compiler_params=_cp("parallel", "arbitrary"),
        name="gla",
    )(p_gla, p_small, wgk, bgk, nrm, s0)


def _unit_lower_inverse_minus_eye(mats, c):
    ri = lax.broadcasted_iota(jnp.int32, (c, c), 0)
    ci = lax.broadcasted_iota(jnp.int32, (c, c), 1)

    def level_mask(k):
        same = (ri ^ ci) < 2 * k
        return same & ((ri & k) != 0) & ((ci & k) == 0)

    m1 = level_mask(1)
    ns = [-jnp.where(m1, a, 0.0) for a in mats]
    k = 2
    while k < c:
        mk = level_mask(k)
        ls = [jnp.where(mk, a, 0.0) for a in mats]
        nbs = [n.astype(BF16) for n in ns]
        ys = [l + _dot(nb, l.astype(BF16)) for l, nb in zip(ls, nbs)]
        xs = [y + _dot(y.astype(BF16), nb) for y, nb in zip(ys, nbs)]
        ns = [n - x for n, x in zip(ns, xs)]
        k *= 2
    return ns


def _blockdiag2(xp, left):
    return jnp.concatenate([jnp.where(left, xp, 0.0), jnp.where(left, 0.0, xp)], axis=0)


def _unit_lower_inverse_minus_eye_packed(mats, c):
    ri = lax.broadcasted_iota(jnp.int32, (c, 2 * c), 0)
    cl = lax.broadcasted_iota(jnp.int32, (c, 2 * c), 1)
    left = cl < c
    cm = cl & (c - 1)

    def level_mask(k):
        return ((ri ^ cm) < 2 * k) & ((ri & k) != 0) & ((cm & k) == 0)

    m1 = level_mask(1)
    ns = [-jnp.where(m1, a, 0.0) for a in mats]
    k = 2
    while k < c:
        mk = level_mask(k)
        ls = [jnp.where(mk, a, 0.0) for a in mats]
        ys = [l + _dot(n.astype(BF16), _blockdiag2(l, left).astype(BF16)) for l, n in zip(ls, ns)]
        xs = [y + _dot(y.astype(BF16), _blockdiag2(n, left).astype(BF16)) for y, n in zip(ys, ns)]
        ns = [n - x for n, x in zip(ns, xs)]
        k *= 2
    return ns


def _gdn_kernel(q_ref, k_ref, v_ref, z_ref, sm_ref, alog_ref, dtb_ref, nrm_ref, s0_ref,
                y_ref, s_ref, *, c):
    t = pl.program_id(1)
    tb = q_ref.shape[0]

    @pl.when(t == 0)
    def _():
        s_ref[...] = s0_ref[...]

    tri = jnp.where(lax.broadcasted_iota(jnp.int32, (c, c), 0) >= lax.broadcasted_iota(jnp.int32, (c, c), 1),
                    1.0, 0.0).astype(BF16)
    ri = lax.broadcasted_iota(jnp.int32, (c, 2 * c), 0)
    cl = lax.broadcasted_iota(jnp.int32, (c, 2 * c), 1)
    left = cl < c
    cm = cl & (c - 1)
    incl = ri >= cm
    strict = ri > cm
    lane = lax.broadcasted_iota(jnp.int32, (1, LANES), 1)
    a_lanes = jnp.logical_and(lane >= SMALL_A0, lane < SMALL_A0 + GDN_H)
    sel2 = jnp.where(lax.broadcasted_iota(jnp.int32, (SUBLANES, LANES), 1)
                     == 2 * lax.broadcasted_iota(jnp.int32, (SUBLANES, LANES), 0) + SMALL_A0,
                     1.0, 0.0).astype(BF16)
    neg_a = -jnp.exp(alog_ref[...])
    dtb = dtb_ref[...]
    nrm = nrm_ref[...]

    def col(x, lane_idx):
        return x[:, lane_idx:lane_idx + 1]

    assert GDN_DK == LANES and GDN_DV == LANES and GDN_H % 2 == 0
    heads = range(GDN_H)
    npair = GDN_H // 2
    pairs = range(npair)
    nchunk = tb // c
    nc = GDN_CHUNKS_PER_STEP if nchunk % GDN_CHUNKS_PER_STEP == 0 else 1

    def lanes2(x, y):
        return jnp.concatenate([x, y], axis=1)

    def body(jj, carry):
        ch = []
        for ci in range(nc):
            rows = pl.ds(pl.multiple_of((jj * nc + ci) * c, c), c)
            sm = sm_ref[rows, :]
            ch.append(dict(
                rows=rows,
                g_all=jnp.where(a_lanes, neg_a * _softplus(sm + dtb), 0.0),
                beta_all=_sigmoid(sm),
                q=[q_ref[rows, h * GDN_DK:(h + 1) * GDN_DK] for h in heads],
                k=[k_ref[rows, h * GDN_DK:(h + 1) * GDN_DK] for h in heads],
                v=[v_ref[rows, h * GDN_DV:(h + 1) * GDN_DV] for h in heads]))
        for d in ch:
            d['gam'] = _tri_cumsum(tri, d['g_all'])
        for d in ch:
            g2 = jnp.concatenate([d['gam'], pltpu.roll(d['gam'], LANES - 1, 1)], axis=0)
            ghi, gmid, glo = _split3(g2)
            d['gam_rows'] = _dot_nt(sel2, ghi) + _dot_nt(sel2, gmid) + _dot_nt(sel2, glo)

        def packed_cols(x, base, p):
            return jnp.where(left, col(x, base + 2 * p), col(x, base + 2 * p + 1))

        for d in ch:
            gam = d['gam']
            g_last = gam[c - 1:c, :]
            d['eg_all'] = jnp.exp(gam)
            d['egl_all'] = jnp.exp(g_last - gam)
            d['eg_last_all'] = jnp.exp(g_last)
            d['beta'] = [col(d['beta_all'], SMALL_B0 + h) for h in heads]
            d['eg'] = [col(d['eg_all'], SMALL_A0 + h) for h in heads]
            d['beta_p'] = [packed_cols(d['beta_all'], SMALL_B0, p) for p in pairs]
            d['decay_p'] = [jnp.where(incl, jnp.exp(jnp.where(
                incl, packed_cols(gam, SMALL_A0, p) - d['gam_rows'][p:p + 1, :], 0.0)), 0.0) for p in pairs]
        probs = [(d, p) for d in ch for p in pairs]
        zk = jnp.zeros((c, GDN_DK), F32)
        qk_kk = [_dot_nt(
            jnp.concatenate([lanes2(d['q'][2 * p], d['q'][2 * p + 1]),
                             lanes2(d['k'][2 * p], d['k'][2 * p + 1])], axis=0).astype(BF16),
            jnp.concatenate([lanes2(d['k'][2 * p], zk), lanes2(zk, d['k'][2 * p + 1])], axis=0).astype(BF16))
            for d, p in probs]
        att = [(r[:c] * d['decay_p'][p]).astype(BF16) for r, (d, p) in zip(qk_kk, probs)]
        a = [jnp.where(strict, d['beta_p'][p] * r[c:] * d['decay_p'][p], 0.0) for r, (d, p) in zip(qk_kk, probs)]
        n = _unit_lower_inverse_minus_eye_packed(a, c)
        zr = jnp.zeros((c, GDN_DV + GDN_DK), F32)
        sol = []
        for (d, p), nn in zip(probs, n):
            r1, r2 = (jnp.concatenate([d['beta'][h] * d['v'][h], (d['beta'][h] * d['eg'][h]) * d['k'][h]], axis=-1)
                      for h in (2 * p, 2 * p + 1))
            bd = jnp.concatenate([lanes2(r1, zr), lanes2(zr, r2)], axis=0).astype(BF16)
            sol.append(lanes2(r1, r2) + _dot(nn.astype(BF16), bd))

        zs = jnp.zeros((GDN_DK, GDN_DV), F32)
        zw = jnp.zeros((c, GDN_DV), F32)
        for ci, d in enumerate(ch):
            sl = sol[ci * npair:(ci + 1) * npair]
            at = att[ci * npair:(ci + 1) * npair]
            s = [s_ref[0, h] for h in heads]
            wq_s = [_dot(
                jnp.concatenate([lanes2(sl[p][:, LANES:2 * LANES], sl[p][:, 3 * LANES:]),
                                 lanes2(d['q'][2 * p], d['q'][2 * p + 1])], axis=0).astype(BF16),
                jnp.concatenate([lanes2(s[2 * p], zs), lanes2(zs, s[2 * p + 1])], axis=0).astype(BF16))
                for p in pairs]
            bdw = [jnp.concatenate([lanes2(sl[p][:, :LANES] - wq_s[p][:c, :LANES], zw),
                                    lanes2(zw, sl[p][:, 2 * LANES:3 * LANES] - wq_s[p][:c, LANES:])],
                                   axis=0).astype(BF16) for p in pairs]
            aw = [_dot(at[p], bdw[p]) for p in pairs]
            kw = [_dot_tn(jnp.concatenate([d['k'][h] * col(d['egl_all'], SMALL_A0 + h)
                                           for h in (2 * p, 2 * p + 1)], axis=0).astype(BF16), bdw[p])
                  for p in pairs]
            for h in heads:
                p, half = h // 2, slice((h % 2) * LANES, (h % 2 + 1) * LANES)
                s_ref[0, h] = col(d['eg_last_all'], SMALL_A0 + h) * s[h] + kw[p][:, half]
            for h in heads:
                p, half = h // 2, slice((h % 2) * LANES, (h % 2 + 1) * LANES)
                o = d['eg'][h] * wq_s[p][c:, half] + aw[p][:, half]
                z = z_ref[d['rows'], h * GDN_DV:(h + 1) * GDN_DV]
                y = _rms(o, nrm) * _silu(z)
                y_ref[d['rows'], h * GDN_DV:(h + 1) * GDN_DV] = y.astype(y_ref.dtype)
        return carry

    lax.fori_loop(0, nchunk // nc, body, 0)


def _gdn(q, k, v, z, p_small, alog_row, dtb_row, nrm, s0, t, tb):
    m = q.shape[0]
    b = m // t
    c = min(CHUNK, t)
    nt = t // tb
    st_spec = pl.BlockSpec((1, GDN_H, GDN_DK, GDN_DV), lambda bi, ti: (bi, 0, 0, 0))
    const = lambda bi, ti: (0, 0)
    row = lambda bi, ti: (bi * nt + ti, 0)
    qk_spec = pl.BlockSpec((tb, GDN_H * GDN_DK), row)
    v_spec = pl.BlockSpec((tb, GDN_H * GDN_DV), row)
    return pl.pallas_call(
        functools.partial(_gdn_kernel, c=c),
        grid=(b, nt),
        in_specs=[qk_spec, qk_spec, v_spec, v_spec,
                  pl.BlockSpec((tb, LANES), row),
                  pl.BlockSpec((1, LANES), const),
                  pl.BlockSpec((1, LANES), const),
                  pl.BlockSpec((1, GDN_DV), const),
                  st_spec],
        out_specs=[v_spec, st_spec],
        out_shape=[jax.ShapeDtypeStruct((m, GDN_H * GDN_DV), BF16),
                   jax.ShapeDtypeStruct((b, GDN_H, GDN_DK, GDN_DV), F32)],
        compiler_params=_cp("parallel", "arbitrary"),
        name="gated_deltanet",
    )(q, k, v, z, p_small, alog_row, dtb_row, nrm, s0)


def _pad_rows_front(buf, rows):
    return jnp.pad(buf, ((0, 0), (0, 0), (rows - buf.shape[2], 0), (0, 0)))


def _in_proj_windows(d):
    sizes = dict(ret=4 * RET_H * RET_DK, gla=2 * GLA_H * GLA_DK + 2 * GLA_H * GLA_DV, lr=GLA_RANK,
                 gdn=2 * GDN_H * GDN_DK + 2 * GDN_H * GDN_DV, ab=2 * GDN_H, gbr=N_BRANCH * d)
    win, o = {}, 0
    for name in ('ret', 'gla', 'lr', 'gdn', 'ab', 'gbr'):
        win[name] = (o, sizes[name])
        o += sizes[name]
    assert win['lr'][0] % LANES == 0 and win['ab'][0] % LANES == SMALL_A0
    return win


def _prep_weights(w):
    lane_row = lambda v, at: jnp.pad(v, ((0, 0), (at, LANES - at - v.shape[1])))[:, None, :]
    return dict(
        w_in_t=jnp.swapaxes(w['w_in'], 1, 2), w_xq=w['w_xq'], w_ffn_in=w['w_ffn_in'],
        w_gk=jnp.pad(w['w_gla_gk'], ((0, 0), (0, LANES - GLA_RANK), (0, 0))).astype(BF16),
        b_gk=w['b_gla_gk'][:, None, :],
        gla_norm=w['gla_norm'][:, None, :], gdn_norm=w['gdn_norm'][:, None, :],
        alog_row=lane_row(w['gdn_a_log'], SMALL_A0), dtb_row=lane_row(w['gdn_dt_bias'], SMALL_A0),
        gdn_conv_w=w['gdn_conv_w'], b_gate=w['b_gate'],
        w_branch=w['w_branch'].astype(BF16), w_out=w['w_out'].astype(BF16), w_xo=w['w_xo'].astype(BF16),
        ffn_conv_w=w['ffn_conv_w'], ffn_conv_b=w['ffn_conv_b'], w_ffn_out=w['w_ffn_out'].astype(BF16),
        ln_mix=w['ln_mix'], ln_xattn=w['ln_xattn'], ln_ffn=w['ln_ffn'], ln_final=w['ln_final'],
    )


def _tile(n, cap):
    t = min(n, cap)
    while n % t:
        t -= 1
    return t


class _Group:
    def __init__(self, x, offset, mem_k, mem_v, s_ret, s_gla, s_gdn, buf_gdn, buf_ffn):
        self.b, self.t, d = x.shape
        b, t = self.b, self.t
        self.m = m = b * t
        self.x2 = x.reshape(m, d)
        self.mem_k, self.mem_v = mem_k, mem_v
        self.s_ret, self.s_gla, self.s_gdn = s_ret, s_gla, s_gdn
        self.buf_gdn8 = _pad_rows_front(buf_gdn, SUBLANES)
        self.buf_ffn8 = _pad_rows_front(buf_ffn, SUBLANES)
        self.tm = _tile(m, MM_TM)
        self.tm_res = _tile(m, RES_TM)
        self.tm_res_k = _tile(m, RES_TM_K)
        self.tm_ffo = _tile(m, FFO_TM)
        self.tb = _tile(t, SEQ_TB)
        self.tb_gdn = _tile(t, SEQ_TB_GDN)
        self.tq = _tile(t, XATTN_TQ)
        self.tm_ffn = _tile(t, FFN_TM) if t >= FFN_TM else t * _tile(b, max(1, FFN_TM // t))
        pos = offset + jnp.arange(t, dtype=F32)
        half = RET_DK // 2
        freqs = 1.0 / (ROPE_BASE ** (jnp.arange(half, dtype=F32) / half))
        ang = pos[:, None] * freqs[None, :]
        self.cos, self.sin = jnp.cos(ang), jnp.sin(ang)
        self.outs = ([], [], [], [], [])
        self.hn = self.y = None


def _trunks(groups, pw):
    d = groups[0].x2.shape[1]
    depth = groups[0].s_ret.shape[0]
    ld = jnp.log1p(-jnp.exp2(-5.0 - jnp.arange(RET_H, dtype=F32)))
    win = _in_proj_windows(d)
    tms = tuple(g.tm for g in groups)

    def shared(acts, w, l, off, n, out_dtype, name, transposed):
        blk0 = off // MM_TN
        shift = off - blk0 * MM_TN
        assert shift < LANES and n % MM_TN == 0
        return _mm_ws(tuple(acts), w, l, blk0, n, shift, out_dtype, tms, MM_TN, name, transposed=transposed)

    for g in groups:
        g.hn = _norm(g.x2, pw['ln_mix'][0], g.tm_res_k)
    for l in range(depth):
        hns = [g.hn for g in groups]
        hq, hv = GDN_H * GDN_DK, GDN_H * GDN_DV
        g_off = win['gdn'][0]
        p_ret = shared(hns, pw['w_in_t'], l, *win['ret'], F32, "in_proj_ret", True)
        p_gla = shared(hns, pw['w_in_t'], l, *win['gla'], F32, "in_proj_gla", True)
        d_z = shared(hns, pw['w_in_t'], l, g_off + 2 * hq + hv, hv, F32, "in_proj_gdn_z", True)
        g_br = shared(hns, pw['w_in_t'], l, *win['gbr'], BF16, "in_proj_gbr", True)
        ys_s = []
        for gi, g in enumerate(groups):
            t, tm, hn = g.t, g.tm, g.hn
            conv_args = (pw['gdn_conv_w'][l], g.buf_gdn8[l])
            d_q, tq8 = _conv_proj(hn, pw['w_in_t'], l, g_off, hq, *conv_args, 0, t, tm, MM_TN,
                                  GDN_DK, GDN_DK ** -0.5, "in_proj_gdn_q")
            d_k, tk8 = _conv_proj(hn, pw['w_in_t'], l, g_off + hq, hq, *conv_args, hq, t, tm, MM_TN,
                                  GDN_DK, 1.0, "in_proj_gdn_k")
            d_v, tv8 = _conv_proj(hn, pw['w_in_t'], l, g_off + 2 * hq, hv, *conv_args, 2 * hq, t, tm, MM_TN,
                                  GDN_DV, None, "in_proj_gdn_v")
            bg8 = jnp.concatenate([tq8, tk8, tv8], axis=-1)
            p_small = _small_proj(hn, pw['w_in_t'], l, win['lr'][0] // LANES, win['ab'][0] // LANES, tm)

            y_ret, sr = _retention(p_ret[gi], g.cos, g.sin, ld, g.s_ret[l], t, g.tb)
            y_gla, sg = _gla(p_gla[gi], p_small, pw['w_gk'][l], pw['b_gk'][l], pw['gla_norm'][l],
                             g.s_gla[l], t, g.tb)
            y_gdn, sd = _gdn(d_q, d_k, d_v, d_z[gi], p_small, pw['alog_row'][l], pw['dtb_row'][l],
                             pw['gdn_norm'][l], g.s_gdn[l], t, g.tb_gdn)

            ys_s.append((y_ret, y_gla, y_gdn))
            for lst, val in zip(g.outs[:4], (sr, sg, sd, bg8[:, SUBLANES - (GDN_CONV - 1):])):
                lst.append(val)

        xs = lambda: [g.x2 for g in groups]
        merged = _merge(ys_s, pw['w_branch'][l], g_br, pw['b_gate'][l], [g.tm_res_k for g in groups])
        res = _proj_res(merged, pw['w_out'][l], xs(), pw['ln_xattn'][l], [g.tm_res for g in groups], ROW_SUB,
                        False, "out_proj")
        for g, (x2, _) in zip(groups, res):
            g.x2 = x2
        qs = shared([hx for _, hx in res], pw['w_xq'], l, 0, d, BF16, "xattn_q", False)
        os_ = [_xattn(q, g.mem_k, g.mem_v, l, g.t, g.tq) for q, g in zip(qs, groups)]
        res = _proj_res(os_, pw['w_xo'][l], xs(), pw['ln_ffn'][l], [g.tm_res for g in groups], ROW_SUB,
                        False, "xattn_out")
        acts = []
        for g, (x2, hf) in zip(groups, res):
            g.x2 = x2
            act, bf8 = _ffn_in(hf, pw['w_ffn_in'], l, pw['ffn_conv_w'][l], pw['ffn_conv_b'][l],
                               g.buf_ffn8[l], g.t, g.tm_ffn, FFN_TN)
            acts.append(act)
            g.outs[4].append(bf8[:, SUBLANES - (FFN_CONV - 1):])
        last = l + 1 == depth
        res = _proj_res(acts, pw['w_ffn_out'][l], xs(), pw['ln_final'] if last else pw['ln_mix'][l + 1],
                        [g.tm_ffo for g in groups], ROW_SUB, last, "ffn_out_final" if last else "ffn_out")
        for g, r in zip(groups, res):
            if last:
                g.y, = r
            else:
                g.x2, g.hn = r
    return [(g.y.reshape(g.b, g.t, d),) + tuple(jnp.stack(lst) for lst in g.outs) for g in groups]


def kernel(x_prompt, x_sample, mem_prompt, state_ret, state_gla, state_gdn, state_gdn_conv, state_ffn_conv, cache_mem_k, cache_mem_v, ln_mix, w_in, w_gla_gk, b_gla_gk, gla_norm, gdn_conv_w, gdn_a_log, gdn_dt_bias, gdn_norm, b_gate, w_branch, w_out, ln_xattn, ln_mem, w_xq, w_xkv, w_xo, ln_ffn, w_ffn_in, ffn_conv_w, ffn_conv_b, w_ffn_out, ln_final):
    pw = _prep_weights(dict(
        w_in=w_in, w_gla_gk=w_gla_gk, b_gla_gk=b_gla_gk, gla_norm=gla_norm, gdn_conv_w=gdn_conv_w,
        gdn_a_log=gdn_a_log, gdn_dt_bias=gdn_dt_bias, gdn_norm=gdn_norm, b_gate=b_gate, w_branch=w_branch,
        w_out=w_out, w_xq=w_xq, w_xo=w_xo, w_ffn_in=w_ffn_in, ffn_conv_w=ffn_conv_w, ffn_conv_b=ffn_conv_b,
        w_ffn_out=w_ffn_out, ln_mix=ln_mix, ln_xattn=ln_xattn, ln_ffn=ln_ffn, ln_final=ln_final))
    depth = w_in.shape[0]
    bp, nmem, d = mem_prompt.shape
    dt = x_prompt.dtype

    mem2 = mem_prompt.reshape(bp * nmem, d)
    mem_k_p, mem_kb = _mem_proj(mem2, ln_mem, w_xkv, 0, bp, nmem, "mem_k")
    mem_v_p, mem_vb = _mem_proj(mem2, ln_mem, w_xkv, 1, bp, nmem, "mem_v")

    zeros = lambda *s: jnp.zeros((depth, bp) + s, dt)
    prompt = _Group(
        x_prompt, 0.0, mem_kb, mem_vb,
        zeros(RET_H, RET_DK, RET_DV), zeros(GLA_H, GLA_DK, GLA_DV), zeros(GDN_H, GDN_DK, GDN_DV),
        zeros(GDN_CONV - 1, state_gdn_conv.shape[-1]), zeros(FFN_CONV - 1, state_ffn_conv.shape[-1]))
    past_len = 4096.0
    sample = _Group(x_sample, past_len, cache_mem_k, cache_mem_v,
                    state_ret, state_gla, state_gdn, state_gdn_conv, state_ffn_conv)
    ((y_p, ret_p, gla_p, gdn_p, gdn_conv_p, ffn_conv_p),
     (y_s, ret_s, gla_s, gdn_s, gdn_conv_s, ffn_conv_s)) = _trunks([prompt, sample], pw)

    return (y_p, y_s, ret_p, gla_p, gdn_p, gdn_conv_p, ffn_conv_p, mem_k_p, mem_v_p,
            ret_s, gla_s, gdn_s, gdn_conv_s, ffn_conv_s)
```

```python
import functools

import jax
import jax.numpy as jnp
from jax import lax
from jax.experimental import pallas as pl
from jax.experimental.pallas import tpu as pltpu

F32 = jnp.float32
BF16 = jnp.bfloat16

EPS = 1e-6
CHUNK = 64
RET_H, RET_DK, RET_DV = 4, 256, 256
GLA_H, GLA_DK, GLA_DV = 4, 128, 256
GLA_RANK = 16
GLA_NORMALIZER = 16.0
GDN_H, GDN_DK, GDN_DV = 8, 128, 128
GDN_CONV = 4
N_BRANCH = 3
X_H = 4
FFN_CONV = 3
ROPE_BASE = 10000.0

LANES = 128
SUBLANES = 8
SMALL_A0 = GLA_RANK
SMALL_B0 = GLA_RANK + GDN_H
VMEM_LIMIT = 52 * 1024 * 1024
MM_TM = 1024
MM_TN = 1024
RES_TM = 512
RES_TM_K = 512
FFO_TM = 256
ROW_SUB = 256
CONV_SUB = 256
SEQ_TB = 512
SEQ_TB_GDN = 256
GDN_CHUNKS_PER_STEP = 4
GLA_CHUNKS_PER_STEP = 2
RET_CHUNKS_PER_STEP = 2
XATTN_TQ = 512
FFN_TM = 1024
FFN_TN = 512
FFN_SUB = 256


def _cp(*sem):
    return pltpu.CompilerParams(dimension_semantics=sem, vmem_limit_bytes=VMEM_LIMIT)


def _dot(a, b):
    return jnp.dot(a, b, preferred_element_type=F32)


def _dot_nt(a, b):
    return lax.dot_general(a, b, (((1,), (1,)), ((), ())), preferred_element_type=F32)


def _dot_tn(a, b):
    return lax.dot_general(a, b, (((0,), (0,)), ((), ())), preferred_element_type=F32)


def _sigmoid(x):
    return 1.0 / (1.0 + jnp.exp(-x))


def _silu(x):
    return x * _sigmoid(x)


def _softplus(x):
    return jnp.maximum(x, 0.0) + jnp.log1p(jnp.exp(-jnp.abs(x)))


def _rms(x, g=None):
    y = x * lax.rsqrt(jnp.mean(x * x, axis=-1, keepdims=True) + EPS)
    return y if g is None else y * g


def _split3(x):
    hi = x.astype(BF16)
    r = x - hi.astype(F32)
    mid = r.astype(BF16)
    lo = (r - mid.astype(F32)).astype(BF16)
    return hi, mid, lo


def _tri_cumsum(tri, x):
    hi, mid, lo = _split3(x)
    return _dot(tri, hi) + _dot(tri, mid) + _dot(tri, lo)


def _norm_kernel(x_ref, g_ref, o_ref):
    o_ref[...] = _rms(x_ref[...], g_ref[...]).astype(o_ref.dtype)


def _norm(x, g, tm):
    m, d = x.shape
    return pl.pallas_call(
        _norm_kernel,
        grid=(m // tm,),
        in_specs=[pl.BlockSpec((tm, d), lambda i: (i, 0)),
                  pl.BlockSpec((1, d), lambda i: (0, 0))],
        out_specs=pl.BlockSpec((tm, d), lambda i: (i, 0)),
        out_shape=jax.ShapeDtypeStruct((m, d), BF16),
        compiler_params=_cp("parallel"),
        name="rmsnorm",
    )(x, g.reshape(1, d))


CAST_ROWS = 256


def _cast_weight_tile(w_ref, wx_ref, wb_ref, shift):
    kk, tn = wb_ref.shape
    for r in range(0, kk, CAST_ROWS):
        rows = slice(r, min(r + CAST_ROWS, kk))
        if shift == 0:
            wb_ref[rows, :] = w_ref[0, rows, :].astype(BF16)
        else:
            wide = jnp.concatenate([w_ref[0, rows, :], wx_ref[0, rows, :]], axis=1)
            wb_ref[rows, :] = pltpu.roll(wide, wide.shape[1] - shift, 1)[:, :tn].astype(BF16)


def _cast_weight_tile_t(w_ref, wx_ref, wb_ref, shift):
    tn, _ = wb_ref.shape
    for r in range(0, tn, CAST_ROWS):
        lo, hi = r + shift, min(r + CAST_ROWS, tn) + shift
        if hi <= tn:
            src = w_ref[0, lo:hi, :]
        else:
            src = jnp.concatenate([w_ref[0, lo:tn, :], wx_ref[0, 0:hi - tn, :]], axis=0)
        wb_ref[r:min(r + CAST_ROWS, tn), :] = src.astype(BF16)


def _group_steps(ms, tms):
    nis = [m // t for m, t in zip(ms, tms)]
    starts = [0]
    for n in nis:
        starts.append(starts[-1] + n)

    def idx(g):
        return lambda i: jnp.clip(i - starts[g], 0, nis[g] - 1)

    return starts, [idx(g) for g in range(len(ms))]


def _on_group_turn(i, starts, g, fn):
    if len(starts) == 2:
        fn()
    else:
        pl.when(jnp.logical_and(i >= starts[g], i < starts[g + 1]))(fn)


def _mm_ws_kernel(*refs, shift, transposed, starts):
    ng = len(starts) - 1
    a_refs, w_ref, rest = refs[:ng], refs[ng], refs[ng + 1:]
    wx_ref = None
    if shift:
        wx_ref, rest = rest[0], rest[1:]
    o_refs, wb_ref = rest[:ng], rest[ng]
    i = pl.program_id(1)

    @pl.when(i == 0)
    def _():
        (_cast_weight_tile_t if transposed else _cast_weight_tile)(w_ref, wx_ref, wb_ref, shift)

    mm = _dot_nt if transposed else _dot
    for g in range(ng):
        def run(a_ref=a_refs[g], o_ref=o_refs[g]):
            o_ref[...] = mm(a_ref[...], wb_ref[...]).astype(o_ref.dtype)
        _on_group_turn(i, starts, g, run)


def _mm_ws(a, w, layer, blk0, n, shift, out_dtype, tm, tn, name, transposed=False):
    single = not isinstance(a, (tuple, list))
    a_s, tms = ((a,), (tm,)) if single else (tuple(a), tuple(tm))
    k = a_s[0].shape[1]
    nj = n // tn
    per = tn // LANES
    starts, idx = _group_steps([x.shape[0] for x in a_s], tms)
    if transposed:
        assert shift % SUBLANES == 0
        w_spec = pl.BlockSpec((1, tn, k), lambda j, i: (layer, blk0 + j, 0))
        wx_spec = pl.BlockSpec((1, LANES, k), lambda j, i: (layer, (blk0 + j + 1) * per, 0))
    else:
        w_spec = pl.BlockSpec((1, k, tn), lambda j, i: (layer, 0, blk0 + j))
        wx_spec = pl.BlockSpec((1, k, LANES), lambda j, i: (layer, 0, (blk0 + j + 1) * per))
    in_specs = [pl.BlockSpec((t_, k), functools.partial(lambda j, i, f: (f(i), 0), f=f))
                for t_, f in zip(tms, idx)] + [w_spec]
    args = list(a_s) + [w]
    if shift:
        in_specs.append(wx_spec)
        args.append(w)
    outs = pl.pallas_call(
        functools.partial(_mm_ws_kernel, shift=shift, transposed=transposed, starts=tuple(starts)),
        grid=(nj, starts[-1]),
        in_specs=in_specs,
        out_specs=[pl.BlockSpec((t_, tn), functools.partial(lambda j, i, f: (f(i), j), f=f))
                   for t_, f in zip(tms, idx)],
        out_shape=[jax.ShapeDtypeStruct((x.shape[0], n), out_dtype) for x in a_s],
        scratch_shapes=[pltpu.VMEM((tn, k) if transposed else (k, tn), BF16)],
        compiler_params=_cp("parallel", "arbitrary"),
        name=name,
    )(*args)
    return outs[0] if single else tuple(outs)


def _small_proj_kernel(a_ref, w0_ref, w1_ref, o_ref):
    a = a_ref[...]
    p0 = _dot_nt(a, w0_ref[0].astype(BF16))
    p1 = _dot_nt(a, w1_ref[0].astype(BF16))
    lane = lax.broadcasted_iota(jnp.int32, p0.shape, 1)
    o_ref[...] = jnp.where(lane < SMALL_A0, p0, p1)


def _small_proj(a, w_t, layer, blk_lr, blk_ab, tm):
    m, k = a.shape
    return pl.pallas_call(
        _small_proj_kernel,
        grid=(m // tm,),
        in_specs=[pl.BlockSpec((tm, k), lambda i: (i, 0)),
                  pl.BlockSpec((1, LANES, k), lambda i: (layer, blk_lr, 0)),
                  pl.BlockSpec((1, LANES, k), lambda i: (layer, blk_ab, 0))],
        out_specs=pl.BlockSpec((tm, LANES), lambda i: (i, 0)),
        out_shape=jax.ShapeDtypeStruct((m, LANES), F32),
        compiler_params=_cp("parallel"),
        name="in_proj_small",
    )(a, w_t, w_t)


def _shift_rows(x, prev8, s):
    xs = pltpu.roll(x, s, 0)
    ps = pltpu.roll(prev8, s, 0)
    r8 = lax.broadcasted_iota(jnp.int32, prev8.shape, 0)
    top = jnp.where(r8 < s, ps, xs[:SUBLANES])
    if x.shape[0] == SUBLANES:
        return top
    return jnp.concatenate([top, xs[SUBLANES:]], axis=0)


def _causal_conv(x, prev8, taps):
    width = len(taps)
    y = None
    for i, w in enumerate(taps):
        s = width - 1 - i
        xs = x if s == 0 else _shift_rows(x, prev8, s)
        y = xs * w if y is None else y + xs * w
    return y


def _seq_tiling(t, tm):
    return (t // tm, None, 1) if t >= tm else (1, t, tm // t)


def _conv_proj_kernel(*refs, shift, starts, tilings, head_dim, norm_scale):
    ng = len(starts) - 1
    w_ref = refs[2 * ng]
    rest = refs[2 * ng + 1:]
    wx_ref = None
    if shift:
        wx_ref, rest = rest[0], rest[1:]
    cw_ref = rest[0]
    outs = rest[1:2 * ng + 1]
    wb_ref = rest[2 * ng + 1]
    carries = rest[2 * ng + 2:]
    i = pl.program_id(1)

    @pl.when(i == 0)
    def _():
        _cast_weight_tile_t(w_ref, wx_ref, wb_ref, shift)

    for gi in range(ng):
        def run(a_ref=refs[2 * gi], buf_ref=refs[2 * gi + 1], o_ref=outs[2 * gi], tail_ref=outs[2 * gi + 1],
                carry_ref=carries[gi], tiles_per_seq=tilings[gi][0], seq_rows=tilings[gi][1], lo_step=starts[gi]):
            tm, tn = o_ref.shape
            sub = min(tm, CONV_SUB)
            seg = sub if seq_rows is None else seq_rows
            if seq_rows is None:
                @pl.when((i - lo_step) % tiles_per_seq == 0)
                def _():
                    carry_ref[...] = buf_ref[0]

                prev = carry_ref[...]
            wb = wb_ref[...]
            taps = [cw_ref[n:n + 1, :] for n in range(cw_ref.shape[0])]
            for r in range(tm // sub):
                p_all = _dot_nt(a_ref[r * sub:(r + 1) * sub, :], wb)
                for s in range(sub // seg):
                    lo = r * sub + s * seg
                    x = p_all[s * seg:(s + 1) * seg]
                    if seq_rows is not None:
                        prev = buf_ref[lo // seg]
                    for h in range(tn // head_dim):
                        hs = slice(h * head_dim, (h + 1) * head_dim)
                        yh = _silu(_causal_conv(x[:, hs], prev[:, hs], [w[:, hs] for w in taps]))
                        if norm_scale is not None:
                            yh = yh * lax.rsqrt(jnp.sum(yh * yh, axis=-1, keepdims=True) + EPS)
                            if norm_scale != 1.0:
                                yh = yh * norm_scale
                        o_ref[lo:lo + seg, hs] = yh
                    prev = x[seg - SUBLANES:, :]
                    if seq_rows is not None:
                        tail_ref[lo // seg] = prev
            if seq_rows is None:
                carry_ref[...] = prev
                tail_ref[0] = prev
        _on_group_turn(i, starts, gi, run)


def _conv_proj(a_s, w_t, layer, off, n, conv_w, buf8_s, coff, t_s, tm_s, tn, head_dim, norm_scale, name):
    k = a_s[0].shape[1]
    nj = n // tn
    blk0 = off // tn
    shift = off - blk0 * tn
    assert shift < LANES and shift % SUBLANES == 0 and n % tn == 0 and coff % tn == 0
    cblk = coff // tn
    per = tn // LANES
    starts, idx = _group_steps([a.shape[0] for a in a_s], tm_s)
    tilings = [_seq_tiling(t, tm) for t, tm in zip(t_s, tm_s)]
    in_specs, args, out_specs, out_shape, scratch = [], [], [], [], []
    for a, buf8, tm, f, (tps, _, nb) in zip(a_s, buf8_s, tm_s, idx, tilings):
        ni = a.shape[0] // tm
        in_specs += [pl.BlockSpec((tm, k), functools.partial(lambda j, i, f: (f(i), 0), f=f)),
                     pl.BlockSpec((nb, SUBLANES, tn),
                                  functools.partial(lambda j, i, f, tps: (f(i) // tps, 0, cblk + j), f=f, tps=tps))]
        args += [a, buf8]
        out_specs += [pl.BlockSpec((tm, tn), functools.partial(lambda j, i, f: (f(i), j), f=f)),
                      pl.BlockSpec((nb, SUBLANES, tn), functools.partial(lambda j, i, f: (f(i), 0, j), f=f))]
        out_shape += [jax.ShapeDtypeStruct((a.shape[0], n), F32),
                      jax.ShapeDtypeStruct((ni * nb, SUBLANES, n), F32)]
        scratch.append(pltpu.VMEM((SUBLANES, tn), F32))
    in_specs.append(pl.BlockSpec((1, tn, k), lambda j, i: (layer, blk0 + j, 0)))
    args.append(w_t)
    if shift:
        in_specs.append(pl.BlockSpec((1, LANES, k), lambda j, i: (layer, (blk0 + j + 1) * per, 0)))
        args.append(w_t)
    in_specs.append(pl.BlockSpec((conv_w.shape[0], tn), lambda j, i: (0, cblk + j)))
    args.append(conv_w)
    outs = pl.pallas_call(
        functools.partial(_conv_proj_kernel, shift=shift, starts=tuple(starts), tilings=tuple(tilings),
                          head_dim=head_dim, norm_scale=norm_scale),
        grid=(nj, starts[-1]),
        in_specs=in_specs,
        out_specs=out_specs,
        out_shape=out_shape,
        scratch_shapes=[pltpu.VMEM((tn, k), BF16)] + scratch,
        compiler_params=_cp("parallel", "arbitrary"),
        name=name,
    )(*args)
    return [(outs[2 * gi], outs[2 * gi + 1][tilings[gi][0] - 1::tilings[gi][0]]) for gi in range(len(a_s))]


def _proj_res_kernel(*refs, sub, final, starts):
    ng = len(starts) - 1
    nout = 1 if final else 2
    a_refs, x_refs = refs[0:2 * ng:2], refs[1:2 * ng:2]
    w_ref, g_ref = refs[2 * ng], refs[2 * ng + 1]
    outs = refs[2 * ng + 2:]
    i = pl.program_id(0)
    for gi in range(ng):
        def run(a_ref=a_refs[gi], x_ref=x_refs[gi], o=outs[gi * nout:(gi + 1) * nout]):
            tm = a_ref.shape[0]
            sb = min(sub, tm)
            w = w_ref[...]
            g = g_ref[...]
            for r in range(tm // sb):
                rows = slice(r * sb, (r + 1) * sb)
                xn = x_ref[rows, :] + _dot(a_ref[rows, :], w)
                if final:
                    o[0][rows, :] = _rms(xn, g)
                else:
                    o[0][rows, :] = xn
                    o[1][rows, :] = _rms(xn, g).astype(o[1].dtype)
        _on_group_turn(i, starts, gi, run)


def _proj_res(a_s, w, x_s, g, tms, sub, final, name):
    kk, d = w.shape
    starts, idx = _group_steps([a.shape[0] for a in a_s], tms)
    in_specs, args, out_specs, out_shape = [], [], [], []
    for a, x, tm, f in zip(a_s, x_s, tms, idx):
        row = functools.partial(lambda i, f: (f(i), 0), f=f)
        in_specs += [pl.BlockSpec((tm, kk), row), pl.BlockSpec((tm, d), row)]
        args += [a, x]
        out_specs.append(pl.BlockSpec((tm, d), row))
        out_shape.append(jax.ShapeDtypeStruct((a.shape[0], d), F32))
        if not final:
            out_specs.append(pl.BlockSpec((tm, d), row))
            out_shape.append(jax.ShapeDtypeStruct((a.shape[0], d), BF16))
    outs = pl.pallas_call(
        functools.partial(_proj_res_kernel, sub=sub, final=final, starts=tuple(starts)),
        grid=(starts[-1],),
        in_specs=in_specs + [pl.BlockSpec((kk, d), lambda i: (0, 0), pipeline_mode=pl.Buffered(1)),
                             pl.BlockSpec((1, d), lambda i: (0, 0))],
        out_specs=out_specs,
        out_shape=out_shape,
        compiler_params=_cp("arbitrary"),
        name=name,
    )(*args, w, g.reshape(1, d))
    nout = 1 if final else 2
    return [tuple(outs[gi * nout:(gi + 1) * nout]) for gi in range(len(a_s))]


def _merge_kernel(*refs, sub, starts):
    ng = len(starts) - 1
    per = N_BRANCH + 1
    w_ref, b_ref = refs[per * ng], refs[per * ng + 1]
    o_refs = refs[per * ng + 2:]
    i = pl.program_id(0)
    for gi in range(ng):
        def run(y_refs=refs[gi * per:gi * per + N_BRANCH], g_ref=refs[gi * per + N_BRANCH], o_ref=o_refs[gi]):
            tm, d = o_ref.shape
            sb = min(sub, tm)
            for r in range(tm // sb):
                rows = slice(r * sb, (r + 1) * sb)
                acc = None
                for n, y_ref in enumerate(y_refs):
                    p = _dot(y_ref[rows, :], w_ref[n])
                    g = g_ref[rows, n * d:(n + 1) * d].astype(F32)
                    t = _sigmoid(g + b_ref[n:n + 1, :]) * p
                    acc = t if acc is None else acc + t
                o_ref[rows, :] = acc.astype(o_ref.dtype)
        _on_group_turn(i, starts, gi, run)


def _merge(ys_s, w_branch, g_br_s, b_gate, tms):
    _, bw, d = w_branch.shape
    starts, idx = _group_steps([g.shape[0] for g in g_br_s], tms)
    in_specs, args, out_specs, out_shape = [], [], [], []
    for ys, g_br, tm, f in zip(ys_s, g_br_s, tms, idx):
        row = functools.partial(lambda i, f: (f(i), 0), f=f)
        in_specs += [pl.BlockSpec((tm, bw), row)] * N_BRANCH + [pl.BlockSpec((tm, N_BRANCH * d), row)]
        args += list(ys) + [g_br]
        out_specs.append(pl.BlockSpec((tm, d), row))
        out_shape.append(jax.ShapeDtypeStruct((g_br.shape[0], d), BF16))
    return pl.pallas_call(
        functools.partial(_merge_kernel, sub=ROW_SUB, starts=tuple(starts)),
        grid=(starts[-1],),
        in_specs=in_specs + [pl.BlockSpec((N_BRANCH, bw, d), lambda i: (0, 0, 0), pipeline_mode=pl.Buffered(1)),
                             pl.BlockSpec((N_BRANCH, d), lambda i: (0, 0))],
        out_specs=out_specs,
        out_shape=out_shape,
        compiler_params=_cp("arbitrary"),
        name="branch_merge",
    )(*args, w_branch, b_gate)


def _xattn_kernel(q_ref, k_ref, v_ref, o_ref, *scratch, scale, dh):
    heads = range(q_ref.shape[1] // dh)
    hs = lambda h: slice(h * dh, (h + 1) * dh)
    if scratch:
        kb_ref, vb_ref = scratch

        @pl.when(pl.program_id(1) == 0)
        def _():
            for h in heads:
                kb_ref[:, hs(h)] = k_ref[0, 0, :, h, :].astype(BF16)
                vb_ref[:, hs(h)] = v_ref[0, 0, :, h, :].astype(BF16)
    else:
        kb_ref, vb_ref = k_ref.at[0], v_ref.at[0]

    s = [_dot_nt(q_ref[:, hs(h)], kb_ref[:, hs(h)]) * scale for h in heads]
    e = [jnp.exp(x - jnp.max(x, axis=-1, keepdims=True)) for x in s]
    p = [(x / jnp.sum(x, axis=-1, keepdims=True)).astype(BF16) for x in e]
    for h in heads:
        o_ref[:, hs(h)] = _dot(p[h], vb_ref[:, hs(h)]).astype(o_ref.dtype)


def _xattn(q, mem_k, mem_v, layer, t, tq):
    m, d = q.shape
    b = m // t
    dh = d // X_H
    nq = t // tq
    if mem_k.ndim == 5:
        nmem = mem_k.shape[2]
        kv_spec = pl.BlockSpec((1, 1, nmem, X_H, dh), lambda bi, i: (layer, bi, 0, 0, 0))
        scratch = [pltpu.VMEM((nmem, d), BF16), pltpu.VMEM((nmem, d), BF16)]
    else:
        nmem = mem_k.shape[1] // b
        kv_spec = pl.BlockSpec((1, nmem, d), lambda bi, i: (layer, bi, 0))
        scratch = []
    qo_spec = pl.BlockSpec((tq, d), lambda bi, i: (bi * nq + i, 0))
    return pl.pallas_call(
        functools.partial(_xattn_kernel, scale=dh ** -0.5, dh=dh),
        grid=(b, nq),
        in_specs=[qo_spec, kv_spec, kv_spec],
        out_specs=qo_spec,
        out_shape=jax.ShapeDtypeStruct((m, d), BF16),
        scratch_shapes=scratch,
        compiler_params=_cp("parallel", "arbitrary"),
        name="xattn",
    )(q, mem_k, mem_v)


def _mem_proj_kernel(x_ref, g_ref, w_ref, o_ref, ob_ref, wb_ref):
    @pl.when(pl.program_id(1) == 0)
    def _():
        _cast_weight_tile(w_ref, None, wb_ref, 0)

    _, nb, nmem, nh, dh = o_ref.shape
    res = _dot(_rms(x_ref[...], g_ref[0]).astype(BF16), wb_ref[...])
    ob_ref[0] = res.astype(BF16)
    for bb in range(nb):
        for h in range(nh):
            o_ref[0, bb, :, h, :] = res[bb * nmem:(bb + 1) * nmem, h * dh:(h + 1) * dh]


def _mem_proj(mem2, ln_mem, w_xkv, half, bsz, nmem, name):
    depth, d, _ = w_xkv.shape
    dh = d // X_H
    nb = 1
    return pl.pallas_call(
        _mem_proj_kernel,
        grid=(depth, bsz // nb),
        in_specs=[pl.BlockSpec((nb * nmem, d), lambda l, i: (i, 0)),
                  pl.BlockSpec((1, 1, d), lambda l, i: (l, 0, 0)),
                  pl.BlockSpec((1, d, d), lambda l, i: (l, 0, half), pipeline_mode=pl.Buffered(1))],
        out_specs=[pl.BlockSpec((1, nb, nmem, X_H, dh), lambda l, i: (l, i, 0, 0, 0)),
                   pl.BlockSpec((1, nb * nmem, d), lambda l, i: (l, i, 0))],
        out_shape=[jax.ShapeDtypeStruct((depth, bsz, nmem, X_H, dh), F32),
                   jax.ShapeDtypeStruct((depth, bsz * nmem, d), BF16)],
        scratch_shapes=[pltpu.VMEM((d, d), BF16)],
        compiler_params=_cp("arbitrary", "arbitrary"),
        name=name,
    )(mem2, ln_mem[:, None, :], w_xkv)


def _gelu_tanh(x):
    c = 0.7978845608028654
    return x * (0.5 * (1.0 + jnp.tanh(c * (x + 0.044715 * (x * x * x)))))


def _ffn_in_kernel(*refs, starts, tilings):
    ng = len(starts) - 1
    wg_ref, wu_ref, cw_ref, cb_ref = refs[2 * ng:2 * ng + 4]
    outs = refs[2 * ng + 4:4 * ng + 4]
    wgb_ref, wub_ref = refs[4 * ng + 4:4 * ng + 6]
    carries = refs[4 * ng + 6:]
    i = pl.program_id(1)

    @pl.when(i == 0)
    def _():
        _cast_weight_tile(wg_ref, None, wgb_ref, 0)
        _cast_weight_tile(wu_ref, None, wub_ref, 0)

    for gi in range(ng):
        def run(h_ref=refs[2 * gi], buf_ref=refs[2 * gi + 1], act_ref=outs[2 * gi], nbuf_ref=outs[2 * gi + 1],
                carry_ref=carries[gi], tiles_per_seq=tilings[gi][0], seq_rows=tilings[gi][1], lo_step=starts[gi]):
            tm = h_ref.shape[0]
            sub = min(tm, FFN_SUB)
            seg = sub if seq_rows is None else seq_rows
            if seq_rows is None:
                @pl.when((i - lo_step) % tiles_per_seq == 0)
                def _():
                    carry_ref[...] = buf_ref[0]

                prev = carry_ref[...]
            wg, wu = wgb_ref[...], wub_ref[...]
            taps, cb = [cw_ref[n:n + 1, :] for n in range(cw_ref.shape[0])], cb_ref[...]
            for r in range(tm // sub):
                h = h_ref[r * sub:(r + 1) * sub, :]
                gate_all = _dot(h, wg)
                up_all = _dot(h, wu)
                for s in range(sub // seg):
                    lo = r * sub + s * seg
                    gate = gate_all[s * seg:(s + 1) * seg]
                    if seq_rows is not None:
                        prev = buf_ref[lo // seg]
                    y = _causal_conv(gate, prev, taps)
                    act = _gelu_tanh(y + cb) * up_all[s * seg:(s + 1) * seg]
                    act_ref[lo:lo + seg, :] = act.astype(act_ref.dtype)
                    prev = gate[seg - SUBLANES:, :]
                    if seq_rows is not None:
                        nbuf_ref[lo // seg] = prev
            if seq_rows is None:
                carry_ref[...] = prev
                nbuf_ref[0] = prev
        _on_group_turn(i, starts, gi, run)


def _ffn_in(h_s, w_ffn_in, layer, conv_w, conv_b, buf8_s, t_s, tm_s, tn):
    d = h_s[0].shape[1]
    dff = w_ffn_in.shape[2] // 2
    nj = dff // tn
    starts, idx = _group_steps([h.shape[0] for h in h_s], tm_s)
    tilings = [_seq_tiling(t, tm) for t, tm in zip(t_s, tm_s)]
    in_specs, args, out_specs, out_shape, scratch = [], [], [], [], []
    for h, buf8, tm, f, (tps, _, nb) in zip(h_s, buf8_s, tm_s, idx, tilings):
        ni = h.shape[0] // tm
        in_specs += [pl.BlockSpec((tm, d), functools.partial(lambda j, i, f: (f(i), 0), f=f)),
                     pl.BlockSpec((nb, SUBLANES, tn),
                                  functools.partial(lambda j, i, f, tps: (f(i) // tps, 0, j), f=f, tps=tps))]
        args += [h, buf8]
        out_specs += [pl.BlockSpec((tm, tn), functools.partial(lambda j, i, f: (f(i), j), f=f)),
                      pl.BlockSpec((nb, SUBLANES, tn), functools.partial(lambda j, i, f: (f(i), 0, j), f=f))]
        out_shape += [jax.ShapeDtypeStruct((h.shape[0], dff), BF16),
                      jax.ShapeDtypeStruct((ni * nb, SUBLANES, dff), F32)]
        scratch.append(pltpu.VMEM((SUBLANES, tn), F32))
    outs = pl.pallas_call(
        functools.partial(_ffn_in_kernel, starts=tuple(starts), tilings=tuple(tilings)),
        grid=(nj, starts[-1]),
        in_specs=in_specs + [pl.BlockSpec((1, d, tn), lambda j, i: (layer, 0, j)),
                             pl.BlockSpec((1, d, tn), lambda j, i: (layer, 0, nj + j)),
                             pl.BlockSpec((FFN_CONV, tn), lambda j, i: (0, j)),
                             pl.BlockSpec((1, tn), lambda j, i: (0, j))],
        out_specs=out_specs,
        out_shape=out_shape,
        scratch_shapes=[pltpu.VMEM((d, tn), BF16), pltpu.VMEM((d, tn), BF16)] + scratch,
        compiler_params=_cp("parallel", "arbitrary"),
        name="ffn_in",
    )(*args, w_ffn_in, w_ffn_in, conv_w, conv_b.reshape(1, dff))
    return [(outs[2 * gi], outs[2 * gi + 1][tilings[gi][0] - 1::tilings[gi][0]]) for gi in range(len(h_s))]


def _ret_kernel(ld_ref, p_ref, cos_ref, sin_ref, s0_ref, y_ref, s_ref, *, c):
    t = pl.program_id(1)
    tb = p_ref.shape[0]
    half = RET_DK // 2
    hw = RET_H * RET_DK
    heads = range(RET_H)

    @pl.when(t == 0)
    def _():
        s_ref[...] = s0_ref[...]

    ri = lax.broadcasted_iota(jnp.int32, (c, c), 0)
    ci = lax.broadcasted_iota(jnp.int32, (c, c), 1)
    diff = (ri - ci).astype(F32)
    causal = ri >= ci
    idx = lax.broadcasted_iota(jnp.int32, (c, RET_DK), 0).astype(F32)
    lds = [ld_ref[h] for h in heads]
    dmask = [jnp.where(causal, jnp.exp(ld * jnp.where(causal, diff, 0.0)), 0.0) for ld in lds]
    q_dec = [jnp.exp(ld * (idx + 1.0)) for ld in lds]
    k_dec = [jnp.exp(ld * (c - 1.0 - idx)) for ld in lds]
    s_dec = [jnp.exp(jnp.full((1, RET_DV), ld * c, F32)) for ld in lds]

    def rot(x, cos, sin):
        x1, x2 = x[:, :half], x[:, half:]
        return jnp.concatenate([x1 * cos - x2 * sin, x1 * sin + x2 * cos], axis=-1)

    nchunk = tb // c
    nc = RET_CHUNKS_PER_STEP if nchunk % RET_CHUNKS_PER_STEP == 0 else 1

    def body(jj, carry):
        ch = []
        for ci in range(nc):
            rows = pl.ds(pl.multiple_of((jj * nc + ci) * c, c), c)
            cos, sin = cos_ref[rows, :], sin_ref[rows, :]
            k = [rot(p_ref[rows, hw + h * RET_DK:hw + (h + 1) * RET_DK], cos, sin) * (RET_DK ** -0.5)
                 for h in heads]
            ch.append(dict(
                rows=rows, k=k,
                qb=[rot(p_ref[rows, h * RET_DK:(h + 1) * RET_DK], cos, sin).astype(BF16) for h in heads],
                vb=[p_ref[rows, 2 * hw + h * RET_DV:2 * hw + (h + 1) * RET_DV].astype(BF16) for h in heads]))
        probs = [(d, h) for d in ch for h in heads]
        att = [(_dot_nt(d['qb'][h], d['k'][h].astype(BF16)) * dmask[h]).astype(BF16) for d, h in probs]
        av = [_dot(a, d['vb'][h]) for a, (d, h) in zip(att, probs)]
        kv = [_dot_tn((d['k'][h] * k_dec[h]).astype(BF16), d['vb'][h]) for d, h in probs]

        for ci, d in enumerate(ch):
            s = [s_ref[0, h] for h in heads]
            o = [av[ci * RET_H + h] + _dot(d['qb'][h], s[h].astype(BF16)) * q_dec[h] for h in heads]
            for h in heads:
                s_ref[0, h] = s_dec[h] * s[h] + kv[ci * RET_H + h]
            for h in heads:
                g = p_ref[d['rows'], 3 * hw + h * RET_DV:3 * hw + (h + 1) * RET_DV]
                y_ref[d['rows'], h * RET_DV:(h + 1) * RET_DV] = (_rms(o[h]) * _silu(g)).astype(y_ref.dtype)
        return carry

    lax.fori_loop(0, nchunk // nc, body, 0)


def _retention(p_ret, cos, sin, ld, s0, t, tb):
    m, width = p_ret.shape
    b = m // t
    c = min(CHUNK, t)
    nt = t // tb
    st_spec = pl.BlockSpec((1, RET_H, RET_DK, RET_DV), lambda bi, ti: (bi, 0, 0, 0))
    rope_spec = pl.BlockSpec((tb, RET_DK // 2), lambda bi, ti: (ti, 0))
    return pl.pallas_call(
        functools.partial(_ret_kernel, c=c),
        grid=(b, nt),
        in_specs=[pl.BlockSpec(memory_space=pltpu.SMEM),
                  pl.BlockSpec((tb, width), lambda bi, ti: (bi * nt + ti, 0)),
                  rope_spec, rope_spec, st_spec],
        out_specs=[pl.BlockSpec((tb, RET_H * RET_DV), lambda bi, ti: (bi * nt + ti, 0)), st_spec],
        out_shape=[jax.ShapeDtypeStruct((m, RET_H * RET_DV), BF16),
                   jax.ShapeDtypeStruct((b, RET_H, RET_DK, RET_DV), F32)],
        compiler_params=_cp("parallel", "arbitrary"),
        name="retention",
    )(ld, p_ret, cos, sin, s0)


def _gla_kernel(p_ref, lr_ref, wgk_ref, bgk_ref, nrm_ref, s0_ref, y_ref, s_ref, st_ref, *, c, nt):
    t = pl.program_id(1)
    tb = p_ref.shape[0]
    hk = GLA_H * GLA_DK
    hv = GLA_H * GLA_DV
    heads = range(GLA_H)

    @pl.when(t == 0)
    def _():
        for h in heads:
            st_ref[h] = s0_ref[0, h].T

    ri = lax.broadcasted_iota(jnp.int32, (c, c), 0)
    ci = lax.broadcasted_iota(jnp.int32, (c, c), 1)
    causal = ri >= ci
    tri = jnp.where(causal, 1.0, 0.0).astype(BF16)
    wgk = wgk_ref[...]
    bgk = bgk_ref[...]
    nrm = nrm_ref[...]

    nchunk = tb // c
    nc = GLA_CHUNKS_PER_STEP if nchunk % GLA_CHUNKS_PER_STEP == 0 else 1
    hs = lambda x, h: x[:, h * GLA_DK:(h + 1) * GLA_DK]

    def body(jj, carry):
        rows = [pl.ds(pl.multiple_of((jj * nc + ci) * c, c), c) for ci in range(nc)]
        z = [_dot(lr_ref[r, :].astype(BF16), wgk) + bgk for r in rows]
        bc = [_tri_cumsum(tri, -_softplus(-zz) / GLA_NORMALIZER) for zz in z]
        ch = []
        for r, b_ in zip(rows, bc):
            q = p_ref[r, 0:hk] * (GLA_DK ** -0.5)
            k = p_ref[r, hk:2 * hk]
            b_last = b_[c - 1:c, :]
            ch.append(dict(
                rows=r, qe=(q * jnp.exp(b_)).astype(BF16), ke=(k * jnp.exp(-b_)).astype(BF16),
                kd=(k * jnp.exp(b_last - b_)).astype(BF16), eb_last=jnp.exp(b_last),
                vb=[p_ref[r, 2 * hk + h * GLA_DV:2 * hk + (h + 1) * GLA_DV].astype(BF16) for h in heads]))
        probs = [(d, h) for d in ch for h in heads]
        att = [jnp.where(causal, _dot_nt(hs(d['qe'], h), hs(d['ke'], h)), 0.0).astype(BF16) for d, h in probs]
        av = [_dot(a, d['vb'][h]) for a, (d, h) in zip(att, probs)]

        for ci, d in enumerate(ch):
            st = [st_ref[h] for h in heads]
            o = [av[ci * GLA_H + h] + _dot_nt(hs(d['qe'], h), st[h].astype(BF16)) for h in heads]
            for h in heads:
                st_ref[h] = hs(d['eb_last'], h) * st[h] + _dot_tn(d['vb'][h], hs(d['kd'], h))
            for h in heads:
                g = p_ref[d['rows'], 2 * hk + hv + h * GLA_DV:2 * hk + hv + (h + 1) * GLA_DV]
                y_ref[d['rows'], h * GLA_DV:(h + 1) * GLA_DV] = (_rms(o[h], nrm) * _silu(g)).astype(y_ref.dtype)
        return carry

    lax.fori_loop(0, nchunk // nc, body, 0)

    @pl.when(t == nt - 1)
    def _():
        for h in heads:
            s_ref[0, h] = st_ref[h].T


def _gla(p_gla, p_small, wgk, bgk, nrm, s0, t, tb):
    m, width = p_gla.shape
    b = m // t
    c = min(CHUNK, t)
    nt = t // tb
    const = lambda bi, ti: (0, 0)
    st_spec = pl.BlockSpec((1, GLA_H, GLA_DK, GLA_DV), lambda bi, ti: (bi, 0, 0, 0))
    return pl.pallas_call(
        functools.partial(_gla_kernel, c=c, nt=nt),
        grid=(b, nt),
        in_specs=[pl.BlockSpec((tb, width), lambda bi, ti: (bi * nt + ti, 0)),
                  pl.BlockSpec((tb, LANES), lambda bi, ti: (bi * nt + ti, 0)),
                  pl.BlockSpec((LANES, GLA_H * GLA_DK), const),
                  pl.BlockSpec((1, GLA_H * GLA_DK), const),
                  pl.BlockSpec((1, GLA_DV), const),
                  st_spec],
        out_specs=[pl.BlockSpec((tb, GLA_H * GLA_DV), lambda bi, ti: (bi * nt + ti, 0)), st_spec],
        out_shape=[jax.ShapeDtypeStruct((m, GLA_H * GLA_DV), BF16),
                   jax.ShapeDtypeStruct((b, GLA_H, GLA_DK, GLA_DV), F32)],
        scratch_shapes=[pltpu.VMEM((GLA_H, GLA_DV, GLA_DK), F32)],
        compiler_params=_cp("parallel", "arbitrary"),
        name="gla",
    )(p_gla, p_small, wgk, bgk, nrm, s0)


def _unit_lower_inverse_minus_eye(mats, c):
    ri = lax.broadcasted_iota(jnp.int32, (c, c), 0)
    ci = lax.broadcasted_iota(jnp.int32, (c, c), 1)

    def level_mask(k):
        same = (ri ^ ci) < 2 * k
        return same & ((ri & k) != 0) & ((ci & k) == 0)

    m1 = level_mask(1)
    ns = [-jnp.where(m1, a, 0.0) for a in mats]
    k = 2
    while k < c:
        mk = level_mask(k)
        ls = [jnp.where(mk, a, 0.0) for a in mats]
        nbs = [n.astype(BF16) for n in ns]
        ys = [l + _dot(nb, l.astype(BF16)) for l, nb in zip(ls, nbs)]
        xs = [y + _dot(y.astype(BF16), nb) for y, nb in zip(ys, nbs)]
        ns = [n - x for n, x in zip(ns, xs)]
        k *= 2
    return ns


def _blockdiag2(xp, left):
    return jnp.concatenate([jnp.where(left, xp, 0.0), jnp.where(left, 0.0, xp)], axis=0)


def _unit_lower_inverse_minus_eye_packed(mats, c):
    ri = lax.broadcasted_iota(jnp.int32, (c, 2 * c), 0)
    cl = lax.broadcasted_iota(jnp.int32, (c, 2 * c), 1)
    left = cl < c
    cm = cl & (c - 1)

    def level_mask(k):
        return ((ri ^ cm) < 2 * k) & ((ri & k) != 0) & ((cm & k) == 0)

    m1 = level_mask(1)
    ns = [-jnp.where(m1, a, 0.0) for a in mats]
    k = 2
    while k < c:
        mk = level_mask(k)
        ls = [jnp.where(mk, a, 0.0) for a in mats]
        ys = [l + _dot(n.astype(BF16), _blockdiag2(l, left).astype(BF16)) for l, n in zip(ls, ns)]
        xs = [y + _dot(y.astype(BF16), _blockdiag2(n, left).astype(BF16)) for y, n in zip(ys, ns)]
        ns = [n - x for n, x in zip(ns, xs)]
        k *= 2
    return ns


def _gdn_kernel(q_ref, k_ref, v_ref, z_ref, sm_ref, alog_ref, dtb_ref, nrm_ref, s0_ref,
                y_ref, s_ref, *, c):
    t = pl.program_id(1)
    tb = q_ref.shape[0]

    @pl.when(t == 0)
    def _():
        s_ref[...] = s0_ref[...]

    tri = jnp.where(lax.broadcasted_iota(jnp.int32, (c, c), 0) >= lax.broadcasted_iota(jnp.int32, (c, c), 1),
                    1.0, 0.0).astype(BF16)
    ri = lax.broadcasted_iota(jnp.int32, (c, 2 * c), 0)
    cl = lax.broadcasted_iota(jnp.int32, (c, 2 * c), 1)
    left = cl < c
    cm = cl & (c - 1)
    incl = ri >= cm
    strict = ri > cm
    lane = lax.broadcasted_iota(jnp.int32, (1, LANES), 1)
    a_lanes = jnp.logical_and(lane >= SMALL_A0, lane < SMALL_A0 + GDN_H)
    sel2 = jnp.where(lax.broadcasted_iota(jnp.int32, (SUBLANES, LANES), 1)
                     == 2 * lax.broadcasted_iota(jnp.int32, (SUBLANES, LANES), 0) + SMALL_A0,
                     1.0, 0.0).astype(BF16)
    neg_a = -jnp.exp(alog_ref[...])
    dtb = dtb_ref[...]
    nrm = nrm_ref[...]

    def col(x, lane_idx):
        return x[:, lane_idx:lane_idx + 1]

    assert GDN_DK == LANES and GDN_DV == LANES and GDN_H % 2 == 0
    heads = range(GDN_H)
    npair = GDN_H // 2
    pairs = range(npair)
    nchunk = tb // c
    nc = GDN_CHUNKS_PER_STEP if nchunk % GDN_CHUNKS_PER_STEP == 0 else 1

    def lanes2(x, y):
        return jnp.concatenate([x, y], axis=1)

    def body(jj, carry):
        ch = []
        for ci in range(nc):
            rows = pl.ds(pl.multiple_of((jj * nc + ci) * c, c), c)
            sm = sm_ref[rows, :]
            ch.append(dict(
                rows=rows,
                g_all=jnp.where(a_lanes, neg_a * _softplus(sm + dtb), 0.0),
                beta_all=_sigmoid(sm),
                q=[q_ref[rows, h * GDN_DK:(h + 1) * GDN_DK] for h in heads],
                k=[k_ref[rows, h * GDN_DK:(h + 1) * GDN_DK] for h in heads],
                v=[v_ref[rows, h * GDN_DV:(h + 1) * GDN_DV] for h in heads]))
        for d in ch:
            d['gam'] = _tri_cumsum(tri, d['g_all'])
        for d in ch:
            g2 = jnp.concatenate([d['gam'], pltpu.roll(d['gam'], LANES - 1, 1)], axis=0)
            ghi, gmid, glo = _split3(g2)
            d['gam_rows'] = _dot_nt(sel2, ghi) + _dot_nt(sel2, gmid) + _dot_nt(sel2, glo)

        def packed_cols(x, base, p):
            return jnp.where(left, col(x, base + 2 * p), col(x, base + 2 * p + 1))

        for d in ch:
            gam = d['gam']
            g_last = gam[c - 1:c, :]
            d['eg_all'] = jnp.exp(gam)
            d['egl_all'] = jnp.exp(g_last - gam)
            d['eg_last_all'] = jnp.exp(g_last)
            d['beta'] = [col(d['beta_all'], SMALL_B0 + h) for h in heads]
            d['eg'] = [col(d['eg_all'], SMALL_A0 + h) for h in heads]
            d['beta_p'] = [packed_cols(d['beta_all'], SMALL_B0, p) for p in pairs]
            d['decay_p'] = [jnp.where(incl, jnp.exp(jnp.where(
                incl, packed_cols(gam, SMALL_A0, p) - d['gam_rows'][p:p + 1, :], 0.0)), 0.0) for p in pairs]
        probs = [(d, p) for d in ch for p in pairs]
        zk = jnp.zeros((c, GDN_DK), F32)
        qk_kk = [_dot_nt(
            jnp.concatenate([lanes2(d['q'][2 * p], d['q'][2 * p + 1]),
                             lanes2(d['k'][2 * p], d['k'][2 * p + 1])], axis=0).astype(BF16),
            jnp.concatenate([lanes2(d['k'][2 * p], zk), lanes2(zk, d['k'][2 * p + 1])], axis=0).astype(BF16))
            for d, p in probs]
        att = [(r[:c] * d['decay_p'][p]).astype(BF16) for r, (d, p) in zip(qk_kk, probs)]
        a = [jnp.where(strict, d['beta_p'][p] * r[c:] * d['decay_p'][p], 0.0) for r, (d, p) in zip(qk_kk, probs)]
        n = _unit_lower_inverse_minus_eye_packed(a, c)
        zr = jnp.zeros((c, GDN_DV + GDN_DK), F32)
        sol = []
        for (d, p), nn in zip(probs, n):
            r1, r2 = (jnp.concatenate([d['beta'][h] * d['v'][h], (d['beta'][h] * d['eg'][h]) * d['k'][h]], axis=-1)
                      for h in (2 * p, 2 * p + 1))
            bd = jnp.concatenate([lanes2(r1, zr), lanes2(zr, r2)], axis=0).astype(BF16)
            sol.append(lanes2(r1, r2) + _dot(nn.astype(BF16), bd))

        zs = jnp.zeros((GDN_DK, GDN_DV), F32)
        zw = jnp.zeros((c, GDN_DV), F32)
        for ci, d in enumerate(ch):
            sl = sol[ci * npair:(ci + 1) * npair]
            at = att[ci * npair:(ci + 1) * npair]
            s = [s_ref[0, h] for h in heads]
            wq_s = [_dot(
                jnp.concatenate([lanes2(sl[p][:, LANES:2 * LANES], sl[p][:, 3 * LANES:]),
                                 lanes2(d['q'][2 * p], d['q'][2 * p + 1])], axis=0).astype(BF16),
                jnp.concatenate([lanes2(s[2 * p], zs), lanes2(zs, s[2 * p + 1])], axis=0).astype(BF16))
                for p in pairs]
            bdw = [jnp.concatenate([lanes2(sl[p][:, :LANES] - wq_s[p][:c, :LANES], zw),
                                    lanes2(zw, sl[p][:, 2 * LANES:3 * LANES] - wq_s[p][:c, LANES:])],
                                   axis=0).astype(BF16) for p in pairs]
            aw = [_dot(at[p], bdw[p]) for p in pairs]
            kw = [_dot_tn(jnp.concatenate([d['k'][h] * col(d['egl_all'], SMALL_A0 + h)
                                           for h in (2 * p, 2 * p + 1)], axis=0).astype(BF16), bdw[p])
                  for p in pairs]
            for h in heads:
                p, half = h // 2, slice((h % 2) * LANES, (h % 2 + 1) * LANES)
                s_ref[0, h] = col(d['eg_last_all'], SMALL_A0 + h) * s[h] + kw[p][:, half]
            for h in heads:
                p, half = h // 2, slice((h % 2) * LANES, (h % 2 + 1) * LANES)
                o = d['eg'][h] * wq_s[p][c:, half] + aw[p][:, half]
                z = z_ref[d['rows'], h * GDN_DV:(h + 1) * GDN_DV]
                y = _rms(o, nrm) * _silu(z)
                y_ref[d['rows'], h * GDN_DV:(h + 1) * GDN_DV] = y.astype(y_ref.dtype)
        return carry

    lax.fori_loop(0, nchunk // nc, body, 0)


def _gdn(q, k, v, z, p_small, alog_row, dtb_row, nrm, s0, t, tb):
    m = q.shape[0]
    b = m // t
    c = min(CHUNK, t)
    nt = t // tb
    st_spec = pl.BlockSpec((1, GDN_H, GDN_DK, GDN_DV), lambda bi, ti: (bi, 0, 0, 0))
    const = lambda bi, ti: (0, 0)
    row = lambda bi, ti: (bi * nt + ti, 0)
    qk_spec = pl.BlockSpec((tb, GDN_H * GDN_DK), row)
    v_spec = pl.BlockSpec((tb, GDN_H * GDN_DV), row)
    return pl.pallas_call(
        functools.partial(_gdn_kernel, c=c),
        grid=(b, nt),
        in_specs=[qk_spec, qk_spec, v_spec, v_spec,
                  pl.BlockSpec((tb, LANES), row),
                  pl.BlockSpec((1, LANES), const),
                  pl.BlockSpec((1, LANES), const),
                  pl.BlockSpec((1, GDN_DV), const),
                  st_spec],
        out_specs=[v_spec, st_spec],
        out_shape=[jax.ShapeDtypeStruct((m, GDN_H * GDN_DV), BF16),
                   jax.ShapeDtypeStruct((b, GDN_H, GDN_DK, GDN_DV), F32)],
        compiler_params=_cp("parallel", "arbitrary"),
        name="gated_deltanet",
    )(q, k, v, z, p_small, alog_row, dtb_row, nrm, s0)


def _pad_rows_front(buf, rows):
    return jnp.pad(buf, ((0, 0), (0, 0), (rows - buf.shape[2], 0), (0, 0)))


def _in_proj_windows(d):
    sizes = dict(ret=4 * RET_H * RET_DK, gla=2 * GLA_H * GLA_DK + 2 * GLA_H * GLA_DV, lr=GLA_RANK,
                 gdn=2 * GDN_H * GDN_DK + 2 * GDN_H * GDN_DV, ab=2 * GDN_H, gbr=N_BRANCH * d)
    win, o = {}, 0
    for name in ('ret', 'gla', 'lr', 'gdn', 'ab', 'gbr'):
        win[name] = (o, sizes[name])
        o += sizes[name]
    assert win['lr'][0] % LANES == 0 and win['ab'][0] % LANES == SMALL_A0
    return win


def _prep_weights(w):
    lane_row = lambda v, at: jnp.pad(v, ((0, 0), (at, LANES - at - v.shape[1])))[:, None, :]
    return dict(
        w_in_t=jnp.swapaxes(w['w_in'], 1, 2), w_xq=w['w_xq'], w_ffn_in=w['w_ffn_in'],
        w_gk=jnp.pad(w['w_gla_gk'], ((0, 0), (0, LANES - GLA_RANK), (0, 0))).astype(BF16),
        b_gk=w['b_gla_gk'][:, None, :],
        gla_norm=w['gla_norm'][:, None, :], gdn_norm=w['gdn_norm'][:, None, :],
        alog_row=lane_row(w['gdn_a_log'], SMALL_A0), dtb_row=lane_row(w['gdn_dt_bias'], SMALL_A0),
        gdn_conv_w=w['gdn_conv_w'], b_gate=w['b_gate'],
        w_branch=w['w_branch'].astype(BF16), w_out=w['w_out'].astype(BF16), w_xo=w['w_xo'].astype(BF16),
        ffn_conv_w=w['ffn_conv_w'], ffn_conv_b=w['ffn_conv_b'], w_ffn_out=w['w_ffn_out'].astype(BF16),
        ln_mix=w['ln_mix'], ln_xattn=w['ln_xattn'], ln_ffn=w['ln_ffn'], ln_final=w['ln_final'],
    )


def _tile(n, cap):
    t = min(n, cap)
    while n % t:
        t -= 1
    return t


class _Group:
    def __init__(self, x, offset, mem_k, mem_v, s_ret, s_gla, s_gdn, buf_gdn, buf_ffn):
        self.b, self.t, d = x.shape
        b, t = self.b, self.t
        self.m = m = b * t
        self.x2 = x.reshape(m, d)
        self.mem_k, self.mem_v = mem_k, mem_v
        self.s_ret, self.s_gla, self.s_gdn = s_ret, s_gla, s_gdn
        self.buf_gdn8 = _pad_rows_front(buf_gdn, SUBLANES)
        self.buf_ffn8 = _pad_rows_front(buf_ffn, SUBLANES)
        self.tm = _tile(m, MM_TM)
        self.tm_res = _tile(m, RES_TM)
        self.tm_res_k = _tile(m, RES_TM_K)
        self.tm_ffo = _tile(m, FFO_TM)
        self.tb = _tile(t, SEQ_TB)
        self.tb_gdn = _tile(t, SEQ_TB_GDN)
        self.tq = _tile(t, XATTN_TQ)
        self.tm_ffn = _tile(t, FFN_TM) if t >= FFN_TM else t * _tile(b, max(1, FFN_TM // t))
        pos = offset + jnp.arange(t, dtype=F32)
        half = RET_DK // 2
        freqs = 1.0 / (ROPE_BASE ** (jnp.arange(half, dtype=F32) / half))
        ang = pos[:, None] * freqs[None, :]
        self.cos, self.sin = jnp.cos(ang), jnp.sin(ang)
        self.outs = ([], [], [], [], [])
        self.hn = self.y = None


def _trunks(groups, pw):
    d = groups[0].x2.shape[1]
    depth = groups[0].s_ret.shape[0]
    ld = jnp.log1p(-jnp.exp2(-5.0 - jnp.arange(RET_H, dtype=F32)))
    win = _in_proj_windows(d)
    tms = tuple(g.tm for g in groups)
    order = sorted(range(len(groups)), key=lambda gi: groups[gi].m)

    def shared(acts, w, l, off, n, out_dtype, name, transposed):
        blk0 = off // MM_TN
        shift = off - blk0 * MM_TN
        assert shift < LANES and n % MM_TN == 0
        outs = _mm_ws(tuple(acts[gi] for gi in order), w, l, blk0, n, shift, out_dtype,
                      tuple(tms[gi] for gi in order), MM_TN, name, transposed=transposed)
        return [outs[order.index(gi)] for gi in range(len(acts))]

    for g in groups:
        g.hn = _norm(g.x2, pw['ln_mix'][0], g.tm_res_k)
    for l in range(depth):
        hns = [g.hn for g in groups]
        hq, hv = GDN_H * GDN_DK, GDN_H * GDN_DV
        g_off = win['gdn'][0]
        p_ret = shared(hns, pw['w_in_t'], l, *win['ret'], F32, "in_proj_ret", True)
        p_gla = shared(hns, pw['w_in_t'], l, *win['gla'], F32, "in_proj_gla", True)
        d_z = shared(hns, pw['w_in_t'], l, g_off + 2 * hq + hv, hv, F32, "in_proj_gdn_z", True)
        g_br = shared(hns, pw['w_in_t'], l, *win['gbr'], BF16, "in_proj_gbr", True)
        def conv(off, n, coff, head_dim, scale, name):
            r = _conv_proj([hns[gi] for gi in order], pw['w_in_t'], l, off, n, pw['gdn_conv_w'][l],
                           [groups[gi].buf_gdn8[l] for gi in order], coff, [groups[gi].t for gi in order],
                           [tms[gi] for gi in order], MM_TN, head_dim, scale, name)
            return [r[order.index(gi)] for gi in range(len(groups))]

        cq = conv(g_off, hq, 0, GDN_DK, GDN_DK ** -0.5, "in_proj_gdn_q")
        ck = conv(g_off + hq, hq, hq, GDN_DK, 1.0, "in_proj_gdn_k")
        cv = conv(g_off + 2 * hq, hv, 2 * hq, GDN_DV, None, "in_proj_gdn_v")
        ys_s = []
        for gi, g in enumerate(groups):
            t, tm, hn = g.t, g.tm, g.hn
            (d_q, tq8), (d_k, tk8), (d_v, tv8) = cq[gi], ck[gi], cv[gi]
            bg8 = jnp.concatenate([tq8, tk8, tv8], axis=-1)
            p_small = _small_proj(hn, pw['w_in_t'], l, win['lr'][0] // LANES, win['ab'][0] // LANES, tm)

            y_ret, sr = _retention(p_ret[gi], g.cos, g.sin, ld, g.s_ret[l], t, g.tb)
            y_gla, sg = _gla(p_gla[gi], p_small, pw['w_gk'][l], pw['b_gk'][l], pw['gla_norm'][l],
                             g.s_gla[l], t, g.tb)
            y_gdn, sd = _gdn(d_q, d_k, d_v, d_z[gi], p_small, pw['alog_row'][l], pw['dtb_row'][l],
                             pw['gdn_norm'][l], g.s_gdn[l], t, g.tb_gdn)

            ys_s.append((y_ret, y_gla, y_gdn))
            for lst, val in zip(g.outs[:4], (sr, sg, sd, bg8[:, SUBLANES - (GDN_CONV - 1):])):
                lst.append(val)

        xs = lambda: [g.x2 for g in groups]
        merged = _merge(ys_s, pw['w_branch'][l], g_br, pw['b_gate'][l], [g.tm_res_k for g in groups])
        res = _proj_res(merged, pw['w_out'][l], xs(), pw['ln_xattn'][l], [g.tm_res for g in groups], ROW_SUB,
                        False, "out_proj")
        for g, (x2, _) in zip(groups, res):
            g.x2 = x2
        qs = shared([hx for _, hx in res], pw['w_xq'], l, 0, d, BF16, "xattn_q", False)
        os_ = [_xattn(q, g.mem_k, g.mem_v, l, g.t, g.tq) for q, g in zip(qs, groups)]
        res = _proj_res(os_, pw['w_xo'][l], xs(), pw['ln_ffn'][l], [g.tm_res for g in groups], ROW_SUB,
                        False, "xattn_out")
        for g, (x2, _) in zip(groups, res):
            g.x2 = x2
        ff = _ffn_in([res[gi][1] for gi in order], pw['w_ffn_in'], l, pw['ffn_conv_w'][l], pw['ffn_conv_b'][l],
                     [groups[gi].buf_ffn8[l] for gi in order], [groups[gi].t for gi in order],
                     [groups[gi].tm_ffn for gi in order], FFN_TN)
        acts = [ff[order.index(gi)][0] for gi in range(len(groups))]
        for gi, g in enumerate(groups):
            g.outs[4].append(ff[order.index(gi)][1][:, SUBLANES - (FFN_CONV - 1):])
        last = l + 1 == depth
        res = _proj_res(acts, pw['w_ffn_out'][l], xs(), pw['ln_final'] if last else pw['ln_mix'][l + 1],
                        [g.tm_ffo for g in groups], ROW_SUB, last, "ffn_out_final" if last else "ffn_out")
        for g, r in zip(groups, res):
            if last:
                g.y, = r
            else:
                g.x2, g.hn = r
    return [(g.y.reshape(g.b, g.t, d),) + tuple(jnp.stack(lst) for lst in g.outs) for g in groups]


def kernel(x_prompt, x_sample, mem_prompt, state_ret, state_gla, state_gdn, state_gdn_conv, state_ffn_conv, cache_mem_k, cache_mem_v, ln_mix, w_in, w_gla_gk, b_gla_gk, gla_norm, gdn_conv_w, gdn_a_log, gdn_dt_bias, gdn_norm, b_gate, w_branch, w_out, ln_xattn, ln_mem, w_xq, w_xkv, w_xo, ln_ffn, w_ffn_in, ffn_conv_w, ffn_conv_b, w_ffn_out, ln_final):
    pw = _prep_weights(dict(
        w_in=w_in, w_gla_gk=w_gla_gk, b_gla_gk=b_gla_gk, gla_norm=gla_norm, gdn_conv_w=gdn_conv_w,
        gdn_a_log=gdn_a_log, gdn_dt_bias=gdn_dt_bias, gdn_norm=gdn_norm, b_gate=b_gate, w_branch=w_branch,
        w_out=w_out, w_xq=w_xq, w_xo=w_xo, w_ffn_in=w_ffn_in, ffn_conv_w=ffn_conv_w, ffn_conv_b=ffn_conv_b,
        w_ffn_out=w_ffn_out, ln_mix=ln_mix, ln_xattn=ln_xattn, ln_ffn=ln_ffn, ln_final=ln_final))
    depth = w_in.shape[0]
    bp, nmem, d = mem_prompt.shape
    dt = x_prompt.dtype

    mem2 = mem_prompt.reshape(bp * nmem, d)
    mem_k_p, mem_kb = _mem_proj(mem2, ln_mem, w_xkv, 0, bp, nmem, "mem_k")
    mem_v_p, mem_vb = _mem_proj(mem2, ln_mem, w_xkv, 1, bp, nmem, "mem_v")

    zeros = lambda *s: jnp.zeros((depth, bp) + s, dt)
    prompt = _Group(
        x_prompt, 0.0, mem_kb, mem_vb,
        zeros(RET_H, RET_DK, RET_DV), zeros(GLA_H, GLA_DK, GLA_DV), zeros(GDN_H, GDN_DK, GDN_DV),
        zeros(GDN_CONV - 1, state_gdn_conv.shape[-1]), zeros(FFN_CONV - 1, state_ffn_conv.shape[-1]))
    past_len = 4096.0
    sample = _Group(x_sample, past_len, cache_mem_k, cache_mem_v,
                    state_ret, state_gla, state_gdn, state_gdn_conv, state_ffn_conv)
    ((y_p, ret_p, gla_p, gdn_p, gdn_conv_p, ffn_conv_p),
     (y_s, ret_s, gla_s, gdn_s, gdn_conv_s, ffn_conv_s)) = _trunks([prompt, sample], pw)

    return (y_p, y_s, ret_p, gla_p, gdn_p, gdn_conv_p, ffn_conv_p, mem_k_p, mem_v_p,
            ret_s, gla_s, gdn_s, gdn_conv_s, ffn_conv_s)
```

```python
import functools

import jax
import jax.numpy as jnp
from jax import lax
from jax.experimental import pallas as pl
from jax.experimental.pallas import tpu as pltpu

F32 = jnp.float32
BF16 = jnp.bfloat16

EPS = 1e-6
CHUNK = 64
RET_H, RET_DK, RET_DV = 4, 256, 256
GLA_H, GLA_DK, GLA_DV = 4, 128, 256
GLA_RANK = 16
GLA_NORMALIZER = 16.0
GDN_H, GDN_DK, GDN_DV = 8, 128, 128
GDN_CONV = 4
N_BRANCH = 3
X_H = 4
FFN_CONV = 3
ROPE_BASE = 10000.0
PAST_LEN = 4096

LANES = 128
SUBLANES = 8
SMALL_A0 = GLA_RANK
SMALL_B0 = GLA_RANK + GDN_H
VMEM_LIMIT = 52 * 1024 * 1024
MM_TM = 1024
MM_TN = 1024
RES_TM = 512
RES_TM_K = 512
FFO_TM = 256
ROW_SUB = 256
CONV_SUB = 256
SEQ_TB = 512
SEQ_TB_GDN = 256
GDN_CHUNKS_PER_STEP = 4
GLA_CHUNKS_PER_STEP = 4
RET_CHUNKS_PER_STEP = 2
XATTN_TQ = 512
FFN_TM = 1024
FFN_TN = 512
FFN_SUB = 256


def _cp(*sem):
    return pltpu.CompilerParams(dimension_semantics=sem, vmem_limit_bytes=VMEM_LIMIT)


def _dot(a, b):
    return jnp.dot(a, b, preferred_element_type=F32)


def _dot_nt(a, b):
    return lax.dot_general(a, b, (((1,), (1,)), ((), ())), preferred_element_type=F32)


def _dot_tn(a, b):
    return lax.dot_general(a, b, (((0,), (0,)), ((), ())), preferred_element_type=F32)


def _sigmoid(x):
    return 1.0 / (1.0 + jnp.exp(-x))


def _silu(x):
    return x * _sigmoid(x)


def _softplus(x):
    return jnp.maximum(x, 0.0) + jnp.log1p(jnp.exp(-jnp.abs(x)))


def _rms(x, g=None):
    y = x * lax.rsqrt(jnp.mean(x * x, axis=-1, keepdims=True) + EPS)
    return y if g is None else y * g


def _split3(x):
    hi = x.astype(BF16)
    r = x - hi.astype(F32)
    mid = r.astype(BF16)
    lo = (r - mid.astype(F32)).astype(BF16)
    return hi, mid, lo


def _tri_cumsum(tri, x):
    hi, mid, lo = _split3(x)
    return _dot(tri, hi) + _dot(tri, mid) + _dot(tri, lo)


def _norm_kernel(x_ref, g_ref, o_ref):
    o_ref[...] = _rms(x_ref[...], g_ref[...]).astype(o_ref.dtype)


def _norm(x, g, tm):
    m, d = x.shape
    return pl.pallas_call(
        _norm_kernel,
        grid=(m // tm,),
        in_specs=[pl.BlockSpec((tm, d), lambda i: (i, 0)),
                  pl.BlockSpec((1, d), lambda i: (0, 0))],
        out_specs=pl.BlockSpec((tm, d), lambda i: (i, 0)),
        out_shape=jax.ShapeDtypeStruct((m, d), BF16),
        compiler_params=_cp("parallel"),
        name="rmsnorm",
    )(x, g.reshape(1, d))


CAST_ROWS = 256


def _cast_weight_tile(w_ref, wx_ref, wb_ref, shift):
    kk, tn = wb_ref.shape
    for r in range(0, kk, CAST_ROWS):
        rows = slice(r, min(r + CAST_ROWS, kk))
        if shift == 0:
            wb_ref[rows, :] = w_ref[0, rows, :].astype(BF16)
        else:
            wide = jnp.concatenate([w_ref[0, rows, :], wx_ref[0, rows, :]], axis=1)
            wb_ref[rows, :] = pltpu.roll(wide, wide.shape[1] - shift, 1)[:, :tn].astype(BF16)


def _cast_weight_tile_t(w_ref, wx_ref, wb_ref, shift):
    tn, _ = wb_ref.shape
    for r in range(0, tn, CAST_ROWS):
        lo, hi = r + shift, min(r + CAST_ROWS, tn) + shift
        if hi <= tn:
            src = w_ref[0, lo:hi, :]
        else:
            src = jnp.concatenate([w_ref[0, lo:tn, :], wx_ref[0, 0:hi - tn, :]], axis=0)
        wb_ref[r:min(r + CAST_ROWS, tn), :] = src.astype(BF16)


def _group_steps(ms, tms):
    nis = [m // t for m, t in zip(ms, tms)]
    starts = [0]
    for n in nis:
        starts.append(starts[-1] + n)

    def idx(g):
        return lambda i: jnp.clip(i - starts[g], 0, nis[g] - 1)

    return starts, [idx(g) for g in range(len(ms))]


def _on_group_turn(i, starts, g, fn):
    if len(starts) == 2:
        fn()
    else:
        pl.when(jnp.logical_and(i >= starts[g], i < starts[g + 1]))(fn)


def _mm_ws_kernel(*refs, shift, transposed, starts):
    ng = len(starts) - 1
    a_refs, w_ref, rest = refs[:ng], refs[ng], refs[ng + 1:]
    wx_ref = None
    if shift:
        wx_ref, rest = rest[0], rest[1:]
    o_refs, wb_ref = rest[:ng], rest[ng]
    i = pl.program_id(1)

    @pl.when(i == 0)
    def _():
        (_cast_weight_tile_t if transposed else _cast_weight_tile)(w_ref, wx_ref, wb_ref, shift)

    mm = _dot_nt if transposed else _dot
    for g in range(ng):
        def run(a_ref=a_refs[g], o_ref=o_refs[g]):
            o_ref[...] = mm(a_ref[...], wb_ref[...]).astype(o_ref.dtype)
        _on_group_turn(i, starts, g, run)


def _mm_ws(a, w, layer, blk0, n, shift, out_dtype, tm, tn, name, transposed=False):
    single = not isinstance(a, (tuple, list))
    a_s, tms = ((a,), (tm,)) if single else (tuple(a), tuple(tm))
    k = a_s[0].shape[1]
    nj = n // tn
    per = tn // LANES
    starts, idx = _group_steps([x.shape[0] for x in a_s], tms)
    if transposed:
        assert shift % SUBLANES == 0
        w_spec = pl.BlockSpec((1, tn, k), lambda j, i: (layer, blk0 + j, 0))
        wx_spec = pl.BlockSpec((1, LANES, k), lambda j, i: (layer, (blk0 + j + 1) * per, 0))
    else:
        w_spec = pl.BlockSpec((1, k, tn), lambda j, i: (layer, 0, blk0 + j))
        wx_spec = pl.BlockSpec((1, k, LANES), lambda j, i: (layer, 0, (blk0 + j + 1) * per))
    in_specs = [pl.BlockSpec((t_, k), functools.partial(lambda j, i, f: (f(i), 0), f=f))
                for t_, f in zip(tms, idx)] + [w_spec]
    args = list(a_s) + [w]
    if shift:
        in_specs.append(wx_spec)
        args.append(w)
    outs = pl.pallas_call(
        functools.partial(_mm_ws_kernel, shift=shift, transposed=transposed, starts=tuple(starts)),
        grid=(nj, starts[-1]),
        in_specs=in_specs,
        out_specs=[pl.BlockSpec((t_, tn), functools.partial(lambda j, i, f: (f(i), j), f=f))
                   for t_, f in zip(tms, idx)],
        out_shape=[jax.ShapeDtypeStruct((x.shape[0], n), out_dtype) for x in a_s],
        scratch_shapes=[pltpu.VMEM((tn, k) if transposed else (k, tn), BF16)],
        compiler_params=_cp("parallel", "arbitrary"),
        name=name,
    )(*args)
    return outs[0] if single else tuple(outs)


def _small_proj_kernel(a_ref, w0_ref, w1_ref, o_ref):
    a = a_ref[...]
    p0 = _dot_nt(a, w0_ref[0].astype(BF16))
    p1 = _dot_nt(a, w1_ref[0].astype(BF16))
    lane = lax.broadcasted_iota(jnp.int32, p0.shape, 1)
    o_ref[...] = jnp.where(lane < SMALL_A0, p0, p1)


def _small_proj(a, w_t, layer, blk_lr, blk_ab, tm):
    m, k = a.shape
    return pl.pallas_call(
        _small_proj_kernel,
        grid=(m // tm,),
        in_specs=[pl.BlockSpec((tm, k), lambda i: (i, 0)),
                  pl.BlockSpec((1, LANES, k), lambda i: (layer, blk_lr, 0)),
                  pl.BlockSpec((1, LANES, k), lambda i: (layer, blk_ab, 0))],
        out_specs=pl.BlockSpec((tm, LANES), lambda i: (i, 0)),
        out_shape=jax.ShapeDtypeStruct((m, LANES), F32),
        compiler_params=_cp("parallel"),
        name="in_proj_small",
    )(a, w_t, w_t)


def _shift_rows(x, prev8, s):
    xs = pltpu.roll(x, s, 0)
    ps = pltpu.roll(prev8, s, 0)
    r8 = lax.broadcasted_iota(jnp.int32, prev8.shape, 0)
    top = jnp.where(r8 < s, ps, xs[:SUBLANES])
    if x.shape[0] == SUBLANES:
        return top
    return jnp.concatenate([top, xs[SUBLANES:]], axis=0)


def _causal_conv(x, prev8, taps):
    width = len(taps)
    y = None
    for i, w in enumerate(taps):
        s = width - 1 - i
        xs = x if s == 0 else _shift_rows(x, prev8, s)
        y = xs * w if y is None else y + xs * w
    return y


def _seq_tiling(t, tm):
    return (t // tm, None, 1) if t >= tm else (1, t, tm // t)


def _conv_proj_kernel(*refs, shift, starts, tilings, head_dim, norm_scale):
    ng = len(starts) - 1
    w_ref = refs[2 * ng]
    rest = refs[2 * ng + 1:]
    wx_ref = None
    if shift:
        wx_ref, rest = rest[0], rest[1:]
    cw_ref = rest[0]
    outs = rest[1:2 * ng + 1]
    wb_ref = rest[2 * ng + 1]
    carries = rest[2 * ng + 2:]
    i = pl.program_id(1)

    @pl.when(i == 0)
    def _():
        _cast_weight_tile_t(w_ref, wx_ref, wb_ref, shift)

    for gi in range(ng):
        def run(a_ref=refs[2 * gi], buf_ref=refs[2 * gi + 1], o_ref=outs[2 * gi], tail_ref=outs[2 * gi + 1],
                carry_ref=carries[gi], tiles_per_seq=tilings[gi][0], seq_rows=tilings[gi][1], lo_step=starts[gi]):
            tm, tn = o_ref.shape
            sub = min(tm, CONV_SUB)
            seg = sub if seq_rows is None else seq_rows
            if seq_rows is None:
                @pl.when((i - lo_step) % tiles_per_seq == 0)
                def _():
                    carry_ref[...] = buf_ref[0]

                prev = carry_ref[...]
            wb = wb_ref[...]
            taps = [cw_ref[n:n + 1, :] for n in range(cw_ref.shape[0])]
            for r in range(tm // sub):
                p_all = _dot_nt(a_ref[r * sub:(r + 1) * sub, :], wb)
                for s in range(sub // seg):
                    lo = r * sub + s * seg
                    x = p_all[s * seg:(s + 1) * seg]
                    if seq_rows is not None:
                        prev = buf_ref[lo // seg]
                    for h in range(tn // head_dim):
                        hs = slice(h * head_dim, (h + 1) * head_dim)
                        yh = _silu(_causal_conv(x[:, hs], prev[:, hs], [w[:, hs] for w in taps]))
                        if norm_scale is not None:
                            yh = yh * lax.rsqrt(jnp.sum(yh * yh, axis=-1, keepdims=True) + EPS)
                            if norm_scale != 1.0:
                                yh = yh * norm_scale
                        o_ref[lo:lo + seg, hs] = yh
                    prev = x[seg - SUBLANES:, :]
                    if seq_rows is not None:
                        tail_ref[lo // seg] = prev
            if seq_rows is None:
                carry_ref[...] = prev
                tail_ref[0] = prev
        _on_group_turn(i, starts, gi, run)


def _conv_proj(a_s, w_t, layer, off, n, conv_w, buf8_s, coff, t_s, tm_s, tn, head_dim, norm_scale, name):
    k = a_s[0].shape[1]
    nj = n // tn
    blk0 = off // tn
    shift = off - blk0 * tn
    assert shift < LANES and shift % SUBLANES == 0 and n % tn == 0 and coff % tn == 0
    cblk = coff // tn
    per = tn // LANES
    starts, idx = _group_steps([a.shape[0] for a in a_s], tm_s)
    tilings = [_seq_tiling(t, tm) for t, tm in zip(t_s, tm_s)]
    in_specs, args, out_specs, out_shape, scratch = [], [], [], [], []
    for a, buf8, tm, f, (tps, _, nb) in zip(a_s, buf8_s, tm_s, idx, tilings):
        ni = a.shape[0] // tm
        in_specs += [pl.BlockSpec((tm, k), functools.partial(lambda j, i, f: (f(i), 0), f=f)),
                     pl.BlockSpec((nb, SUBLANES, tn),
                                  functools.partial(lambda j, i, f, tps: (f(i) // tps, 0, cblk + j), f=f, tps=tps))]
        args += [a, buf8]
        out_specs += [pl.BlockSpec((tm, tn), functools.partial(lambda j, i, f: (f(i), j), f=f)),
                      pl.BlockSpec((nb, SUBLANES, tn), functools.partial(lambda j, i, f: (f(i), 0, j), f=f))]
        out_shape += [jax.ShapeDtypeStruct((a.shape[0], n), F32),
                      jax.ShapeDtypeStruct((ni * nb, SUBLANES, n), F32)]
        scratch.append(pltpu.VMEM((SUBLANES, tn), F32))
    in_specs.append(pl.BlockSpec((1, tn, k), lambda j, i: (layer, blk0 + j, 0)))
    args.append(w_t)
    if shift:
        in_specs.append(pl.BlockSpec((1, LANES, k), lambda j, i: (layer, (blk0 + j + 1) * per, 0)))
        args.append(w_t)
    in_specs.append(pl.BlockSpec((conv_w.shape[0], tn), lambda j, i: (0, cblk + j)))
    args.append(conv_w)
    outs = pl.pallas_call(
        functools.partial(_conv_proj_kernel, shift=shift, starts=tuple(starts), tilings=tuple(tilings),
                          head_dim=head_dim, norm_scale=norm_scale),
        grid=(nj, starts[-1]),
        in_specs=in_specs,
        out_specs=out_specs,
        out_shape=out_shape,
        scratch_shapes=[pltpu.VMEM((tn, k), BF16)] + scratch,
        compiler_params=_cp("parallel", "arbitrary"),
        name=name,
    )(*args)
    return [(outs[2 * gi], outs[2 * gi + 1][tilings[gi][0] - 1::tilings[gi][0]]) for gi in range(len(a_s))]


def _proj_res_kernel(*refs, sub, final, starts):
    ng = len(starts) - 1
    nout = 1 if final else 2
    a_refs, x_refs = refs[0:2 * ng:2], refs[1:2 * ng:2]
    w_ref, g_ref = refs[2 * ng], refs[2 * ng + 1]
    outs = refs[2 * ng + 2:]
    i = pl.program_id(0)
    for gi in range(ng):
        def run(a_ref=a_refs[gi], x_ref=x_refs[gi], o=outs[gi * nout:(gi + 1) * nout]):
            tm = a_ref.shape[0]
            sb = min(sub, tm)
            w = w_ref[...]
            g = g_ref[...]
            for r in range(tm // sb):
                rows = slice(r * sb, (r + 1) * sb)
                xn = x_ref[rows, :] + _dot(a_ref[rows, :], w)
                if final:
                    o[0][rows, :] = _rms(xn, g)
                else:
                    o[0][rows, :] = xn
                    o[1][rows, :] = _rms(xn, g).astype(o[1].dtype)
        _on_group_turn(i, starts, gi, run)


def _proj_res(a_s, w, x_s, g, tms, sub, final, name):
    kk, d = w.shape
    starts, idx = _group_steps([a.shape[0] for a in a_s], tms)
    in_specs, args, out_specs, out_shape = [], [], [], []
    for a, x, tm, f in zip(a_s, x_s, tms, idx):
        row = functools.partial(lambda i, f: (f(i), 0), f=f)
        in_specs += [pl.BlockSpec((tm, kk), row), pl.BlockSpec((tm, d), row)]
        args += [a, x]
        out_specs.append(pl.BlockSpec((tm, d), row))
        out_shape.append(jax.ShapeDtypeStruct((a.shape[0], d), F32))
        if not final:
            out_specs.append(pl.BlockSpec((tm, d), row))
            out_shape.append(jax.ShapeDtypeStruct((a.shape[0], d), BF16))
    outs = pl.pallas_call(
        functools.partial(_proj_res_kernel, sub=sub, final=final, starts=tuple(starts)),
        grid=(starts[-1],),
        in_specs=in_specs + [pl.BlockSpec((kk, d), lambda i: (0, 0), pipeline_mode=pl.Buffered(1)),
                             pl.BlockSpec((1, d), lambda i: (0, 0))],
        out_specs=out_specs,
        out_shape=out_shape,
        compiler_params=_cp("arbitrary"),
        name=name,
    )(*args, w, g.reshape(1, d))
    nout = 1 if final else 2
    return [tuple(outs[gi * nout:(gi + 1) * nout]) for gi in range(len(a_s))]


def _merge_kernel(*refs, sub, starts):
    ng = len(starts) - 1
    per = N_BRANCH + 1
    w_ref, b_ref = refs[per * ng], refs[per * ng + 1]
    o_refs = refs[per * ng + 2:]
    i = pl.program_id(0)
    for gi in range(ng):
        def run(y_refs=refs[gi * per:gi * per + N_BRANCH], g_ref=refs[gi * per + N_BRANCH], o_ref=o_refs[gi]):
            tm, d = o_ref.shape
            sb = min(sub, tm)
            for r in range(tm // sb):
                rows = slice(r * sb, (r + 1) * sb)
                acc = None
                for n, y_ref in enumerate(y_refs):
                    p = _dot(y_ref[rows, :], w_ref[n])
                    g = g_ref[rows, n * d:(n + 1) * d].astype(F32)
                    t = _sigmoid(g + b_ref[n:n + 1, :]) * p
                    acc = t if acc is None else acc + t
                o_ref[rows, :] = acc.astype(o_ref.dtype)
        _on_group_turn(i, starts, gi, run)


def _merge(ys_s, w_branch, g_br_s, b_gate, tms):
    _, bw, d = w_branch.shape
    starts, idx = _group_steps([g.shape[0] for g in g_br_s], tms)
    in_specs, args, out_specs, out_shape = [], [], [], []
    for ys, g_br, tm, f in zip(ys_s, g_br_s, tms, idx):
        row = functools.partial(lambda i, f: (f(i), 0), f=f)
        in_specs += [pl.BlockSpec((tm, bw), row)] * N_BRANCH + [pl.BlockSpec((tm, N_BRANCH * d), row)]
        args += list(ys) + [g_br]
        out_specs.append(pl.BlockSpec((tm, d), row))
        out_shape.append(jax.ShapeDtypeStruct((g_br.shape[0], d), BF16))
    return pl.pallas_call(
        functools.partial(_merge_kernel, sub=ROW_SUB, starts=tuple(starts)),
        grid=(starts[-1],),
        in_specs=in_specs + [pl.BlockSpec((N_BRANCH, bw, d), lambda i: (0, 0, 0), pipeline_mode=pl.Buffered(1)),
                             pl.BlockSpec((N_BRANCH, d), lambda i: (0, 0))],
        out_specs=out_specs,
        out_shape=out_shape,
        compiler_params=_cp("arbitrary"),
        name="branch_merge",
    )(*args, w_branch, b_gate)


def _xattn_kernel(q_ref, k_ref, v_ref, o_ref, *scratch, scale, dh):
    heads = range(q_ref.shape[1] // dh)
    hs = lambda h: slice(h * dh, (h + 1) * dh)
    if scratch:
        kb_ref, vb_ref = scratch

        @pl.when(pl.program_id(1) == 0)
        def _():
            for h in heads:
                kb_ref[:, hs(h)] = k_ref[0, 0, :, h, :].astype(BF16)
                vb_ref[:, hs(h)] = v_ref[0, 0, :, h, :].astype(BF16)
    else:
        kb_ref, vb_ref = k_ref.at[0], v_ref.at[0]

    s = [_dot_nt(q_ref[:, hs(h)], kb_ref[:, hs(h)]) * scale for h in heads]
    e = [jnp.exp(x - jnp.max(x, axis=-1, keepdims=True)) for x in s]
    p = [(x / jnp.sum(x, axis=-1, keepdims=True)).astype(BF16) for x in e]
    for h in heads:
        o_ref[:, hs(h)] = _dot(p[h], vb_ref[:, hs(h)]).astype(o_ref.dtype)


def _xattn(q, mem_k, mem_v, layer, t, tq):
    m, d = q.shape
    b = m // t
    dh = d // X_H
    nq = t // tq
    if mem_k.ndim == 5:
        nmem = mem_k.shape[2]
        kv_spec = pl.BlockSpec((1, 1, nmem, X_H, dh), lambda bi, i: (layer, bi, 0, 0, 0))
        scratch = [pltpu.VMEM((nmem, d), BF16), pltpu.VMEM((nmem, d), BF16)]
    else:
        nmem = mem_k.shape[1] // b
        kv_spec = pl.BlockSpec((1, nmem, d), lambda bi, i: (layer, bi, 0))
        scratch = []
    qo_spec = pl.BlockSpec((tq, d), lambda bi, i: (bi * nq + i, 0))
    return pl.pallas_call(
        functools.partial(_xattn_kernel, scale=dh ** -0.5, dh=dh),
        grid=(b, nq),
        in_specs=[qo_spec, kv_spec, kv_spec],
        out_specs=qo_spec,
        out_shape=jax.ShapeDtypeStruct((m, d), BF16),
        scratch_shapes=scratch,
        compiler_params=_cp("parallel", "arbitrary"),
        name="xattn",
    )(q, mem_k, mem_v)


def _mem_proj_kernel(x_ref, g_ref, w_ref, o_ref, ob_ref, wb_ref):
    @pl.when(pl.program_id(1) == 0)
    def _():
        _cast_weight_tile(w_ref, None, wb_ref, 0)

    _, nb, nmem, nh, dh = o_ref.shape
    res = _dot(_rms(x_ref[...], g_ref[0]).astype(BF16), wb_ref[...])
    ob_ref[0] = res.astype(BF16)
    for bb in range(nb):
        for h in range(nh):
            o_ref[0, bb, :, h, :] = res[bb * nmem:(bb + 1) * nmem, h * dh:(h + 1) * dh]


def _mem_proj(mem2, ln_mem, w_xkv, half, bsz, nmem, name):
    depth, d, _ = w_xkv.shape
    dh = d // X_H
    nb = 1
    return pl.pallas_call(
        _mem_proj_kernel,
        grid=(depth, bsz // nb),
        in_specs=[pl.BlockSpec((nb * nmem, d), lambda l, i: (i, 0)),
                  pl.BlockSpec((1, 1, d), lambda l, i: (l, 0, 0)),
                  pl.BlockSpec((1, d, d), lambda l, i: (l, 0, half), pipeline_mode=pl.Buffered(1))],
        out_specs=[pl.BlockSpec((1, nb, nmem, X_H, dh), lambda l, i: (l, i, 0, 0, 0)),
                   pl.BlockSpec((1, nb * nmem, d), lambda l, i: (l, i, 0))],
        out_shape=[jax.ShapeDtypeStruct((depth, bsz, nmem, X_H, dh), F32),
                   jax.ShapeDtypeStruct((depth, bsz * nmem, d), BF16)],
        scratch_shapes=[pltpu.VMEM((d, d), BF16)],
        compiler_params=_cp("arbitrary", "arbitrary"),
        name=name,
    )(mem2, ln_mem[:, None, :], w_xkv)


def _gelu_tanh(x):
    c = 0.7978845608028654
    return x * (0.5 * (1.0 + jnp.tanh(c * (x + 0.044715 * (x * x * x)))))


def _ffn_in_kernel(*refs, starts, tilings):
    ng = len(starts) - 1
    wg_ref, wu_ref, cw_ref, cb_ref = refs[2 * ng:2 * ng + 4]
    outs = refs[2 * ng + 4:4 * ng + 4]
    wgb_ref, wub_ref = refs[4 * ng + 4:4 * ng + 6]
    carries = refs[4 * ng + 6:]
    i = pl.program_id(1)

    @pl.when(i == 0)
    def _():
        _cast_weight_tile(wg_ref, None, wgb_ref, 0)
        _cast_weight_tile(wu_ref, None, wub_ref, 0)

    for gi in range(ng):
        def run(h_ref=refs[2 * gi], buf_ref=refs[2 * gi + 1], act_ref=outs[2 * gi], nbuf_ref=outs[2 * gi + 1],
                carry_ref=carries[gi], tiles_per_seq=tilings[gi][0], seq_rows=tilings[gi][1], lo_step=starts[gi]):
            tm = h_ref.shape[0]
            sub = min(tm, FFN_SUB)
            seg = sub if seq_rows is None else seq_rows
            if seq_rows is None:
                @pl.when((i - lo_step) % tiles_per_seq == 0)
                def _():
                    carry_ref[...] = buf_ref[0]

                prev = carry_ref[...]
            wg, wu = wgb_ref[...], wub_ref[...]
            taps, cb = [cw_ref[n:n + 1, :] for n in range(cw_ref.shape[0])], cb_ref[...]
            for r in range(tm // sub):
                h = h_ref[r * sub:(r + 1) * sub, :]
                gate_all = _dot(h, wg)
                up_all = _dot(h, wu)
                for s in range(sub // seg):
                    lo = r * sub + s * seg
                    gate = gate_all[s * seg:(s + 1) * seg]
                    if seq_rows is not None:
                        prev = buf_ref[lo // seg]
                    y = _causal_conv(gate, prev, taps)
                    act = _gelu_tanh(y + cb) * up_all[s * seg:(s + 1) * seg]
                    act_ref[lo:lo + seg, :] = act.astype(act_ref.dtype)
                    prev = gate[seg - SUBLANES:, :]
                    if seq_rows is not None:
                        nbuf_ref[lo // seg] = prev
            if seq_rows is None:
                carry_ref[...] = prev
                nbuf_ref[0] = prev
        _on_group_turn(i, starts, gi, run)


def _ffn_in(h_s, w_ffn_in, layer, conv_w, conv_b, buf8_s, t_s, tm_s, tn):
    d = h_s[0].shape[1]
    dff = w_ffn_in.shape[2] // 2
    nj = dff // tn
    starts, idx = _group_steps([h.shape[0] for h in h_s], tm_s)
    tilings = [_seq_tiling(t, tm) for t, tm in zip(t_s, tm_s)]
    in_specs, args, out_specs, out_shape, scratch = [], [], [], [], []
    for h, buf8, tm, f, (tps, _, nb) in zip(h_s, buf8_s, tm_s, idx, tilings):
        ni = h.shape[0] // tm
        in_specs += [pl.BlockSpec((tm, d), functools.partial(lambda j, i, f: (f(i), 0), f=f)),
                     pl.BlockSpec((nb, SUBLANES, tn),
                                  functools.partial(lambda j, i, f, tps: (f(i) // tps, 0, j), f=f, tps=tps))]
        args += [h, buf8]
        out_specs += [pl.BlockSpec((tm, tn), functools.partial(lambda j, i, f: (f(i), j), f=f)),
                      pl.BlockSpec((nb, SUBLANES, tn), functools.partial(lambda j, i, f: (f(i), 0, j), f=f))]
        out_shape += [jax.ShapeDtypeStruct((h.shape[0], dff), BF16),
                      jax.ShapeDtypeStruct((ni * nb, SUBLANES, dff), F32)]
        scratch.append(pltpu.VMEM((SUBLANES, tn), F32))
    outs = pl.pallas_call(
        functools.partial(_ffn_in_kernel, starts=tuple(starts), tilings=tuple(tilings)),
        grid=(nj, starts[-1]),
        in_specs=in_specs + [pl.BlockSpec((1, d, tn), lambda j, i: (layer, 0, j)),
                             pl.BlockSpec((1, d, tn), lambda j, i: (layer, 0, nj + j)),
                             pl.BlockSpec((FFN_CONV, tn), lambda j, i: (0, j)),
                             pl.BlockSpec((1, tn), lambda j, i: (0, j))],
        out_specs=out_specs,
        out_shape=out_shape,
        scratch_shapes=[pltpu.VMEM((d, tn), BF16), pltpu.VMEM((d, tn), BF16)] + scratch,
        compiler_params=_cp("parallel", "arbitrary"),
        name="ffn_in",
    )(*args, w_ffn_in, w_ffn_in, conv_w, conv_b.reshape(1, dff))
    return [(outs[2 * gi], outs[2 * gi + 1][tilings[gi][0] - 1::tilings[gi][0]]) for gi in range(len(h_s))]


def _ret_kernel(ld_ref, p_ref, cos_ref, sin_ref, s0_ref, y_ref, s_ref, *, c):
    t = pl.program_id(1)
    tb = p_ref.shape[0]
    half = RET_DK // 2
    hw = RET_H * RET_DK
    heads = range(RET_H)

    @pl.when(t == 0)
    def _():
        s_ref[...] = s0_ref[...]

    ri = lax.broadcasted_iota(jnp.int32, (c, c), 0)
    ci = lax.broadcasted_iota(jnp.int32, (c, c), 1)
    diff = (ri - ci).astype(F32)
    causal = ri >= ci
    idx = lax.broadcasted_iota(jnp.int32, (c, RET_DK), 0).astype(F32)
    lds = [ld_ref[h] for h in heads]
    dmask = [jnp.where(causal, jnp.exp(ld * jnp.where(causal, diff, 0.0)), 0.0) for ld in lds]
    q_dec = [jnp.exp(ld * (idx + 1.0)) for ld in lds]
    k_dec = [jnp.exp(ld * (c - 1.0 - idx)) for ld in lds]
    s_dec = [jnp.exp(jnp.full((1, RET_DV), ld * c, F32)) for ld in lds]

    def rot(x, cos, sin):
        x1, x2 = x[:, :half], x[:, half:]
        return jnp.concatenate([x1 * cos - x2 * sin, x1 * sin + x2 * cos], axis=-1)

    nchunk = tb // c
    nc = RET_CHUNKS_PER_STEP if nchunk % RET_CHUNKS_PER_STEP == 0 else 1

    def body(jj, carry):
        ch = []
        for ci in range(nc):
            rows = pl.ds(pl.multiple_of((jj * nc + ci) * c, c), c)
            cos, sin = cos_ref[rows, :], sin_ref[rows, :]
            k = [rot(p_ref[rows, hw + h * RET_DK:hw + (h + 1) * RET_DK], cos, sin) * (RET_DK ** -0.5)
                 for h in heads]
            ch.append(dict(
                rows=rows, k=k,
                qb=[rot(p_ref[rows, h * RET_DK:(h + 1) * RET_DK], cos, sin).astype(BF16) for h in heads],
                vb=[p_ref[rows, 2 * hw + h * RET_DV:2 * hw + (h + 1) * RET_DV].astype(BF16) for h in heads]))
        probs = [(d, h) for d in ch for h in heads]
        att = [(_dot_nt(d['qb'][h], d['k'][h].astype(BF16)) * dmask[h]).astype(BF16) for d, h in probs]
        av = [_dot(a, d['vb'][h]) for a, (d, h) in zip(att, probs)]
        kv = [_dot_tn((d['k'][h] * k_dec[h]).astype(BF16), d['vb'][h]) for d, h in probs]

        for ci, d in enumerate(ch):
            s = [s_ref[0, h] for h in heads]
            o = [av[ci * RET_H + h] + _dot(d['qb'][h], s[h].astype(BF16)) * q_dec[h] for h in heads]
            for h in heads:
                s_ref[0, h] = s_dec[h] * s[h] + kv[ci * RET_H + h]
            for h in heads:
                g = p_ref[d['rows'], 3 * hw + h * RET_DV:3 * hw + (h + 1) * RET_DV]
                y_ref[d['rows'], h * RET_DV:(h + 1) * RET_DV] = (_rms(o[h]) * _silu(g)).astype(y_ref.dtype)
        return carry

    lax.fori_loop(0, nchunk // nc, body, 0)


def _retention(p_ret, cos, sin, ld, s0, t, tb):
    m, width = p_ret.shape
    b = m // t
    c = min(CHUNK, t)
    nt = t // tb
    st_spec = pl.BlockSpec((1, RET_H, RET_DK, RET_DV), lambda bi, ti: (bi, 0, 0, 0))
    rope_spec = pl.BlockSpec((tb, RET_DK // 2), lambda bi, ti: (ti, 0))
    return pl.pallas_call(
        functools.partial(_ret_kernel, c=c),
        grid=(b, nt),
        in_specs=[pl.BlockSpec(memory_space=pltpu.SMEM),
                  pl.BlockSpec((tb, width), lambda bi, ti: (bi * nt + ti, 0)),
                  rope_spec, rope_spec, st_spec],
        out_specs=[pl.BlockSpec((tb, RET_H * RET_DV), lambda bi, ti: (bi * nt + ti, 0)), st_spec],
        out_shape=[jax.ShapeDtypeStruct((m, RET_H * RET_DV), BF16),
                   jax.ShapeDtypeStruct((b, RET_H, RET_DK, RET_DV), F32)],
        compiler_params=_cp("parallel", "arbitrary"),
        name="retention",
    )(ld, p_ret, cos, sin, s0)


def _gla_kernel(p_ref, lr_ref, wgk_ref, bgk_ref, nrm_ref, s0_ref, y_ref, s_ref, st_ref, *, c, nt):
    t = pl.program_id(1)
    tb = p_ref.shape[0]
    hk = GLA_H * GLA_DK
    hv = GLA_H * GLA_DV
    heads = range(GLA_H)

    @pl.when(t == 0)
    def _():
        for h in heads:
            st_ref[h] = s0_ref[0, h].T

    ri = lax.broadcasted_iota(jnp.int32, (c, c), 0)
    ci = lax.broadcasted_iota(jnp.int32, (c, c), 1)
    causal = ri >= ci
    tri = jnp.where(causal, 1.0, 0.0).astype(BF16)
    wgk = wgk_ref[...]
    bgk = bgk_ref[...]
    nrm = nrm_ref[...]

    nchunk = tb // c
    nc = GLA_CHUNKS_PER_STEP if nchunk % GLA_CHUNKS_PER_STEP == 0 else 1
    hs = lambda x, h: x[:, h * GLA_DK:(h + 1) * GLA_DK]

    def body(jj, carry):
        rows = [pl.ds(pl.multiple_of((jj * nc + ci) * c, c), c) for ci in range(nc)]
        z = [_dot(lr_ref[r, :].astype(BF16), wgk) + bgk for r in rows]
        bc = [_tri_cumsum(tri, -_softplus(-zz) / GLA_NORMALIZER) for zz in z]
        ch = []
        for r, b_ in zip(rows, bc):
            q = p_ref[r, 0:hk] * (GLA_DK ** -0.5)
            k = p_ref[r, hk:2 * hk]
            b_last = b_[c - 1:c, :]
            ch.append(dict(
                rows=r, qe=(q * jnp.exp(b_)).astype(BF16), ke=(k * jnp.exp(-b_)).astype(BF16),
                kd=(k * jnp.exp(b_last - b_)).astype(BF16), eb_last=jnp.exp(b_last),
                vb=[p_ref[r, 2 * hk + h * GLA_DV:2 * hk + (h + 1) * GLA_DV].astype(BF16) for h in heads]))
        probs = [(d, h) for d in ch for h in heads]
        att = [jnp.where(causal, _dot_nt(hs(d['qe'], h), hs(d['ke'], h)), 0.0).astype(BF16) for d, h in probs]
        av = [_dot(a, d['vb'][h]) for a, (d, h) in zip(att, probs)]

        for ci, d in enumerate(ch):
            st = [st_ref[h] for h in heads]
            o = [av[ci * GLA_H + h] + _dot_nt(hs(d['qe'], h), st[h].astype(BF16)) for h in heads]
            for h in heads:
                st_ref[h] = hs(d['eb_last'], h) * st[h] + _dot_tn(d['vb'][h], hs(d['kd'], h))
            for h in heads:
                g = p_ref[d['rows'], 2 * hk + hv + h * GLA_DV:2 * hk + hv + (h + 1) * GLA_DV]
                y_ref[d['rows'], h * GLA_DV:(h + 1) * GLA_DV] = (_rms(o[h], nrm) * _silu(g)).astype(y_ref.dtype)
        return carry

    lax.fori_loop(0, nchunk // nc, body, 0)

    @pl.when(t == nt - 1)
    def _():
        for h in heads:
            s_ref[0, h] = st_ref[h].T


def _gla(p_gla, p_small, wgk, bgk, nrm, s0, t, tb):
    m, width = p_gla.shape
    b = m // t
    c = min(CHUNK, t)
    nt = t // tb
    const = lambda bi, ti: (0, 0)
    st_spec = pl.BlockSpec((1, GLA_H, GLA_DK, GLA_DV), lambda bi, ti: (bi, 0, 0, 0))
    return pl.pallas_call(
        functools.partial(_gla_kernel, c=c, nt=nt),
        grid=(b, nt),
        in_specs=[pl.BlockSpec((tb, width), lambda bi, ti: (bi * nt + ti, 0)),
                  pl.BlockSpec((tb, LANES), lambda bi, ti: (bi * nt + ti, 0)),
                  pl.BlockSpec((LANES, GLA_H * GLA_DK), const),
                  pl.BlockSpec((1, GLA_H * GLA_DK), const),
                  pl.BlockSpec((1, GLA_DV), const),
                  st_spec],
        out_specs=[pl.BlockSpec((tb, GLA_H * GLA_DV), lambda bi, ti: (bi * nt + ti, 0)), st_spec],
        out_shape=[jax.ShapeDtypeStruct((m, GLA_H * GLA_DV), BF16),
                   jax.ShapeDtypeStruct((b, GLA_H, GLA_DK, GLA_DV), F32)],
        scratch_shapes=[pltpu.VMEM((GLA_H, GLA_DV, GLA_DK), F32)],
        compiler_params=_cp("parallel", "arbitrary"),
        name="gla",
    )(p_gla, p_small, wgk, bgk, nrm, s0)


def _blockdiag2(xp, left):
    return jnp.concatenate([jnp.where(left, xp, 0.0), jnp.where(left, 0.0, xp)], axis=0)


def _unit_lower_inverse_minus_eye_packed(mats, c):
    ri = lax.broadcasted_iota(jnp.int32, (c, 2 * c), 0)
    cl = lax.broadcasted_iota(jnp.int32, (c, 2 * c), 1)
    left = cl < c
    cm = cl & (c - 1)

    def level_mask(k):
        return ((ri ^ cm) < 2 * k) & ((ri & k) != 0) & ((cm & k) == 0)

    m1 = level_mask(1)
    ns = [-jnp.where(m1, a, 0.0) for a in mats]
    k = 2
    while k < c:
        mk = level_mask(k)
        ls = [jnp.where(mk, a, 0.0) for a in mats]
        ys = [l + _dot(n.astype(BF16), _blockdiag2(l, left).astype(BF16)) for l, n in zip(ls, ns)]
        xs = [y + _dot(y.astype(BF16), _blockdiag2(n, left).astype(BF16)) for y, n in zip(ys, ns)]
        ns = [n - x for n, x in zip(ns, xs)]
        k *= 2
    return ns


def _gdn_kernel(q_ref, k_ref, v_ref, z_ref, sm_ref, alog_ref, dtb_ref, nrm_ref, s0_ref,
                y_ref, s_ref, *, c):
    t = pl.program_id(1)
    tb = q_ref.shape[0]

    @pl.when(t == 0)
    def _():
        s_ref[...] = s0_ref[...]

    tri = jnp.where(lax.broadcasted_iota(jnp.int32, (c, c), 0) >= lax.broadcasted_iota(jnp.int32, (c, c), 1),
                    1.0, 0.0).astype(BF16)
    ri = lax.broadcasted_iota(jnp.int32, (c, 2 * c), 0)
    cl = lax.broadcasted_iota(jnp.int32, (c, 2 * c), 1)
    left = cl < c
    cm = cl & (c - 1)
    incl = ri >= cm
    strict = ri > cm
    lane = lax.broadcasted_iota(jnp.int32, (1, LANES), 1)
    a_lanes = jnp.logical_and(lane >= SMALL_A0, lane < SMALL_A0 + GDN_H)
    sel2 = jnp.where(lax.broadcasted_iota(jnp.int32, (SUBLANES, LANES), 1)
                     == 2 * lax.broadcasted_iota(jnp.int32, (SUBLANES, LANES), 0) + SMALL_A0,
                     1.0, 0.0).astype(BF16)
    neg_a = -jnp.exp(alog_ref[...])
    dtb = dtb_ref[...]
    nrm = nrm_ref[...]

    def col(x, lane_idx):
        return x[:, lane_idx:lane_idx + 1]

    assert GDN_DK == LANES and GDN_DV == LANES and GDN_H % 2 == 0
    heads = range(GDN_H)
    npair = GDN_H // 2
    pairs = range(npair)
    nchunk = tb // c
    nc = GDN_CHUNKS_PER_STEP if nchunk % GDN_CHUNKS_PER_STEP == 0 else 1

    def lanes2(x, y):
        return jnp.concatenate([x, y], axis=1)

    def body(jj, carry):
        ch = []
        for ci in range(nc):
            rows = pl.ds(pl.multiple_of((jj * nc + ci) * c, c), c)
            sm = sm_ref[rows, :]
            ch.append(dict(
                rows=rows,
                g_all=jnp.where(a_lanes, neg_a * _softplus(sm + dtb), 0.0),
                beta_all=_sigmoid(sm),
                q=[q_ref[rows, h * GDN_DK:(h + 1) * GDN_DK] for h in heads],
                k=[k_ref[rows, h * GDN_DK:(h + 1) * GDN_DK] for h in heads],
                v=[v_ref[rows, h * GDN_DV:(h + 1) * GDN_DV] for h in heads]))
        for d in ch:
            d['gam'] = _tri_cumsum(tri, d['g_all'])
        for d in ch:
            g2 = jnp.concatenate([d['gam'], pltpu.roll(d['gam'], LANES - 1, 1)], axis=0)
            ghi, gmid, glo = _split3(g2)
            d['gam_rows'] = _dot_nt(sel2, ghi) + _dot_nt(sel2, gmid) + _dot_nt(sel2, glo)

        def packed_cols(x, base, p):
            return jnp.where(left, col(x, base + 2 * p), col(x, base + 2 * p + 1))

        for d in ch:
            gam = d['gam']
            g_last = gam[c - 1:c, :]
            d['eg_all'] = jnp.exp(gam)
            d['egl_all'] = jnp.exp(g_last - gam)
            d['eg_last_all'] = jnp.exp(g_last)
            d['beta'] = [col(d['beta_all'], SMALL_B0 + h) for h in heads]
            d['eg'] = [col(d['eg_all'], SMALL_A0 + h) for h in heads]
            d['beta_p'] = [packed_cols(d['beta_all'], SMALL_B0, p) for p in pairs]
            d['decay_p'] = [jnp.where(incl, jnp.exp(jnp.where(
                incl, packed_cols(gam, SMALL_A0, p) - d['gam_rows'][p:p + 1, :], 0.0)), 0.0) for p in pairs]
        probs = [(d, p) for d in ch for p in pairs]
        zk = jnp.zeros((c, GDN_DK), F32)
        qk_kk = [_dot_nt(
            jnp.concatenate([lanes2(d['q'][2 * p], d['q'][2 * p + 1]),
                             lanes2(d['k'][2 * p], d['k'][2 * p + 1])], axis=0).astype(BF16),
            jnp.concatenate([lanes2(d['k'][2 * p], zk), lanes2(zk, d['k'][2 * p + 1])], axis=0).astype(BF16))
            for d, p in probs]
        att = [(r[:c] * d['decay_p'][p]).astype(BF16) for r, (d, p) in zip(qk_kk, probs)]
        a = [jnp.where(strict, d['beta_p'][p] * r[c:] * d['decay_p'][p], 0.0) for r, (d, p) in zip(qk_kk, probs)]
        n = _unit_lower_inverse_minus_eye_packed(a, c)
        zr = jnp.zeros((c, GDN_DV + GDN_DK), F32)
        sol = []
        for (d, p), nn in zip(probs, n):
            r1, r2 = (jnp.concatenate([d['beta'][h] * d['v'][h], (d['beta'][h] * d['eg'][h]) * d['k'][h]], axis=-1)
                      for h in (2 * p, 2 * p + 1))
            bd = jnp.concatenate([lanes2(r1, zr), lanes2(zr, r2)], axis=0).astype(BF16)
            sol.append(lanes2(r1, r2) + _dot(nn.astype(BF16), bd))

        zs = jnp.zeros((GDN_DK, GDN_DV), F32)
        zw = jnp.zeros((c, GDN_DV), F32)
        for ci, d in enumerate(ch):
            sl = sol[ci * npair:(ci + 1) * npair]
            at = att[ci * npair:(ci + 1) * npair]
            s = [s_ref[0, h] for h in heads]
            wq_s = [_dot(
                jnp.concatenate([lanes2(sl[p][:, LANES:2 * LANES], sl[p][:, 3 * LANES:]),
                                 lanes2(d['q'][2 * p], d['q'][2 * p + 1])], axis=0).astype(BF16),
                jnp.concatenate([lanes2(s[2 * p], zs), lanes2(zs, s[2 * p + 1])], axis=0).astype(BF16))
                for p in pairs]
            bdw = [jnp.concatenate([lanes2(sl[p][:, :LANES] - wq_s[p][:c, :LANES], zw),
                                    lanes2(zw, sl[p][:, 2 * LANES:3 * LANES] - wq_s[p][:c, LANES:])],
                                   axis=0).astype(BF16) for p in pairs]
            aw = [_dot(at[p], bdw[p]) for p in pairs]
            kw = [_dot_tn(jnp.concatenate([d['k'][h] * col(d['egl_all'], SMALL_A0 + h)
                                           for h in (2 * p, 2 * p + 1)], axis=0).astype(BF16), bdw[p])
                  for p in pairs]
            for h in heads:
                p, half = h // 2, slice((h % 2) * LANES, (h % 2 + 1) * LANES)
                s_ref[0, h] = col(d['eg_last_all'], SMALL_A0 + h) * s[h] + kw[p][:, half]
            for h in heads:
                p, half = h // 2, slice((h % 2) * LANES, (h % 2 + 1) * LANES)
                o = d['eg'][h] * wq_s[p][c:, half] + aw[p][:, half]
                z = z_ref[d['rows'], h * GDN_DV:(h + 1) * GDN_DV]
                y = _rms(o, nrm) * _silu(z)
                y_ref[d['rows'], h * GDN_DV:(h + 1) * GDN_DV] = y.astype(y_ref.dtype)
        return carry

    lax.fori_loop(0, nchunk // nc, body, 0)


def _gdn(q, k, v, z, p_small, alog_row, dtb_row, nrm, s0, t, tb):
    m = q.shape[0]
    b = m // t
    c = min(CHUNK, t)
    nt = t // tb
    st_spec = pl.BlockSpec((1, GDN_H, GDN_DK, GDN_DV), lambda bi, ti: (bi, 0, 0, 0))
    const = lambda bi, ti: (0, 0)
    row = lambda bi, ti: (bi * nt + ti, 0)
    qk_spec = pl.BlockSpec((tb, GDN_H * GDN_DK), row)
    v_spec = pl.BlockSpec((tb, GDN_H * GDN_DV), row)
    return pl.pallas_call(
        functools.partial(_gdn_kernel, c=c),
        grid=(b, nt),
        in_specs=[qk_spec, qk_spec, v_spec, v_spec,
                  pl.BlockSpec((tb, LANES), row),
                  pl.BlockSpec((1, LANES), const),
                  pl.BlockSpec((1, LANES), const),
                  pl.BlockSpec((1, GDN_DV), const),
                  st_spec],
        out_specs=[v_spec, st_spec],
        out_shape=[jax.ShapeDtypeStruct((m, GDN_H * GDN_DV), BF16),
                   jax.ShapeDtypeStruct((b, GDN_H, GDN_DK, GDN_DV), F32)],
        compiler_params=_cp("parallel", "arbitrary"),
        name="gated_deltanet",
    )(q, k, v, z, p_small, alog_row, dtb_row, nrm, s0)


def _pad_rows_front(buf, rows):
    return jnp.pad(buf, ((0, 0), (0, 0), (rows - buf.shape[2], 0), (0, 0)))


def _in_proj_windows(d):
    sizes = dict(ret=4 * RET_H * RET_DK, gla=2 * GLA_H * GLA_DK + 2 * GLA_H * GLA_DV, lr=GLA_RANK,
                 gdn=2 * GDN_H * GDN_DK + 2 * GDN_H * GDN_DV, ab=2 * GDN_H, gbr=N_BRANCH * d)
    win, o = {}, 0
    for name in ('ret', 'gla', 'lr', 'gdn', 'ab', 'gbr'):
        win[name] = (o, sizes[name])
        o += sizes[name]
    assert win['lr'][0] % LANES == 0 and win['ab'][0] % LANES == SMALL_A0
    return win


def _prep_weights(w):
    lane_row = lambda v, at: jnp.pad(v, ((0, 0), (at, LANES - at - v.shape[1])))[:, None, :]
    return dict(
        w_in_t=jnp.swapaxes(w['w_in'], 1, 2), w_xq=w['w_xq'], w_ffn_in=w['w_ffn_in'],
        w_gk=jnp.pad(w['w_gla_gk'], ((0, 0), (0, LANES - GLA_RANK), (0, 0))).astype(BF16),
        b_gk=w['b_gla_gk'][:, None, :],
        gla_norm=w['gla_norm'][:, None, :], gdn_norm=w['gdn_norm'][:, None, :],
        alog_row=lane_row(w['gdn_a_log'], SMALL_A0), dtb_row=lane_row(w['gdn_dt_bias'], SMALL_A0),
        gdn_conv_w=w['gdn_conv_w'], b_gate=w['b_gate'],
        w_branch=w['w_branch'].astype(BF16), w_out=w['w_out'].astype(BF16), w_xo=w['w_xo'].astype(BF16),
        ffn_conv_w=w['ffn_conv_w'], ffn_conv_b=w['ffn_conv_b'], w_ffn_out=w['w_ffn_out'].astype(BF16),
        ln_mix=w['ln_mix'], ln_xattn=w['ln_xattn'], ln_ffn=w['ln_ffn'], ln_final=w['ln_final'],
    )


def _tile(n, cap):
    t = min(n, cap)
    while n % t:
        t -= 1
    return t


class _Group:
    def __init__(self, x, offset, mem_k, mem_v, s_ret, s_gla, s_gdn, buf_gdn, buf_ffn):
        self.b, self.t, d = x.shape
        b, t = self.b, self.t
        self.m = m = b * t
        self.x2 = x.reshape(m, d)
        self.mem_k, self.mem_v = mem_k, mem_v
        self.s_ret, self.s_gla, self.s_gdn = s_ret, s_gla, s_gdn
        self.buf_gdn8 = _pad_rows_front(buf_gdn, SUBLANES)
        self.buf_ffn8 = _pad_rows_front(buf_ffn, SUBLANES)
        self.tm = _tile(m, MM_TM)
        self.tm_res = _tile(m, RES_TM)
        self.tm_res_k = _tile(m, RES_TM_K)
        self.tm_ffo = _tile(m, FFO_TM)
        self.tb = _tile(t, SEQ_TB)
        self.tb_gdn = _tile(t, SEQ_TB_GDN)
        self.tq = _tile(t, XATTN_TQ)
        self.tm_ffn = _tile(t, FFN_TM) if t >= FFN_TM else t * _tile(b, max(1, FFN_TM // t))
        pos = offset + jnp.arange(t, dtype=F32)
        half = RET_DK // 2
        freqs = 1.0 / (ROPE_BASE ** (jnp.arange(half, dtype=F32) / half))
        ang = pos[:, None] * freqs[None, :]
        self.cos, self.sin = jnp.cos(ang), jnp.sin(ang)
        self.outs = ([], [], [], [], [])
        self.hn = self.y = None


def _trunks(groups, pw):
    d = groups[0].x2.shape[1]
    depth = groups[0].s_ret.shape[0]
    ld = jnp.log1p(-jnp.exp2(-5.0 - jnp.arange(RET_H, dtype=F32)))
    win = _in_proj_windows(d)
    tms = tuple(g.tm for g in groups)
    order = sorted(range(len(groups)), key=lambda gi: groups[gi].m)

    def shared(acts, w, l, off, n, out_dtype, name, transposed):
        blk0 = off // MM_TN
        shift = off - blk0 * MM_TN
        assert shift < LANES and n % MM_TN == 0
        outs = _mm_ws(tuple(acts[gi] for gi in order), w, l, blk0, n, shift, out_dtype,
                      tuple(tms[gi] for gi in order), MM_TN, name, transposed=transposed)
        return [outs[order.index(gi)] for gi in range(len(acts))]

    for g in groups:
        g.hn = _norm(g.x2, pw['ln_mix'][0], g.tm_res_k)
    for l in range(depth):
        hns = [g.hn for g in groups]
        hq, hv = GDN_H * GDN_DK, GDN_H * GDN_DV
        g_off = win['gdn'][0]
        p_ret = shared(hns, pw['w_in_t'], l, *win['ret'], F32, "in_proj_ret", True)
        p_gla = shared(hns, pw['w_in_t'], l, *win['gla'], F32, "in_proj_gla", True)
        d_z = shared(hns, pw['w_in_t'], l, g_off + 2 * hq + hv, hv, F32, "in_proj_gdn_z", True)
        g_br = shared(hns, pw['w_in_t'], l, *win['gbr'], BF16, "in_proj_gbr", True)
        def conv(off, n, coff, head_dim, scale, name):
            r = _conv_proj([hns[gi] for gi in order], pw['w_in_t'], l, off, n, pw['gdn_conv_w'][l],
                           [groups[gi].buf_gdn8[l] for gi in order], coff, [groups[gi].t for gi in order],
                           [tms[gi] for gi in order], MM_TN, head_dim, scale, name)
            return [r[order.index(gi)] for gi in range(len(groups))]

        cq = conv(g_off, hq, 0, GDN_DK, GDN_DK ** -0.5, "in_proj_gdn_q")
        ck = conv(g_off + hq, hq, hq, GDN_DK, 1.0, "in_proj_gdn_k")
        cv = conv(g_off + 2 * hq, hv, 2 * hq, GDN_DV, None, "in_proj_gdn_v")
        ys_s = []
        for gi, g in enumerate(groups):
            t, tm, hn = g.t, g.tm, g.hn
            (d_q, tq8), (d_k, tk8), (d_v, tv8) = cq[gi], ck[gi], cv[gi]
            bg8 = jnp.concatenate([tq8, tk8, tv8], axis=-1)
            p_small = _small_proj(hn, pw['w_in_t'], l, win['lr'][0] // LANES, win['ab'][0] // LANES, tm)

            y_ret, sr = _retention(p_ret[gi], g.cos, g.sin, ld, g.s_ret[l], t, g.tb)
            y_gla, sg = _gla(p_gla[gi], p_small, pw['w_gk'][l], pw['b_gk'][l], pw['gla_norm'][l],
                             g.s_gla[l], t, g.tb)
            y_gdn, sd = _gdn(d_q, d_k, d_v, d_z[gi], p_small, pw['alog_row'][l], pw['dtb_row'][l],
                             pw['gdn_norm'][l], g.s_gdn[l], t, g.tb_gdn)

            ys_s.append((y_ret, y_gla, y_gdn))
            for lst, val in zip(g.outs[:4], (sr, sg, sd, bg8[:, SUBLANES - (GDN_CONV - 1):])):
                lst.append(val)

        xs = lambda: [g.x2 for g in groups]
        merged = _merge(ys_s, pw['w_branch'][l], g_br, pw['b_gate'][l], [g.tm_res_k for g in groups])
        res = _proj_res(merged, pw['w_out'][l], xs(), pw['ln_xattn'][l], [g.tm_res for g in groups], ROW_SUB,
                        False, "out_proj")
        for g, (x2, _) in zip(groups, res):
            g.x2 = x2
        qs = shared([hx for _, hx in res], pw['w_xq'], l, 0, d, BF16, "xattn_q", False)
        os_ = [_xattn(q, g.mem_k, g.mem_v, l, g.t, g.tq) for q, g in zip(qs, groups)]
        res = _proj_res(os_, pw['w_xo'][l], xs(), pw['ln_ffn'][l], [g.tm_res for g in groups], ROW_SUB,
                        False, "xattn_out")
        for g, (x2, _) in zip(groups, res):
            g.x2 = x2
        ff = _ffn_in([res[gi][1] for gi in order], pw['w_ffn_in'], l, pw['ffn_conv_w'][l], pw['ffn_conv_b'][l],
                     [groups[gi].buf_ffn8[l] for gi in order], [groups[gi].t for gi in order],
                     [groups[gi].tm_ffn for gi in order], FFN_TN)
        acts = [ff[order.index(gi)][0] for gi in range(len(groups))]
        for gi, g in enumerate(groups):
            g.outs[4].append(ff[order.index(gi)][1][:, SUBLANES - (FFN_CONV - 1):])
        last = l + 1 == depth
        res = _proj_res(acts, pw['w_ffn_out'][l], xs(), pw['ln_final'] if last else pw['ln_mix'][l + 1],
                        [g.tm_ffo for g in groups], ROW_SUB, last, "ffn_out_final" if last else "ffn_out")
        for g, r in zip(groups, res):
            if last:
                g.y, = r
            else:
                g.x2, g.hn = r
    return [(g.y.reshape(g.b, g.t, d),) + tuple(jnp.stack(lst) for lst in g.outs) for g in groups]


def kernel(x_prompt, x_sample, mem_prompt, state_ret, state_gla, state_gdn, state_gdn_conv, state_ffn_conv, cache_mem_k, cache_mem_v, ln_mix, w_in, w_gla_gk, b_gla_gk, gla_norm, gdn_conv_w, gdn_a_log, gdn_dt_bias, gdn_norm, b_gate, w_branch, w_out, ln_xattn, ln_mem, w_xq, w_xkv, w_xo, ln_ffn, w_ffn_in, ffn_conv_w, ffn_conv_b, w_ffn_out, ln_final):
    pw = _prep_weights(dict(
        w_in=w_in, w_gla_gk=w_gla_gk, b_gla_gk=b_gla_gk, gla_norm=gla_norm, gdn_conv_w=gdn_conv_w,
        gdn_a_log=gdn_a_log, gdn_dt_bias=gdn_dt_bias, gdn_norm=gdn_norm, b_gate=b_gate, w_branch=w_branch,
        w_out=w_out, w_xq=w_xq, w_xo=w_xo, w_ffn_in=w_ffn_in, ffn_conv_w=ffn_conv_w, ffn_conv_b=ffn_conv_b,
        w_ffn_out=w_ffn_out, ln_mix=ln_mix, ln_xattn=ln_xattn, ln_ffn=ln_ffn, ln_final=ln_final))
    depth = w_in.shape[0]
    bp, nmem, d = mem_prompt.shape
    dt = x_prompt.dtype

    mem2 = mem_prompt.reshape(bp * nmem, d)
    mem_k_p, mem_kb = _mem_proj(mem2, ln_mem, w_xkv, 0, bp, nmem, "mem_k")
    mem_v_p, mem_vb = _mem_proj(mem2, ln_mem, w_xkv, 1, bp, nmem, "mem_v")

    zeros = lambda *s: jnp.zeros((depth, bp) + s, dt)
    prompt = _Group(
        x_prompt, 0.0, mem_kb, mem_vb,
        zeros(RET_H, RET_DK, RET_DV), zeros(GLA_H, GLA_DK, GLA_DV), zeros(GDN_H, GDN_DK, GDN_DV),
        zeros(GDN_CONV - 1, state_gdn_conv.shape[-1]), zeros(FFN_CONV - 1, state_ffn_conv.shape[-1]))
    sample = _Group(x_sample, float(PAST_LEN), cache_mem_k, cache_mem_v,
                    state_ret, state_gla, state_gdn, state_gdn_conv, state_ffn_conv)
    ((y_p, ret_p, gla_p, gdn_p, gdn_conv_p, ffn_conv_p),
     (y_s, ret_s, gla_s, gdn_s, gdn_conv_s, ffn_conv_s)) = _trunks([prompt, sample], pw)

    return (y_p, y_s, ret_p, gla_p, gdn_p, gdn_conv_p, ffn_conv_p, mem_k_p, mem_v_p,
            ret_s, gla_s, gdn_s, gdn_conv_s, ffn_conv_s)
```

```python
import functools

import jax
import jax.numpy as jnp
from jax import lax
from jax.experimental import pallas as pl
from jax.experimental.pallas import tpu as pltpu

F32 = jnp.float32
BF16 = jnp.bfloat16

EPS = 1e-6
CHUNK = 64
RET_H, RET_DK, RET_DV = 4, 256, 256
GLA_H, GLA_DK, GLA_DV = 4, 128, 256
GLA_RANK = 16
GLA_NORMALIZER = 16.0
GDN_H, GDN_DK, GDN_DV = 8, 128, 128
GDN_CONV = 4
N_BRANCH = 3
X_H = 4
FFN_CONV = 3
ROPE_BASE = 10000.0
PAST_LEN = 4096

LANES = 128
SUBLANES = 8
SMALL_A0 = GLA_RANK
SMALL_B0 = GLA_RANK + GDN_H
VMEM_LIMIT = 52 * 1024 * 1024
MM_TM = 1024
MM_TN = 1024
RES_TM = 512
RES_TM_K = 512
FFO_TM = 256
ROW_SUB = 256
CONV_SUB = 256
SEQ_TB = 512
SEQ_TB_GDN = 256
GDN_CHUNKS_PER_STEP = 4
GLA_CHUNKS_PER_STEP = 4
RET_CHUNKS_PER_STEP = 2
XATTN_TQ = 512
FFN_TM = 1024
FFN_TN = 512
FFN_SUB = 256


def _cp(*sem):
    return pltpu.CompilerParams(dimension_semantics=sem, vmem_limit_bytes=VMEM_LIMIT)


def _dot(a, b):
    return jnp.dot(a, b, preferred_element_type=F32)


def _dot_nt(a, b):
    return lax.dot_general(a, b, (((1,), (1,)), ((), ())), preferred_element_type=F32)


def _dot_tn(a, b):
    return lax.dot_general(a, b, (((0,), (0,)), ((), ())), preferred_element_type=F32)


def _sigmoid(x):
    return 1.0 / (1.0 + jnp.exp(-x))


def _silu(x):
    return x * _sigmoid(x)


def _softplus(x):
    return jnp.maximum(x, 0.0) + jnp.log1p(jnp.exp(-jnp.abs(x)))


def _rms(x, g=None):
    y = x * lax.rsqrt(jnp.mean(x * x, axis=-1, keepdims=True) + EPS)
    return y if g is None else y * g


def _split3(x):
    hi = x.astype(BF16)
    r = x - hi.astype(F32)
    mid = r.astype(BF16)
    lo = (r - mid.astype(F32)).astype(BF16)
    return hi, mid, lo


def _tri_cumsum(tri, x):
    hi, mid, lo = _split3(x)
    return _dot(tri, hi) + _dot(tri, mid) + _dot(tri, lo)


def _norm_kernel(x_ref, g_ref, o_ref):
    o_ref[...] = _rms(x_ref[...], g_ref[...]).astype(o_ref.dtype)


def _norm(x, g, tm):
    m, d = x.shape
    return pl.pallas_call(
        _norm_kernel,
        grid=(m // tm,),
        in_specs=[pl.BlockSpec((tm, d), lambda i: (i, 0)),
                  pl.BlockSpec((1, d), lambda i: (0, 0))],
        out_specs=pl.BlockSpec((tm, d), lambda i: (i, 0)),
        out_shape=jax.ShapeDtypeStruct((m, d), BF16),
        compiler_params=_cp("parallel"),
        name="rmsnorm",
    )(x, g.reshape(1, d))


CAST_ROWS = 256


def _cast_weight_tile(w_ref, wx_ref, wb_ref, shift):
    kk, tn = wb_ref.shape
    for r in range(0, kk, CAST_ROWS):
        rows = slice(r, min(r + CAST_ROWS, kk))
        if shift == 0:
            wb_ref[rows, :] = w_ref[0, rows, :].astype(BF16)
        else:
            wide = jnp.concatenate([w_ref[0, rows, :], wx_ref[0, rows, :]], axis=1)
            wb_ref[rows, :] = pltpu.roll(wide, wide.shape[1] - shift, 1)[:, :tn].astype(BF16)


def _cast_weight_tile_t(w_ref, wx_ref, wb_ref, shift):
    tn, _ = wb_ref.shape
    for r in range(0, tn, CAST_ROWS):
        lo, hi = r + shift, min(r + CAST_ROWS, tn) + shift
        if hi <= tn:
            src = w_ref[0, lo:hi, :]
        else:
            src = jnp.concatenate([w_ref[0, lo:tn, :], wx_ref[0, 0:hi - tn, :]], axis=0)
        wb_ref[r:min(r + CAST_ROWS, tn), :] = src.astype(BF16)


def _group_steps(ms, tms):
    nis = [m // t for m, t in zip(ms, tms)]
    starts = [0]
    for n in nis:
        starts.append(starts[-1] + n)

    def idx(g):
        return lambda i: jnp.clip(i - starts[g], 0, nis[g] - 1)

    return starts, [idx(g) for g in range(len(ms))]


def _on_group_turn(i, starts, g, fn):
    if len(starts) == 2:
        fn()
    else:
        pl.when(jnp.logical_and(i >= starts[g], i < starts[g + 1]))(fn)


def _mm_ws_kernel(*refs, shift, transposed, starts):
    ng = len(starts) - 1
    a_refs, w_ref, rest = refs[:ng], refs[ng], refs[ng + 1:]
    wx_ref = None
    if shift:
        wx_ref, rest = rest[0], rest[1:]
    o_refs, wb_ref = rest[:ng], rest[ng]
    i = pl.program_id(1)

    @pl.when(i == 0)
    def _():
        (_cast_weight_tile_t if transposed else _cast_weight_tile)(w_ref, wx_ref, wb_ref, shift)

    mm = _dot_nt if transposed else _dot
    for g in range(ng):
        def run(a_ref=a_refs[g], o_ref=o_refs[g]):
            o_ref[...] = mm(a_ref[...], wb_ref[...]).astype(o_ref.dtype)
        _on_group_turn(i, starts, g, run)


def _mm_ws(a, w, layer, blk0, n, shift, out_dtype, tm, tn, name, transposed=False):
    single = not isinstance(a, (tuple, list))
    a_s, tms = ((a,), (tm,)) if single else (tuple(a), tuple(tm))
    k = a_s[0].shape[1]
    nj = n // tn
    per = tn // LANES
    starts, idx = _group_steps([x.shape[0] for x in a_s], tms)
    if transposed:
        assert shift % SUBLANES == 0
        w_spec = pl.BlockSpec((1, tn, k), lambda j, i: (layer, blk0 + j, 0))
        wx_spec = pl.BlockSpec((1, LANES, k), lambda j, i: (layer, (blk0 + j + 1) * per, 0))
    else:
        w_spec = pl.BlockSpec((1, k, tn), lambda j, i: (layer, 0, blk0 + j))
        wx_spec = pl.BlockSpec((1, k, LANES), lambda j, i: (layer, 0, (blk0 + j + 1) * per))
    in_specs = [pl.BlockSpec((t_, k), functools.partial(lambda j, i, f: (f(i), 0), f=f))
                for t_, f in zip(tms, idx)] + [w_spec]
    args = list(a_s) + [w]
    if shift:
        in_specs.append(wx_spec)
        args.append(w)
    outs = pl.pallas_call(
        functools.partial(_mm_ws_kernel, shift=shift, transposed=transposed, starts=tuple(starts)),
        grid=(nj, starts[-1]),
        in_specs=in_specs,
        out_specs=[pl.BlockSpec((t_, tn), functools.partial(lambda j, i, f: (f(i), j), f=f))
                   for t_, f in zip(tms, idx)],
        out_shape=[jax.ShapeDtypeStruct((x.shape[0], n), out_dtype) for x in a_s],
        scratch_shapes=[pltpu.VMEM((tn, k) if transposed else (k, tn), BF16)],
        compiler_params=_cp("parallel", "arbitrary"),
        name=name,
    )(*args)
    return outs[0] if single else tuple(outs)


def _small_proj_kernel(a_ref, w0_ref, w1_ref, o_ref):
    w = jnp.concatenate([w0_ref[0], w1_ref[0]], axis=0).astype(BF16)
    p = _dot_nt(a_ref[...], w)
    p0, p1 = p[:, :LANES], p[:, LANES:]
    lane = lax.broadcasted_iota(jnp.int32, p0.shape, 1)
    o_ref[...] = jnp.where(lane < SMALL_A0, p0, p1)


def _small_proj(a, w_t, layer, blk_lr, blk_ab, tm):
    m, k = a.shape
    return pl.pallas_call(
        _small_proj_kernel,
        grid=(m // tm,),
        in_specs=[pl.BlockSpec((tm, k), lambda i: (i, 0)),
                  pl.BlockSpec((1, LANES, k), lambda i: (layer, blk_lr, 0)),
                  pl.BlockSpec((1, LANES, k), lambda i: (layer, blk_ab, 0))],
        out_specs=pl.BlockSpec((tm, LANES), lambda i: (i, 0)),
        out_shape=jax.ShapeDtypeStruct((m, LANES), F32),
        compiler_params=_cp("parallel"),
        name="in_proj_small",
    )(a, w_t, w_t)


def _shift_rows(x, prev8, s):
    xs = pltpu.roll(x, s, 0)
    ps = pltpu.roll(prev8, s, 0)
    r8 = lax.broadcasted_iota(jnp.int32, prev8.shape, 0)
    top = jnp.where(r8 < s, ps, xs[:SUBLANES])
    if x.shape[0] == SUBLANES:
        return top
    return jnp.concatenate([top, xs[SUBLANES:]], axis=0)


def _causal_conv(x, prev8, taps):
    width = len(taps)
    y = None
    for i, w in enumerate(taps):
        s = width - 1 - i
        xs = x if s == 0 else _shift_rows(x, prev8, s)
        y = xs * w if y is None else y + xs * w
    return y


def _seq_tiling(t, tm):
    return (t // tm, None, 1) if t >= tm else (1, t, tm // t)


def _conv_proj_kernel(*refs, shift, starts, tilings, head_dim, norm_scale):
    ng = len(starts) - 1
    w_ref = refs[2 * ng]
    rest = refs[2 * ng + 1:]
    wx_ref = None
    if shift:
        wx_ref, rest = rest[0], rest[1:]
    cw_ref = rest[0]
    outs = rest[1:2 * ng + 1]
    wb_ref = rest[2 * ng + 1]
    carries = rest[2 * ng + 2:]
    i = pl.program_id(1)

    @pl.when(i == 0)
    def _():
        _cast_weight_tile_t(w_ref, wx_ref, wb_ref, shift)

    for gi in range(ng):
        def run(a_ref=refs[2 * gi], buf_ref=refs[2 * gi + 1], o_ref=outs[2 * gi], tail_ref=outs[2 * gi + 1],
                carry_ref=carries[gi], tiles_per_seq=tilings[gi][0], seq_rows=tilings[gi][1], lo_step=starts[gi]):
            tm, tn = o_ref.shape
            sub = min(tm, CONV_SUB)
            seg = sub if seq_rows is None else seq_rows
            if seq_rows is None:
                @pl.when((i - lo_step) % tiles_per_seq == 0)
                def _():
                    carry_ref[...] = buf_ref[0]

                prev = carry_ref[...]
            wb = wb_ref[...]
            taps = [cw_ref[n:n + 1, :] for n in range(cw_ref.shape[0])]
            for r in range(tm // sub):
                p_all = _dot_nt(a_ref[r * sub:(r + 1) * sub, :], wb)
                for s in range(sub // seg):
                    lo = r * sub + s * seg
                    x = p_all[s * seg:(s + 1) * seg]
                    if seq_rows is not None:
                        prev = buf_ref[lo // seg]
                    for h in range(tn // head_dim):
                        hs = slice(h * head_dim, (h + 1) * head_dim)
                        yh = _silu(_causal_conv(x[:, hs], prev[:, hs], [w[:, hs] for w in taps]))
                        if norm_scale is not None:
                            yh = yh * lax.rsqrt(jnp.sum(yh * yh, axis=-1, keepdims=True) + EPS)
                            if norm_scale != 1.0:
                                yh = yh * norm_scale
                        o_ref[lo:lo + seg, hs] = yh
                    prev = x[seg - SUBLANES:, :]
                    if seq_rows is not None:
                        tail_ref[lo // seg] = prev
            if seq_rows is None:
                carry_ref[...] = prev
                tail_ref[0] = prev
        _on_group_turn(i, starts, gi, run)


def _conv_proj(a_s, w_t, layer, off, n, conv_w, buf8_s, coff, t_s, tm_s, tn, head_dim, norm_scale, name):
    k = a_s[0].shape[1]
    nj = n // tn
    blk0 = off // tn
    shift = off - blk0 * tn
    assert shift < LANES and shift % SUBLANES == 0 and n % tn == 0 and coff % tn == 0
    cblk = coff // tn
    per = tn // LANES
    starts, idx = _group_steps([a.shape[0] for a in a_s], tm_s)
    tilings = [_seq_tiling(t, tm) for t, tm in zip(t_s, tm_s)]
    in_specs, args, out_specs, out_shape, scratch = [], [], [], [], []
    for a, buf8, tm, f, (tps, _, nb) in zip(a_s, buf8_s, tm_s, idx, tilings):
        ni = a.shape[0] // tm
        in_specs += [pl.BlockSpec((tm, k), functools.partial(lambda j, i, f: (f(i), 0), f=f)),
                     pl.BlockSpec((nb, SUBLANES, tn),
                                  functools.partial(lambda j, i, f, tps: (f(i) // tps, 0, cblk + j), f=f, tps=tps))]
        args += [a, buf8]
        out_specs += [pl.BlockSpec((tm, tn), functools.partial(lambda j, i, f: (f(i), j), f=f)),
                      pl.BlockSpec((nb, SUBLANES, tn), functools.partial(lambda j, i, f: (f(i), 0, j), f=f))]
        out_shape += [jax.ShapeDtypeStruct((a.shape[0], n), F32),
                      jax.ShapeDtypeStruct((ni * nb, SUBLANES, n), F32)]
        scratch.append(pltpu.VMEM((SUBLANES, tn), F32))
    in_specs.append(pl.BlockSpec((1, tn, k), lambda j, i: (layer, blk0 + j, 0)))
    args.append(w_t)
    if shift:
        in_specs.append(pl.BlockSpec((1, LANES, k), lambda j, i: (layer, (blk0 + j + 1) * per, 0)))
        args.append(w_t)
    in_specs.append(pl.BlockSpec((conv_w.shape[0], tn), lambda j, i: (0, cblk + j)))
    args.append(conv_w)
    outs = pl.pallas_call(
        functools.partial(_conv_proj_kernel, shift=shift, starts=tuple(starts), tilings=tuple(tilings),
                          head_dim=head_dim, norm_scale=norm_scale),
        grid=(nj, starts[-1]),
        in_specs=in_specs,
        out_specs=out_specs,
        out_shape=out_shape,
        scratch_shapes=[pltpu.VMEM((tn, k), BF16)] + scratch,
        compiler_params=_cp("parallel", "arbitrary"),
        name=name,
    )(*args)
    return [(outs[2 * gi], outs[2 * gi + 1][tilings[gi][0] - 1::tilings[gi][0]]) for gi in range(len(a_s))]


def _proj_res_kernel(*refs, sub, final, starts, chained):
    ng = len(starts) - 1
    nout = 1 if final else 2
    a_refs, x_refs = refs[0:2 * ng:2], refs[1:2 * ng:2]
    w_ref, g_ref = refs[2 * ng], refs[2 * ng + 1]
    nw = 3 if chained else 2
    w2_ref = refs[2 * ng + 2] if chained else None
    outs = refs[2 * ng + nw:]
    i = pl.program_id(0)
    for gi in range(ng):
        def run(a_ref=a_refs[gi], x_ref=x_refs[gi], o=outs[gi * nout:(gi + 1) * nout]):
            tm = a_ref.shape[0]
            sb = min(sub, tm)
            w = w_ref[...]
            g = g_ref[...]
            for r in range(tm // sb):
                rows = slice(r * sb, (r + 1) * sb)
                xn = x_ref[rows, :] + _dot(a_ref[rows, :], w)
                if final:
                    o[0][rows, :] = _rms(xn, g)
                else:
                    o[0][rows, :] = xn
                    hn = _rms(xn, g).astype(BF16)
                    o[1][rows, :] = _dot(hn, w2_ref[...]).astype(o[1].dtype) if chained else hn
        _on_group_turn(i, starts, gi, run)


def _proj_res(a_s, w, x_s, g, tms, sub, final, name, w2=None):
    kk, d = w.shape
    starts, idx = _group_steps([a.shape[0] for a in a_s], tms)
    in_specs, args, out_specs, out_shape = [], [], [], []
    for a, x, tm, f in zip(a_s, x_s, tms, idx):
        row = functools.partial(lambda i, f: (f(i), 0), f=f)
        in_specs += [pl.BlockSpec((tm, kk), row), pl.BlockSpec((tm, d), row)]
        args += [a, x]
        out_specs.append(pl.BlockSpec((tm, d), row))
        out_shape.append(jax.ShapeDtypeStruct((a.shape[0], d), F32))
        if not final:
            out_specs.append(pl.BlockSpec((tm, d), row))
            out_shape.append(jax.ShapeDtypeStruct((a.shape[0], d), BF16))
    chained = w2 is not None
    assert not (chained and final)
    w_specs = [pl.BlockSpec((kk, d), lambda i: (0, 0), pipeline_mode=pl.Buffered(1)),
               pl.BlockSpec((1, d), lambda i: (0, 0))]
    w_args = [w, g.reshape(1, d)]
    if chained:
        w_specs.append(pl.BlockSpec(w2.shape, lambda i: (0, 0), pipeline_mode=pl.Buffered(1)))
        w_args.append(w2)
    outs = pl.pallas_call(
        functools.partial(_proj_res_kernel, sub=sub, final=final, starts=tuple(starts), chained=chained),
        grid=(starts[-1],),
        in_specs=in_specs + w_specs,
        out_specs=out_specs,
        out_shape=out_shape,
        compiler_params=_cp("arbitrary"),
        name=name,
    )(*args, *w_args)
    nout = 1 if final else 2
    return [tuple(outs[gi * nout:(gi + 1) * nout]) for gi in range(len(a_s))]


def _merge_kernel(*refs, sub, starts):
    ng = len(starts) - 1
    per = N_BRANCH + 1
    w_ref, b_ref = refs[per * ng], refs[per * ng + 1]
    o_refs = refs[per * ng + 2:]
    i = pl.program_id(0)
    for gi in range(ng):
        def run(y_refs=refs[gi * per:gi * per + N_BRANCH], g_ref=refs[gi * per + N_BRANCH], o_ref=o_refs[gi]):
            tm, d = o_ref.shape
            sb = min(sub, tm)
            for r in range(tm // sb):
                rows = slice(r * sb, (r + 1) * sb)
                acc = None
                for n, y_ref in enumerate(y_refs):
                    p = _dot(y_ref[rows, :], w_ref[n])
                    g = g_ref[rows, n * d:(n + 1) * d].astype(F32)
                    t = _sigmoid(g + b_ref[n:n + 1, :]) * p
                    acc = t if acc is None else acc + t
                o_ref[rows, :] = acc.astype(o_ref.dtype)
        _on_group_turn(i, starts, gi, run)


def _merge(ys_s, w_branch, g_br_s, b_gate, tms):
    _, bw, d = w_branch.shape
    starts, idx = _group_steps([g.shape[0] for g in g_br_s], tms)
    in_specs, args, out_specs, out_shape = [], [], [], []
    for ys, g_br, tm, f in zip(ys_s, g_br_s, tms, idx):
        row = functools.partial(lambda i, f: (f(i), 0), f=f)
        in_specs += [pl.BlockSpec((tm, bw), row)] * N_BRANCH + [pl.BlockSpec((tm, N_BRANCH * d), row)]
        args += list(ys) + [g_br]
        out_specs.append(pl.BlockSpec((tm, d), row))
        out_shape.append(jax.ShapeDtypeStruct((g_br.shape[0], d), BF16))
    return pl.pallas_call(
        functools.partial(_merge_kernel, sub=ROW_SUB, starts=tuple(starts)),
        grid=(starts[-1],),
        in_specs=in_specs + [pl.BlockSpec((N_BRANCH, bw, d), lambda i: (0, 0, 0), pipeline_mode=pl.Buffered(1)),
                             pl.BlockSpec((N_BRANCH, d), lambda i: (0, 0))],
        out_specs=out_specs,
        out_shape=out_shape,
        compiler_params=_cp("arbitrary"),
        name="branch_merge",
    )(*args, w_branch, b_gate)


def _xattn_kernel(q_ref, k_ref, v_ref, o_ref, *scratch, scale, dh):
    heads = range(q_ref.shape[1] // dh)
    hs = lambda h: slice(h * dh, (h + 1) * dh)
    if scratch:
        kb_ref, vb_ref = scratch

        @pl.when(pl.program_id(1) == 0)
        def _():
            for h in heads:
                kb_ref[:, hs(h)] = k_ref[0, 0, :, h, :].astype(BF16)
                vb_ref[:, hs(h)] = v_ref[0, 0, :, h, :].astype(BF16)
    else:
        kb_ref, vb_ref = k_ref.at[0], v_ref.at[0]

    s = [_dot_nt(q_ref[:, hs(h)], kb_ref[:, hs(h)]) * scale for h in heads]
    e = [jnp.exp(x - jnp.max(x, axis=-1, keepdims=True)) for x in s]
    p = [(x / jnp.sum(x, axis=-1, keepdims=True)).astype(BF16) for x in e]
    for h in heads:
        o_ref[:, hs(h)] = _dot(p[h], vb_ref[:, hs(h)]).astype(o_ref.dtype)


def _xattn(q, mem_k, mem_v, layer, t, tq):
    m, d = q.shape
    b = m // t
    dh = d // X_H
    nq = t // tq
    if mem_k.ndim == 5:
        nmem = mem_k.shape[2]
        kv_spec = pl.BlockSpec((1, 1, nmem, X_H, dh), lambda bi, i: (layer, bi, 0, 0, 0))
        scratch = [pltpu.VMEM((nmem, d), BF16), pltpu.VMEM((nmem, d), BF16)]
    else:
        nmem = mem_k.shape[1] // b
        kv_spec = pl.BlockSpec((1, nmem, d), lambda bi, i: (layer, bi, 0))
        scratch = []
    qo_spec = pl.BlockSpec((tq, d), lambda bi, i: (bi * nq + i, 0))
    return pl.pallas_call(
        functools.partial(_xattn_kernel, scale=dh ** -0.5, dh=dh),
        grid=(b, nq),
        in_specs=[qo_spec, kv_spec, kv_spec],
        out_specs=qo_spec,
        out_shape=jax.ShapeDtypeStruct((m, d), BF16),
        scratch_shapes=scratch,
        compiler_params=_cp("parallel", "arbitrary"),
        name="xattn",
    )(q, mem_k, mem_v)


def _mem_proj_kernel(x_ref, g_ref, w_ref, o_ref, ob_ref, wb_ref):
    @pl.when(pl.program_id(1) == 0)
    def _():
        _cast_weight_tile(w_ref, None, wb_ref, 0)

    _, nb, nmem, nh, dh = o_ref.shape
    res = _dot(_rms(x_ref[...], g_ref[0]).astype(BF16), wb_ref[...])
    ob_ref[0] = res.astype(BF16)
    for bb in range(nb):
        for h in range(nh):
            o_ref[0, bb, :, h, :] = res[bb * nmem:(bb + 1) * nmem, h * dh:(h + 1) * dh]


def _mem_proj(mem2, ln_mem, w_xkv, half, bsz, nmem, name):
    depth, d, _ = w_xkv.shape
    dh = d // X_H
    nb = 1
    return pl.pallas_call(
        _mem_proj_kernel,
        grid=(depth, bsz // nb),
        in_specs=[pl.BlockSpec((nb * nmem, d), lambda l, i: (i, 0)),
                  pl.BlockSpec((1, 1, d), lambda l, i: (l, 0, 0)),
                  pl.BlockSpec((1, d, d), lambda l, i: (l, 0, half), pipeline_mode=pl.Buffered(1))],
        out_specs=[pl.BlockSpec((1, nb, nmem, X_H, dh), lambda l, i: (l, i, 0, 0, 0)),
                   pl.BlockSpec((1, nb * nmem, d), lambda l, i: (l, i, 0))],
        out_shape=[jax.ShapeDtypeStruct((depth, bsz, nmem, X_H, dh), F32),
                   jax.ShapeDtypeStruct((depth, bsz * nmem, d), BF16)],
        scratch_shapes=[pltpu.VMEM((d, d), BF16)],
        compiler_params=_cp("arbitrary", "arbitrary"),
        name=name,
    )(mem2, ln_mem[:, None, :], w_xkv)


def _gelu_tanh(x):
    c = 0.7978845608028654
    return x * (0.5 * (1.0 + jnp.tanh(c * (x + 0.044715 * (x * x * x)))))


def _ffn_in_kernel(*refs, starts, tilings):
    ng = len(starts) - 1
    wg_ref, wu_ref, cw_ref, cb_ref = refs[2 * ng:2 * ng + 4]
    outs = refs[2 * ng + 4:4 * ng + 4]
    wgb_ref, wub_ref = refs[4 * ng + 4:4 * ng + 6]
    carries = refs[4 * ng + 6:]
    i = pl.program_id(1)

    @pl.when(i == 0)
    def _():
        _cast_weight_tile(wg_ref, None, wgb_ref, 0)
        _cast_weight_tile(wu_ref, None, wub_ref, 0)

    for gi in range(ng):
        def run(h_ref=refs[2 * gi], buf_ref=refs[2 * gi + 1], act_ref=outs[2 * gi], nbuf_ref=outs[2 * gi + 1],
                carry_ref=carries[gi], tiles_per_seq=tilings[gi][0], seq_rows=tilings[gi][1], lo_step=starts[gi]):
            tm = h_ref.shape[0]
            sub = min(tm, FFN_SUB)
            seg = sub if seq_rows is None else seq_rows
            if seq_rows is None:
                @pl.when((i - lo_step) % tiles_per_seq == 0)
                def _():
                    carry_ref[...] = buf_ref[0]

                prev = carry_ref[...]
            wg, wu = wgb_ref[...], wub_ref[...]
            taps, cb = [cw_ref[n:n + 1, :] for n in range(cw_ref.shape[0])], cb_ref[...]
            for r in range(tm // sub):
                h = h_ref[r * sub:(r + 1) * sub, :]
                gate_all = _dot(h, wg)
                up_all = _dot(h, wu)
                for s in range(sub // seg):
                    lo = r * sub + s * seg
                    gate = gate_all[s * seg:(s + 1) * seg]
                    if seq_rows is not None:
                        prev = buf_ref[lo // seg]
                    y = _causal_conv(gate, prev, taps)
                    act = _gelu_tanh(y + cb) * up_all[s * seg:(s + 1) * seg]
                    act_ref[lo:lo + seg, :] = act.astype(act_ref.dtype)
                    prev = gate[seg - SUBLANES:, :]
                    if seq_rows is not None:
                        nbuf_ref[lo // seg] = prev
            if seq_rows is None:
                carry_ref[...] = prev
                nbuf_ref[0] = prev
        _on_group_turn(i, starts, gi, run)


def _ffn_in(h_s, w_ffn_in, layer, conv_w, conv_b, buf8_s, t_s, tm_s, tn):
    d = h_s[0].shape[1]
    dff = w_ffn_in.shape[2] // 2
    nj = dff // tn
    starts, idx = _group_steps([h.shape[0] for h in h_s], tm_s)
    tilings = [_seq_tiling(t, tm) for t, tm in zip(t_s, tm_s)]
    in_specs, args, out_specs, out_shape, scratch = [], [], [], [], []
    for h, buf8, tm, f, (tps, _, nb) in zip(h_s, buf8_s, tm_s, idx, tilings):
        ni = h.shape[0] // tm
        in_specs += [pl.BlockSpec((tm, d), functools.partial(lambda j, i, f: (f(i), 0), f=f)),
                     pl.BlockSpec((nb, SUBLANES, tn),
                                  functools.partial(lambda j, i, f, tps: (f(i) // tps, 0, j), f=f, tps=tps))]
        args += [h, buf8]
        out_specs += [pl.BlockSpec((tm, tn), functools.partial(lambda j, i, f: (f(i), j), f=f)),
                      pl.BlockSpec((nb, SUBLANES, tn), functools.partial(lambda j, i, f: (f(i), 0, j), f=f))]
        out_shape += [jax.ShapeDtypeStruct((h.shape[0], dff), BF16),
                      jax.ShapeDtypeStruct((ni * nb, SUBLANES, dff), F32)]
        scratch.append(pltpu.VMEM((SUBLANES, tn), F32))
    outs = pl.pallas_call(
        functools.partial(_ffn_in_kernel, starts=tuple(starts), tilings=tuple(tilings)),
        grid=(nj, starts[-1]),
        in_specs=in_specs + [pl.BlockSpec((1, d, tn), lambda j, i: (layer, 0, j)),
                             pl.BlockSpec((1, d, tn), lambda j, i: (layer, 0, nj + j)),
                             pl.BlockSpec((FFN_CONV, tn), lambda j, i: (0, j)),
                             pl.BlockSpec((1, tn), lambda j, i: (0, j))],
        out_specs=out_specs,
        out_shape=out_shape,
        scratch_shapes=[pltpu.VMEM((d, tn), BF16), pltpu.VMEM((d, tn), BF16)] + scratch,
        compiler_params=_cp("parallel", "arbitrary"),
        name="ffn_in",
    )(*args, w_ffn_in, w_ffn_in, conv_w, conv_b.reshape(1, dff))
    return [(outs[2 * gi], outs[2 * gi + 1][tilings[gi][0] - 1::tilings[gi][0]]) for gi in range(len(h_s))]


def _ret_kernel(ld_ref, p_ref, cos_ref, sin_ref, s0_ref, y_ref, s_ref, *, c):
    t = pl.program_id(1)
    tb = p_ref.shape[0]
    half = RET_DK // 2
    hw = RET_H * RET_DK
    heads = range(RET_H)

    @pl.when(t == 0)
    def _():
        s_ref[...] = s0_ref[...]

    ri = lax.broadcasted_iota(jnp.int32, (c, c), 0)
    ci = lax.broadcasted_iota(jnp.int32, (c, c), 1)
    diff = (ri - ci).astype(F32)
    causal = ri >= ci
    idx = lax.broadcasted_iota(jnp.int32, (c, RET_DK), 0).astype(F32)
    lds = [ld_ref[h] for h in heads]
    dmask = [jnp.where(causal, jnp.exp(ld * jnp.where(causal, diff, 0.0)), 0.0) for ld in lds]
    q_dec = [jnp.exp(ld * (idx + 1.0)) for ld in lds]
    k_dec = [jnp.exp(ld * (c - 1.0 - idx)) for ld in lds]
    s_dec = [jnp.exp(jnp.full((1, RET_DV), ld * c, F32)) for ld in lds]

    def rot(x, cos, sin):
        x1, x2 = x[:, :half], x[:, half:]
        return jnp.concatenate([x1 * cos - x2 * sin, x1 * sin + x2 * cos], axis=-1)

    nchunk = tb // c
    nc = RET_CHUNKS_PER_STEP if nchunk % RET_CHUNKS_PER_STEP == 0 else 1

    def body(jj, carry):
        ch = []
        for ci in range(nc):
            rows = pl.ds(pl.multiple_of((jj * nc + ci) * c, c), c)
            cos, sin = cos_ref[rows, :], sin_ref[rows, :]
            k = [rot(p_ref[rows, hw + h * RET_DK:hw + (h + 1) * RET_DK], cos, sin) * (RET_DK ** -0.5)
                 for h in heads]
            ch.append(dict(
                rows=rows, k=k,
                qb=[rot(p_ref[rows, h * RET_DK:(h + 1) * RET_DK], cos, sin).astype(BF16) for h in heads],
                vb=[p_ref[rows, 2 * hw + h * RET_DV:2 * hw + (h + 1) * RET_DV].astype(BF16) for h in heads]))
        probs = [(d, h) for d in ch for h in heads]
        att = [(_dot_nt(d['qb'][h], d['k'][h].astype(BF16)) * dmask[h]).astype(BF16) for d, h in probs]
        av = [_dot(a, d['vb'][h]) for a, (d, h) in zip(att, probs)]
        kv = [_dot_tn((d['k'][h] * k_dec[h]).astype(BF16), d['vb'][h]) for d, h in probs]

        for ci, d in enumerate(ch):
            s = [s_ref[0, h] for h in heads]
            o = [av[ci * RET_H + h] + _dot(d['qb'][h], s[h].astype(BF16)) * q_dec[h] for h in heads]
            for h in heads:
                s_ref[0, h] = s_dec[h] * s[h] + kv[ci * RET_H + h]
            for h in heads:
                g = p_ref[d['rows'], 3 * hw + h * RET_DV:3 * hw + (h + 1) * RET_DV]
                y_ref[d['rows'], h * RET_DV:(h + 1) * RET_DV] = (_rms(o[h]) * _silu(g)).astype(y_ref.dtype)
        return carry

    lax.fori_loop(0, nchunk // nc, body, 0)


def _retention(p_ret, cos, sin, ld, s0, t, tb):
    m, width = p_ret.shape
    b = m // t
    c = min(CHUNK, t)
    nt = t // tb
    st_spec = pl.BlockSpec((1, RET_H, RET_DK, RET_DV), lambda bi, ti: (bi, 0, 0, 0))
    rope_spec = pl.BlockSpec((tb, RET_DK // 2), lambda bi, ti: (ti, 0))
    return pl.pallas_call(
        functools.partial(_ret_kernel, c=c),
        grid=(b, nt),
        in_specs=[pl.BlockSpec(memory_space=pltpu.SMEM),
                  pl.BlockSpec((tb, width), lambda bi, ti: (bi * nt + ti, 0)),
                  rope_spec, rope_spec, st_spec],
        out_specs=[pl.BlockSpec((tb, RET_H * RET_DV), lambda bi, ti: (bi * nt + ti, 0)), st_spec],
        out_shape=[jax.ShapeDtypeStruct((m, RET_H * RET_DV), BF16),
                   jax.ShapeDtypeStruct((b, RET_H, RET_DK, RET_DV), F32)],
        compiler_params=_cp("parallel", "arbitrary"),
        name="retention",
    )(ld, p_ret, cos, sin, s0)


def _gla_kernel(p_ref, lr_ref, wgk_ref, bgk_ref, nrm_ref, s0_ref, y_ref, s_ref, st_ref, *, c, nt):
    t = pl.program_id(1)
    tb = p_ref.shape[0]
    hk = GLA_H * GLA_DK
    hv = GLA_H * GLA_DV
    heads = range(GLA_H)

    @pl.when(t == 0)
    def _():
        for h in heads:
            st_ref[h] = s0_ref[0, h].T

    ri = lax.broadcasted_iota(jnp.int32, (c, c), 0)
    ci = lax.broadcasted_iota(jnp.int32, (c, c), 1)
    causal = ri >= ci
    tri = jnp.where(causal, 1.0, 0.0).astype(BF16)
    wgk = wgk_ref[...]
    bgk = bgk_ref[...]
    nrm = nrm_ref[...]

    nchunk = tb // c
    nc = GLA_CHUNKS_PER_STEP if nchunk % GLA_CHUNKS_PER_STEP == 0 else 1
    hs = lambda x, h: x[:, h * GLA_DK:(h + 1) * GLA_DK]

    def body(jj, carry):
        rows = [pl.ds(pl.multiple_of((jj * nc + ci) * c, c), c) for ci in range(nc)]
        z = [_dot(lr_ref[r, :].astype(BF16), wgk) + bgk for r in rows]
        bc = [_tri_cumsum(tri, -_softplus(-zz) / GLA_NORMALIZER) for zz in z]
        ch = []
        for r, b_ in zip(rows, bc):
            q = p_ref[r, 0:hk] * (GLA_DK ** -0.5)
            k = p_ref[r, hk:2 * hk]
            b_last = b_[c - 1:c, :]
            ch.append(dict(
                rows=r, qe=(q * jnp.exp(b_)).astype(BF16), ke=(k * jnp.exp(-b_)).astype(BF16),
                kd=(k * jnp.exp(b_last - b_)).astype(BF16), eb_last=jnp.exp(b_last),
                vb=[p_ref[r, 2 * hk + h * GLA_DV:2 * hk + (h + 1) * GLA_DV].astype(BF16) for h in heads]))
        probs = [(d, h) for d in ch for h in heads]
        att = [jnp.where(causal, _dot_nt(hs(d['qe'], h), hs(d['ke'], h)), 0.0).astype(BF16) for d, h in probs]
        av = [_dot(a, d['vb'][h]) for a, (d, h) in zip(att, probs)]

        for ci, d in enumerate(ch):
            st = [st_ref[h] for h in heads]
            o = [av[ci * GLA_H + h] + _dot_nt(hs(d['qe'], h), st[h].astype(BF16)) for h in heads]
            for h in heads:
                st_ref[h] = hs(d['eb_last'], h) * st[h] + _dot_tn(d['vb'][h], hs(d['kd'], h))
            for h in heads:
                g = p_ref[d['rows'], 2 * hk + hv + h * GLA_DV:2 * hk + hv + (h + 1) * GLA_DV]
                y_ref[d['rows'], h * GLA_DV:(h + 1) * GLA_DV] = (_rms(o[h], nrm) * _silu(g)).astype(y_ref.dtype)
        return carry

    lax.fori_loop(0, nchunk // nc, body, 0)

    @pl.when(t == nt - 1)
    def _():
        for h in heads:
            s_ref[0, h] = st_ref[h].T


def _gla(p_gla, p_small, wgk, bgk, nrm, s0, t, tb):
    m, width = p_gla.shape
    b = m // t
    c = min(CHUNK, t)
    nt = t // tb
    const = lambda bi, ti: (0, 0)
    st_spec = pl.BlockSpec((1, GLA_H, GLA_DK, GLA_DV), lambda bi, ti: (bi, 0, 0, 0))
    return pl.pallas_call(
        functools.partial(_gla_kernel, c=c, nt=nt),
        grid=(b, nt),
        in_specs=[pl.BlockSpec((tb, width), lambda bi, ti: (bi * nt + ti, 0)),
                  pl.BlockSpec((tb, LANES), lambda bi, ti: (bi * nt + ti, 0)),
                  pl.BlockSpec((LANES, GLA_H * GLA_DK), const),
                  pl.BlockSpec((1, GLA_H * GLA_DK), const),
                  pl.BlockSpec((1, GLA_DV), const),
                  st_spec],
        out_specs=[pl.BlockSpec((tb, GLA_H * GLA_DV), lambda bi, ti: (bi * nt + ti, 0)), st_spec],
        out_shape=[jax.ShapeDtypeStruct((m, GLA_H * GLA_DV), BF16),
                   jax.ShapeDtypeStruct((b, GLA_H, GLA_DK, GLA_DV), F32)],
        scratch_shapes=[pltpu.VMEM((GLA_H, GLA_DV, GLA_DK), F32)],
        compiler_params=_cp("parallel", "arbitrary"),
        name="gla",
    )(p_gla, p_small, wgk, bgk, nrm, s0)


def _blockdiag2(xp, left):
    return jnp.concatenate([jnp.where(left, xp, 0.0), jnp.where(left, 0.0, xp)], axis=0)


def _unit_lower_inverse_minus_eye_packed(mats, c):
    ri = lax.broadcasted_iota(jnp.int32, (c, 2 * c), 0)
    cl = lax.broadcasted_iota(jnp.int32, (c, 2 * c), 1)
    left = cl < c
    cm = cl & (c - 1)

    def level_mask(k):
        return ((ri ^ cm) < 2 * k) & ((ri & k) != 0) & ((cm & k) == 0)

    m1 = level_mask(1)
    ns = [-jnp.where(m1, a, 0.0) for a in mats]
    k = 2
    while k < c:
        mk = level_mask(k)
        ls = [jnp.where(mk, a, 0.0) for a in mats]
        ys = [l + _dot(n.astype(BF16), _blockdiag2(l, left).astype(BF16)) for l, n in zip(ls, ns)]
        xs = [y + _dot(y.astype(BF16), _blockdiag2(n, left).astype(BF16)) for y, n in zip(ys, ns)]
        ns = [n - x for n, x in zip(ns, xs)]
        k *= 2
    return ns


def _gdn_kernel(q_ref, k_ref, v_ref, z_ref, sm_ref, alog_ref, dtb_ref, nrm_ref, s0_ref,
                y_ref, s_ref, *, c):
    t = pl.program_id(1)
    tb = q_ref.shape[0]

    @pl.when(t == 0)
    def _():
        s_ref[...] = s0_ref[...]

    tri = jnp.where(lax.broadcasted_iota(jnp.int32, (c, c), 0) >= lax.broadcasted_iota(jnp.int32, (c, c), 1),
                    1.0, 0.0).astype(BF16)
    ri = lax.broadcasted_iota(jnp.int32, (c, 2 * c), 0)
    cl = lax.broadcasted_iota(jnp.int32, (c, 2 * c), 1)
    left = cl < c
    cm = cl & (c - 1)
    incl = ri >= cm
    strict = ri > cm
    lane = lax.broadcasted_iota(jnp.int32, (1, LANES), 1)
    a_lanes = jnp.logical_and(lane >= SMALL_A0, lane < SMALL_A0 + GDN_H)
    sel2 = jnp.where(lax.broadcasted_iota(jnp.int32, (SUBLANES, LANES), 1)
                     == 2 * lax.broadcasted_iota(jnp.int32, (SUBLANES, LANES), 0) + SMALL_A0,
                     1.0, 0.0).astype(BF16)
    neg_a = -jnp.exp(alog_ref[...])
    dtb = dtb_ref[...]
    nrm = nrm_ref[...]

    def col(x, lane_idx):
        return x[:, lane_idx:lane_idx + 1]

    assert GDN_DK == LANES and GDN_DV == LANES and GDN_H % 2 == 0
    heads = range(GDN_H)
    npair = GDN_H // 2
    pairs = range(npair)
    nchunk = tb // c
    nc = GDN_CHUNKS_PER_STEP if nchunk % GDN_CHUNKS_PER_STEP == 0 else 1

    def lanes2(x, y):
        return jnp.concatenate([x, y], axis=1)

    def body(jj, carry):
        ch = []
        for ci in range(nc):
            rows = pl.ds(pl.multiple_of((jj * nc + ci) * c, c), c)
            sm = sm_ref[rows, :]
            ch.append(dict(
                rows=rows,
                g_all=jnp.where(a_lanes, neg_a * _softplus(sm + dtb), 0.0),
                beta_all=_sigmoid(sm),
                q=[q_ref[rows, h * GDN_DK:(h + 1) * GDN_DK] for h in heads],
                k=[k_ref[rows, h * GDN_DK:(h + 1) * GDN_DK] for h in heads],
                v=[v_ref[rows, h * GDN_DV:(h + 1) * GDN_DV] for h in heads]))
        for d in ch:
            d['gam'] = _tri_cumsum(tri, d['g_all'])
        for d in ch:
            g2 = jnp.concatenate([d['gam'], pltpu.roll(d['gam'], LANES - 1, 1)], axis=0)
            ghi, gmid, glo = _split3(g2)
            d['gam_rows'] = _dot_nt(sel2, ghi) + _dot_nt(sel2, gmid) + _dot_nt(sel2, glo)

        def packed_cols(x, base, p):
            return jnp.where(left, col(x, base + 2 * p), col(x, base + 2 * p + 1))

        for d in ch:
            gam = d['gam']
            g_last = gam[c - 1:c, :]
            d['eg_all'] = jnp.exp(gam)
            d['egl_all'] = jnp.exp(g_last - gam)
            d['eg_last_all'] = jnp.exp(g_last)
            d['beta'] = [col(d['beta_all'], SMALL_B0 + h) for h in heads]
            d['eg'] = [col(d['eg_all'], SMALL_A0 + h) for h in heads]
            d['beta_p'] = [packed_cols(d['beta_all'], SMALL_B0, p) for p in pairs]
            d['decay_p'] = [jnp.where(incl, jnp.exp(jnp.where(
                incl, packed_cols(gam, SMALL_A0, p) - d['gam_rows'][p:p + 1, :], 0.0)), 0.0) for p in pairs]
        probs = [(d, p) for d in ch for p in pairs]
        zk = jnp.zeros((c, GDN_DK), F32)
        qk_kk = [_dot_nt(
            jnp.concatenate([lanes2(d['q'][2 * p], d['q'][2 * p + 1]),
                             lanes2(d['k'][2 * p], d['k'][2 * p + 1])], axis=0).astype(BF16),
            jnp.concatenate([lanes2(d['k'][2 * p], zk), lanes2(zk, d['k'][2 * p + 1])], axis=0).astype(BF16))
            for d, p in probs]
        att = [(r[:c] * d['decay_p'][p]).astype(BF16) for r, (d, p) in zip(qk_kk, probs)]
        a = [jnp.where(strict, d['beta_p'][p] * r[c:] * d['decay_p'][p], 0.0) for r, (d, p) in zip(qk_kk, probs)]
        n = _unit_lower_inverse_minus_eye_packed(a, c)
        zr = jnp.zeros((c, GDN_DV + GDN_DK), F32)
        sol = []
        for (d, p), nn in zip(probs, n):
            r1, r2 = (jnp.concatenate([d['beta'][h] * d['v'][h], (d['beta'][h] * d['eg'][h]) * d['k'][h]], axis=-1)
                      for h in (2 * p, 2 * p + 1))
            bd = jnp.concatenate([lanes2(r1, zr), lanes2(zr, r2)], axis=0).astype(BF16)
            sol.append(lanes2(r1, r2) + _dot(nn.astype(BF16), bd))

        zs = jnp.zeros((GDN_DK, GDN_DV), F32)
        zw = jnp.zeros((c, GDN_DV), F32)
        for ci, d in enumerate(ch):
            sl = sol[ci * npair:(ci + 1) * npair]
            at = att[ci * npair:(ci + 1) * npair]
            s = [s_ref[0, h] for h in heads]
            wq_s = [_dot(
                jnp.concatenate([lanes2(sl[p][:, LANES:2 * LANES], sl[p][:, 3 * LANES:]),
                                 lanes2(d['q'][2 * p], d['q'][2 * p + 1])], axis=0).astype(BF16),
                jnp.concatenate([lanes2(s[2 * p], zs), lanes2(zs, s[2 * p + 1])], axis=0).astype(BF16))
                for p in pairs]
            bdw = [jnp.concatenate([lanes2(sl[p][:, :LANES] - wq_s[p][:c, :LANES], zw),
                                    lanes2(zw, sl[p][:, 2 * LANES:3 * LANES] - wq_s[p][:c, LANES:])],
                                   axis=0).astype(BF16) for p in pairs]
            aw = [_dot(at[p], bdw[p]) for p in pairs]
            kw = [_dot_tn(jnp.concatenate([d['k'][h] * col(d['egl_all'], SMALL_A0 + h)
                                           for h in (2 * p, 2 * p + 1)], axis=0).astype(BF16), bdw[p])
                  for p in pairs]
            for h in heads:
                p, half = h // 2, slice((h % 2) * LANES, (h % 2 + 1) * LANES)
                s_ref[0, h] = col(d['eg_last_all'], SMALL_A0 + h) * s[h] + kw[p][:, half]
            for h in heads:
                p, half = h // 2, slice((h % 2) * LANES, (h % 2 + 1) * LANES)
                o = d['eg'][h] * wq_s[p][c:, half] + aw[p][:, half]
                z = z_ref[d['rows'], h * GDN_DV:(h + 1) * GDN_DV]
                y = _rms(o, nrm) * _silu(z)
                y_ref[d['rows'], h * GDN_DV:(h + 1) * GDN_DV] = y.astype(y_ref.dtype)
        return carry

    lax.fori_loop(0, nchunk // nc, body, 0)


def _gdn(q, k, v, z, p_small, alog_row, dtb_row, nrm, s0, t, tb):
    m = q.shape[0]
    b = m // t
    c = min(CHUNK, t)
    nt = t // tb
    st_spec = pl.BlockSpec((1, GDN_H, GDN_DK, GDN_DV), lambda bi, ti: (bi, 0, 0, 0))
    const = lambda bi, ti: (0, 0)
    row = lambda bi, ti: (bi * nt + ti, 0)
    qk_spec = pl.BlockSpec((tb, GDN_H * GDN_DK), row)
    v_spec = pl.BlockSpec((tb, GDN_H * GDN_DV), row)
    return pl.pallas_call(
        functools.partial(_gdn_kernel, c=c),
        grid=(b, nt),
        in_specs=[qk_spec, qk_spec, v_spec, v_spec,
                  pl.BlockSpec((tb, LANES), row),
                  pl.BlockSpec((1, LANES), const),
                  pl.BlockSpec((1, LANES), const),
                  pl.BlockSpec((1, GDN_DV), const),
                  st_spec],
        out_specs=[v_spec, st_spec],
        out_shape=[jax.ShapeDtypeStruct((m, GDN_H * GDN_DV), BF16),
                   jax.ShapeDtypeStruct((b, GDN_H, GDN_DK, GDN_DV), F32)],
        compiler_params=_cp("parallel", "arbitrary"),
        name="gated_deltanet",
    )(q, k, v, z, p_small, alog_row, dtb_row, nrm, s0)


def _pad_rows_front(buf, rows):
    return jnp.pad(buf, ((0, 0), (0, 0), (rows - buf.shape[2], 0), (0, 0)))


def _in_proj_windows(d):
    sizes = dict(ret=4 * RET_H * RET_DK, gla=2 * GLA_H * GLA_DK + 2 * GLA_H * GLA_DV, lr=GLA_RANK,
                 gdn=2 * GDN_H * GDN_DK + 2 * GDN_H * GDN_DV, ab=2 * GDN_H, gbr=N_BRANCH * d)
    win, o = {}, 0
    for name in ('ret', 'gla', 'lr', 'gdn', 'ab', 'gbr'):
        win[name] = (o, sizes[name])
        o += sizes[name]
    assert win['lr'][0] % LANES == 0 and win['ab'][0] % LANES == SMALL_A0
    return win


def _prep_weights(w):
    lane_row = lambda v, at: jnp.pad(v, ((0, 0), (at, LANES - at - v.shape[1])))[:, None, :]
    return dict(
        w_in_t=jnp.swapaxes(w['w_in'], 1, 2), w_xq=w['w_xq'].astype(BF16), w_ffn_in=w['w_ffn_in'],
        w_gk=jnp.pad(w['w_gla_gk'], ((0, 0), (0, LANES - GLA_RANK), (0, 0))).astype(BF16),
        b_gk=w['b_gla_gk'][:, None, :],
        gla_norm=w['gla_norm'][:, None, :], gdn_norm=w['gdn_norm'][:, None, :],
        alog_row=lane_row(w['gdn_a_log'], SMALL_A0), dtb_row=lane_row(w['gdn_dt_bias'], SMALL_A0),
        gdn_conv_w=w['gdn_conv_w'], b_gate=w['b_gate'],
        w_branch=w['w_branch'].astype(BF16), w_out=w['w_out'].astype(BF16), w_xo=w['w_xo'].astype(BF16),
        ffn_conv_w=w['ffn_conv_w'], ffn_conv_b=w['ffn_conv_b'], w_ffn_out=w['w_ffn_out'].astype(BF16),
        ln_mix=w['ln_mix'], ln_xattn=w['ln_xattn'], ln_ffn=w['ln_ffn'], ln_final=w['ln_final'],
    )


def _tile(n, cap):
    t = min(n, cap)
    while n % t:
        t -= 1
    return t


class _Group:
    def __init__(self, x, offset, mem_k, mem_v, s_ret, s_gla, s_gdn, buf_gdn, buf_ffn):
        self.b, self.t, d = x.shape
        b, t = self.b, self.t
        self.m = m = b * t
        self.x2 = x.reshape(m, d)
        self.mem_k, self.mem_v = mem_k, mem_v
        self.s_ret, self.s_gla, self.s_gdn = s_ret, s_gla, s_gdn
        self.buf_gdn8 = _pad_rows_front(buf_gdn, SUBLANES)
        self.buf_ffn8 = _pad_rows_front(buf_ffn, SUBLANES)
        self.tm = _tile(m, MM_TM)
        self.tm_res = _tile(m, RES_TM)
        self.tm_res_k = _tile(m, RES_TM_K)
        self.tm_ffo = _tile(m, FFO_TM)
        self.tb = _tile(t, SEQ_TB)
        self.tb_gdn = _tile(t, SEQ_TB_GDN)
        self.tq = _tile(t, XATTN_TQ)
        self.tm_ffn = _tile(t, FFN_TM) if t >= FFN_TM else t * _tile(b, max(1, FFN_TM // t))
        pos = offset + jnp.arange(t, dtype=F32)
        half = RET_DK // 2
        freqs = 1.0 / (ROPE_BASE ** (jnp.arange(half, dtype=F32) / half))
        ang = pos[:, None] * freqs[None, :]
        self.cos, self.sin = jnp.cos(ang), jnp.sin(ang)
        self.outs = ([], [], [], [], [])
        self.hn = self.y = None


def _trunks(groups, pw):
    d = groups[0].x2.shape[1]
    depth = groups[0].s_ret.shape[0]
    ld = jnp.log1p(-jnp.exp2(-5.0 - jnp.arange(RET_H, dtype=F32)))
    win = _in_proj_windows(d)
    tms = tuple(g.tm for g in groups)
    order = sorted(range(len(groups)), key=lambda gi: groups[gi].m)

    def shared(acts, w, l, off, n, out_dtype, name, transposed):
        blk0 = off // MM_TN
        shift = off - blk0 * MM_TN
        assert shift < LANES and n % MM_TN == 0
        outs = _mm_ws(tuple(acts[gi] for gi in order), w, l, blk0, n, shift, out_dtype,
                      tuple(tms[gi] for gi in order), MM_TN, name, transposed=transposed)
        return [outs[order.index(gi)] for gi in range(len(acts))]

    for g in groups:
        g.hn = _norm(g.x2, pw['ln_mix'][0], g.tm_res_k)
    for l in range(depth):
        hns = [g.hn for g in groups]
        hq, hv = GDN_H * GDN_DK, GDN_H * GDN_DV
        g_off = win['gdn'][0]
        p_ret = shared(hns, pw['w_in_t'], l, *win['ret'], F32, "in_proj_ret", True)
        p_gla = shared(hns, pw['w_in_t'], l, *win['gla'], F32, "in_proj_gla", True)
        d_z = shared(hns, pw['w_in_t'], l, g_off + 2 * hq + hv, hv, F32, "in_proj_gdn_z", True)
        g_br = shared(hns, pw['w_in_t'], l, *win['gbr'], BF16, "in_proj_gbr", True)
        def conv(off, n, coff, head_dim, scale, name):
            r = _conv_proj([hns[gi] for gi in order], pw['w_in_t'], l, off, n, pw['gdn_conv_w'][l],
                           [groups[gi].buf_gdn8[l] for gi in order], coff, [groups[gi].t for gi in order],
                           [tms[gi] for gi in order], MM_TN, head_dim, scale, name)
            return [r[order.index(gi)] for gi in range(len(groups))]

        cq = conv(g_off, hq, 0, GDN_DK, GDN_DK ** -0.5, "in_proj_gdn_q")
        ck = conv(g_off + hq, hq, hq, GDN_DK, 1.0, "in_proj_gdn_k")
        cv = conv(g_off + 2 * hq, hv, 2 * hq, GDN_DV, None, "in_proj_gdn_v")
        ys_s = []
        for gi, g in enumerate(groups):
            t, tm, hn = g.t, g.tm, g.hn
            (d_q, tq8), (d_k, tk8), (d_v, tv8) = cq[gi], ck[gi], cv[gi]
            bg8 = jnp.concatenate([tq8, tk8, tv8], axis=-1)
            p_small = _small_proj(hn, pw['w_in_t'], l, win['lr'][0] // LANES, win['ab'][0] // LANES, tm)

            y_ret, sr = _retention(p_ret[gi], g.cos, g.sin, ld, g.s_ret[l], t, g.tb)
            y_gla, sg = _gla(p_gla[gi], p_small, pw['w_gk'][l], pw['b_gk'][l], pw['gla_norm'][l],
                             g.s_gla[l], t, g.tb)
            y_gdn, sd = _gdn(d_q, d_k, d_v, d_z[gi], p_small, pw['alog_row'][l], pw['dtb_row'][l],
                             pw['gdn_norm'][l], g.s_gdn[l], t, g.tb_gdn)

            ys_s.append((y_ret, y_gla, y_gdn))
            for lst, val in zip(g.outs[:4], (sr, sg, sd, bg8[:, SUBLANES - (GDN_CONV - 1):])):
                lst.append(val)

        xs = lambda: [g.x2 for g in groups]
        merged = _merge(ys_s, pw['w_branch'][l], g_br, pw['b_gate'][l], [g.tm_res_k for g in groups])
        res = _proj_res(merged, pw['w_out'][l], xs(), pw['ln_xattn'][l], [g.tm_ffo for g in groups], ROW_SUB,
                        False, "out_proj_xq", w2=pw['w_xq'][l])
        for g, (x2, _) in zip(groups, res):
            g.x2 = x2
        os_ = [_xattn(q, g.mem_k, g.mem_v, l, g.t, g.tq) for (_, q), g in zip(res, groups)]
        res = _proj_res(os_, pw['w_xo'][l], xs(), pw['ln_ffn'][l], [g.tm_res for g in groups], ROW_SUB,
                        False, "xattn_out")
        for g, (x2, _) in zip(groups, res):
            g.x2 = x2
        ff = _ffn_in([res[gi][1] for gi in order], pw['w_ffn_in'], l, pw['ffn_conv_w'][l], pw['ffn_conv_b'][l],
                     [groups[gi].buf_ffn8[l] for gi in order], [groups[gi].t for gi in order],
                     [groups[gi].tm_ffn for gi in order], FFN_TN)
        acts = [ff[order.index(gi)][0] for gi in range(len(groups))]
        for gi, g in enumerate(groups):
            g.outs[4].append(ff[order.index(gi)][1][:, SUBLANES - (FFN_CONV - 1):])
        last = l + 1 == depth
        res = _proj_res(acts, pw['w_ffn_out'][l], xs(), pw['ln_final'] if last else pw['ln_mix'][l + 1],
                        [g.tm_ffo for g in groups], ROW_SUB, last, "ffn_out_final" if last else "ffn_out")
        for g, r in zip(groups, res):
            if last:
                g.y, = r
            else:
                g.x2, g.hn = r
    return [(g.y.reshape(g.b, g.t, d),) + tuple(jnp.stack(lst) for lst in g.outs) for g in groups]


def kernel(x_prompt, x_sample, mem_prompt, state_ret, state_gla, state_gdn, state_gdn_conv, state_ffn_conv, cache_mem_k, cache_mem_v, ln_mix, w_in, w_gla_gk, b_gla_gk, gla_norm, gdn_conv_w, gdn_a_log, gdn_dt_bias, gdn_norm, b_gate, w_branch, w_out, ln_xattn, ln_mem, w_xq, w_xkv, w_xo, ln_ffn, w_ffn_in, ffn_conv_w, ffn_conv_b, w_ffn_out, ln_final):
    pw = _prep_weights(dict(
        w_in=w_in, w_gla_gk=w_gla_gk, b_gla_gk=b_gla_gk, gla_norm=gla_norm, gdn_conv_w=gdn_conv_w,
        gdn_a_log=gdn_a_log, gdn_dt_bias=gdn_dt_bias, gdn_norm=gdn_norm, b_gate=b_gate, w_branch=w_branch,
        w_out=w_out, w_xq=w_xq, w_xo=w_xo, w_ffn_in=w_ffn_in, ffn_conv_w=ffn_conv_w, ffn_conv_b=ffn_conv_b,
        w_ffn_out=w_ffn_out, ln_mix=ln_mix, ln_xattn=ln_xattn, ln_ffn=ln_ffn, ln_final=ln_final))
    depth = w_in.shape[0]
    bp, nmem, d = mem_prompt.shape
    dt = x_prompt.dtype

    mem2 = mem_prompt.reshape(bp * nmem, d)
    mem_k_p, mem_kb = _mem_proj(mem2, ln_mem, w_xkv, 0, bp, nmem, "mem_k")
    mem_v_p, mem_vb = _mem_proj(mem2, ln_mem, w_xkv, 1, bp, nmem, "mem_v")

    zeros = lambda *s: jnp.zeros((depth, bp) + s, dt)
    prompt = _Group(
        x_prompt, 0.0, mem_kb, mem_vb,
        zeros(RET_H, RET_DK, RET_DV), zeros(GLA_H, GLA_DK, GLA_DV), zeros(GDN_H, GDN_DK, GDN_DV),
        zeros(GDN_CONV - 1, state_gdn_conv.shape[-1]), zeros(FFN_CONV - 1, state_ffn_conv.shape[-1]))
    sample = _Group(x_sample, float(PAST_LEN), cache_mem_k, cache_mem_v,
                    state_ret, state_gla, state_gdn, state_gdn_conv, state_ffn_conv)
    ((y_p, ret_p, gla_p, gdn_p, gdn_conv_p, ffn_conv_p),
     (y_s, ret_s, gla_s, gdn_s, gdn_conv_s, ffn_conv_s)) = _trunks([prompt, sample], pw)

    return (y_p, y_s, ret_p, gla_p, gdn_p, gdn_conv_p, ffn_conv_p, mem_k_p, mem_v_p,
            ret_s, gla_s, gdn_s, gdn_conv_s, ffn_conv_s)
```

```python
import functools

import jax
import jax.numpy as jnp
from jax import lax
from jax.experimental import pallas as pl
from jax.experimental.pallas import tpu as pltpu

F32 = jnp.float32
BF16 = jnp.bfloat16

EPS = 1e-6
CHUNK = 64
RET_H, RET_DK, RET_DV = 4, 256, 256
GLA_H, GLA_DK, GLA_DV = 4, 128, 256
GLA_RANK = 16
GLA_NORMALIZER = 16.0
GDN_H, GDN_DK, GDN_DV = 8, 128, 128
GDN_CONV = 4
N_BRANCH = 3
X_H = 4
FFN_CONV = 3
ROPE_BASE = 10000.0
PAST_LEN = 4096

LANES = 128
SUBLANES = 8
SMALL_A0 = GLA_RANK
SMALL_B0 = GLA_RANK + GDN_H
VMEM_LIMIT = 52 * 1024 * 1024
MM_TM = 1024
MM_TN = 1024
RES_TM = 512
RES_TM_K = 512
FFO_TM = 256
ROW_SUB = 256
CONV_SUB = 256
SEQ_TB = 512
SEQ_TB_GDN = 512
GDN_CHUNKS_PER_STEP = 4
GLA_CHUNKS_PER_STEP = 4
RET_CHUNKS_PER_STEP = 4
XATTN_TQ = 512
FFN_TM = 1024
FFN_TN = 512
FFN_SUB = 256


def _cp(*sem):
    return pltpu.CompilerParams(dimension_semantics=sem, vmem_limit_bytes=VMEM_LIMIT)


def _dot(a, b):
    return jnp.dot(a, b, preferred_element_type=F32)


def _dot_nt(a, b):
    return lax.dot_general(a, b, (((1,), (1,)), ((), ())), preferred_element_type=F32)


def _dot_tn(a, b):
    return lax.dot_general(a, b, (((0,), (0,)), ((), ())), preferred_element_type=F32)


def _sigmoid(x):
    return 1.0 / (1.0 + jnp.exp(-x))


def _silu(x):
    return x * _sigmoid(x)


def _softplus(x):
    return jnp.maximum(x, 0.0) + jnp.log1p(jnp.exp(-jnp.abs(x)))


def _rms(x, g=None):
    y = x * lax.rsqrt(jnp.mean(x * x, axis=-1, keepdims=True) + EPS)
    return y if g is None else y * g


def _split3(x):
    hi = x.astype(BF16)
    r = x - hi.astype(F32)
    mid = r.astype(BF16)
    lo = (r - mid.astype(F32)).astype(BF16)
    return hi, mid, lo


def _tri_cumsum(tri, x):
    hi, mid, lo = _split3(x)
    return _dot(tri, hi) + _dot(tri, mid) + _dot(tri, lo)


def _norm_kernel(x_ref, g_ref, o_ref):
    o_ref[...] = _rms(x_ref[...], g_ref[...]).astype(o_ref.dtype)


def _norm(x, g, tm):
    m, d = x.shape
    return pl.pallas_call(
        _norm_kernel,
        grid=(m // tm,),
        in_specs=[pl.BlockSpec((tm, d), lambda i: (i, 0)),
                  pl.BlockSpec((1, d), lambda i: (0, 0))],
        out_specs=pl.BlockSpec((tm, d), lambda i: (i, 0)),
        out_shape=jax.ShapeDtypeStruct((m, d), BF16),
        compiler_params=_cp("parallel"),
        name="rmsnorm",
    )(x, g.reshape(1, d))


CAST_ROWS = 256


def _cast_weight_tile(w_ref, wx_ref, wb_ref, shift):
    kk, tn = wb_ref.shape
    for r in range(0, kk, CAST_ROWS):
        rows = slice(r, min(r + CAST_ROWS, kk))
        if shift == 0:
            wb_ref[rows, :] = w_ref[0, rows, :].astype(BF16)
        else:
            wide = jnp.concatenate([w_ref[0, rows, :], wx_ref[0, rows, :]], axis=1)
            wb_ref[rows, :] = pltpu.roll(wide, wide.shape[1] - shift, 1)[:, :tn].astype(BF16)


def _cast_weight_tile_t(w_ref, wx_ref, wb_ref, shift):
    tn, _ = wb_ref.shape
    for r in range(0, tn, CAST_ROWS):
        lo, hi = r + shift, min(r + CAST_ROWS, tn) + shift
        if hi <= tn:
            src = w_ref[0, lo:hi, :]
        else:
            src = jnp.concatenate([w_ref[0, lo:tn, :], wx_ref[0, 0:hi - tn, :]], axis=0)
        wb_ref[r:min(r + CAST_ROWS, tn), :] = src.astype(BF16)


def _group_steps(ms, tms):
    nis = [m // t for m, t in zip(ms, tms)]
    starts = [0]
    for n in nis:
        starts.append(starts[-1] + n)

    def idx(g):
        return lambda i: jnp.clip(i - starts[g], 0, nis[g] - 1)

    return starts, [idx(g) for g in range(len(ms))]


def _on_group_turn(i, starts, g, fn):
    if len(starts) == 2:
        fn()
    else:
        pl.when(jnp.logical_and(i >= starts[g], i < starts[g + 1]))(fn)


def _mm_ws_kernel(*refs, shift, transposed, starts):
    ng = len(starts) - 1
    a_refs, w_ref, rest = refs[:ng], refs[ng], refs[ng + 1:]
    wx_ref = None
    if shift:
        wx_ref, rest = rest[0], rest[1:]
    o_refs, wb_ref = rest[:ng], rest[ng]
    i = pl.program_id(1)

    @pl.when(i == 0)
    def _():
        (_cast_weight_tile_t if transposed else _cast_weight_tile)(w_ref, wx_ref, wb_ref, shift)

    mm = _dot_nt if transposed else _dot
    for g in range(ng):
        def run(a_ref=a_refs[g], o_ref=o_refs[g]):
            o_ref[...] = mm(a_ref[...], wb_ref[...]).astype(o_ref.dtype)
        _on_group_turn(i, starts, g, run)


def _mm_ws(a, w, layer, blk0, n, shift, out_dtype, tm, tn, name, transposed=False):
    single = not isinstance(a, (tuple, list))
    a_s, tms = ((a,), (tm,)) if single else (tuple(a), tuple(tm))
    k = a_s[0].shape[1]
    nj = n // tn
    per = tn // LANES
    starts, idx = _group_steps([x.shape[0] for x in a_s], tms)
    if transposed:
        assert shift % SUBLANES == 0
        w_spec = pl.BlockSpec((1, tn, k), lambda j, i: (layer, blk0 + j, 0))
        wx_spec = pl.BlockSpec((1, LANES, k), lambda j, i: (layer, (blk0 + j + 1) * per, 0))
    else:
        w_spec = pl.BlockSpec((1, k, tn), lambda j, i: (layer, 0, blk0 + j))
        wx_spec = pl.BlockSpec((1, k, LANES), lambda j, i: (layer, 0, (blk0 + j + 1) * per))
    in_specs = [pl.BlockSpec((t_, k), functools.partial(lambda j, i, f: (f(i), 0), f=f))
                for t_, f in zip(tms, idx)] + [w_spec]
    args = list(a_s) + [w]
    if shift:
        in_specs.append(wx_spec)
        args.append(w)
    outs = pl.pallas_call(
        functools.partial(_mm_ws_kernel, shift=shift, transposed=transposed, starts=tuple(starts)),
        grid=(nj, starts[-1]),
        in_specs=in_specs,
        out_specs=[pl.BlockSpec((t_, tn), functools.partial(lambda j, i, f: (f(i), j), f=f))
                   for t_, f in zip(tms, idx)],
        out_shape=[jax.ShapeDtypeStruct((x.shape[0], n), out_dtype) for x in a_s],
        scratch_shapes=[pltpu.VMEM((tn, k) if transposed else (k, tn), BF16)],
        compiler_params=_cp("parallel", "arbitrary"),
        name=name,
    )(*args)
    return outs[0] if single else tuple(outs)


def _small_proj_kernel(a_ref, w0_ref, w1_ref, o_ref):
    w = jnp.concatenate([w0_ref[0], w1_ref[0]], axis=0).astype(BF16)
    p = _dot_nt(a_ref[...], w)
    p0, p1 = p[:, :LANES], p[:, LANES:]
    lane = lax.broadcasted_iota(jnp.int32, p0.shape, 1)
    o_ref[...] = jnp.where(lane < SMALL_A0, p0, p1)


def _small_proj(a, w_t, layer, blk_lr, blk_ab, tm):
    m, k = a.shape
    return pl.pallas_call(
        _small_proj_kernel,
        grid=(m // tm,),
        in_specs=[pl.BlockSpec((tm, k), lambda i: (i, 0)),
                  pl.BlockSpec((1, LANES, k), lambda i: (layer, blk_lr, 0)),
                  pl.BlockSpec((1, LANES, k), lambda i: (layer, blk_ab, 0))],
        out_specs=pl.BlockSpec((tm, LANES), lambda i: (i, 0)),
        out_shape=jax.ShapeDtypeStruct((m, LANES), F32),
        compiler_params=_cp("parallel"),
        name="in_proj_small",
    )(a, w_t, w_t)


def _shift_rows(x, prev8, s):
    xs = pltpu.roll(x, s, 0)
    ps = pltpu.roll(prev8, s, 0)
    r8 = lax.broadcasted_iota(jnp.int32, prev8.shape, 0)
    top = jnp.where(r8 < s, ps, xs[:SUBLANES])
    if x.shape[0] == SUBLANES:
        return top
    return jnp.concatenate([top, xs[SUBLANES:]], axis=0)


def _causal_conv(x, prev8, taps):
    width = len(taps)
    y = None
    for i, w in enumerate(taps):
        s = width - 1 - i
        xs = x if s == 0 else _shift_rows(x, prev8, s)
        y = xs * w if y is None else y + xs * w
    return y


def _seq_tiling(t, tm):
    return (t // tm, None, 1) if t >= tm else (1, t, tm // t)


def _conv_proj_kernel(*refs, shift, starts, tilings, head_dim, norm_scale):
    ng = len(starts) - 1
    w_ref = refs[2 * ng]
    rest = refs[2 * ng + 1:]
    wx_ref = None
    if shift:
        wx_ref, rest = rest[0], rest[1:]
    cw_ref = rest[0]
    outs = rest[1:2 * ng + 1]
    wb_ref = rest[2 * ng + 1]
    carries = rest[2 * ng + 2:]
    i = pl.program_id(1)

    @pl.when(i == 0)
    def _():
        _cast_weight_tile_t(w_ref, wx_ref, wb_ref, shift)

    for gi in range(ng):
        def run(a_ref=refs[2 * gi], buf_ref=refs[2 * gi + 1], o_ref=outs[2 * gi], tail_ref=outs[2 * gi + 1],
                carry_ref=carries[gi], tiles_per_seq=tilings[gi][0], seq_rows=tilings[gi][1], lo_step=starts[gi]):
            tm, tn = o_ref.shape
            sub = min(tm, CONV_SUB)
            seg = sub if seq_rows is None else seq_rows
            if seq_rows is None:
                @pl.when((i - lo_step) % tiles_per_seq == 0)
                def _():
                    carry_ref[...] = buf_ref[0]

                prev = carry_ref[...]
            wb = wb_ref[...]
            taps = [cw_ref[n:n + 1, :] for n in range(cw_ref.shape[0])]
            for r in range(tm // sub):
                p_all = _dot_nt(a_ref[r * sub:(r + 1) * sub, :], wb)
                for s in range(sub // seg):
                    lo = r * sub + s * seg
                    x = p_all[s * seg:(s + 1) * seg]
                    if seq_rows is not None:
                        prev = buf_ref[lo // seg]
                    for h in range(tn // head_dim):
                        hs = slice(h * head_dim, (h + 1) * head_dim)
                        yh = _silu(_causal_conv(x[:, hs], prev[:, hs], [w[:, hs] for w in taps]))
                        if norm_scale is not None:
                            yh = yh * lax.rsqrt(jnp.sum(yh * yh, axis=-1, keepdims=True) + EPS)
                            if norm_scale != 1.0:
                                yh = yh * norm_scale
                        o_ref[lo:lo + seg, hs] = yh
                    prev = x[seg - SUBLANES:, :]
                    if seq_rows is not None:
                        tail_ref[lo // seg] = prev
            if seq_rows is None:
                carry_ref[...] = prev
                tail_ref[0] = prev
        _on_group_turn(i, starts, gi, run)


def _conv_proj(a_s, w_t, layer, off, n, conv_w, buf8_s, coff, t_s, tm_s, tn, head_dim, norm_scale, name):
    k = a_s[0].shape[1]
    nj = n // tn
    blk0 = off // tn
    shift = off - blk0 * tn
    assert shift < LANES and shift % SUBLANES == 0 and n % tn == 0 and coff % tn == 0
    cblk = coff // tn
    per = tn // LANES
    starts, idx = _group_steps([a.shape[0] for a in a_s], tm_s)
    tilings = [_seq_tiling(t, tm) for t, tm in zip(t_s, tm_s)]
    in_specs, args, out_specs, out_shape, scratch = [], [], [], [], []
    for a, buf8, tm, f, (tps, _, nb) in zip(a_s, buf8_s, tm_s, idx, tilings):
        ni = a.shape[0] // tm
        in_specs += [pl.BlockSpec((tm, k), functools.partial(lambda j, i, f: (f(i), 0), f=f)),
                     pl.BlockSpec((nb, SUBLANES, tn),
                                  functools.partial(lambda j, i, f, tps: (f(i) // tps, 0, cblk + j), f=f, tps=tps))]
        args += [a, buf8]
        out_specs += [pl.BlockSpec((tm, tn), functools.partial(lambda j, i, f: (f(i), j), f=f)),
                      pl.BlockSpec((nb, SUBLANES, tn), functools.partial(lambda j, i, f: (f(i), 0, j), f=f))]
        out_shape += [jax.ShapeDtypeStruct((a.shape[0], n), F32),
                      jax.ShapeDtypeStruct((ni * nb, SUBLANES, n), F32)]
        scratch.append(pltpu.VMEM((SUBLANES, tn), F32))
    in_specs.append(pl.BlockSpec((1, tn, k), lambda j, i: (layer, blk0 + j, 0)))
    args.append(w_t)
    if shift:
        in_specs.append(pl.BlockSpec((1, LANES, k), lambda j, i: (layer, (blk0 + j + 1) * per, 0)))
        args.append(w_t)
    in_specs.append(pl.BlockSpec((conv_w.shape[0], tn), lambda j, i: (0, cblk + j)))
    args.append(conv_w)
    outs = pl.pallas_call(
        functools.partial(_conv_proj_kernel, shift=shift, starts=tuple(starts), tilings=tuple(tilings),
                          head_dim=head_dim, norm_scale=norm_scale),
        grid=(nj, starts[-1]),
        in_specs=in_specs,
        out_specs=out_specs,
        out_shape=out_shape,
        scratch_shapes=[pltpu.VMEM((tn, k), BF16)] + scratch,
        compiler_params=_cp("parallel", "arbitrary"),
        name=name,
    )(*args)
    return [(outs[2 * gi], outs[2 * gi + 1][tilings[gi][0] - 1::tilings[gi][0]]) for gi in range(len(a_s))]


def _proj_res_kernel(*refs, sub, final, starts, chained):
    ng = len(starts) - 1
    nout = 1 if final else 2
    a_refs, x_refs = refs[0:2 * ng:2], refs[1:2 * ng:2]
    w_ref, g_ref = refs[2 * ng], refs[2 * ng + 1]
    nw = 3 if chained else 2
    w2_ref = refs[2 * ng + 2] if chained else None
    outs = refs[2 * ng + nw:]
    i = pl.program_id(0)
    for gi in range(ng):
        def run(a_ref=a_refs[gi], x_ref=x_refs[gi], o=outs[gi * nout:(gi + 1) * nout]):
            tm = a_ref.shape[0]
            sb = min(sub, tm)
            w = w_ref[...]
            g = g_ref[...]
            for r in range(tm // sb):
                rows = slice(r * sb, (r + 1) * sb)
                xn = x_ref[rows, :] + _dot(a_ref[rows, :], w)
                if final:
                    o[0][rows, :] = _rms(xn, g)
                else:
                    o[0][rows, :] = xn
                    hn = _rms(xn, g).astype(BF16)
                    o[1][rows, :] = _dot(hn, w2_ref[...]).astype(o[1].dtype) if chained else hn
        _on_group_turn(i, starts, gi, run)


def _proj_res(a_s, w, x_s, g, tms, sub, final, name, w2=None):
    kk, d = w.shape
    starts, idx = _group_steps([a.shape[0] for a in a_s], tms)
    in_specs, args, out_specs, out_shape = [], [], [], []
    for a, x, tm, f in zip(a_s, x_s, tms, idx):
        row = functools.partial(lambda i, f: (f(i), 0), f=f)
        in_specs += [pl.BlockSpec((tm, kk), row), pl.BlockSpec((tm, d), row)]
        args += [a, x]
        out_specs.append(pl.BlockSpec((tm, d), row))
        out_shape.append(jax.ShapeDtypeStruct((a.shape[0], d), F32))
        if not final:
            out_specs.append(pl.BlockSpec((tm, d), row))
            out_shape.append(jax.ShapeDtypeStruct((a.shape[0], d), BF16))
    chained = w2 is not None
    assert not (chained and final)
    w_specs = [pl.BlockSpec((kk, d), lambda i: (0, 0), pipeline_mode=pl.Buffered(1)),
               pl.BlockSpec((1, d), lambda i: (0, 0))]
    w_args = [w, g.reshape(1, d)]
    if chained:
        w_specs.append(pl.BlockSpec(w2.shape, lambda i: (0, 0), pipeline_mode=pl.Buffered(1)))
        w_args.append(w2)
    outs = pl.pallas_call(
        functools.partial(_proj_res_kernel, sub=sub, final=final, starts=tuple(starts), chained=chained),
        grid=(starts[-1],),
        in_specs=in_specs + w_specs,
        out_specs=out_specs,
        out_shape=out_shape,
        compiler_params=_cp("arbitrary"),
        name=name,
    )(*args, *w_args)
    nout = 1 if final else 2
    return [tuple(outs[gi * nout:(gi + 1) * nout]) for gi in range(len(a_s))]


def _merge_kernel(*refs, sub, starts):
    ng = len(starts) - 1
    per = N_BRANCH + 1
    w_ref, b_ref = refs[per * ng], refs[per * ng + 1]
    o_refs = refs[per * ng + 2:]
    i = pl.program_id(0)
    for gi in range(ng):
        def run(y_refs=refs[gi * per:gi * per + N_BRANCH], g_ref=refs[gi * per + N_BRANCH], o_ref=o_refs[gi]):
            tm, d = o_ref.shape
            sb = min(sub, tm)
            for r in range(tm // sb):
                rows = slice(r * sb, (r + 1) * sb)
                acc = None
                for n, y_ref in enumerate(y_refs):
                    p = _dot(y_ref[rows, :], w_ref[n])
                    g = g_ref[rows, n * d:(n + 1) * d].astype(F32)
                    t = _sigmoid(g + b_ref[n:n + 1, :]) * p
                    acc = t if acc is None else acc + t
                o_ref[rows, :] = acc.astype(o_ref.dtype)
        _on_group_turn(i, starts, gi, run)


def _merge(ys_s, w_branch, g_br_s, b_gate, tms):
    _, bw, d = w_branch.shape
    starts, idx = _group_steps([g.shape[0] for g in g_br_s], tms)
    in_specs, args, out_specs, out_shape = [], [], [], []
    for ys, g_br, tm, f in zip(ys_s, g_br_s, tms, idx):
        row = functools.partial(lambda i, f: (f(i), 0), f=f)
        in_specs += [pl.BlockSpec((tm, bw), row)] * N_BRANCH + [pl.BlockSpec((tm, N_BRANCH * d), row)]
        args += list(ys) + [g_br]
        out_specs.append(pl.BlockSpec((tm, d), row))
        out_shape.append(jax.ShapeDtypeStruct((g_br.shape[0], d), BF16))
    return pl.pallas_call(
        functools.partial(_merge_kernel, sub=ROW_SUB, starts=tuple(starts)),
        grid=(starts[-1],),
        in_specs=in_specs + [pl.BlockSpec((N_BRANCH, bw, d), lambda i: (0, 0, 0), pipeline_mode=pl.Buffered(1)),
                             pl.BlockSpec((N_BRANCH, d), lambda i: (0, 0))],
        out_specs=out_specs,
        out_shape=out_shape,
        compiler_params=_cp("arbitrary"),
        name="branch_merge",
    )(*args, w_branch, b_gate)


def _xattn_kernel(q_ref, k_ref, v_ref, o_ref, *scratch, scale, dh):
    heads = range(q_ref.shape[1] // dh)
    hs = lambda h: slice(h * dh, (h + 1) * dh)
    if scratch:
        kb_ref, vb_ref = scratch

        @pl.when(pl.program_id(1) == 0)
        def _():
            for h in heads:
                kb_ref[:, hs(h)] = k_ref[0, 0, :, h, :].astype(BF16)
                vb_ref[:, hs(h)] = v_ref[0, 0, :, h, :].astype(BF16)
    else:
        kb_ref, vb_ref = k_ref.at[0], v_ref.at[0]

    s = [_dot_nt(q_ref[:, hs(h)], kb_ref[:, hs(h)]) * scale for h in heads]
    e = [jnp.exp(x - jnp.max(x, axis=-1, keepdims=True)) for x in s]
    p = [(x / jnp.sum(x, axis=-1, keepdims=True)).astype(BF16) for x in e]
    for h in heads:
        o_ref[:, hs(h)] = _dot(p[h], vb_ref[:, hs(h)]).astype(o_ref.dtype)


def _xattn(q, mem_k, mem_v, layer, t, tq):
    m, d = q.shape
    b = m // t
    dh = d // X_H
    nq = t // tq
    if mem_k.ndim == 5:
        nmem = mem_k.shape[2]
        kv_spec = pl.BlockSpec((1, 1, nmem, X_H, dh), lambda bi, i: (layer, bi, 0, 0, 0))
        scratch = [pltpu.VMEM((nmem, d), BF16), pltpu.VMEM((nmem, d), BF16)]
    else:
        nmem = mem_k.shape[1] // b
        kv_spec = pl.BlockSpec((1, nmem, d), lambda bi, i: (layer, bi, 0))
        scratch = []
    qo_spec = pl.BlockSpec((tq, d), lambda bi, i: (bi * nq + i, 0))
    return pl.pallas_call(
        functools.partial(_xattn_kernel, scale=dh ** -0.5, dh=dh),
        grid=(b, nq),
        in_specs=[qo_spec, kv_spec, kv_spec],
        out_specs=qo_spec,
        out_shape=jax.ShapeDtypeStruct((m, d), BF16),
        scratch_shapes=scratch,
        compiler_params=_cp("parallel", "arbitrary"),
        name="xattn",
    )(q, mem_k, mem_v)


def _mem_proj_kernel(x_ref, g_ref, w_ref, o_ref, ob_ref, wb_ref):
    @pl.when(pl.program_id(1) == 0)
    def _():
        _cast_weight_tile(w_ref, None, wb_ref, 0)

    _, nb, nmem, nh, dh = o_ref.shape
    res = _dot(_rms(x_ref[...], g_ref[0]).astype(BF16), wb_ref[...])
    ob_ref[0] = res.astype(BF16)
    for bb in range(nb):
        for h in range(nh):
            o_ref[0, bb, :, h, :] = res[bb * nmem:(bb + 1) * nmem, h * dh:(h + 1) * dh]


def _mem_proj(mem2, ln_mem, w_xkv, half, bsz, nmem, name):
    depth, d, _ = w_xkv.shape
    dh = d // X_H
    nb = 1
    return pl.pallas_call(
        _mem_proj_kernel,
        grid=(depth, bsz // nb),
        in_specs=[pl.BlockSpec((nb * nmem, d), lambda l, i: (i, 0)),
                  pl.BlockSpec((1, 1, d), lambda l, i: (l, 0, 0)),
                  pl.BlockSpec((1, d, d), lambda l, i: (l, 0, half), pipeline_mode=pl.Buffered(1))],
        out_specs=[pl.BlockSpec((1, nb, nmem, X_H, dh), lambda l, i: (l, i, 0, 0, 0)),
                   pl.BlockSpec((1, nb * nmem, d), lambda l, i: (l, i, 0))],
        out_shape=[jax.ShapeDtypeStruct((depth, bsz, nmem, X_H, dh), F32),
                   jax.ShapeDtypeStruct((depth, bsz * nmem, d), BF16)],
        scratch_shapes=[pltpu.VMEM((d, d), BF16)],
        compiler_params=_cp("arbitrary", "arbitrary"),
        name=name,
    )(mem2, ln_mem[:, None, :], w_xkv)


def _gelu_tanh(x):
    c = 0.7978845608028654
    return x * (0.5 * (1.0 + jnp.tanh(c * (x + 0.044715 * (x * x * x)))))


def _ffn_in_kernel(*refs, starts, tilings):
    ng = len(starts) - 1
    wg_ref, wu_ref, cw_ref, cb_ref = refs[2 * ng:2 * ng + 4]
    outs = refs[2 * ng + 4:4 * ng + 4]
    wgb_ref, wub_ref = refs[4 * ng + 4:4 * ng + 6]
    carries = refs[4 * ng + 6:]
    i = pl.program_id(1)

    @pl.when(i == 0)
    def _():
        _cast_weight_tile(wg_ref, None, wgb_ref, 0)
        _cast_weight_tile(wu_ref, None, wub_ref, 0)

    for gi in range(ng):
        def run(h_ref=refs[2 * gi], buf_ref=refs[2 * gi + 1], act_ref=outs[2 * gi], nbuf_ref=outs[2 * gi + 1],
                carry_ref=carries[gi], tiles_per_seq=tilings[gi][0], seq_rows=tilings[gi][1], lo_step=starts[gi]):
            tm = h_ref.shape[0]
            sub = min(tm, FFN_SUB)
            seg = sub if seq_rows is None else seq_rows
            if seq_rows is None:
                @pl.when((i - lo_step) % tiles_per_seq == 0)
                def _():
                    carry_ref[...] = buf_ref[0]

                prev = carry_ref[...]
            wg, wu = wgb_ref[...], wub_ref[...]
            taps, cb = [cw_ref[n:n + 1, :] for n in range(cw_ref.shape[0])], cb_ref[...]
            for r in range(tm // sub):
                h = h_ref[r * sub:(r + 1) * sub, :]
                gate_all = _dot(h, wg)
                up_all = _dot(h, wu)
                for s in range(sub // seg):
                    lo = r * sub + s * seg
                    gate = gate_all[s * seg:(s + 1) * seg]
                    if seq_rows is not None:
                        prev = buf_ref[lo // seg]
                    y = _causal_conv(gate, prev, taps)
                    act = _gelu_tanh(y + cb) * up_all[s * seg:(s + 1) * seg]
                    act_ref[lo:lo + seg, :] = act.astype(act_ref.dtype)
                    prev = gate[seg - SUBLANES:, :]
                    if seq_rows is not None:
                        nbuf_ref[lo // seg] = prev
            if seq_rows is None:
                carry_ref[...] = prev
                nbuf_ref[0] = prev
        _on_group_turn(i, starts, gi, run)


def _ffn_in(h_s, w_ffn_in, layer, conv_w, conv_b, buf8_s, t_s, tm_s, tn):
    d = h_s[0].shape[1]
    dff = w_ffn_in.shape[2] // 2
    nj = dff // tn
    starts, idx = _group_steps([h.shape[0] for h in h_s], tm_s)
    tilings = [_seq_tiling(t, tm) for t, tm in zip(t_s, tm_s)]
    in_specs, args, out_specs, out_shape, scratch = [], [], [], [], []
    for h, buf8, tm, f, (tps, _, nb) in zip(h_s, buf8_s, tm_s, idx, tilings):
        ni = h.shape[0] // tm
        in_specs += [pl.BlockSpec((tm, d), functools.partial(lambda j, i, f: (f(i), 0), f=f)),
                     pl.BlockSpec((nb, SUBLANES, tn),
                                  functools.partial(lambda j, i, f, tps: (f(i) // tps, 0, j), f=f, tps=tps))]
        args += [h, buf8]
        out_specs += [pl.BlockSpec((tm, tn), functools.partial(lambda j, i, f: (f(i), j), f=f)),
                      pl.BlockSpec((nb, SUBLANES, tn), functools.partial(lambda j, i, f: (f(i), 0, j), f=f))]
        out_shape += [jax.ShapeDtypeStruct((h.shape[0], dff), BF16),
                      jax.ShapeDtypeStruct((ni * nb, SUBLANES, dff), F32)]
        scratch.append(pltpu.VMEM((SUBLANES, tn), F32))
    outs = pl.pallas_call(
        functools.partial(_ffn_in_kernel, starts=tuple(starts), tilings=tuple(tilings)),
        grid=(nj, starts[-1]),
        in_specs=in_specs + [pl.BlockSpec((1, d, tn), lambda j, i: (layer, 0, j)),
                             pl.BlockSpec((1, d, tn), lambda j, i: (layer, 0, nj + j)),
                             pl.BlockSpec((FFN_CONV, tn), lambda j, i: (0, j)),
                             pl.BlockSpec((1, tn), lambda j, i: (0, j))],
        out_specs=out_specs,
        out_shape=out_shape,
        scratch_shapes=[pltpu.VMEM((d, tn), BF16), pltpu.VMEM((d, tn), BF16)] + scratch,
        compiler_params=_cp("parallel", "arbitrary"),
        name="ffn_in",
    )(*args, w_ffn_in, w_ffn_in, conv_w, conv_b.reshape(1, dff))
    return [(outs[2 * gi], outs[2 * gi + 1][tilings[gi][0] - 1::tilings[gi][0]]) for gi in range(len(h_s))]


def _ret_kernel(ld_ref, p_ref, cos_ref, sin_ref, s0_ref, y_ref, s_ref, *, c):
    t = pl.program_id(1)
    tb = p_ref.shape[0]
    half = RET_DK // 2
    hw = RET_H * RET_DK
    heads = range(RET_H)

    @pl.when(t == 0)
    def _():
        s_ref[...] = s0_ref[...]

    ri = lax.broadcasted_iota(jnp.int32, (c, c), 0)
    ci = lax.broadcasted_iota(jnp.int32, (c, c), 1)
    diff = (ri - ci).astype(F32)
    causal = ri >= ci
    idx = lax.broadcasted_iota(jnp.int32, (c, RET_DK), 0).astype(F32)
    lds = [ld_ref[h] for h in heads]
    dmask = [jnp.where(causal, jnp.exp(ld * jnp.where(causal, diff, 0.0)), 0.0) for ld in lds]
    q_dec = [jnp.exp(ld * (idx + 1.0)) for ld in lds]
    k_dec = [jnp.exp(ld * (c - 1.0 - idx)) for ld in lds]
    s_dec = [jnp.exp(jnp.full((1, RET_DV), ld * c, F32)) for ld in lds]

    def rot(x, cos, sin):
        x1, x2 = x[:, :half], x[:, half:]
        return jnp.concatenate([x1 * cos - x2 * sin, x1 * sin + x2 * cos], axis=-1)

    nchunk = tb // c
    nc = RET_CHUNKS_PER_STEP if nchunk % RET_CHUNKS_PER_STEP == 0 else 1

    def body(jj, carry):
        ch = []
        for ci in range(nc):
            rows = pl.ds(pl.multiple_of((jj * nc + ci) * c, c), c)
            cos, sin = cos_ref[rows, :], sin_ref[rows, :]
            k = [rot(p_ref[rows, hw + h * RET_DK:hw + (h + 1) * RET_DK], cos, sin) * (RET_DK ** -0.5)
                 for h in heads]
            ch.append(dict(
                rows=rows, k=k,
                qb=[rot(p_ref[rows, h * RET_DK:(h + 1) * RET_DK], cos, sin).astype(BF16) for h in heads],
                vb=[p_ref[rows, 2 * hw + h * RET_DV:2 * hw + (h + 1) * RET_DV].astype(BF16) for h in heads]))
        probs = [(d, h) for d in ch for h in heads]
        att = [(_dot_nt(d['qb'][h], d['k'][h].astype(BF16)) * dmask[h]).astype(BF16) for d, h in probs]
        av = [_dot(a, d['vb'][h]) for a, (d, h) in zip(att, probs)]
        kv = [_dot_tn((d['k'][h] * k_dec[h]).astype(BF16), d['vb'][h]) for d, h in probs]

        for ci, d in enumerate(ch):
            s = [s_ref[0, h] for h in heads]
            o = [av[ci * RET_H + h] + _dot(d['qb'][h], s[h].astype(BF16)) * q_dec[h] for h in heads]
            for h in heads:
                s_ref[0, h] = s_dec[h] * s[h] + kv[ci * RET_H + h]
            for h in heads:
                g = p_ref[d['rows'], 3 * hw + h * RET_DV:3 * hw + (h + 1) * RET_DV]
                y_ref[d['rows'], h * RET_DV:(h + 1) * RET_DV] = (_rms(o[h]) * _silu(g)).astype(y_ref.dtype)
        return carry

    lax.fori_loop(0, nchunk // nc, body, 0)


def _retention(p_ret, cos, sin, ld, s0, t, tb):
    m, width = p_ret.shape
    b = m // t
    c = min(CHUNK, t)
    nt = t // tb
    st_spec = pl.BlockSpec((1, RET_H, RET_DK, RET_DV), lambda bi, ti: (bi, 0, 0, 0))
    rope_spec = pl.BlockSpec((tb, RET_DK // 2), lambda bi, ti: (ti, 0))
    return pl.pallas_call(
        functools.partial(_ret_kernel, c=c),
        grid=(b, nt),
        in_specs=[pl.BlockSpec(memory_space=pltpu.SMEM),
                  pl.BlockSpec((tb, width), lambda bi, ti: (bi * nt + ti, 0)),
                  rope_spec, rope_spec, st_spec],
        out_specs=[pl.BlockSpec((tb, RET_H * RET_DV), lambda bi, ti: (bi * nt + ti, 0)), st_spec],
        out_shape=[jax.ShapeDtypeStruct((m, RET_H * RET_DV), BF16),
                   jax.ShapeDtypeStruct((b, RET_H, RET_DK, RET_DV), F32)],
        compiler_params=_cp("parallel", "arbitrary"),
        name="retention",
    )(ld, p_ret, cos, sin, s0)


def _gla_kernel(p_ref, lr_ref, wgk_ref, bgk_ref, nrm_ref, s0_ref, y_ref, s_ref, st_ref, *, c, nt):
    t = pl.program_id(1)
    tb = p_ref.shape[0]
    hk = GLA_H * GLA_DK
    hv = GLA_H * GLA_DV
    heads = range(GLA_H)

    @pl.when(t == 0)
    def _():
        for h in heads:
            st_ref[h] = s0_ref[0, h].T

    ri = lax.broadcasted_iota(jnp.int32, (c, c), 0)
    ci = lax.broadcasted_iota(jnp.int32, (c, c), 1)
    causal = ri >= ci
    tri = jnp.where(causal, 1.0, 0.0).astype(BF16)
    wgk = wgk_ref[...]
    bgk = bgk_ref[...]
    nrm = nrm_ref[...]

    nchunk = tb // c
    nc = GLA_CHUNKS_PER_STEP if nchunk % GLA_CHUNKS_PER_STEP == 0 else 1
    hs = lambda x, h: x[:, h * GLA_DK:(h + 1) * GLA_DK]

    def body(jj, carry):
        rows = [pl.ds(pl.multiple_of((jj * nc + ci) * c, c), c) for ci in range(nc)]
        z = [_dot(lr_ref[r, :].astype(BF16), wgk) + bgk for r in rows]
        bc = [_tri_cumsum(tri, -_softplus(-zz) / GLA_NORMALIZER) for zz in z]
        ch = []
        for r, b_ in zip(rows, bc):
            q = p_ref[r, 0:hk] * (GLA_DK ** -0.5)
            k = p_ref[r, hk:2 * hk]
            b_last = b_[c - 1:c, :]
            ch.append(dict(
                rows=r, qe=(q * jnp.exp(b_)).astype(BF16), ke=(k * jnp.exp(-b_)).astype(BF16),
                kd=(k * jnp.exp(b_last - b_)).astype(BF16), eb_last=jnp.exp(b_last),
                vb=[p_ref[r, 2 * hk + h * GLA_DV:2 * hk + (h + 1) * GLA_DV].astype(BF16) for h in heads]))
        probs = [(d, h) for d in ch for h in heads]
        att = [jnp.where(causal, _dot_nt(hs(d['qe'], h), hs(d['ke'], h)), 0.0).astype(BF16) for d, h in probs]
        av = [_dot(a, d['vb'][h]) for a, (d, h) in zip(att, probs)]

        for ci, d in enumerate(ch):
            st = [st_ref[h] for h in heads]
            o = [av[ci * GLA_H + h] + _dot_nt(hs(d['qe'], h), st[h].astype(BF16)) for h in heads]
            for h in heads:
                st_ref[h] = hs(d['eb_last'], h) * st[h] + _dot_tn(d['vb'][h], hs(d['kd'], h))
            for h in heads:
                g = p_ref[d['rows'], 2 * hk + hv + h * GLA_DV:2 * hk + hv + (h + 1) * GLA_DV]
                y_ref[d['rows'], h * GLA_DV:(h + 1) * GLA_DV] = (_rms(o[h], nrm) * _silu(g)).astype(y_ref.dtype)
        return carry

    lax.fori_loop(0, nchunk // nc, body, 0)

    @pl.when(t == nt - 1)
    def _():
        for h in heads:
            s_ref[0, h] = st_ref[h].T


def _gla(p_gla, p_small, wgk, bgk, nrm, s0, t, tb):
    m, width = p_gla.shape
    b = m // t
    c = min(CHUNK, t)
    nt = t // tb
    const = lambda bi, ti: (0, 0)
    st_spec = pl.BlockSpec((1, GLA_H, GLA_DK, GLA_DV), lambda bi, ti: (bi, 0, 0, 0))
    return pl.pallas_call(
        functools.partial(_gla_kernel, c=c, nt=nt),
        grid=(b, nt),
        in_specs=[pl.BlockSpec((tb, width), lambda bi, ti: (bi * nt + ti, 0)),
                  pl.BlockSpec((tb, LANES), lambda bi, ti: (bi * nt + ti, 0)),
                  pl.BlockSpec((LANES, GLA_H * GLA_DK), const),
                  pl.BlockSpec((1, GLA_H * GLA_DK), const),
                  pl.BlockSpec((1, GLA_DV), const),
                  st_spec],
        out_specs=[pl.BlockSpec((tb, GLA_H * GLA_DV), lambda bi, ti: (bi * nt + ti, 0)), st_spec],
        out_shape=[jax.ShapeDtypeStruct((m, GLA_H * GLA_DV), BF16),
                   jax.ShapeDtypeStruct((b, GLA_H, GLA_DK, GLA_DV), F32)],
        scratch_shapes=[pltpu.VMEM((GLA_H, GLA_DV, GLA_DK), F32)],
        compiler_params=_cp("parallel", "arbitrary"),
        name="gla",
    )(p_gla, p_small, wgk, bgk, nrm, s0)


def _blockdiag2(xp, left):
    return jnp.concatenate([jnp.where(left, xp, 0.0), jnp.where(left, 0.0, xp)], axis=0)


def _unit_lower_inverse_minus_eye_packed(mats, c):
    ri = lax.broadcasted_iota(jnp.int32, (c, 2 * c), 0)
    cl = lax.broadcasted_iota(jnp.int32, (c, 2 * c), 1)
    left = cl < c
    cm = cl & (c - 1)

    def level_mask(k):
        return ((ri ^ cm) < 2 * k) & ((ri & k) != 0) & ((cm & k) == 0)

    m1 = level_mask(1)
    ns = [-jnp.where(m1, a, 0.0) for a in mats]
    k = 2
    while k < c:
        mk = level_mask(k)
        ls = [jnp.where(mk, a, 0.0) for a in mats]
        ys = [l + _dot(n.astype(BF16), _blockdiag2(l, left).astype(BF16)) for l, n in zip(ls, ns)]
        xs = [y + _dot(y.astype(BF16), _blockdiag2(n, left).astype(BF16)) for y, n in zip(ys, ns)]
        ns = [n - x for n, x in zip(ns, xs)]
        k *= 2
    return ns


def _gdn_kernel(q_ref, k_ref, v_ref, z_ref, sm_ref, alog_ref, dtb_ref, nrm_ref, s0_ref,
                y_ref, s_ref, *, c):
    t = pl.program_id(1)
    tb = q_ref.shape[0]

    @pl.when(t == 0)
    def _():
        s_ref[...] = s0_ref[...]

    tri = jnp.where(lax.broadcasted_iota(jnp.int32, (c, c), 0) >= lax.broadcasted_iota(jnp.int32, (c, c), 1),
                    1.0, 0.0).astype(BF16)
    ri = lax.broadcasted_iota(jnp.int32, (c, 2 * c), 0)
    cl = lax.broadcasted_iota(jnp.int32, (c, 2 * c), 1)
    left = cl < c
    cm = cl & (c - 1)
    incl = ri >= cm
    strict = ri > cm
    lane = lax.broadcasted_iota(jnp.int32, (1, LANES), 1)
    a_lanes = jnp.logical_and(lane >= SMALL_A0, lane < SMALL_A0 + GDN_H)
    sel2 = jnp.where(lax.broadcasted_iota(jnp.int32, (SUBLANES, LANES), 1)
                     == 2 * lax.broadcasted_iota(jnp.int32, (SUBLANES, LANES), 0) + SMALL_A0,
                     1.0, 0.0).astype(BF16)
    neg_a = -jnp.exp(alog_ref[...])
    dtb = dtb_ref[...]
    nrm = nrm_ref[...]

    def col(x, lane_idx):
        return x[:, lane_idx:lane_idx + 1]

    assert GDN_DK == LANES and GDN_DV == LANES and GDN_H % 2 == 0
    heads = range(GDN_H)
    npair = GDN_H // 2
    pairs = range(npair)
    nchunk = tb // c
    nc = GDN_CHUNKS_PER_STEP if nchunk % GDN_CHUNKS_PER_STEP == 0 else 1

    def lanes2(x, y):
        return jnp.concatenate([x, y], axis=1)

    def body(jj, carry):
        ch = []
        for ci in range(nc):
            rows = pl.ds(pl.multiple_of((jj * nc + ci) * c, c), c)
            sm = sm_ref[rows, :]
            ch.append(dict(
                rows=rows,
                g_all=jnp.where(a_lanes, neg_a * _softplus(sm + dtb), 0.0),
                beta_all=_sigmoid(sm),
                q=[q_ref[rows, h * GDN_DK:(h + 1) * GDN_DK] for h in heads],
                k=[k_ref[rows, h * GDN_DK:(h + 1) * GDN_DK] for h in heads],
                v=[v_ref[rows, h * GDN_DV:(h + 1) * GDN_DV] for h in heads]))
        for d in ch:
            d['gam'] = _tri_cumsum(tri, d['g_all'])
        for d in ch:
            g2 = jnp.concatenate([d['gam'], pltpu.roll(d['gam'], LANES - 1, 1)], axis=0)
            ghi, gmid, glo = _split3(g2)
            d['gam_rows'] = _dot_nt(sel2, ghi) + _dot_nt(sel2, gmid) + _dot_nt(sel2, glo)

        def packed_cols(x, base, p):
            return jnp.where(left, col(x, base + 2 * p), col(x, base + 2 * p + 1))

        for d in ch:
            gam = d['gam']
            g_last = gam[c - 1:c, :]
            d['eg_all'] = jnp.exp(gam)
            d['egl_all'] = jnp.exp(g_last - gam)
            d['eg_last_all'] = jnp.exp(g_last)
            d['beta'] = [col(d['beta_all'], SMALL_B0 + h) for h in heads]
            d['eg'] = [col(d['eg_all'], SMALL_A0 + h) for h in heads]
            d['beta_p'] = [packed_cols(d['beta_all'], SMALL_B0, p) for p in pairs]
            d['decay_p'] = [jnp.where(incl, jnp.exp(jnp.where(
                incl, packed_cols(gam, SMALL_A0, p) - d['gam_rows'][p:p + 1, :], 0.0)), 0.0) for p in pairs]
        probs = [(d, p) for d in ch for p in pairs]
        zk = jnp.zeros((c, GDN_DK), F32)
        qk_kk = [_dot_nt(
            jnp.concatenate([lanes2(d['q'][2 * p], d['q'][2 * p + 1]),
                             lanes2(d['k'][2 * p], d['k'][2 * p + 1])], axis=0).astype(BF16),
            jnp.concatenate([lanes2(d['k'][2 * p], zk), lanes2(zk, d['k'][2 * p + 1])], axis=0).astype(BF16))
            for d, p in probs]
        att = [(r[:c] * d['decay_p'][p]).astype(BF16) for r, (d, p) in zip(qk_kk, probs)]
        a = [jnp.where(strict, d['beta_p'][p] * r[c:] * d['decay_p'][p], 0.0) for r, (d, p) in zip(qk_kk, probs)]
        n = _unit_lower_inverse_minus_eye_packed(a, c)
        zr = jnp.zeros((c, GDN_DV + GDN_DK), F32)
        sol = []
        for (d, p), nn in zip(probs, n):
            r1, r2 = (jnp.concatenate([d['beta'][h] * d['v'][h], (d['beta'][h] * d['eg'][h]) * d['k'][h]], axis=-1)
                      for h in (2 * p, 2 * p + 1))
            bd = jnp.concatenate([lanes2(r1, zr), lanes2(zr, r2)], axis=0).astype(BF16)
            sol.append(lanes2(r1, r2) + _dot(nn.astype(BF16), bd))

        zs = jnp.zeros((GDN_DK, GDN_DV), F32)
        zw = jnp.zeros((c, GDN_DV), F32)
        for ci, d in enumerate(ch):
            sl = sol[ci * npair:(ci + 1) * npair]
            at = att[ci * npair:(ci + 1) * npair]
            s = [s_ref[0, h] for h in heads]
            wq_s = [_dot(
                jnp.concatenate([lanes2(sl[p][:, LANES:2 * LANES], sl[p][:, 3 * LANES:]),
                                 lanes2(d['q'][2 * p], d['q'][2 * p + 1])], axis=0).astype(BF16),
                jnp.concatenate([lanes2(s[2 * p], zs), lanes2(zs, s[2 * p + 1])], axis=0).astype(BF16))
                for p in pairs]
            bdw = [jnp.concatenate([lanes2(sl[p][:, :LANES] - wq_s[p][:c, :LANES], zw),
                                    lanes2(zw, sl[p][:, 2 * LANES:3 * LANES] - wq_s[p][:c, LANES:])],
                                   axis=0).astype(BF16) for p in pairs]
            aw = [_dot(at[p], bdw[p]) for p in pairs]
            kw = [_dot_tn(jnp.concatenate([d['k'][h] * col(d['egl_all'], SMALL_A0 + h)
                                           for h in (2 * p, 2 * p + 1)], axis=0).astype(BF16), bdw[p])
                  for p in pairs]
            for h in heads:
                p, half = h // 2, slice((h % 2) * LANES, (h % 2 + 1) * LANES)
                s_ref[0, h] = col(d['eg_last_all'], SMALL_A0 + h) * s[h] + kw[p][:, half]
            for h in heads:
                p, half = h // 2, slice((h % 2) * LANES, (h % 2 + 1) * LANES)
                o = d['eg'][h] * wq_s[p][c:, half] + aw[p][:, half]
                z = z_ref[d['rows'], h * GDN_DV:(h + 1) * GDN_DV]
                y = _rms(o, nrm) * _silu(z)
                y_ref[d['rows'], h * GDN_DV:(h + 1) * GDN_DV] = y.astype(y_ref.dtype)
        return carry

    lax.fori_loop(0, nchunk // nc, body, 0)


def _gdn(q, k, v, z, p_small, alog_row, dtb_row, nrm, s0, t, tb):
    m = q.shape[0]
    b = m // t
    c = min(CHUNK, t)
    nt = t // tb
    st_spec = pl.BlockSpec((1, GDN_H, GDN_DK, GDN_DV), lambda bi, ti: (bi, 0, 0, 0))
    const = lambda bi, ti: (0, 0)
    row = lambda bi, ti: (bi * nt + ti, 0)
    qk_spec = pl.BlockSpec((tb, GDN_H * GDN_DK), row)
    v_spec = pl.BlockSpec((tb, GDN_H * GDN_DV), row)
    return pl.pallas_call(
        functools.partial(_gdn_kernel, c=c),
        grid=(b, nt),
        in_specs=[qk_spec, qk_spec, v_spec, v_spec,
                  pl.BlockSpec((tb, LANES), row),
                  pl.BlockSpec((1, LANES), const),
                  pl.BlockSpec((1, LANES), const),
                  pl.BlockSpec((1, GDN_DV), const),
                  st_spec],
        out_specs=[v_spec, st_spec],
        out_shape=[jax.ShapeDtypeStruct((m, GDN_H * GDN_DV), BF16),
                   jax.ShapeDtypeStruct((b, GDN_H, GDN_DK, GDN_DV), F32)],
        compiler_params=_cp("parallel", "arbitrary"),
        name="gated_deltanet",
    )(q, k, v, z, p_small, alog_row, dtb_row, nrm, s0)


def _pad_rows_front(buf, rows):
    return jnp.pad(buf, ((0, 0), (0, 0), (rows - buf.shape[2], 0), (0, 0)))


def _in_proj_windows(d):
    sizes = dict(ret=4 * RET_H * RET_DK, gla=2 * GLA_H * GLA_DK + 2 * GLA_H * GLA_DV, lr=GLA_RANK,
                 gdn=2 * GDN_H * GDN_DK + 2 * GDN_H * GDN_DV, ab=2 * GDN_H, gbr=N_BRANCH * d)
    win, o = {}, 0
    for name in ('ret', 'gla', 'lr', 'gdn', 'ab', 'gbr'):
        win[name] = (o, sizes[name])
        o += sizes[name]
    assert win['lr'][0] % LANES == 0 and win['ab'][0] % LANES == SMALL_A0
    return win


def _prep_weights(w):
    lane_row = lambda v, at: jnp.pad(v, ((0, 0), (at, LANES - at - v.shape[1])))[:, None, :]
    return dict(
        w_in_t=jnp.swapaxes(w['w_in'], 1, 2), w_xq=w['w_xq'].astype(BF16), w_ffn_in=w['w_ffn_in'],
        w_gk=jnp.pad(w['w_gla_gk'], ((0, 0), (0, LANES - GLA_RANK), (0, 0))).astype(BF16),
        b_gk=w['b_gla_gk'][:, None, :],
        gla_norm=w['gla_norm'][:, None, :], gdn_norm=w['gdn_norm'][:, None, :],
        alog_row=lane_row(w['gdn_a_log'], SMALL_A0), dtb_row=lane_row(w['gdn_dt_bias'], SMALL_A0),
        gdn_conv_w=w['gdn_conv_w'], b_gate=w['b_gate'],
        w_branch=w['w_branch'].astype(BF16), w_out=w['w_out'].astype(BF16), w_xo=w['w_xo'].astype(BF16),
        ffn_conv_w=w['ffn_conv_w'], ffn_conv_b=w['ffn_conv_b'], w_ffn_out=w['w_ffn_out'].astype(BF16),
        ln_mix=w['ln_mix'], ln_xattn=w['ln_xattn'], ln_ffn=w['ln_ffn'], ln_final=w['ln_final'],
    )


def _tile(n, cap):
    t = min(n, cap)
    while n % t:
        t -= 1
    return t


class _Group:
    def __init__(self, x, offset, mem_k, mem_v, s_ret, s_gla, s_gdn, buf_gdn, buf_ffn):
        self.b, self.t, d = x.shape
        b, t = self.b, self.t
        self.m = m = b * t
        self.x2 = x.reshape(m, d)
        self.mem_k, self.mem_v = mem_k, mem_v
        self.s_ret, self.s_gla, self.s_gdn = s_ret, s_gla, s_gdn
        self.buf_gdn8 = _pad_rows_front(buf_gdn, SUBLANES)
        self.buf_ffn8 = _pad_rows_front(buf_ffn, SUBLANES)
        self.tm = _tile(m, MM_TM)
        self.tm_res = _tile(m, RES_TM)
        self.tm_res_k = _tile(m, RES_TM_K)
        self.tm_ffo = _tile(m, FFO_TM)
        self.tb = _tile(t, SEQ_TB)
        self.tb_gdn = _tile(t, SEQ_TB_GDN)
        self.tq = _tile(t, XATTN_TQ)
        self.tm_ffn = _tile(t, FFN_TM) if t >= FFN_TM else t * _tile(b, max(1, FFN_TM // t))
        pos = offset + jnp.arange(t, dtype=F32)
        half = RET_DK // 2
        freqs = 1.0 / (ROPE_BASE ** (jnp.arange(half, dtype=F32) / half))
        ang = pos[:, None] * freqs[None, :]
        self.cos, self.sin = jnp.cos(ang), jnp.sin(ang)
        self.outs = ([], [], [], [], [])
        self.hn = self.y = None


def _trunks(groups, pw):
    d = groups[0].x2.shape[1]
    depth = groups[0].s_ret.shape[0]
    ld = jnp.log1p(-jnp.exp2(-5.0 - jnp.arange(RET_H, dtype=F32)))
    win = _in_proj_windows(d)
    tms = tuple(g.tm for g in groups)
    order = sorted(range(len(groups)), key=lambda gi: groups[gi].m)

    def shared(acts, w, l, off, n, out_dtype, name, transposed):
        blk0 = off // MM_TN
        shift = off - blk0 * MM_TN
        assert shift < LANES and n % MM_TN == 0
        outs = _mm_ws(tuple(acts[gi] for gi in order), w, l, blk0, n, shift, out_dtype,
                      tuple(tms[gi] for gi in order), MM_TN, name, transposed=transposed)
        return [outs[order.index(gi)] for gi in range(len(acts))]

    for g in groups:
        g.hn = _norm(g.x2, pw['ln_mix'][0], g.tm_res_k)
    for l in range(depth):
        hns = [g.hn for g in groups]
        hq, hv = GDN_H * GDN_DK, GDN_H * GDN_DV
        g_off = win['gdn'][0]
        p_ret = shared(hns, pw['w_in_t'], l, *win['ret'], F32, "in_proj_ret", True)
        p_gla = shared(hns, pw['w_in_t'], l, *win['gla'], F32, "in_proj_gla", True)
        d_z = shared(hns, pw['w_in_t'], l, g_off + 2 * hq + hv, hv, F32, "in_proj_gdn_z", True)
        g_br = shared(hns, pw['w_in_t'], l, *win['gbr'], BF16, "in_proj_gbr", True)
        def conv(off, n, coff, head_dim, scale, name):
            r = _conv_proj([hns[gi] for gi in order], pw['w_in_t'], l, off, n, pw['gdn_conv_w'][l],
                           [groups[gi].buf_gdn8[l] for gi in order], coff, [groups[gi].t for gi in order],
                           [tms[gi] for gi in order], MM_TN, head_dim, scale, name)
            return [r[order.index(gi)] for gi in range(len(groups))]

        cq = conv(g_off, hq, 0, GDN_DK, GDN_DK ** -0.5, "in_proj_gdn_q")
        ck = conv(g_off + hq, hq, hq, GDN_DK, 1.0, "in_proj_gdn_k")
        cv = conv(g_off + 2 * hq, hv, 2 * hq, GDN_DV, None, "in_proj_gdn_v")
        ys_s = []
        for gi, g in enumerate(groups):
            t, tm, hn = g.t, g.tm, g.hn
            (d_q, tq8), (d_k, tk8), (d_v, tv8) = cq[gi], ck[gi], cv[gi]
            bg8 = jnp.concatenate([tq8, tk8, tv8], axis=-1)
            p_small = _small_proj(hn, pw['w_in_t'], l, win['lr'][0] // LANES, win['ab'][0] // LANES, tm)

            y_ret, sr = _retention(p_ret[gi], g.cos, g.sin, ld, g.s_ret[l], t, g.tb)
            y_gla, sg = _gla(p_gla[gi], p_small, pw['w_gk'][l], pw['b_gk'][l], pw['gla_norm'][l],
                             g.s_gla[l], t, g.tb)
            y_gdn, sd = _gdn(d_q, d_k, d_v, d_z[gi], p_small, pw['alog_row'][l], pw['dtb_row'][l],
                             pw['gdn_norm'][l], g.s_gdn[l], t, g.tb_gdn)

            ys_s.append((y_ret, y_gla, y_gdn))
            for lst, val in zip(g.outs[:4], (sr, sg, sd, bg8[:, SUBLANES - (GDN_CONV - 1):])):
                lst.append(val)

        xs = lambda: [g.x2 for g in groups]
        merged = _merge(ys_s, pw['w_branch'][l], g_br, pw['b_gate'][l], [g.tm_res_k for g in groups])
        res = _proj_res(merged, pw['w_out'][l], xs(), pw['ln_xattn'][l], [g.tm_ffo for g in groups], ROW_SUB,
                        False, "out_proj_xq", w2=pw['w_xq'][l])
        for g, (x2, _) in zip(groups, res):
            g.x2 = x2
        os_ = [_xattn(q, g.mem_k, g.mem_v, l, g.t, g.tq) for (_, q), g in zip(res, groups)]
        res = _proj_res(os_, pw['w_xo'][l], xs(), pw['ln_ffn'][l], [g.tm_res for g in groups], ROW_SUB,
                        False, "xattn_out")
        for g, (x2, _) in zip(groups, res):
            g.x2 = x2
        ff = _ffn_in([res[gi][1] for gi in order], pw['w_ffn_in'], l, pw['ffn_conv_w'][l], pw['ffn_conv_b'][l],
                     [groups[gi].buf_ffn8[l] for gi in order], [groups[gi].t for gi in order],
                     [groups[gi].tm_ffn for gi in order], FFN_TN)
        acts = [ff[order.index(gi)][0] for gi in range(len(groups))]
        for gi, g in enumerate(groups):
            g.outs[4].append(ff[order.index(gi)][1][:, SUBLANES - (FFN_CONV - 1):])
        last = l + 1 == depth
        res = _proj_res(acts, pw['w_ffn_out'][l], xs(), pw['ln_final'] if last else pw['ln_mix'][l + 1],
                        [g.tm_ffo for g in groups], ROW_SUB, last, "ffn_out_final" if last else "ffn_out")
        for g, r in zip(groups, res):
            if last:
                g.y, = r
            else:
                g.x2, g.hn = r
    return [(g.y.reshape(g.b, g.t, d),) + tuple(jnp.stack(lst) for lst in g.outs) for g in groups]


def kernel(x_prompt, x_sample, mem_prompt, state_ret, state_gla, state_gdn, state_gdn_conv, state_ffn_conv, cache_mem_k, cache_mem_v, ln_mix, w_in, w_gla_gk, b_gla_gk, gla_norm, gdn_conv_w, gdn_a_log, gdn_dt_bias, gdn_norm, b_gate, w_branch, w_out, ln_xattn, ln_mem, w_xq, w_xkv, w_xo, ln_ffn, w_ffn_in, ffn_conv_w, ffn_conv_b, w_ffn_out, ln_final):
    pw = _prep_weights(dict(
        w_in=w_in, w_gla_gk=w_gla_gk, b_gla_gk=b_gla_gk, gla_norm=gla_norm, gdn_conv_w=gdn_conv_w,
        gdn_a_log=gdn_a_log, gdn_dt_bias=gdn_dt_bias, gdn_norm=gdn_norm, b_gate=b_gate, w_branch=w_branch,
        w_out=w_out, w_xq=w_xq, w_xo=w_xo, w_ffn_in=w_ffn_in, ffn_conv_w=ffn_conv_w, ffn_conv_b=ffn_conv_b,
        w_ffn_out=w_ffn_out, ln_mix=ln_mix, ln_xattn=ln_xattn, ln_ffn=ln_ffn, ln_final=ln_final))
    depth = w_in.shape[0]
    bp, nmem, d = mem_prompt.shape
    dt = x_prompt.dtype

    mem2 = mem_prompt.reshape(bp * nmem, d)
    mem_k_p, mem_kb = _mem_proj(mem2, ln_mem, w_xkv, 0, bp, nmem, "mem_k")
    mem_v_p, mem_vb = _mem_proj(mem2, ln_mem, w_xkv, 1, bp, nmem, "mem_v")

    zeros = lambda *s: jnp.zeros((depth, bp) + s, dt)
    prompt = _Group(
        x_prompt, 0.0, mem_kb, mem_vb,
        zeros(RET_H, RET_DK, RET_DV), zeros(GLA_H, GLA_DK, GLA_DV), zeros(GDN_H, GDN_DK, GDN_DV),
        zeros(GDN_CONV - 1, state_gdn_conv.shape[-1]), zeros(FFN_CONV - 1, state_ffn_conv.shape[-1]))
    sample = _Group(x_sample, float(PAST_LEN), cache_mem_k, cache_mem_v,
                    state_ret, state_gla, state_gdn, state_gdn_conv, state_ffn_conv)
    ((y_p, ret_p, gla_p, gdn_p, gdn_conv_p, ffn_conv_p),
     (y_s, ret_s, gla_s, gdn_s, gdn_conv_s, ffn_conv_s)) = _trunks([prompt, sample], pw)

    return (y_p, y_s, ret_p, gla_p, gdn_p, gdn_conv_p, ffn_conv_p, mem_k_p, mem_v_p,
            ret_s, gla_s, gdn_s, gdn_conv_s, ffn_conv_s)
```
